```python
import jax
import jax.numpy as jnp
from jax import lax
import numpy as np

D_MODEL = 2048
BATCH = 4
SEQ = 2048
DEPTH = 4
DEC_BATCH = 128
DEC_SEQ = 1
PAST_LEN = 16384
PAGE_SIZE = 128

N_MIXERS = 3
N_META = 16
NORM_EPS = 1e-6
N_A = (DEPTH + N_MIXERS - 1) // N_MIXERS
N_B = (DEPTH - 1 + N_MIXERS - 1) // N_MIXERS
N_C = (DEPTH - 2 + N_MIXERS - 1) // N_MIXERS
HG_HEAD = 128
HG_HEADS = D_MODEL // HG_HEAD
HG_CHUNK = 64
GATE_EXP_CLIP = 60.0
LRU_WIDTH = D_MODEL
LRU_BLOCKS = 8
LRU_BW = LRU_WIDTH // LRU_BLOCKS
CONV_W = 4
LRU_C = 8.0
RW_HEAD = 64
RW_HEADS = D_MODEL // RW_HEAD
RW_DECAY_LORA = max(32, int(round(1.8 * D_MODEL ** 0.5 / 32)) * 32)
RW_A_LORA = max(32, int(round(1.8 * D_MODEL ** 0.5 / 32)) * 32)
RW_GATE_LORA = max(32, int(round(0.6 * D_MODEL ** 0.8 / 32)) * 32)
RW_LN_EPS = 64e-5
D_FF = ((8 * D_MODEL // 3 + 255) // 256) * 256

kernel_name = 'hybrid_hgrn2_rglru_rwkv7_step'


def rms_norm(x, g, eps=NORM_EPS):
    xf = x.astype(jnp.float32)
    y = xf * lax.rsqrt(jnp.mean(xf * xf, axis=-1, keepdims=True) + eps)
    return (y * g.astype(jnp.float32)).astype(x.dtype)


def swiglu(x, w_in, w_out):
    gate, up = jnp.split(x @ w_in, 2, axis=-1)
    return (jax.nn.silu(gate) * up) @ w_out


def gla_chunked(q, k, v, log_g, s0):
    B, L, H, K = q.shape
    C = min(HG_CHUNK, L)
    pad = (-L) % C
    if pad:
        cfg = ((0, 0), (0, pad), (0, 0), (0, 0))
        q, k, v, log_g = (jnp.pad(t, cfg) for t in (q, k, v, log_g))
    n = (L + pad) // C

    def blocks(t):
        return jnp.moveaxis(t.reshape(B, n, C, H, t.shape[-1]), 1, 0)

    causal = jnp.tril(jnp.ones((C, C), bool))[None, :, :, None, None]

    def step(S, inp):
        qc, kc, vc, gc = inp
        b = jnp.cumsum(gc, axis=1)
        o_inter = jnp.einsum('bthk,bhkv->bthv', qc * jnp.exp(b), S)
        diff = jnp.where(causal, b[:, :, None] - b[:, None, :], 0.0)
        decay = jnp.where(causal, jnp.exp(diff), 0.0)
        attn = jnp.einsum('bthk,btshk,bshk->bhts', qc, decay, kc)
        o_intra = jnp.einsum('bhts,bshv->bthv', attn, vc)
        b_end = b[:, -1]
        S_new = jnp.exp(b_end)[..., None] * S + jnp.einsum(
            'bshk,bshv->bhkv', kc * jnp.exp(b_end[:, None] - b), vc)
        return S_new.astype(S.dtype), o_inter + o_intra

    S_fin, o = lax.scan(step, s0, (blocks(q), blocks(k), blocks(v), blocks(log_g)))
    o = jnp.moveaxis(o, 0, 1).reshape(B, n * C, H, v.shape[-1])[:, :L]
    return o, S_fin


def hgrn2_mixer(x, s0, segments, w_in, lb, g_norm, w_out):
    B, L, _ = x.shape
    q, f, i, g = jnp.split(x @ w_in, 4, axis=-1)
    q = jax.nn.silu(q)
    log_f = jax.nn.log_sigmoid(f) + jnp.log1p(lb * jnp.exp(jnp.minimum(-f, GATE_EXP_CLIP)))
    k = (1 - lb) * jax.nn.sigmoid(-f)
    heads = lambda t: t.reshape(B, L, HG_HEADS, HG_HEAD)
    q, k, i, log_f = heads(q), heads(k), heads(i), heads(log_f)
    outs, S, start = [], s0, 0
    for seg in segments:
        o, S = gla_chunked(q[:, start:start + seg], k[:, start:start + seg],
                           i[:, start:start + seg], log_f[:, start:start + seg], S)
        outs.append(o)
        start += seg
    o = outs[0] if len(outs) == 1 else jnp.concatenate(outs, axis=1)
    o = rms_norm(o, g_norm).reshape(B, L, D_MODEL) * jax.nn.silu(g)
    return o @ w_out, S


def _lin_combine(left, right):
    a1, b1 = left
    a2, b2 = right
    return a1 * a2, a2 * b1 + b2


def rglru_mixer(x, h0, conv_buf, reset_first, w_in, conv_w, conv_b, gate_w, gate_b, lam, w_out):
    B, L, _ = x.shape
    y_br, x_br = jnp.split(x @ w_in, 2, axis=-1)
    y_br = jax.nn.gelu(y_br, approximate=True)
    xpad = jnp.concatenate([conv_buf.astype(x_br.dtype), x_br], axis=1)
    xc = conv_b
    for j in range(CONV_W):
        xc = xc + xpad[:, j:j + L] * conv_w[j]
    new_buf = xpad[:, -(CONV_W - 1):]
    xb = xc.reshape(B, L, LRU_BLOCKS, LRU_BW)
    gates = jnp.einsum('blnc,gncd->gblnd', xb, gate_w).reshape(2, B, L, LRU_WIDTH) + gate_b[:, None, None]
    r = jax.nn.sigmoid(gates[0])
    ig = jax.nn.sigmoid(gates[1])
    log_a = -LRU_C * r * jax.nn.softplus(-lam)
    a = jnp.exp(log_a)
    mult = jnp.sqrt(jnp.maximum(-jnp.expm1(2 * log_a), 0.0))
    if reset_first:
        mult = mult.at[:, 0].set(1)
    b = xc * ig * mult
    b = b.at[:, 0].add(a[:, 0] * h0.astype(b.dtype))
    _, h = lax.associative_scan(_lin_combine, (a, b), axis=1)
    return (h * y_br) @ w_out, h[:, -1], new_buf


def rwkv7_mixer(x, S0, shift0, mu, w_rkv, w0, w1, w2, a0, a1, a2, g1, g2, k_k, k_a, r_k, ln_w, ln_b, w_o):
    B, L, D = x.shape
    H, N = RW_HEADS, RW_HEAD
    x_prev = jnp.concatenate([shift0[:, None].astype(x.dtype), x[:, :-1]], axis=1)
    xmix = x[None] + (x_prev - x)[None] * mu[:, None, None, :]
    r, k, v = jnp.einsum('nbld,nde->nble', xmix[:3], w_rkv)
    w = -jax.nn.softplus(-(w0 + jnp.tanh(xmix[3] @ w1) @ w2)) - 0.5
    a = jax.nn.sigmoid(a0 + (xmix[4] @ a1) @ a2)
    g = jax.nn.sigmoid(xmix[5] @ g1) @ g2
    heads = lambda t: t.reshape(B, L, H, N)
    kk = heads(k * k_k).astype(jnp.float32)
    kk = (kk / jnp.maximum(jnp.sqrt(jnp.sum(kk * kk, axis=-1, keepdims=True)), 1e-12)).astype(x.dtype)
    k = heads(k * (1 + (a - 1) * k_a))
    a, r, v = heads(a), heads(r), heads(v)
    decay = jnp.exp(-jnp.exp(heads(w)))

    def step(S, inp):
        r_t, d_t, k_t, v_t, kk_t, a_t = inp
        sa = jnp.einsum('bhij,bhj->bhi', S, -kk_t)
        S = (S * d_t[:, :, None, :] + sa[..., None] * (kk_t * a_t)[:, :, None, :]
             + v_t[..., None] * k_t[:, :, None, :]).astype(S.dtype)
        return S, jnp.einsum('bhij,bhj->bhi', S, r_t)

    seq = tuple(jnp.moveaxis(t, 1, 0) for t in (r, decay, k, v, kk, a))
    S_fin, y = lax.scan(step, S0, seq)
    y = jnp.moveaxis(y, 0, 1).astype(jnp.float32)
    mean = jnp.mean(y, axis=-1, keepdims=True)
    var = jnp.mean(jnp.square(y - mean), axis=-1, keepdims=True)
    yn = ((y - mean) * lax.rsqrt(var + RW_LN_EPS)).reshape(B, L, D)
    yn = (yn * ln_w.astype(jnp.float32) + ln_b.astype(jnp.float32)).astype(x.dtype)
    bonus = (jnp.sum(r * k * r_k, axis=-1, keepdims=True) * v).reshape(B, L, D)
    return ((yn + bonus) * g) @ w_o, S_fin, x[:, -1]


def setup_inputs(seed: int = 0) -> dict:
    key = jax.random.key(seed)
    ks = iter(jax.random.split(key, 64))
    D, R = D_MODEL, LRU_WIDTH

    def nrm(shape, scale):
        return jax.random.normal(next(ks), shape, jnp.float32) * scale

    def uni(shape, lo, hi):
        return jax.random.uniform(next(ks), shape, jnp.float32, lo, hi)

    u = uni((N_B, R), 0.9, 0.999)
    return {
        'x_prompt': nrm((BATCH, SEQ, D), 1.0),
        'x_sample': nrm((DEC_BATCH, DEC_SEQ, D), 1.0),
        'state_hgrn': nrm((N_A, DEC_BATCH, HG_HEADS, HG_HEAD, HG_HEAD), 0.5),
        'state_lru_h': nrm((N_B, DEC_BATCH, R), 0.5),
        'state_lru_conv': nrm((N_B, DEC_BATCH, CONV_W - 1, R), 1.0),
        'state_rwkv': nrm((N_C, DEC_BATCH, RW_HEADS, RW_HEAD, RW_HEAD), 0.2),
        'state_rwkv_shift': nrm((N_C, DEC_BATCH, D), 1.0),
        'meta_tokens': nrm((N_META, D), 1.0),
        'norm_mix': 1.0 + nrm((DEPTH, D), 0.02),
        'norm_ffn': 1.0 + nrm((DEPTH, D), 0.02),
        'norm_final': 1.0 + nrm((D,), 0.02),
        'hgrn_w_in': nrm((N_A, D, 4 * D), D ** -0.5),
        'hgrn_lb_logits': nrm((N_A, D), 1.0),
        'hgrn_norm': 1.0 + nrm((N_A, HG_HEADS, HG_HEAD), 0.02),
        'hgrn_w_out': nrm((N_A, D, D), D ** -0.5),
        'lru_w_in': nrm((N_B, D, 2 * R), D ** -0.5),
        'lru_conv_w': nrm((N_B, CONV_W, R), CONV_W ** -0.5),
        'lru_conv_b': nrm((N_B, R), 0.01),
        'lru_gate_w': nrm((N_B, 2, LRU_BLOCKS, LRU_BW, LRU_BW), LRU_BW ** -0.5),
        'lru_gate_b': nrm((N_B, 2, R), 0.01),
        'lru_lambda': jnp.log(u) - jnp.log1p(-u),
        'lru_w_out': nrm((N_B, R, D), R ** -0.5),
        'rwkv_mu': uni((N_C, 6, D), 0.0, 1.0),
        'rwkv_w_rkv': nrm((N_C, 3, D, D), D ** -0.5),
        'rwkv_w0': uni((N_C, D), -3.0, 1.0),
        'rwkv_w1': nrm((N_C, D, RW_DECAY_LORA), D ** -0.5),
        'rwkv_w2': nrm((N_C, RW_DECAY_LORA, D), 0.1 * RW_DECAY_LORA ** -0.5),
        'rwkv_a0': nrm((N_C, D), 0.1),
        'rwkv_a1': nrm((N_C, D, RW_A_LORA), D ** -0.5),
        'rwkv_a2': nrm((N_C, RW_A_LORA, D), RW_A_LORA ** -0.5),
        'rwkv_g1': nrm((N_C, D, RW_GATE_LORA), D ** -0.5),
        'rwkv_g2': nrm((N_C, RW_GATE_LORA, D), RW_GATE_LORA ** -0.5),
        'rwkv_k_k': 0.85 + nrm((N_C, D), 0.02),
        'rwkv_k_a': 1.0 + nrm((N_C, D), 0.02),
        'rwkv_r_k': nrm((N_C, RW_HEADS, RW_HEAD), 0.1),
        'rwkv_ln_w': 1.0 + nrm((N_C, D), 0.02),
        'rwkv_ln_b': nrm((N_C, D), 0.01),
        'rwkv_w_o': nrm((N_C, D, D), D ** -0.5),
        'ffn_w_in': nrm((DEPTH, D, 2 * D_FF), D ** -0.5),
        'ffn_w_out': nrm((DEPTH, D_FF, D), D_FF ** -0.5),
    }


def reference(x_prompt, x_sample, state_hgrn, state_lru_h, state_lru_conv, state_rwkv, state_rwkv_shift,
              meta_tokens, norm_mix, norm_ffn, norm_final, hgrn_w_in, hgrn_lb_logits, hgrn_norm, hgrn_w_out,
              lru_w_in, lru_conv_w, lru_conv_b, lru_gate_w, lru_gate_b, lru_lambda, lru_w_out,
              rwkv_mu, rwkv_w_rkv, rwkv_w0, rwkv_w1, rwkv_w2, rwkv_a0, rwkv_a1, rwkv_a2, rwkv_g1, rwkv_g2,
              rwkv_k_k, rwkv_k_a, rwkv_r_k, rwkv_ln_w, rwkv_ln_b, rwkv_w_o, ffn_w_in, ffn_w_out):
    dt = x_prompt.dtype
    bp, bs = x_prompt.shape[0], x_sample.shape[0]
    p = jax.nn.softmax(hgrn_lb_logits.astype(jnp.float32), axis=0)
    lower_bounds = jnp.clip(jnp.cumsum(p, axis=0) - p[0], 0.0, 1.0).astype(dt)
    meta = jnp.broadcast_to(meta_tokens.astype(dt)[None], (bp, N_META, D_MODEL))
    hp = jnp.concatenate([meta, x_prompt], axis=1)
    hs = x_sample
    seg_p = (N_META, hp.shape[1] - N_META)
    seg_s = (hs.shape[1],)
    hg_p, hg_s, lh_p, lh_s, lc_p, lc_s, rw_p, rw_s, rs_p, rs_s = [[] for _ in range(10)]
    for i in range(DEPTH):
        m, j = i % N_MIXERS, i // N_MIXERS
        up = rms_norm(hp, norm_mix[i])
        us = rms_norm(hs, norm_mix[i])
        if m == 0:
            w = (hgrn_w_in[j], lower_bounds[j], hgrn_norm[j], hgrn_w_out[j])
            op, sp = hgrn2_mixer(up, jnp.zeros((bp, HG_HEADS, HG_HEAD, HG_HEAD), dt), seg_p, *w)
            os_, ss = hgrn2_mixer(us, state_hgrn[j], seg_s, *w)
            hg_p.append(sp)
            hg_s.append(ss)
        elif m == 1:
            w = (lru_w_in[j], lru_conv_w[j], lru_conv_b[j], lru_gate_w[j], lru_gate_b[j], lru_lambda[j], lru_w_out[j])
            op, hpf, cp = rglru_mixer(up, jnp.zeros((bp, LRU_WIDTH), dt),
                                      jnp.zeros((bp, CONV_W - 1, LRU_WIDTH), dt), True, *w)
            os_, hsf, cs = rglru_mixer(us, state_lru_h[j], state_lru_conv[j], False, *w)
            lh_p.append(hpf)
            lh_s.append(hsf)
            lc_p.append(cp)
            lc_s.append(cs)
        else:
            w = (rwkv_mu[j], rwkv_w_rkv[j], rwkv_w0[j], rwkv_w1[j], rwkv_w2[j], rwkv_a0[j], rwkv_a1[j],
                 rwkv_a2[j], rwkv_g1[j], rwkv_g2[j], rwkv_k_k[j], rwkv_k_a[j], rwkv_r_k[j],
                 rwkv_ln_w[j], rwkv_ln_b[j], rwkv_w_o[j])
            op, sp, shp = rwkv7_mixer(up, jnp.zeros((bp, RW_HEADS, RW_HEAD, RW_HEAD), dt),
                                      jnp.zeros((bp, D_MODEL), dt), *w)
            os_, ss, shs = rwkv7_mixer(us, state_rwkv[j], state_rwkv_shift[j], *w)
            rw_p.append(sp)
            rw_s.append(ss)
            rs_p.append(shp)
            rs_s.append(shs)
        hp = hp + op
        hs = hs + os_
        hp = hp + swiglu(rms_norm(hp, norm_ffn[i]), ffn_w_in[i], ffn_w_out[i])
        hs = hs + swiglu(rms_norm(hs, norm_ffn[i]), ffn_w_in[i], ffn_w_out[i])
    y_prompt = rms_norm(hp, norm_final)[:, N_META:]
    y_sample = rms_norm(hs, norm_final)
    return (y_prompt, y_sample, jnp.stack(hg_p), jnp.stack(hg_s), jnp.stack(lh_p), jnp.stack(lh_s),
            jnp.stack(lc_p), jnp.stack(lc_s), jnp.stack(rw_p), jnp.stack(rw_s), jnp.stack(rs_p), jnp.stack(rs_s))
```

```python
import functools

import jax
import jax.numpy as jnp
from jax import lax
from jax.experimental import pallas as pl
from jax.experimental.pallas import tpu as pltpu

f32 = jnp.float32
bf16 = jnp.bfloat16

D = 2048
NB = 4
LP = 2048
NM = 16
NS = 128
ROWS_MAIN = NB * LP
ROW_S = ROWS_MAIN
ROW_M = ROW_S + NS
AUX = 256
ROWS = ROWS_MAIN + AUX
EPS = 1e-6
HG_H, HG_K = 16, 128
HG_C = 64
HG_SUB = 16
GATE_EXP_CLIP = 60.0
LRU_NBLK, LRU_BW = 8, 256
LRU_C = 8.0
LRU_T = 256
RW_H, RW_N = 32, 64
RW_LN_EPS = 64e-5
RW_TC = 48
LORA_PAD = 128
D_FF = 5632
VMEM_LIMIT = 56 * 1024 * 1024


def _cp(sem):
    return pltpu.CompilerParams(dimension_semantics=sem, vmem_limit_bytes=VMEM_LIMIT)


def _norm_kernel(h_ref, g_ref, o_ref):
    x = h_ref[...]
    ms = jnp.mean(x * x, axis=-1, keepdims=True)
    o_ref[...] = (x * lax.rsqrt(ms + EPS) * g_ref[...]).astype(o_ref.dtype)


def _rmsnorm(h, g, *, out_dtype, tm, row_tile0, n_row_tiles):
    return pl.pallas_call(
        _norm_kernel,
        grid=(n_row_tiles,),
        in_specs=[pl.BlockSpec((tm, D), lambda i: (i + row_tile0, 0)),
                  pl.BlockSpec((1, D), lambda i: (0, 0))],
        out_specs=pl.BlockSpec((tm, D), lambda i: (i, 0)),
        out_shape=jax.ShapeDtypeStruct((n_row_tiles * tm, D), out_dtype),
        compiler_params=_cp(("arbitrary",)),
    )(h, g)


def _mm_body(*refs, n_x, n_w, n_kv, n_cv, n_rm, n_out, prologue, epilogue):
    p = 0
    x_refs = refs[p:p + n_x]; p += n_x
    w_refs = refs[p:p + n_w]; p += n_w
    kv_refs = refs[p:p + n_kv]; p += n_kv
    cv_refs = refs[p:p + n_cv]; p += n_cv
    rm_refs = refs[p:p + n_rm]; p += n_rm
    out_refs = refs[p:p + n_out]; p += n_out
    w_scr = refs[p:p + n_w]

    @pl.when(pl.program_id(1) == 0)
    def _cast_weights():
        for w, s in zip(w_refs, w_scr):
            s[...] = w[...].astype(bf16)

    x = prologue([r[...] for r in x_refs], [r[...] for r in kv_refs])
    accs = [jnp.dot(x, s[...], preferred_element_type=f32) for s in w_scr]
    res = epilogue(accs, [r[...] for r in cv_refs], [r[...] for r in rm_refs])
    for o, r in zip(out_refs, res):
        o[...] = r.astype(o.dtype)


def _mm(*, xs, ws, kvecs=(), cvecs=(), rmats=(), outs, K, tm, tn, n_row_tiles, n_col_tiles,
        prologue, epilogue, aliases=None):
    in_specs, args = [], []
    for a, r0 in xs:
        in_specs.append(pl.BlockSpec((tm, K), lambda j, i, r0=r0: (i + r0, 0)))
        args.append(a)
    for a, lead, c0 in ws:
        nl = len(lead)
        in_specs.append(pl.BlockSpec((None,) * nl + (K, tn), lambda j, i, lead=lead, c0=c0: lead + (0, j + c0)))
        args.append(a)
    for a in kvecs:
        in_specs.append(pl.BlockSpec(a.shape, lambda j, i: (0, 0)))
        args.append(a)
    for a, c0 in cvecs:
        in_specs.append(pl.BlockSpec((a.shape[0], tn), lambda j, i, c0=c0: (0, j + c0)))
        args.append(a)
    for a, r0, c0 in rmats:
        in_specs.append(pl.BlockSpec((tm, tn), lambda j, i, r0=r0, c0=c0: (i + r0, j + c0)))
        args.append(a)
    out_specs = [pl.BlockSpec((tm, tn), lambda j, i, r0=r0, c0=c0: (i + r0, j + c0)) for _, r0, c0 in outs]
    body = functools.partial(_mm_body, n_x=len(xs), n_w=len(ws), n_kv=len(kvecs), n_cv=len(cvecs),
                             n_rm=len(rmats), n_out=len(outs), prologue=prologue, epilogue=epilogue)
    res = pl.pallas_call(
        body,
        grid=(n_col_tiles, n_row_tiles),
        in_specs=in_specs,
        out_specs=out_specs,
        out_shape=[s for s, _, _ in outs],
        scratch_shapes=[pltpu.VMEM((K, tn), bf16) for _ in ws],
        input_output_aliases=aliases or {},
        compiler_params=_cp(("arbitrary", "arbitrary")),
    )(*args)
    return res


def _pro_id(xs, kvs):
    return xs[0]


def _pro_mul(xs, kvs):
    return (xs[0] * xs[1]).astype(bf16)


def _sds(shape, dtype):
    return jax.ShapeDtypeStruct(shape, dtype)


def _residual_out(h, x_main, x_aux, w, lead, K, prologue, extra_main=(), extra_aux=()):
    tn = 512
    n_in_main = 1 + len(extra_main) + 1
    h = _mm(xs=[(x_main, 0)] + [(a, r0) for a, r0 in extra_main], ws=[(w, lead, 0)],
            rmats=[(h, 0, 0)], outs=[(_sds((ROWS, D), f32), 0, 0)],
            K=K, tm=512, tn=tn, n_row_tiles=ROWS_MAIN // 512, n_col_tiles=D // tn,
            prologue=prologue, epilogue=lambda accs, cvs, rms: (rms[0] + accs[0],),
            aliases={n_in_main: 0})[0]
    n_in_aux = 1 + len(extra_aux) + 1
    h = _mm(xs=[(x_aux, 0)] + [(a, r0) for a, r0 in extra_aux], ws=[(w, lead, 0)],
            rmats=[(h, ROWS_MAIN // AUX, 0)], outs=[(_sds((ROWS, D), f32), ROWS_MAIN // AUX, 0)],
            K=K, tm=AUX, tn=tn, n_row_tiles=1, n_col_tiles=D // tn,
            prologue=prologue, epilogue=lambda accs, cvs, rms: (rms[0] + accs[0],),
            aliases={n_in_aux: 0})[0]
    return h


def _ffn(h, norm_g, w_in, w_out, layer):
    xn = _rmsnorm(h, norm_g, out_dtype=bf16, tm=768, row_tile0=0, n_row_tiles=ROWS // 768)
    tn = 512
    nct = D_FF // tn
    act = _mm(xs=[(xn, 0)], ws=[(w_in, (layer,), 0), (w_in, (layer,), nct)],
              outs=[(_sds((ROWS, D_FF), bf16), 0, 0)],
              K=D, tm=768, tn=tn, n_row_tiles=ROWS // 768, n_col_tiles=nct,
              prologue=_pro_id,
              epilogue=lambda accs, cvs, rms: (jax.nn.silu(accs[0]) * accs[1],))[0]
    h = _mm(xs=[(act, 0)], ws=[(w_out, (layer,), 0)], rmats=[(h, 0, 0)],
            outs=[(_sds((ROWS, D), f32), 0, 0)],
            K=D_FF, tm=384, tn=tn, n_row_tiles=ROWS // 384, n_col_tiles=D // tn,
            prologue=_pro_id, epilogue=lambda accs, cvs, rms: (rms[0] + accs[0],),
            aliases={2: 0})[0]
    return h


def _hgrn_lower_bound(logits, j):
    m = jnp.max(logits, axis=0, keepdims=True)
    e = jnp.exp(logits - m)
    p = e / jnp.sum(e, axis=0, keepdims=True)
    cs = p[0:1]
    for r in range(1, j + 1):
        cs = cs + p[r:r + 1]
    return jnp.clip(cs - p[0:1], 0.0, 1.0)


def _hgrn_in(xn, w_in, lb_logits, j):
    tn = 512
    nct = D // tn
    common = dict(K=D, tm=768, tn=tn, n_row_tiles=ROWS // 768, n_col_tiles=nct, prologue=_pro_id)
    o = (_sds((ROWS, D), f32), 0, 0)
    q = _mm(xs=[(xn, 0)], ws=[(w_in, (j,), 0)], outs=[o],
            epilogue=lambda accs, cvs, rms: (jax.nn.silu(accs[0]),), **common)[0]

    def f_epi(accs, cvs, rms):
        f = accs[0]
        lb = _hgrn_lower_bound(cvs[0], j)
        log_f = jax.nn.log_sigmoid(f) + jnp.log1p(lb * jnp.exp(jnp.minimum(-f, GATE_EXP_CLIP)))
        k = (1.0 - lb) * jax.nn.sigmoid(-f)
        return log_f, k

    log_f, k = _mm(xs=[(xn, 0)], ws=[(w_in, (j,), nct)], cvecs=[(lb_logits, 0)], outs=[o, o],
                   epilogue=f_epi, **common)
    v = _mm(xs=[(xn, 0)], ws=[(w_in, (j,), 2 * nct)], outs=[o],
            epilogue=lambda accs, cvs, rms: (accs[0],), **common)[0]
    g = _mm(xs=[(xn, 0)], ws=[(w_in, (j,), 3 * nct)], outs=[o],
            epilogue=lambda accs, cvs, rms: (jax.nn.silu(accs[0]),), **common)[0]
    return q, k, v, log_f, g


def _split3(x):
    hi = x.astype(bf16)
    r = x - hi.astype(f32)
    mid = r.astype(bf16)
    lo = (r - mid.astype(f32)).astype(bf16)
    return hi, mid, lo


def _gla_block(q_ref, k_ref, v_ref, f_ref, g_ref, gn_ref, o_ref, st_ref, T):
    ri = lax.broadcasted_iota(jnp.int32, (T, T), 0)
    ci = lax.broadcasted_iota(jnp.int32, (T, T), 1)
    tri = jnp.where(ri >= ci, 1.0, 0.0).astype(bf16)
    rows16 = lax.broadcasted_iota(jnp.int32, (HG_SUB, 1), 0)
    dn_t = (((1,), (1,)), ((), ()))
    dn_l = (((0,), (0,)), ((), ()))

    def head(h, carry):
        sl = pl.ds(pl.multiple_of(h * HG_K, HG_K), HG_K)
        q = q_ref[:, sl]
        k = k_ref[:, sl]
        v = v_ref[:, sl]
        lf = f_ref[:, sl]
        hi, mid, lo = _split3(lf)
        b = (jnp.dot(tri, hi, preferred_element_type=f32) + jnp.dot(tri, mid, preferred_element_type=f32)
             + jnp.dot(tri, lo, preferred_element_type=f32))
        bend = b[T - 1:T, :]
        st = st_ref[h]
        qe = (q * jnp.exp(b)).astype(bf16)
        o = lax.dot_general(qe, st.astype(bf16), dn_t, preferred_element_type=f32)
        vb = v.astype(bf16)
        parts = []
        for blk in range(T // HG_SUB):
            r0 = blk * HG_SUB
            qI = q[r0:r0 + HG_SUB]
            bI = b[r0:r0 + HG_SUB]
            kI = k[r0:r0 + HG_SUB]
            vI = v[r0:r0 + HG_SUB]
            oI = o[r0:r0 + HG_SUB]
            if blk > 0:
                b0 = bI[0:1]
                qd = (qI * jnp.exp(bI - b0)).astype(bf16)
                kd = (k[:r0] * jnp.exp(b0 - b[:r0])).astype(bf16)
                att = lax.dot_general(qd, kd, dn_t, preferred_element_type=f32)
                oI = oI + jnp.dot(att.astype(bf16), vb[:r0], preferred_element_type=f32)
            for s in range(HG_SUB):
                dec = jnp.exp(jnp.minimum(bI - bI[s:s + 1], 0.0))
                col = jnp.sum(qI * kI[s:s + 1] * dec, axis=-1, keepdims=True)
                col = jnp.where(rows16 >= s, col, 0.0)
                oI = oI + col * vI[s:s + 1]
            parts.append(oI)
        o = parts[0] if len(parts) == 1 else jnp.concatenate(parts, axis=0)
        kdec = (k * jnp.exp(bend - b)).astype(bf16)
        upd = lax.dot_general(vb, kdec, dn_l, preferred_element_type=f32)
        st_ref[h] = st * jnp.exp(bend) + upd
        ms = jnp.mean(o * o, axis=-1, keepdims=True)
        on = o * lax.rsqrt(ms + EPS) * gn_ref[:, sl]
        o_ref[:, sl] = (on * g_ref[:, sl]).astype(o_ref.dtype)
        return carry

    lax.fori_loop(0, HG_H, head, 0)


def _gla_kernel(qm, km, vm, fm, gm, qa, ka, va, fa, ga, gn_ref, om_ref, oa_ref, s_ref, st_ref):
    c = pl.program_id(1)

    @pl.when(c == 0)
    def _meta():
        st_ref[...] = jnp.zeros_like(st_ref)
        _gla_block(qa, ka, va, fa, ga, gn_ref, oa_ref, st_ref, NM)

    @pl.when(c > 0)
    def _main():
        _gla_block(qm, km, vm, fm, gm, gn_ref, om_ref, st_ref, HG_C)

    @pl.when(c == pl.num_programs(1) - 1)
    def _final():
        for h in range(HG_H):
            s_ref[0, h] = st_ref[h].T


def _hgrn_prompt(q, k, v, log_f, g, gn):
    nc = LP // HG_C
    main_spec = pl.BlockSpec((HG_C, D), lambda b, c: (b * nc + jnp.maximum(c - 1, 0), 0))
    meta_spec = pl.BlockSpec((NM, D), lambda b, c: (ROW_M // NM + b, 0))
    arrs = [q, k, v, log_f, g]
    return pl.pallas_call(
        _gla_kernel,
        grid=(NB, nc + 1),
        in_specs=[main_spec] * 5 + [meta_spec] * 5 + [pl.BlockSpec((1, D), lambda b, c: (0, 0))],
        out_specs=[pl.BlockSpec((HG_C, D), lambda b, c: (b * nc + jnp.maximum(c - 1, 0), 0)),
                   pl.BlockSpec((NM, D), lambda b, c: (b, 0)),
                   pl.BlockSpec((1, HG_H, HG_K, HG_K), lambda b, c: (b, 0, 0, 0))],
        out_shape=[_sds((ROWS_MAIN, D), bf16), _sds((NB * NM, D), bf16), _sds((NB, HG_H, HG_K, HG_K), f32)],
        scratch_shapes=[pltpu.VMEM((HG_H, HG_K, HG_K), f32)],
        compiler_params=_cp(("arbitrary", "arbitrary")),
    )(*arrs, *arrs, gn)


HG_BT = 8


def _hgrn_dec_kernel(s_ref, qT, kT, fT, v_ref, g_ref, gn_ref, so_ref, o_ref):
    def per_b(bb, carry):
        qm = qT[bb]
        km = kT[bb]
        gm = jnp.exp(fT[bb])
        vall = v_ref[pl.ds(bb, 1), :]
        outs = []
        for h in range(HG_H):
            hs = slice(h * HG_K, (h + 1) * HG_K)
            sn = gm[:, h:h + 1] * s_ref[bb, h] + km[:, h:h + 1] * vall[:, hs]
            so_ref[bb, h] = sn
            o = jnp.sum(qm[:, h:h + 1] * sn, axis=0, keepdims=True)
            ms = jnp.mean(o * o, axis=-1, keepdims=True)
            outs.append(o * lax.rsqrt(ms + EPS))
        on = jnp.concatenate(outs, axis=1) * gn_ref[...]
        o_ref[pl.ds(bb, 1), :] = on * g_ref[pl.ds(bb, 1), :]
        return carry

    lax.fori_loop(0, HG_BT, per_b, 0)


def _hgrn_sample(state, j, q, k, v, log_f, g, gn):
    def colform(x):
        return x[ROW_S:ROW_S + NS].reshape(NS, HG_H, HG_K).transpose(0, 2, 1)

    col_spec = pl.BlockSpec((HG_BT, HG_K, HG_H), lambda i: (i, 0, 0))
    row_spec = pl.BlockSpec((HG_BT, D), lambda i: (ROW_S // HG_BT + i, 0))
    return pl.pallas_call(
        _hgrn_dec_kernel,
        grid=(NS // HG_BT,),
        in_specs=[pl.BlockSpec((None, HG_BT, HG_H, HG_K, HG_K), lambda i: (j, i, 0, 0, 0)),
                  col_spec, col_spec, col_spec, row_spec, row_spec,
                  pl.BlockSpec((1, D), lambda i: (0, 0))],
        out_specs=[pl.BlockSpec((HG_BT, HG_H, HG_K, HG_K), lambda i: (i, 0, 0, 0)),
                   pl.BlockSpec((HG_BT, D), lambda i: (i, 0))],
        out_shape=[_sds((NS, HG_H, HG_K, HG_K), f32), _sds((NS, D), f32)],
        compiler_params=_cp(("arbitrary",)),
    )(state, colform(q), colform(k), colform(log_f), v, g, gn)


def _aux_rows(sample_rows, meta_rows):
    pad = jnp.zeros((AUX - NS - NB * NM, sample_rows.shape[1]), sample_rows.dtype)
    return jnp.concatenate([sample_rows, meta_rows.astype(sample_rows.dtype), pad], axis=0)


def _hgrn_layer(h, norm_g, state, j, w_in, lb_logits, gnorm, w_out):
    xn = _rmsnorm(h, norm_g, out_dtype=bf16, tm=768, row_tile0=0, n_row_tiles=ROWS // 768)
    q, k, v, log_f, g = _hgrn_in(xn, w_in, lb_logits, j)
    gn = gnorm[j].reshape(1, D)
    o_main, o_meta, s_p = _hgrn_prompt(q, k, v, log_f, g, gn)
    s_s, o_s = _hgrn_sample(state, j, q, k, v, log_f, g, gn)
    o_aux = _aux_rows(o_s.astype(bf16), o_meta)
    h = _residual_out(h, o_main, o_aux, w_out, (j,), D, _pro_id)
    return h, s_p, s_s


def _lru_in(xn, w_in, j):
    tn = 512
    nct = D // tn
    common = dict(K=D, tm=768, tn=tn, n_row_tiles=ROWS // 768, n_col_tiles=nct, prologue=_pro_id)
    o = (_sds((ROWS, D), f32), 0, 0)
    y = _mm(xs=[(xn, 0)], ws=[(w_in, (j,), 0)], outs=[o],
            epilogue=lambda accs, cvs, rms: (jax.nn.gelu(accs[0], approximate=True),), **common)[0]
    x = _mm(xs=[(xn, 0)], ws=[(w_in, (j,), nct)], outs=[o],
            epilogue=lambda accs, cvs, rms: (accs[0],), **common)[0]
    return y, x


def _lru_gates(xc, gw_ref, gb_ref, lam_ref, n):
    ls = slice(n * LRU_BW, (n + 1) * LRU_BW)
    xb = xc[:, ls].astype(bf16)
    r = jax.nn.sigmoid(jnp.dot(xb, gw_ref[0, n], preferred_element_type=f32) + gb_ref[0:1, ls])
    ig = jax.nn.sigmoid(jnp.dot(xb, gw_ref[1, n], preferred_element_type=f32) + gb_ref[1:2, ls])
    log_a = -LRU_C * r * jax.nn.softplus(-lam_ref[:, ls])
    a = jnp.exp(log_a)
    t = jnp.tanh(log_a)
    mult = jnp.sqrt(jnp.maximum(-2.0 * t / (1.0 - t), 0.0))
    return a, ig, mult


def _lru_conv(cb_ref, cw_ref, x0, x1, x2, x3):
    xc = cb_ref[...] + x0 * cw_ref[0:1, :]
    xc = xc + x1 * cw_ref[1:2, :]
    xc = xc + x2 * cw_ref[2:3, :]
    return xc + x3 * cw_ref[3:4, :]


def _lru_block(x_ref, y_ref, o_ref, cw_ref, cb_ref, gw_s, gb_ref, lam_ref, xbuf, a_s, b_s, hcar, T, first):
    xbuf[8:8 + T, :] = x_ref[...]
    xc = _lru_conv(cb_ref, cw_ref, xbuf[5:5 + T, :], xbuf[6:6 + T, :], xbuf[7:7 + T, :], xbuf[8:8 + T, :])
    xbuf[0:8, :] = xbuf[T:T + 8, :]
    row = lax.broadcasted_iota(jnp.int32, (T, 1), 0)
    for n in range(LRU_NBLK):
        ls = slice(n * LRU_BW, (n + 1) * LRU_BW)
        a, ig, mult = _lru_gates(xc, gw_s, gb_ref, lam_ref, n)
        if first:
            mult = jnp.where(row == 0, 1.0, mult)
        a_s[0:T, ls] = a
        b_s[0:T, ls] = xc[:, ls] * ig * mult

    def step(t, hprev):
        hnew = a_s[pl.ds(t, 1), :] * hprev + b_s[pl.ds(t, 1), :]
        b_s[pl.ds(t, 1), :] = hnew
        return hnew

    hcar[...] = lax.fori_loop(0, T, step, hcar[...], unroll=8)
    o_ref[...] = (b_s[0:T, :] * y_ref[...]).astype(o_ref.dtype)


def _lru_kernel(xm, ym, xa, ya, cw_ref, cb_ref, gw_ref, gb_ref, lam_ref,
                om_ref, oa_ref, hl_ref, cv_ref, gw_s, xbuf, a_s, b_s, hcar):
    c = pl.program_id(1)

    @pl.when((pl.program_id(0) == 0) & (c == 0))
    def _cast():
        gw_s[...] = gw_ref[...].astype(bf16)

    @pl.when(c == 0)
    def _meta():
        xbuf[0:8, :] = jnp.zeros((8, D), f32)
        hcar[...] = jnp.zeros_like(hcar)
        _lru_block(xa, ya, oa_ref, cw_ref, cb_ref, gw_s, gb_ref, lam_ref, xbuf, a_s, b_s, hcar, NM, True)

    @pl.when(c > 0)
    def _main():
        _lru_block(xm, ym, om_ref, cw_ref, cb_ref, gw_s, gb_ref, lam_ref, xbuf, a_s, b_s, hcar, LRU_T, False)

    @pl.when(c == pl.num_programs(1) - 1)
    def _final():
        hl_ref[0] = hcar[...]
        cv_ref[0] = xbuf[5:8, :]


def _lru_prompt(x, y, j, conv_w, conv_b, gate_w, gate_b, lam):
    nc = LP // LRU_T
    main_spec = pl.BlockSpec((LRU_T, D), lambda b, c: (b * nc + jnp.maximum(c - 1, 0), 0))
    meta_spec = pl.BlockSpec((NM, D), lambda b, c: (ROW_M // NM + b, 0))
    return pl.pallas_call(
        _lru_kernel,
        grid=(NB, nc + 1),
        in_specs=[main_spec, main_spec, meta_spec, meta_spec,
                  pl.BlockSpec((None, 4, D), lambda b, c: (j, 0, 0)),
                  pl.BlockSpec((1, D), lambda b, c: (j, 0)),
                  pl.BlockSpec((None, 2, LRU_NBLK, LRU_BW, LRU_BW), lambda b, c: (j, 0, 0, 0, 0)),
                  pl.BlockSpec((None, 2, D), lambda b, c: (j, 0, 0)),
                  pl.BlockSpec((1, D), lambda b, c: (j, 0))],
        out_specs=[pl.BlockSpec((LRU_T, D), lambda b, c: (b * nc + jnp.maximum(c - 1, 0), 0)),
                   pl.BlockSpec((NM, D), lambda b, c: (b, 0)),
                   pl.BlockSpec((1, 1, D), lambda b, c: (b, 0, 0)),
                   pl.BlockSpec((1, 3, D), lambda b, c: (b, 0, 0))],
        out_shape=[_sds((ROWS_MAIN, D), bf16), _sds((NB * NM, D), bf16),
                   _sds((NB, 1, D), f32), _sds((NB, 3, D), f32)],
        scratch_shapes=[pltpu.VMEM((2, LRU_NBLK, LRU_BW, LRU_BW), bf16),
                        pltpu.VMEM((LRU_T + 8, D), f32), pltpu.VMEM((LRU_T, D), f32),
                        pltpu.VMEM((LRU_T, D), f32), pltpu.VMEM((1, D), f32)],
        compiler_params=_cp(("arbitrary", "arbitrary")),
    )(x, y, x, y, conv_w, conv_b, gate_w, gate_b, lam)


LRU_ST = 64


def _lru_dec_kernel(x_ref, y_ref, h0_ref, cbuf_ref, cw_ref, cb_ref, gw_ref, gb_ref, lam_ref,
                    o_ref, hn_ref, cn_ref, gw_s):
    gw_s[...] = gw_ref[...].astype(bf16)
    x = x_ref[...]
    b0 = cbuf_ref[:, 0, :]
    b1 = cbuf_ref[:, 1, :]
    b2 = cbuf_ref[:, 2, :]
    xc = _lru_conv(cb_ref, cw_ref, b0, b1, b2, x)
    cn_ref[:, 0, :] = b1
    cn_ref[:, 1, :] = b2
    cn_ref[:, 2, :] = x
    for n in range(LRU_NBLK):
        ls = slice(n * LRU_BW, (n + 1) * LRU_BW)
        a, ig, mult = _lru_gates(xc, gw_s, gb_ref, lam_ref, n)
        hn = a * h0_ref[:, ls] + xc[:, ls] * ig * mult
        hn_ref[:, ls] = hn
        o_ref[:, ls] = (hn * y_ref[:, ls]).astype(o_ref.dtype)


def _lru_sample(x, y, j, h0, cbuf, conv_w, conv_b, gate_w, gate_b, lam):
    row_spec = pl.BlockSpec((LRU_ST, D), lambda i: (ROW_S // LRU_ST + i, 0))
    return pl.pallas_call(
        _lru_dec_kernel,
        grid=(NS // LRU_ST,),
        in_specs=[row_spec, row_spec,
                  pl.BlockSpec((None, LRU_ST, D), lambda i: (j, i, 0)),
                  pl.BlockSpec((None, LRU_ST, 3, D), lambda i: (j, i, 0, 0)),
                  pl.BlockSpec((None, 4, D), lambda i: (j, 0, 0)),
                  pl.BlockSpec((1, D), lambda i: (j, 0)),
                  pl.BlockSpec((None, 2, LRU_NBLK, LRU_BW, LRU_BW), lambda i: (j, 0, 0, 0, 0)),
                  pl.BlockSpec((None, 2, D), lambda i: (j, 0, 0)),
                  pl.BlockSpec((1, D), lambda i: (j, 0))],
        out_specs=[pl.BlockSpec((LRU_ST, D), lambda i: (i, 0)),
                   pl.BlockSpec((LRU_ST, D), lambda i: (i, 0)),
                   pl.BlockSpec((LRU_ST, 3, D), lambda i: (i, 0, 0))],
        out_shape=[_sds((NS, D), bf16), _sds((NS, D), f32), _sds((NS, 3, D), f32)],
        scratch_shapes=[pltpu.VMEM((2, LRU_NBLK, LRU_BW, LRU_BW), bf16)],
        compiler_params=_cp(("arbitrary",)),
    )(x, y, h0, cbuf, conv_w, conv_b, gate_w, gate_b, lam)


def _lru_layer(h, norm_g, state_h, state_conv, j, w_in, conv_w, conv_b, gate_w, gate_b, lam, w_out):
    xn = _rmsnorm(h, norm_g, out_dtype=bf16, tm=768, row_tile0=0, n_row_tiles=ROWS // 768)
    y, x = _lru_in(xn, w_in, j)
    o_main, o_meta, hl_p, cv_p = _lru_prompt(x, y, j, conv_w, conv_b, gate_w, gate_b, lam)
    o_s, hl_s, cv_s = _lru_sample(x, y, j, state_h, state_conv, conv_w, conv_b, gate_w, gate_b, lam)
    o_aux = _aux_rows(o_s, o_meta)
    h = _residual_out(h, o_main, o_aux, w_out, (j,), D, _pro_id)
    return h, hl_p.reshape(NB, D), hl_s, cv_p, cv_s


def _rwkv_lora_kernel(x_ref, xp_ref, mu_ref, w1_ref, w2_ref, a1_ref, a2_ref, g1_ref, g2_ref, w0_ref, a0_ref,
                      d_ref, a_ref, g_ref):
    x = x_ref[...]
    dx = xp_ref[...] - x

    def mix(n):
        return (x + dx * mu_ref[n:n + 1, :]).astype(bf16)

    def mm(u, w_ref):
        return jnp.dot(u, w_ref[...].astype(bf16), preferred_element_type=f32)

    tw = jnp.tanh(mm(mix(3), w1_ref)).astype(bf16)
    w = -jax.nn.softplus(-(w0_ref[...] + mm(tw, w2_ref))) - 0.5
    d_ref[...] = jnp.exp(-jnp.exp(w))
    ta = mm(mix(4), a1_ref).astype(bf16)
    a_ref[...] = jax.nn.sigmoid(a0_ref[...] + mm(ta, a2_ref))
    tg = jax.nn.sigmoid(mm(mix(5), g1_ref)).astype(bf16)
    g_ref[...] = mm(tg, g2_ref)


def _rwkv_lora(xn, xprev, mu, w1, w2, a1, a2, g1, g2, w0, a0):
    tm = 256
    full = lambda a: pl.BlockSpec(a.shape, lambda i: (0,) * a.ndim)
    row = pl.BlockSpec((tm, D), lambda i: (i, 0))
    small = [mu, w1, w2, a1, a2, g1, g2, w0, a0]
    return pl.pallas_call(
        _rwkv_lora_kernel,
        grid=(ROWS // tm,),
        in_specs=[row, row] + [full(a) for a in small],
        out_specs=[row, row, row],
        out_shape=[_sds((ROWS, D), f32)] * 3,
        compiler_params=_cp(("arbitrary",)),
    )(xn, xprev, *small)


def _rwkv_scan_kernel(r_ref, k_ref, v_ref, d_ref, a_ref, s0_ref, kk_p, ka_p, rk_p, lw_p, lb_p,
                      z_ref, sf_ref, s_s, kk_s, d_s, ka_s, k2_s, r_s):
    c = pl.program_id(1)

    @pl.when(c == 0)
    def _init():
        s_s[...] = s0_ref[...]

    tc = r_ref.shape[0]

    def step(t, carry):
        r = r_ref[t]
        k = k_ref[t]
        v = v_ref[t]
        a = a_ref[t]
        kkr = k * kk_p[...]
        nrm = jnp.sqrt(jnp.sum(kkr * kkr, axis=0, keepdims=True))
        kk = kkr / jnp.maximum(nrm, 1e-12)
        k2 = k * (1.0 + (a - 1.0) * ka_p[...])
        kk_s[...] = kk
        d_s[...] = d_ref[t]
        ka_s[...] = kk * a
        k2_s[...] = k2
        r_s[...] = r

        def p1(j, acc):
            return acc + s_s[j] * kk_s[pl.ds(j, 1), :]

        sa = -lax.fori_loop(0, RW_N, p1, jnp.zeros((RW_N, 128), f32), unroll=8)

        def p2(j, y):
            sn = s_s[j] * d_s[pl.ds(j, 1), :] + sa * ka_s[pl.ds(j, 1), :] + v * k2_s[pl.ds(j, 1), :]
            s_s[j] = sn
            return y + sn * r_s[pl.ds(j, 1), :]

        y = lax.fori_loop(0, RW_N, p2, jnp.zeros((RW_N, 128), f32), unroll=8)
        mean = jnp.mean(y, axis=0, keepdims=True)
        yc = y - mean
        var = jnp.mean(yc * yc, axis=0, keepdims=True)
        yn = yc * lax.rsqrt(var + RW_LN_EPS) * lw_p[...] + lb_p[...]
        bonus = jnp.sum(r * k2 * rk_p[...], axis=0, keepdims=True) * v
        z_ref[t] = yn + bonus
        return carry

    lax.fori_loop(0, tc, step, 0)

    @pl.when(c == pl.num_programs(1) - 1)
    def _final():
        sf_ref[...] = s_s[...]


def _rwkv_scan(rT, kT, vT, dT, aT, s0T, params, tc):
    L, _, lanes = rT.shape
    nl = lanes // 128
    seq = pl.BlockSpec((tc, RW_N, 128), lambda l, c: (c, 0, l))
    st = pl.BlockSpec((RW_N, RW_N, 128), lambda l, c: (0, 0, l))
    par = pl.BlockSpec((RW_N, 128), lambda l, c: (0, l))
    return pl.pallas_call(
        _rwkv_scan_kernel,
        grid=(nl, L // tc),
        in_specs=[seq] * 5 + [st] + [par] * 5,
        out_specs=[seq, st],
        out_shape=[_sds((L, RW_N, lanes), f32), _sds((RW_N, RW_N, lanes), f32)],
        scratch_shapes=[pltpu.VMEM((RW_N, RW_N, 128), f32)] + [pltpu.VMEM((RW_N, 128), f32)] * 5,
        compiler_params=_cp(("arbitrary", "arbitrary")),
    )(rT, kT, vT, dT, aT, s0T, *params)


def _rwkv_layer(h, norm_g, state, shift, j, mu, w_rkv, w0, w1, w2, a0, a1, a2, g1, g2, k_k, k_a, r_k,
                ln_w, ln_b, w_o):
    xn = _rmsnorm(h, norm_g, out_dtype=f32, tm=768, row_tile0=0, n_row_tiles=ROWS // 768)
    xm = xn[:ROWS_MAIN].reshape(NB, LP, D)
    xa = xn[ROW_M:ROW_M + NB * NM].reshape(NB, NM, D)
    xp_main = jnp.concatenate([xa[:, NM - 1:], xm[:, :-1]], axis=1).reshape(ROWS_MAIN, D)
    xp_meta = jnp.concatenate([jnp.zeros((NB, 1, D), f32), xa[:, :-1]], axis=1).reshape(NB * NM, D)
    xprev = jnp.concatenate([xp_main, shift[j], xp_meta, jnp.zeros((AUX - NS - NB * NM, D), f32)], axis=0)
    shift_p = xm[:, -1]
    shift_s = xn[ROW_S:ROW_S + NS]

    mu_j = mu[j]
    tn = 512
    common = dict(K=D, tm=768, tn=tn, n_row_tiles=ROWS // 768, n_col_tiles=D // tn)
    o = (_sds((ROWS, D), f32), 0, 0)

    def proj(n):
        def pro(xs, kvs):
            x = xs[0]
            return (x + (xs[1] - x) * kvs[0][n:n + 1, :]).astype(bf16)
        return _mm(xs=[(xn, 0), (xprev, 0)], ws=[(w_rkv, (j, n), 0)], kvecs=[mu_j], outs=[o],
                   prologue=pro, epilogue=lambda accs, cvs, rms: (accs[0],), **common)[0]

    r, k, v = proj(0), proj(1), proj(2)

    def padc(w):
        return jnp.pad(w, ((0, 0), (0, LORA_PAD - w.shape[1])))

    def padr(w):
        return jnp.pad(w, ((0, LORA_PAD - w.shape[0]), (0, 0)))

    dec, a, g = _rwkv_lora(xn, xprev, mu_j, padc(w1[j]), padr(w2[j]), padc(a1[j]), padr(a2[j]), g1[j], g2[j],
                           w0[j].reshape(1, D), a0[j].reshape(1, D))

    def lanes_param(p, reps):
        return jnp.tile(p.reshape(RW_H, RW_N).T, (1, reps))

    def params(reps):
        return [lanes_param(k_k[j], reps), lanes_param(k_a[j], reps), lanes_param(r_k[j].reshape(D), reps),
                lanes_param(ln_w[j], reps), lanes_param(ln_b[j], reps)]

    def to_seq(x):
        xm_ = x[:ROWS_MAIN].reshape(NB, LP, RW_H, RW_N).transpose(1, 3, 0, 2).reshape(LP, RW_N, NB * RW_H)
        xa_ = x[ROW_M:ROW_M + NB * NM].reshape(NB, NM, RW_H, RW_N).transpose(1, 3, 0, 2).reshape(NM, RW_N, NB * RW_H)
        return jnp.concatenate([xa_, xm_], axis=0)

    zT, sT = _rwkv_scan(to_seq(r), to_seq(k), to_seq(v), to_seq(dec), to_seq(a),
                        jnp.zeros((RW_N, RW_N, NB * RW_H), f32), params(NB), RW_TC)
    s_p = sT.reshape(RW_N, RW_N, NB, RW_H).transpose(2, 3, 1, 0)
    z_meta = zT[:NM].reshape(NM, RW_N, NB, RW_H).transpose(2, 0, 3, 1).reshape(NB * NM, D)
    z_main = zT[NM:].reshape(LP, RW_N, NB, RW_H).transpose(2, 0, 3, 1).reshape(ROWS_MAIN, D)

    def to_step(x):
        return x[ROW_S:ROW_S + NS].reshape(NS, RW_H, RW_N).transpose(2, 0, 1).reshape(1, RW_N, NS * RW_H)

    s0T = state[j].transpose(3, 2, 0, 1).reshape(RW_N, RW_N, NS * RW_H)
    zsT, ssT = _rwkv_scan(to_step(r), to_step(k), to_step(v), to_step(dec), to_step(a), s0T, params(NS), 1)
    s_s = ssT.reshape(RW_N, RW_N, NS, RW_H).transpose(2, 3, 1, 0)
    z_s = zsT.reshape(RW_N, NS, RW_H).transpose(1, 2, 0).reshape(NS, D)

    z_aux = _aux_rows(z_s, z_meta)
    h = _residual_out(h, z_main, z_aux, w_o, (j,), D, _pro_mul,
                      extra_main=[(g, 0)], extra_aux=[(g, ROWS_MAIN // AUX)])
    return h, s_p, s_s, shift_p, shift_s


def kernel(x_prompt, x_sample, state_hgrn, state_lru_h, state_lru_conv, state_rwkv, state_rwkv_shift, meta_tokens, norm_mix, norm_ffn, norm_final, hgrn_w_in, hgrn_lb_logits, hgrn_norm, hgrn_w_out, lru_w_in, lru_conv_w, lru_conv_b, lru_gate_w, lru_gate_b, lru_lambda, lru_w_out, rwkv_mu, rwkv_w_rkv, rwkv_w0, rwkv_w1, rwkv_w2, rwkv_a0, rwkv_a1, rwkv_a2, rwkv_g1, rwkv_g2, rwkv_k_k, rwkv_k_a, rwkv_r_k, rwkv_ln_w, rwkv_ln_b, rwkv_w_o, ffn_w_in, ffn_w_out):
    depth = norm_mix.shape[0]
    h = jnp.concatenate([x_prompt.reshape(ROWS_MAIN, D), x_sample.reshape(NS, D),
                         jnp.tile(meta_tokens, (NB, 1)), jnp.zeros((AUX - NS - NB * NM, D), f32)], axis=0)
    hg_p, hg_s, lh_p, lh_s, lc_p, lc_s, rw_p, rw_s, rs_p, rs_s = [[] for _ in range(10)]
    for i in range(depth):
        m, j = i % 3, i // 3
        ng = norm_mix[i].reshape(1, D)
        if m == 0:
            h, sp, ss = _hgrn_layer(h, ng, state_hgrn, j, hgrn_w_in, hgrn_lb_logits, hgrn_norm, hgrn_w_out)
            hg_p.append(sp)
            hg_s.append(ss)
        elif m == 1:
            h, hp_, hs_, cp_, cs_ = _lru_layer(h, ng, state_lru_h, state_lru_conv, j, lru_w_in, lru_conv_w,
                                               lru_conv_b, lru_gate_w, lru_gate_b, lru_lambda, lru_w_out)
            lh_p.append(hp_)
            lh_s.append(hs_)
            lc_p.append(cp_)
            lc_s.append(cs_)
        else:
            h, sp, ss, shp, shs = _rwkv_layer(h, ng, state_rwkv, state_rwkv_shift, j, rwkv_mu, rwkv_w_rkv,
                                              rwkv_w0, rwkv_w1, rwkv_w2, rwkv_a0, rwkv_a1, rwkv_a2, rwkv_g1,
                                              rwkv_g2, rwkv_k_k, rwkv_k_a, rwkv_r_k, rwkv_ln_w, rwkv_ln_b,
                                              rwkv_w_o)
            rw_p.append(sp)
            rw_s.append(ss)
            rs_p.append(shp)
            rs_s.append(shs)
        h = _ffn(h, norm_ffn[i].reshape(1, D), ffn_w_in, ffn_w_out, i)
    nf = norm_final.reshape(1, D)
    y_main = _rmsnorm(h, nf, out_dtype=f32, tm=512, row_tile0=0, n_row_tiles=ROWS_MAIN // 512)
    y_aux = _rmsnorm(h, nf, out_dtype=f32, tm=AUX, row_tile0=ROWS_MAIN // AUX, n_row_tiles=1)
    y_prompt = y_main.reshape(NB, LP, D)
    y_sample = y_aux[:NS].reshape(NS, 1, D)
    return (y_prompt, y_sample, jnp.stack(hg_p), jnp.stack(hg_s), jnp.stack(lh_p), jnp.stack(lh_s),
            jnp.stack(lc_p), jnp.stack(lc_s), jnp.stack(rw_p), jnp.stack(rw_s), jnp.stack(rs_p), jnp.stack(rs_s))
```

```python
import functools

import jax
import jax.numpy as jnp
import numpy as np
from jax import lax
from jax.experimental import pallas as pl
from jax.experimental.pallas import tpu as pltpu

f32 = jnp.float32
bf16 = jnp.bfloat16

D = 2048
NB = 4
LP = 2048
NM = 16
NS = 128
ROWS_MAIN = NB * LP
ROW_S = ROWS_MAIN
ROW_M = ROW_S + NS
AUX = 256
ROWS = ROWS_MAIN + AUX
EPS = 1e-6
HG_H, HG_K = 16, 128
HG_C = 64
HG_UNROLL = 4
GATE_EXP_CLIP = 60.0
LRU_NBLK, LRU_BW = 8, 256
LRU_C = 8.0
LRU_T = 256
RW_H, RW_N = 32, 64
RW_LN_EPS = 64e-5
RW_TS = 16
PM_T = 256
MAIN_TM = 512
LORA_PAD = 128
D_FF = 5632
VMEM_LIMIT = 56 * 1024 * 1024


def _cp(sem):
    return pltpu.CompilerParams(dimension_semantics=sem, vmem_limit_bytes=VMEM_LIMIT)


def _norm_kernel(h_ref, g_ref, o_ref):
    x = h_ref[...]
    ms = jnp.mean(x * x, axis=-1, keepdims=True)
    o_ref[...] = (x * lax.rsqrt(ms + EPS) * g_ref[...]).astype(o_ref.dtype)


def _rmsnorm(h, g, *, out_dtype, tm, row_tile0, n_row_tiles):
    return pl.pallas_call(
        _norm_kernel,
        grid=(n_row_tiles,),
        in_specs=[pl.BlockSpec((tm, D), lambda i: (i + row_tile0, 0)),
                  pl.BlockSpec((1, D), lambda i: (0, 0))],
        out_specs=pl.BlockSpec((tm, D), lambda i: (i, 0)),
        out_shape=jax.ShapeDtypeStruct((n_row_tiles * tm, D), out_dtype),
        name="rmsnorm",
        compiler_params=_cp(("arbitrary",)),
    )(h, g)


def _mm_body(*refs, n_x, n_w, n_kv, n_cv, n_rm, n_out, prologue, epilogue):
    p = 0
    x_refs = refs[p:p + n_x]; p += n_x
    w_refs = refs[p:p + n_w]; p += n_w
    kv_refs = refs[p:p + n_kv]; p += n_kv
    cv_refs = refs[p:p + n_cv]; p += n_cv
    rm_refs = refs[p:p + n_rm]; p += n_rm
    out_refs = refs[p:p + n_out]; p += n_out
    w_scr = refs[p:p + n_w]

    @pl.when(pl.program_id(1) == 0)
    def _cast_weights():
        for w, s in zip(w_refs, w_scr):
            s[...] = w[...].astype(bf16)

    x = prologue([r[...] for r in x_refs], [r[...] for r in kv_refs])
    accs = [jnp.dot(x, s[...], preferred_element_type=f32) for s in w_scr]
    res = epilogue(accs, [r[...] for r in cv_refs], [r[...] for r in rm_refs])
    for o, r in zip(out_refs, res):
        o[...] = r.astype(o.dtype)


def _mm(*, name="proj", xs, ws, kvecs=(), cvecs=(), rmats=(), outs, K, tm, tn, n_row_tiles, n_col_tiles,
        prologue, epilogue, aliases=None):
    in_specs, args = [], []
    for a, r0 in xs:
        in_specs.append(pl.BlockSpec((tm, K), lambda j, i, r0=r0: (i + r0, 0)))
        args.append(a)
    for a, lead, c0 in ws:
        nl = len(lead)
        in_specs.append(pl.BlockSpec((None,) * nl + (K, tn), lambda j, i, lead=lead, c0=c0: lead + (0, j + c0)))
        args.append(a)
    for a in kvecs:
        in_specs.append(pl.BlockSpec(a.shape, lambda j, i: (0, 0)))
        args.append(a)
    for a, c0 in cvecs:
        in_specs.append(pl.BlockSpec((a.shape[0], tn), lambda j, i, c0=c0: (0, j + c0)))
        args.append(a)
    for a, r0, c0 in rmats:
        in_specs.append(pl.BlockSpec((tm, tn), lambda j, i, r0=r0, c0=c0: (i + r0, j + c0)))
        args.append(a)
    out_specs = [pl.BlockSpec((tm, tn), lambda j, i, r0=r0, c0=c0: (i + r0, j + c0)) for _, r0, c0 in outs]
    body = functools.partial(_mm_body, n_x=len(xs), n_w=len(ws), n_kv=len(kvecs), n_cv=len(cvecs),
                             n_rm=len(rmats), n_out=len(outs), prologue=prologue, epilogue=epilogue)
    res = pl.pallas_call(
        body,
        grid=(n_col_tiles, n_row_tiles),
        in_specs=in_specs,
        out_specs=out_specs,
        out_shape=[s for s, _, _ in outs],
        scratch_shapes=[pltpu.VMEM((K, tn), bf16) for _ in ws],
        input_output_aliases=aliases or {},
        name=name,
        compiler_params=_cp(("arbitrary", "arbitrary")),
    )(*args)
    return res


def _pro_id(xs, kvs):
    return xs[0]


def _pro_mul(xs, kvs):
    return (xs[0] * xs[1]).astype(bf16)


def _sds(shape, dtype):
    return jax.ShapeDtypeStruct(shape, dtype)


def _segments_out(h, segs, w, lead, K, prologue):
    tn = 512
    for xs, tm, nrt, h0 in segs:
        h = _mm(name="mixer_out", xs=xs, ws=[(w, lead, 0)], rmats=[(h, h0, 0)],
                outs=[(_sds((ROWS, D), f32), h0, 0)], K=K, tm=tm, tn=tn, n_row_tiles=nrt, n_col_tiles=D // tn,
                prologue=prologue, epilogue=lambda accs, cvs, rms: (rms[0] + accs[0],),
                aliases={len(xs) + 1: 0})[0]
    return h


def _residual_out(h, x_main, x_aux, w, lead, K):
    return _segments_out(h, [([(x_main, 0)], MAIN_TM, ROWS_MAIN // MAIN_TM, 0),
                             ([(x_aux, 0)], AUX, 1, ROWS_MAIN // AUX)], w, lead, K, _pro_id)


def _ffn(h, norm_g, w_in, w_out, layer):
    xn = _rmsnorm(h, norm_g, out_dtype=bf16, tm=768, row_tile0=0, n_row_tiles=ROWS // 768)
    tn = 512
    nct = D_FF // tn
    act = _mm(name="ffn_in", xs=[(xn, 0)], ws=[(w_in, (layer,), 0), (w_in, (layer,), nct)],
              outs=[(_sds((ROWS, D_FF), bf16), 0, 0)],
              K=D, tm=768, tn=tn, n_row_tiles=ROWS // 768, n_col_tiles=nct,
              prologue=_pro_id,
              epilogue=lambda accs, cvs, rms: (jax.nn.silu(accs[0]) * accs[1],))[0]
    h = _mm(name="ffn_out", xs=[(act, 0)], ws=[(w_out, (layer,), 0)], rmats=[(h, 0, 0)],
            outs=[(_sds((ROWS, D), f32), 0, 0)],
            K=D_FF, tm=384, tn=tn, n_row_tiles=ROWS // 384, n_col_tiles=D // tn,
            prologue=_pro_id, epilogue=lambda accs, cvs, rms: (rms[0] + accs[0],),
            aliases={2: 0})[0]
    return h


def _hgrn_lower_bound(logits, j):
    m = jnp.max(logits, axis=0, keepdims=True)
    e = jnp.exp(logits - m)
    p = e / jnp.sum(e, axis=0, keepdims=True)
    cs = p[0:1]
    for r in range(1, j + 1):
        cs = cs + p[r:r + 1]
    return jnp.clip(cs - p[0:1], 0.0, 1.0)


def _hgrn_in(xn, w_in, lb_logits, j):
    tn = 512
    nct = D // tn
    common = dict(K=D, tm=768, tn=tn, n_row_tiles=ROWS // 768, n_col_tiles=nct, prologue=_pro_id)
    o = (_sds((ROWS, D), f32), 0, 0)
    q = _mm(xs=[(xn, 0)], ws=[(w_in, (j,), 0)], outs=[o],
            epilogue=lambda accs, cvs, rms: (jax.nn.silu(accs[0]),), **common)[0]

    def f_epi(accs, cvs, rms):
        f = accs[0]
        lb = _hgrn_lower_bound(cvs[0], j)
        log_f = jax.nn.log_sigmoid(f) + jnp.log1p(lb * jnp.exp(jnp.minimum(-f, GATE_EXP_CLIP)))
        k = (1.0 - lb) * jax.nn.sigmoid(-f)
        return log_f, k

    log_f, k = _mm(xs=[(xn, 0)], ws=[(w_in, (j,), nct)], cvecs=[(lb_logits, 0)], outs=[o, o],
                   epilogue=f_epi, **common)
    v = _mm(xs=[(xn, 0)], ws=[(w_in, (j,), 2 * nct)], outs=[o],
            epilogue=lambda accs, cvs, rms: (accs[0],), **common)[0]
    g = _mm(xs=[(xn, 0)], ws=[(w_in, (j,), 3 * nct)], outs=[o],
            epilogue=lambda accs, cvs, rms: (jax.nn.silu(accs[0]),), **common)[0]
    return q, k, v, log_f, g


def _split3(x):
    hi = x.astype(bf16)
    r = x - hi.astype(f32)
    mid = r.astype(bf16)
    lo = (r - mid.astype(f32)).astype(bf16)
    return hi, mid, lo


def _gla_consts(T):
    t = np.arange(T)[:, None]
    w = np.arange(T)[None, :]
    mats = [w <= t, w > t]
    masks = []
    s = T // 2
    while s >= 1:
        piece = t // (2 * s)
        lower = (t % (2 * s)) >= s
        ref = piece * 2 * s + s
        mats.append(np.where(lower, (w > ref) & (w <= t), (w > t) & (w <= ref)))
        upiece = w // (2 * s)
        ulower = (w % (2 * s)) >= s
        masks.append((piece == upiece) & lower & ~ulower)
        s //= 2
    masks.append(t == w)
    m = np.concatenate(mats, 0)
    return (jnp.asarray(np.concatenate([m, m, m], 1), dtype=bf16), jnp.asarray(np.stack(masks), dtype=f32))


def _gla_block(q_ref, k_ref, v_ref, f_ref, g_ref, gn_ref, dm_ref, pm_ref, o_ref, st_ref, e_s, ql_s, kl_s, T):
    nlev = T.bit_length() - 1
    dn_t = (((1,), (1,)), ((), ()))
    dn_l = (((0,), (0,)), ((), ()))
    e_s[0:(nlev + 2) * T, :] = jnp.dot(dm_ref[...], jnp.concatenate(_split3(f_ref[...]), axis=0),
                                       preferred_element_type=f32)
    q = q_ref[...]
    k = k_ref[...]
    ql_s[0, 0:T, :] = (q * jnp.exp(e_s[0:T, :])).astype(bf16)
    kl_s[0, 0:T, :] = (k * jnp.exp(e_s[T:2 * T, :])).astype(bf16)
    for l in range(nlev):
        ex = jnp.exp(e_s[(l + 2) * T:(l + 3) * T, :])
        ql_s[l + 1, 0:T, :] = (q * ex).astype(bf16)
        kl_s[l + 1, 0:T, :] = (k * ex).astype(bf16)
    ql_s[nlev + 1, 0:T, :] = q.astype(bf16)
    kl_s[nlev + 1, 0:T, :] = k.astype(bf16)

    for h in range(HG_H):
        sl = slice(h * HG_K, (h + 1) * HG_K)
        st = st_ref[h]
        vb = v_ref[:, sl].astype(bf16)
        o = lax.dot_general(ql_s[0, 0:T, sl], st.astype(bf16), dn_t, preferred_element_type=f32)
        att = jnp.zeros((T, T), f32)
        for l in range(nlev + 1):
            a_l = lax.dot_general(ql_s[l + 1, 0:T, sl], kl_s[l + 1, 0:T, sl], dn_t,
                                  preferred_element_type=f32)
            att = att + a_l * pm_ref[l]
        o = o + jnp.dot(att.astype(bf16), vb, preferred_element_type=f32)
        upd = lax.dot_general(vb, kl_s[0, 0:T, sl], dn_l, preferred_element_type=f32)
        st_ref[h] = st * jnp.exp(e_s[T - 1:T, sl]) + upd
        ms = jnp.mean(o * o, axis=-1, keepdims=True)
        on = o * lax.rsqrt(ms + EPS) * gn_ref[:, sl]
        o_ref[:, sl] = (on * g_ref[:, sl]).astype(o_ref.dtype)


def _gla_kernel(qm, km, vm, fm, gm, qa, ka, va, fa, ga, gn_ref, dmm, pmm, dma, pma,
                om_ref, oa_ref, s_ref, st_ref, e_s, ql_s, kl_s):
    c = pl.program_id(1)

    @pl.when(c == 0)
    def _meta():
        st_ref[...] = jnp.zeros_like(st_ref)
        _gla_block(qa, ka, va, fa, ga, gn_ref, dma, pma, oa_ref, st_ref, e_s, ql_s, kl_s, NM)

    @pl.when(c > 0)
    def _main():
        _gla_block(qm, km, vm, fm, gm, gn_ref, dmm, pmm, om_ref, st_ref, e_s, ql_s, kl_s, HG_C)

    @pl.when(c == pl.num_programs(1) - 1)
    def _final():
        for h in range(HG_H):
            s_ref[0, h] = st_ref[h].T


def _hgrn_prompt(q, k, v, log_f, g, gn):
    nc = LP // HG_C
    main_spec = pl.BlockSpec((HG_C, D), lambda b, c: (b * nc + jnp.maximum(c - 1, 0), 0))
    meta_spec = pl.BlockSpec((NM, D), lambda b, c: (ROW_M // NM + b, 0))
    arrs = [q, k, v, log_f, g]
    consts = [*_gla_consts(HG_C), *_gla_consts(NM)]
    nslot = HG_C.bit_length() + 1
    return pl.pallas_call(
        _gla_kernel,
        grid=(NB, nc + 1),
        in_specs=[main_spec] * 5 + [meta_spec] * 5 + [pl.BlockSpec((1, D), lambda b, c: (0, 0))]
                 + [pl.BlockSpec(a.shape, lambda b, c, n=a.ndim: (0,) * n) for a in consts],
        out_specs=[pl.BlockSpec((HG_C, D), lambda b, c: (b * nc + jnp.maximum(c - 1, 0), 0)),
                   pl.BlockSpec((NM, D), lambda b, c: (b, 0)),
                   pl.BlockSpec((1, HG_H, HG_K, HG_K), lambda b, c: (b, 0, 0, 0))],
        out_shape=[_sds((ROWS_MAIN, D), bf16), _sds((NB * NM, D), bf16), _sds((NB, HG_H, HG_K, HG_K), f32)],
        scratch_shapes=[pltpu.VMEM((HG_H, HG_K, HG_K), f32),
                        pltpu.VMEM((nslot * HG_C, D), f32),
                        pltpu.VMEM((nslot, HG_C, D), bf16),
                        pltpu.VMEM((nslot, HG_C, D), bf16)],
        name="hgrn_prompt",
        compiler_params=_cp(("arbitrary", "arbitrary")),
    )(*arrs, *arrs, gn, *consts)


HG_BT = 8


def _hgrn_dec_kernel(s_ref, qT, kT, fT, v_ref, g_ref, gn_ref, so_ref, o_ref):
    def per_b(bb, carry):
        qm = qT[bb]
        km = kT[bb]
        gm = jnp.exp(fT[bb])
        vall = v_ref[pl.ds(bb, 1), :]
        outs = []
        for h in range(HG_H):
            hs = slice(h * HG_K, (h + 1) * HG_K)
            sn = gm[:, h:h + 1] * s_ref[bb, h] + km[:, h:h + 1] * vall[:, hs]
            so_ref[bb, h] = sn
            o = jnp.sum(qm[:, h:h + 1] * sn, axis=0, keepdims=True)
            ms = jnp.mean(o * o, axis=-1, keepdims=True)
            outs.append(o * lax.rsqrt(ms + EPS))
        on = jnp.concatenate(outs, axis=1) * gn_ref[...]
        o_ref[pl.ds(bb, 1), :] = on * g_ref[pl.ds(bb, 1), :]
        return carry

    lax.fori_loop(0, HG_BT, per_b, 0)


def _hgrn_sample(state, j, q, k, v, log_f, g, gn):
    def colform(x):
        return x[ROW_S:ROW_S + NS].reshape(NS, HG_H, HG_K).transpose(0, 2, 1)

    col_spec = pl.BlockSpec((HG_BT, HG_K, HG_H), lambda i: (i, 0, 0))
    row_spec = pl.BlockSpec((HG_BT, D), lambda i: (ROW_S // HG_BT + i, 0))
    return pl.pallas_call(
        _hgrn_dec_kernel,
        grid=(NS // HG_BT,),
        in_specs=[pl.BlockSpec((None, HG_BT, HG_H, HG_K, HG_K), lambda i: (j, i, 0, 0, 0)),
                  col_spec, col_spec, col_spec, row_spec, row_spec,
                  pl.BlockSpec((1, D), lambda i: (0, 0))],
        out_specs=[pl.BlockSpec((HG_BT, HG_H, HG_K, HG_K), lambda i: (i, 0, 0, 0)),
                   pl.BlockSpec((HG_BT, D), lambda i: (i, 0))],
        out_shape=[_sds((NS, HG_H, HG_K, HG_K), f32), _sds((NS, D), f32)],
        name="hgrn_decode",
        compiler_params=_cp(("arbitrary",)),
    )(state, colform(q), colform(k), colform(log_f), v, g, gn)


def _aux_rows(sample_rows, meta_rows):
    pad = jnp.zeros((AUX - NS - NB * NM, sample_rows.shape[1]), sample_rows.dtype)
    return jnp.concatenate([sample_rows, meta_rows.astype(sample_rows.dtype), pad], axis=0)


def _hgrn_layer(h, norm_g, state, j, w_in, lb_logits, gnorm, w_out):
    xn = _rmsnorm(h, norm_g, out_dtype=bf16, tm=768, row_tile0=0, n_row_tiles=ROWS // 768)
    q, k, v, log_f, g = _hgrn_in(xn, w_in, lb_logits, j)
    gn = gnorm[j].reshape(1, D)
    o_main, o_meta, s_p = _hgrn_prompt(q, k, v, log_f, g, gn)
    s_s, o_s = _hgrn_sample(state, j, q, k, v, log_f, g, gn)
    o_aux = _aux_rows(o_s.astype(bf16), o_meta)
    h = _residual_out(h, o_main, o_aux, w_out, (j,), D)
    return h, s_p, s_s


def _lru_in(xn, w_in, j):
    tn = 512
    nct = D // tn
    common = dict(K=D, tm=768, tn=tn, n_row_tiles=ROWS // 768, n_col_tiles=nct, prologue=_pro_id)
    o = (_sds((ROWS, D), f32), 0, 0)
    y = _mm(xs=[(xn, 0)], ws=[(w_in, (j,), 0)], outs=[o],
            epilogue=lambda accs, cvs, rms: (jax.nn.gelu(accs[0], approximate=True),), **common)[0]
    x = _mm(xs=[(xn, 0)], ws=[(w_in, (j,), nct)], outs=[o],
            epilogue=lambda accs, cvs, rms: (accs[0],), **common)[0]
    return y, x


def _lru_gates(xc, gw_ref, gb_ref, lam_ref, n):
    ls = slice(n * LRU_BW, (n + 1) * LRU_BW)
    xb = xc[:, ls].astype(bf16)
    r = jax.nn.sigmoid(jnp.dot(xb, gw_ref[0, n], preferred_element_type=f32) + gb_ref[0:1, ls])
    ig = jax.nn.sigmoid(jnp.dot(xb, gw_ref[1, n], preferred_element_type=f32) + gb_ref[1:2, ls])
    log_a = -LRU_C * r * jax.nn.softplus(-lam_ref[:, ls])
    a = jnp.exp(log_a)
    t = jnp.tanh(log_a)
    mult = jnp.sqrt(jnp.maximum(-2.0 * t / (1.0 - t), 0.0))
    return a, ig, mult


def _lru_conv(cb_ref, cw_ref, x0, x1, x2, x3):
    xc = cb_ref[...] + x0 * cw_ref[0:1, :]
    xc = xc + x1 * cw_ref[1:2, :]
    xc = xc + x2 * cw_ref[2:3, :]
    return xc + x3 * cw_ref[3:4, :]


def _lru_block(x_ref, y_ref, o_ref, cw_ref, cb_ref, gw_s, gb_ref, lam_ref, xbuf, a_s, b_s, hcar, T, first):
    xbuf[8:8 + T, :] = x_ref[...]
    xc = _lru_conv(cb_ref, cw_ref, xbuf[5:5 + T, :], xbuf[6:6 + T, :], xbuf[7:7 + T, :], xbuf[8:8 + T, :])
    xbuf[0:8, :] = xbuf[T:T + 8, :]
    row = lax.broadcasted_iota(jnp.int32, (T, 1), 0)
    for n in range(LRU_NBLK):
        ls = slice(n * LRU_BW, (n + 1) * LRU_BW)
        a, ig, mult = _lru_gates(xc, gw_s, gb_ref, lam_ref, n)
        if first:
            mult = jnp.where(row == 0, 1.0, mult)
        a_s[0:T, ls] = a
        b_s[0:T, ls] = xc[:, ls] * ig * mult

    def step(t, hprev):
        hnew = a_s[pl.ds(t, 1), :] * hprev + b_s[pl.ds(t, 1), :]
        b_s[pl.ds(t, 1), :] = hnew
        return hnew

    hcar[...] = lax.fori_loop(0, T, step, hcar[...], unroll=8)
    o_ref[...] = (b_s[0:T, :] * y_ref[...]).astype(o_ref.dtype)


def _lru_kernel(xm, ym, xa, ya, cw_ref, cb_ref, gw_ref, gb_ref, lam_ref,
                om_ref, oa_ref, hl_ref, cv_ref, gw_s, xbuf, a_s, b_s, hcar):
    c = pl.program_id(1)

    @pl.when((pl.program_id(0) == 0) & (c == 0))
    def _cast():
        gw_s[...] = gw_ref[...].astype(bf16)

    @pl.when(c == 0)
    def _meta():
        xbuf[0:8, :] = jnp.zeros((8, D), f32)
        hcar[...] = jnp.zeros_like(hcar)
        _lru_block(xa, ya, oa_ref, cw_ref, cb_ref, gw_s, gb_ref, lam_ref, xbuf, a_s, b_s, hcar, NM, True)

    @pl.when(c > 0)
    def _main():
        _lru_block(xm, ym, om_ref, cw_ref, cb_ref, gw_s, gb_ref, lam_ref, xbuf, a_s, b_s, hcar, LRU_T, False)

    @pl.when(c == pl.num_programs(1) - 1)
    def _final():
        hl_ref[0] = hcar[...]
        cv_ref[0] = xbuf[5:8, :]


def _lru_prompt(x, y, j, conv_w, conv_b, gate_w, gate_b, lam):
    nc = LP // LRU_T
    main_spec = pl.BlockSpec((LRU_T, D), lambda b, c: (b * nc + jnp.maximum(c - 1, 0), 0))
    meta_spec = pl.BlockSpec((NM, D), lambda b, c: (ROW_M // NM + b, 0))
    return pl.pallas_call(
        _lru_kernel,
        grid=(NB, nc + 1),
        in_specs=[main_spec, main_spec, meta_spec, meta_spec,
                  pl.BlockSpec((None, 4, D), lambda b, c: (j, 0, 0)),
                  pl.BlockSpec((1, D), lambda b, c: (j, 0)),
                  pl.BlockSpec((None, 2, LRU_NBLK, LRU_BW, LRU_BW), lambda b, c: (j, 0, 0, 0, 0)),
                  pl.BlockSpec((None, 2, D), lambda b, c: (j, 0, 0)),
                  pl.BlockSpec((1, D), lambda b, c: (j, 0))],
        out_specs=[pl.BlockSpec((LRU_T, D), lambda b, c: (b * nc + jnp.maximum(c - 1, 0), 0)),
                   pl.BlockSpec((NM, D), lambda b, c: (b, 0)),
                   pl.BlockSpec((1, 1, D), lambda b, c: (b, 0, 0)),
                   pl.BlockSpec((1, 3, D), lambda b, c: (b, 0, 0))],
        out_shape=[_sds((ROWS_MAIN, D), bf16), _sds((NB * NM, D), bf16),
                   _sds((NB, 1, D), f32), _sds((NB, 3, D), f32)],
        scratch_shapes=[pltpu.VMEM((2, LRU_NBLK, LRU_BW, LRU_BW), bf16),
                        pltpu.VMEM((LRU_T + 8, D), f32), pltpu.VMEM((LRU_T, D), f32),
                        pltpu.VMEM((LRU_T, D), f32), pltpu.VMEM((1, D), f32)],
        name="lru_prompt",
        compiler_params=_cp(("arbitrary", "arbitrary")),
    )(x, y, x, y, conv_w, conv_b, gate_w, gate_b, lam)


LRU_ST = 64


def _lru_dec_kernel(x_ref, y_ref, h0_ref, cbuf_ref, cw_ref, cb_ref, gw_ref, gb_ref, lam_ref,
                    o_ref, hn_ref, cn_ref, gw_s):
    gw_s[...] = gw_ref[...].astype(bf16)
    x = x_ref[...]
    b0 = cbuf_ref[:, 0, :]
    b1 = cbuf_ref[:, 1, :]
    b2 = cbuf_ref[:, 2, :]
    xc = _lru_conv(cb_ref, cw_ref, b0, b1, b2, x)
    cn_ref[:, 0, :] = b1
    cn_ref[:, 1, :] = b2
    cn_ref[:, 2, :] = x
    for n in range(LRU_NBLK):
        ls = slice(n * LRU_BW, (n + 1) * LRU_BW)
        a, ig, mult = _lru_gates(xc, gw_s, gb_ref, lam_ref, n)
        hn = a * h0_ref[:, ls] + xc[:, ls] * ig * mult
        hn_ref[:, ls] = hn
        o_ref[:, ls] = (hn * y_ref[:, ls]).astype(o_ref.dtype)


def _lru_sample(x, y, j, h0, cbuf, conv_w, conv_b, gate_w, gate_b, lam):
    row_spec = pl.BlockSpec((LRU_ST, D), lambda i: (ROW_S // LRU_ST + i, 0))
    return pl.pallas_call(
        _lru_dec_kernel,
        grid=(NS // LRU_ST,),
        in_specs=[row_spec, row_spec,
                  pl.BlockSpec((None, LRU_ST, D), lambda i: (j, i, 0)),
                  pl.BlockSpec((None, LRU_ST, 3, D), lambda i: (j, i, 0, 0)),
                  pl.BlockSpec((None, 4, D), lambda i: (j, 0, 0)),
                  pl.BlockSpec((1, D), lambda i: (j, 0)),
                  pl.BlockSpec((None, 2, LRU_NBLK, LRU_BW, LRU_BW), lambda i: (j, 0, 0, 0, 0)),
                  pl.BlockSpec((None, 2, D), lambda i: (j, 0, 0)),
                  pl.BlockSpec((1, D), lambda i: (j, 0))],
        out_specs=[pl.BlockSpec((LRU_ST, D), lambda i: (i, 0)),
                   pl.BlockSpec((LRU_ST, D), lambda i: (i, 0)),
                   pl.BlockSpec((LRU_ST, 3, D), lambda i: (i, 0, 0))],
        out_shape=[_sds((NS, D), bf16), _sds((NS, D), f32), _sds((NS, 3, D), f32)],
        scratch_shapes=[pltpu.VMEM((2, LRU_NBLK, LRU_BW, LRU_BW), bf16)],
        name="lru_decode",
        compiler_params=_cp(("arbitrary",)),
    )(x, y, h0, cbuf, conv_w, conv_b, gate_w, gate_b, lam)


def _lru_layer(h, norm_g, state_h, state_conv, j, w_in, conv_w, conv_b, gate_w, gate_b, lam, w_out):
    xn = _rmsnorm(h, norm_g, out_dtype=bf16, tm=768, row_tile0=0, n_row_tiles=ROWS // 768)
    y, x = _lru_in(xn, w_in, j)
    o_main, o_meta, hl_p, cv_p = _lru_prompt(x, y, j, conv_w, conv_b, gate_w, gate_b, lam)
    o_s, hl_s, cv_s = _lru_sample(x, y, j, state_h, state_conv, conv_w, conv_b, gate_w, gate_b, lam)
    o_aux = _aux_rows(o_s, o_meta)
    h = _residual_out(h, o_main, o_aux, w_out, (j,), D)
    return h, hl_p.reshape(NB, D), hl_s, cv_p, cv_s


def _rwkv_premix_kernel(h_ref, g_ref, mu_ref, sh_ref, *refs):
    x_refs = refs[:6]
    sp_ref, ss_ref, xbuf, meta_last = refs[6:]
    i = pl.program_id(0)
    x = h_ref[...]
    ms = jnp.mean(x * x, axis=-1, keepdims=True)
    xn = x * lax.rsqrt(ms + EPS) * g_ref[...]
    xbuf[8:8 + PM_T, :] = xn
    row = lax.broadcasted_iota(jnp.int32, (PM_T, 1), 0)

    @pl.when(i == 0)
    def _aux():
        xbuf[7:8, :] = jnp.zeros((1, D), f32)
        ss_ref[...] = xn[0:NS]
        for b in range(NB):
            meta_last[b:b + 1, :] = xn[NS + b * NM + NM - 1:NS + b * NM + NM]

    @pl.when(i > 0)
    def _main():
        m = i - 1
        b = m // (LP // PM_T)

        @pl.when(m % (LP // PM_T) == 0)
        def _start():
            xbuf[7:8, :] = meta_last[pl.ds(b, 1), :]

        @pl.when(m % (LP // PM_T) == LP // PM_T - 1)
        def _end():
            sp_ref[pl.ds(b, 1), :] = xn[PM_T - 1:PM_T]

    shifted = xbuf[7:7 + PM_T, :]
    is_meta = (row >= NS) & (row < NS + NB * NM) & ((row - NS) % NM != 0)
    sh_pad = jnp.concatenate([sh_ref[...], jnp.zeros((PM_T - NS, D), f32)], axis=0)
    prev_aux = jnp.where(row < NS, sh_pad, jnp.where(is_meta, shifted, 0.0))
    prev = jnp.where(i == 0, prev_aux, shifted)
    dx = prev - xn
    for n in range(6):
        x_refs[n][...] = (xn + dx * mu_ref[n:n + 1, :]).astype(bf16)
    xbuf[7:8, :] = xn[PM_T - 1:PM_T]


def _rwkv_premix(h, norm_g, mu_j, shift_j):
    nt = ROWS // PM_T
    rows = lambda i: (jnp.where(i == 0, nt - 1, i - 1), 0)
    return pl.pallas_call(
        _rwkv_premix_kernel,
        grid=(nt,),
        in_specs=[pl.BlockSpec((PM_T, D), rows),
                  pl.BlockSpec((1, D), lambda i: (0, 0)),
                  pl.BlockSpec((6, D), lambda i: (0, 0)),
                  pl.BlockSpec((NS, D), lambda i: (0, 0))],
        out_specs=[pl.BlockSpec((PM_T, D), rows)] * 6 + [pl.BlockSpec((NB, D), lambda i: (0, 0)),
                                                         pl.BlockSpec((NS, D), lambda i: (0, 0))],
        out_shape=[_sds((ROWS, D), bf16)] * 6 + [_sds((NB, D), f32), _sds((NS, D), f32)],
        scratch_shapes=[pltpu.VMEM((PM_T + 8, D), f32), pltpu.VMEM((8, D), f32)],
        name="rwkv_premix",
        compiler_params=_cp(("arbitrary",)),
    )(h, norm_g, mu_j, shift_j)


def _rwkv_lora_kernel(xw_ref, xa_ref, xg_ref, w1_ref, w2_ref, a1_ref, a2_ref, g1_ref, g2_ref, w0_ref, a0_ref,
                      d_ref, a_ref, g_ref, w1_s, w2_s, a1_s, a2_s, g1_s, g2_s):
    @pl.when(pl.program_id(0) == 0)
    def _cast():
        for src, dst in ((w1_ref, w1_s), (w2_ref, w2_s), (a1_ref, a1_s), (a2_ref, a2_s), (g1_ref, g1_s),
                         (g2_ref, g2_s)):
            dst[...] = src[...].astype(bf16)

    def mm(u, w_s):
        return jnp.dot(u, w_s[...], preferred_element_type=f32)

    tw = jnp.tanh(mm(xw_ref[...], w1_s)).astype(bf16)
    w = -jax.nn.softplus(-(w0_ref[...] + mm(tw, w2_s))) - 0.5
    d_ref[...] = jnp.exp(-jnp.exp(w))
    ta = mm(xa_ref[...], a1_s).astype(bf16)
    a_ref[...] = jax.nn.sigmoid(a0_ref[...] + mm(ta, a2_s))
    tg = jax.nn.sigmoid(mm(xg_ref[...], g1_s)).astype(bf16)
    g_ref[...] = mm(tg, g2_s)


def _rwkv_lora(xw, xa, xg, w1, w2, a1, a2, g1, g2, w0, a0):
    tm = 384
    full = lambda a: pl.BlockSpec(a.shape, lambda i: (0,) * a.ndim)
    row = pl.BlockSpec((tm, D), lambda i: (i, 0))
    ws = [w1, w2, a1, a2, g1, g2]
    return pl.pallas_call(
        _rwkv_lora_kernel,
        grid=(ROWS // tm,),
        in_specs=[row, row, row] + [full(a) for a in ws + [w0, a0]],
        out_specs=[row, row, row],
        out_shape=[_sds((ROWS, D), f32)] * 3,
        scratch_shapes=[pltpu.VMEM(a.shape, bf16) for a in ws],
        name="rwkv_lora",
        compiler_params=_cp(("arbitrary",)),
    )(xw, xa, xg, *ws, w0, a0)


def _rwkv_step(t, r_s, k_s, v_s, d_s, a_s, z_s, s_s, tmp, kk_p, ka_p, rk_p, lw_p, lb_p):
    kk_t, d_t, ka_t, k2_t, r_t = tmp
    r = r_s[t]
    k = k_s[t]
    v = v_s[t]
    a = a_s[t]
    kkr = k * kk_p[...]
    nrm = jnp.sqrt(jnp.sum(kkr * kkr, axis=0, keepdims=True))
    kk = kkr / jnp.maximum(nrm, 1e-12)
    k2 = k * (1.0 + (a - 1.0) * ka_p[...])
    kk_t[...] = kk
    d_t[...] = d_s[t]
    ka_t[...] = kk * a
    k2_t[...] = k2
    r_t[...] = r

    def p1(j, acc):
        return acc + s_s[j] * kk_t[pl.ds(j, 1), :]

    sa = -lax.fori_loop(0, RW_N, p1, jnp.zeros((RW_N, 128), f32), unroll=8)

    def p2(j, y):
        sn = s_s[j] * d_t[pl.ds(j, 1), :] + sa * ka_t[pl.ds(j, 1), :] + v * k2_t[pl.ds(j, 1), :]
        s_s[j] = sn
        return y + sn * r_t[pl.ds(j, 1), :]

    y = lax.fori_loop(0, RW_N, p2, jnp.zeros((RW_N, 128), f32), unroll=8)
    mean = jnp.mean(y, axis=0, keepdims=True)
    yc = y - mean
    var = jnp.mean(yc * yc, axis=0, keepdims=True)
    yn = yc * lax.rsqrt(var + RW_LN_EPS) * lw_p[...] + lb_p[...]
    bonus = jnp.sum(r * k2 * rk_p[...], axis=0, keepdims=True) * v
    z_s[t] = yn + bonus


def _rwkv_scan_kernel(r_ref, k_ref, v_ref, d_ref, a_ref, s0_ref, kk_p, ka_p, rk_p, lw_p, lb_p,
                      z_ref, sf_ref, s_s, *tmp):
    c = pl.program_id(1)

    @pl.when(c == 0)
    def _init():
        s_s[...] = s0_ref[...]

    def step(t, carry):
        _rwkv_step(t, r_ref, k_ref, v_ref, d_ref, a_ref, z_ref, s_s, tmp, kk_p, ka_p, rk_p, lw_p, lb_p)
        return carry

    lax.fori_loop(0, r_ref.shape[0], step, 0)

    @pl.when(c == pl.num_programs(1) - 1)
    def _final():
        sf_ref[...] = s_s[...]


RW_Q = D // 128


def _half_transpose(x):
    xt = x.T
    return jnp.concatenate([xt[0:RW_N], xt[RW_N:2 * RW_N]], axis=1)


def _rwkv_prompt_kernel(*refs):
    n_in = 5 * NB
    in_refs = refs[:n_in]
    params = refs[n_in:n_in + 5]
    zmain = refs[n_in + 5:n_in + 5 + NB]
    zmeta = refs[n_in + 5 + NB:n_in + 5 + 2 * NB]
    sf_ref = refs[n_in + 5 + 2 * NB]
    scr = refs[n_in + 6 + 2 * NB:]
    seq, z_s, s_s, tmp = scr[:5], scr[5], scr[6], scr[7:]
    c = pl.program_id(0)

    @pl.when(c == 0)
    def _init():
        s_s[...] = jnp.zeros_like(s_s)

    for a in range(5):
        for t in range(RW_TS):
            rows = [in_refs[a * NB + b][t * RW_Q:(t + 1) * RW_Q, :] for b in range(NB)]
            seq[a][t] = _half_transpose(jnp.concatenate(rows, axis=0))

    def step(t, carry):
        _rwkv_step(t, *seq, z_s, s_s, tmp, *params)
        return carry

    lax.fori_loop(0, RW_TS, step, 0)

    def write(z_refs):
        for t in range(RW_TS):
            zt = _half_transpose(z_s[t])
            for b in range(NB):
                z_refs[b][t * RW_Q:(t + 1) * RW_Q, :] = zt[b * RW_Q:(b + 1) * RW_Q]

    @pl.when(c == 0)
    def _write_meta():
        write(zmeta)

    @pl.when(c > 0)
    def _write_main():
        write(zmain)

    @pl.when(c == pl.num_programs(0) - 1)
    def _final():
        sf_ref[...] = s_s[...]


def _rwkv_prompt(r, k, v, dec, a, params):
    nc = LP // RW_TS
    blk = RW_TS * RW_Q
    arrs = [x.reshape(ROWS * RW_Q, 128) for x in (r, k, v, dec, a)]
    in_specs, args = [], []
    for x in arrs:
        for b in range(NB):
            in_specs.append(pl.BlockSpec(
                (blk, 128), lambda c, b=b: (jnp.where(c == 0, ROW_M // RW_TS + b, b * nc + c - 1), 0)))
            args.append(x)
    par = pl.BlockSpec((RW_N, 128), lambda c: (0, 0))
    outs = pl.pallas_call(
        _rwkv_prompt_kernel,
        grid=(nc + 1,),
        in_specs=in_specs + [par] * 5,
        out_specs=[pl.BlockSpec((blk, 128), lambda c: (jnp.maximum(c - 1, 0), 0))] * NB
                  + [pl.BlockSpec((blk, 128), lambda c: (0, 0))] * NB
                  + [pl.BlockSpec((RW_N, RW_N, 128), lambda c: (0, 0, 0))],
        out_shape=[_sds((LP * RW_Q, 128), f32)] * NB + [_sds((NM * RW_Q, 128), f32)] * NB
                  + [_sds((RW_N, RW_N, 128), f32)],
        scratch_shapes=[pltpu.VMEM((RW_TS, RW_N, 128), f32)] * 6 + [pltpu.VMEM((RW_N, RW_N, 128), f32)]
                       + [pltpu.VMEM((RW_N, 128), f32)] * 5,
        name="rwkv_prompt",
        compiler_params=_cp(("arbitrary",)),
    )(*args, *params)
    z_main = [o.reshape(LP, D) for o in outs[:NB]]
    z_meta = [o.reshape(NM, D) for o in outs[NB:2 * NB]]
    return z_main, z_meta, outs[2 * NB]


def _rwkv_scan(rT, kT, vT, dT, aT, s0T, params, tc):
    L, _, lanes = rT.shape
    nl = lanes // 128
    seq = pl.BlockSpec((tc, RW_N, 128), lambda l, c: (c, 0, l))
    st = pl.BlockSpec((RW_N, RW_N, 128), lambda l, c: (0, 0, l))
    par = pl.BlockSpec((RW_N, 128), lambda l, c: (0, l))
    return pl.pallas_call(
        _rwkv_scan_kernel,
        grid=(nl, L // tc),
        in_specs=[seq] * 5 + [st] + [par] * 5,
        out_specs=[seq, st],
        out_shape=[_sds((L, RW_N, lanes), f32), _sds((RW_N, RW_N, lanes), f32)],
        scratch_shapes=[pltpu.VMEM((RW_N, RW_N, 128), f32)] + [pltpu.VMEM((RW_N, 128), f32)] * 5,
        name="rwkv_scan",
        compiler_params=_cp(("arbitrary", "arbitrary")),
    )(rT, kT, vT, dT, aT, s0T, *params)


def _rwkv_layer(h, norm_g, state, shift, j, mu, w_rkv, w0, w1, w2, a0, a1, a2, g1, g2, k_k, k_a, r_k,
                ln_w, ln_b, w_o):
    *xmix, shift_p, shift_s = _rwkv_premix(h, norm_g, mu[j], shift[j])
    tn = 512
    common = dict(K=D, tm=768, tn=tn, n_row_tiles=ROWS // 768, n_col_tiles=D // tn, prologue=_pro_id,
                  epilogue=lambda accs, cvs, rms: (accs[0],))
    o = (_sds((ROWS, D), f32), 0, 0)
    r, k, v = [_mm(name="rwkv_rkv", xs=[(xmix[n], 0)], ws=[(w_rkv, (j, n), 0)], outs=[o], **common)[0]
               for n in range(3)]

    def padc(w):
        return jnp.pad(w, ((0, 0), (0, LORA_PAD - w.shape[1])))

    def padr(w):
        return jnp.pad(w, ((0, LORA_PAD - w.shape[0]), (0, 0)))

    dec, a, g = _rwkv_lora(xmix[3], xmix[4], xmix[5], padc(w1[j]), padr(w2[j]), padc(a1[j]), padr(a2[j]),
                           g1[j], g2[j], w0[j].reshape(1, D), a0[j].reshape(1, D))

    def lanes_param(p, reps):
        return jnp.tile(p.reshape(RW_H, RW_N).T, (1, reps))

    def prompt_param(p):
        pt = p.reshape(RW_Q, 2, RW_N).transpose(2, 1, 0)
        return jnp.broadcast_to(pt[:, :, None, :], (RW_N, 2, NB, RW_Q)).reshape(RW_N, 128)

    def params(reps):
        return [lanes_param(k_k[j], reps), lanes_param(k_a[j], reps), lanes_param(r_k[j].reshape(D), reps),
                lanes_param(ln_w[j], reps), lanes_param(ln_b[j], reps)]

    z_main, z_meta, sT = _rwkv_prompt(r, k, v, dec, a, [prompt_param(p) for p in (
        k_k[j], k_a[j], r_k[j].reshape(D), ln_w[j], ln_b[j])])
    s_p = sT.reshape(RW_N, RW_N, 2, NB, RW_Q).transpose(3, 4, 2, 1, 0).reshape(NB, RW_H, RW_N, RW_N)

    def to_step(x):
        return x[ROW_S:ROW_S + NS].reshape(NS, RW_H, RW_N).transpose(2, 0, 1).reshape(1, RW_N, NS * RW_H)

    s0T = state[j].transpose(3, 2, 0, 1).reshape(RW_N, RW_N, NS * RW_H)
    zsT, ssT = _rwkv_scan(to_step(r), to_step(k), to_step(v), to_step(dec), to_step(a), s0T, params(NS), 1)
    s_s = ssT.reshape(RW_N, RW_N, NS, RW_H).transpose(2, 3, 1, 0)
    z_s = zsT.reshape(RW_N, NS, RW_H).transpose(1, 2, 0).reshape(NS, D)

    z_aux = _aux_rows(z_s, jnp.concatenate(z_meta, axis=0))
    tpb = LP // MAIN_TM
    segs = [([(z_main[b], 0), (g, b * tpb)], MAIN_TM, tpb, b * tpb) for b in range(NB)]
    segs.append(([(z_aux, 0), (g, ROWS_MAIN // AUX)], AUX, 1, ROWS_MAIN // AUX))
    h = _segments_out(h, segs, w_o, (j,), D, _pro_mul)
    return h, s_p, s_s, shift_p, shift_s


def kernel(x_prompt, x_sample, state_hgrn, state_lru_h, state_lru_conv, state_rwkv, state_rwkv_shift, meta_tokens, norm_mix, norm_ffn, norm_final, hgrn_w_in, hgrn_lb_logits, hgrn_norm, hgrn_w_out, lru_w_in, lru_conv_w, lru_conv_b, lru_gate_w, lru_gate_b, lru_lambda, lru_w_out, rwkv_mu, rwkv_w_rkv, rwkv_w0, rwkv_w1, rwkv_w2, rwkv_a0, rwkv_a1, rwkv_a2, rwkv_g1, rwkv_g2, rwkv_k_k, rwkv_k_a, rwkv_r_k, rwkv_ln_w, rwkv_ln_b, rwkv_w_o, ffn_w_in, ffn_w_out):
    depth = norm_mix.shape[0]
    h = jnp.concatenate([x_prompt.reshape(ROWS_MAIN, D), x_sample.reshape(NS, D),
                         jnp.tile(meta_tokens, (NB, 1)), jnp.zeros((AUX - NS - NB * NM, D), f32)], axis=0)
    hg_p, hg_s, lh_p, lh_s, lc_p, lc_s, rw_p, rw_s, rs_p, rs_s = [[] for _ in range(10)]
    for i in range(depth):
        m, j = i % 3, i // 3
        ng = norm_mix[i].reshape(1, D)
        if m == 0:
            h, sp, ss = _hgrn_layer(h, ng, state_hgrn, j, hgrn_w_in, hgrn_lb_logits, hgrn_norm, hgrn_w_out)
            hg_p.append(sp)
            hg_s.append(ss)
        elif m == 1:
            h, hp_, hs_, cp_, cs_ = _lru_layer(h, ng, state_lru_h, state_lru_conv, j, lru_w_in, lru_conv_w,
                                               lru_conv_b, lru_gate_w, lru_gate_b, lru_lambda, lru_w_out)
            lh_p.append(hp_)
            lh_s.append(hs_)
            lc_p.append(cp_)
            lc_s.append(cs_)
        else:
            h, sp, ss, shp, shs = _rwkv_layer(h, ng, state_rwkv, state_rwkv_shift, j, rwkv_mu, rwkv_w_rkv,
                                              rwkv_w0, rwkv_w1, rwkv_w2, rwkv_a0, rwkv_a1, rwkv_a2, rwkv_g1,
                                              rwkv_g2, rwkv_k_k, rwkv_k_a, rwkv_r_k, rwkv_ln_w, rwkv_ln_b,
                                              rwkv_w_o)
            rw_p.append(sp)
            rw_s.append(ss)
            rs_p.append(shp)
            rs_s.append(shs)
        h = _ffn(h, norm_ffn[i].reshape(1, D), ffn_w_in, ffn_w_out, i)
    nf = norm_final.reshape(1, D)
    y_main = _rmsnorm(h, nf, out_dtype=f32, tm=512, row_tile0=0, n_row_tiles=ROWS_MAIN // 512)
    y_aux = _rmsnorm(h, nf, out_dtype=f32, tm=AUX, row_tile0=ROWS_MAIN // AUX, n_row_tiles=1)
    y_prompt = y_main.reshape(NB, LP, D)
    y_sample = y_aux[:NS].reshape(NS, 1, D)
    return (y_prompt, y_sample, jnp.stack(hg_p), jnp.stack(hg_s), jnp.stack(lh_p), jnp.stack(lh_s),
            jnp.stack(lc_p), jnp.stack(lc_s), jnp.stack(rw_p), jnp.stack(rw_s), jnp.stack(rs_p), jnp.stack(rs_s))
```

```python
import functools

import jax
import jax.numpy as jnp
import numpy as np
from jax import lax
from jax.experimental import pallas as pl
from jax.experimental.pallas import tpu as pltpu

f32 = jnp.float32
bf16 = jnp.bfloat16

D = 2048
NB = 4
LP = 2048
NM = 16
NS = 128
ROWS_MAIN = NB * LP
ROW_S = ROWS_MAIN
ROW_M = ROW_S + NS
AUX = 256
ROWS = ROWS_MAIN + AUX
EPS = 1e-6
HG_H, HG_K = 16, 128
HG_C = 64
HG_UNROLL = 4
GATE_EXP_CLIP = 60.0
LRU_NBLK, LRU_BW = 8, 256
LRU_C = 8.0
LRU_T = 256
RW_H, RW_N = 32, 64
RW_LN_EPS = 64e-5
RW_TS = 16
PM_T = 256
MAIN_TM = 512
LORA_PAD = 128
D_FF = 5632
VMEM_LIMIT = 56 * 1024 * 1024


def _cp(sem):
    return pltpu.CompilerParams(dimension_semantics=sem, vmem_limit_bytes=VMEM_LIMIT)


def _norm_kernel(h_ref, g_ref, o_ref):
    x = h_ref[...]
    ms = jnp.mean(x * x, axis=-1, keepdims=True)
    o_ref[...] = (x * lax.rsqrt(ms + EPS) * g_ref[...]).astype(o_ref.dtype)


def _rmsnorm(h, g, *, out_dtype, tm, row_tile0, n_row_tiles):
    return pl.pallas_call(
        _norm_kernel,
        grid=(n_row_tiles,),
        in_specs=[pl.BlockSpec((tm, D), lambda i: (i + row_tile0, 0)),
                  pl.BlockSpec((1, D), lambda i: (0, 0))],
        out_specs=pl.BlockSpec((tm, D), lambda i: (i, 0)),
        out_shape=jax.ShapeDtypeStruct((n_row_tiles * tm, D), out_dtype),
        name="rmsnorm",
        compiler_params=_cp(("arbitrary",)),
    )(h, g)


def _assemble_kernel(xm_ref, xa_ref, g_ref, h_ref, xn_ref):
    def emit(x):
        h_ref[...] = x
        ms = jnp.mean(x * x, axis=-1, keepdims=True)
        xn_ref[...] = (x * lax.rsqrt(ms + EPS) * g_ref[...]).astype(xn_ref.dtype)

    @pl.when(pl.program_id(0) < ROWS_MAIN // AUX)
    def _main():
        emit(xm_ref[...])

    @pl.when(pl.program_id(0) == ROWS_MAIN // AUX)
    def _aux():
        emit(xa_ref[...])


def _assemble(x_main, x_aux, g):
    nm = ROWS_MAIN // AUX
    return pl.pallas_call(
        _assemble_kernel,
        grid=(nm + 1,),
        in_specs=[pl.BlockSpec((AUX, D), lambda i: (jnp.minimum(i, nm - 1), 0)),
                  pl.BlockSpec((AUX, D), lambda i: (0, 0)),
                  pl.BlockSpec((1, D), lambda i: (0, 0))],
        out_specs=[pl.BlockSpec((AUX, D), lambda i: (i, 0)), pl.BlockSpec((AUX, D), lambda i: (i, 0))],
        out_shape=[_sds((ROWS, D), f32), _sds((ROWS, D), bf16)],
        name="assemble",
        compiler_params=_cp(("arbitrary",)),
    )(x_main, x_aux, g)


def _mm_body(*refs, n_x, n_w, n_kv, n_cv, n_rm, n_out, prologue, epilogue):
    p = 0
    x_refs = refs[p:p + n_x]; p += n_x
    w_refs = refs[p:p + n_w]; p += n_w
    kv_refs = refs[p:p + n_kv]; p += n_kv
    cv_refs = refs[p:p + n_cv]; p += n_cv
    rm_refs = refs[p:p + n_rm]; p += n_rm
    out_refs = refs[p:p + n_out]; p += n_out
    w_scr = refs[p:p + n_w]

    @pl.when(pl.program_id(1) == 0)
    def _cast_weights():
        for w, s in zip(w_refs, w_scr):
            s[...] = w[...].astype(bf16)

    def flat(v):
        return v.reshape(v.shape[0], v.shape[1] * v.shape[2]) if v.ndim == 3 else v

    x = prologue([flat(r[...]) for r in x_refs], [r[...] for r in kv_refs])
    accs = [jnp.dot(x, s[...], preferred_element_type=f32) for s in w_scr]
    res = epilogue(accs, [r[...] for r in cv_refs], [r[...] for r in rm_refs])
    for o, r in zip(out_refs, res):
        o[...] = r.astype(o.dtype).reshape(o.shape)


def _mm(*, name="proj", xs, ws, kvecs=(), cvecs=(), rmats=(), outs, K, tm, tn, n_row_tiles, n_col_tiles,
        prologue, epilogue, aliases=None):
    in_specs, args = [], []
    for a, r0 in xs:
        if a.ndim == 3:
            in_specs.append(pl.BlockSpec((tm, K // 128, 128), lambda j, i, r0=r0: (i + r0, 0, 0)))
        else:
            in_specs.append(pl.BlockSpec((tm, K), lambda j, i, r0=r0: (i + r0, 0)))
        args.append(a)
    for a, lead, c0 in ws:
        nl = len(lead)
        in_specs.append(pl.BlockSpec((None,) * nl + (K, tn), lambda j, i, lead=lead, c0=c0: lead + (0, j + c0)))
        args.append(a)
    for a in kvecs:
        in_specs.append(pl.BlockSpec(a.shape, lambda j, i: (0, 0)))
        args.append(a)
    for a, c0 in cvecs:
        in_specs.append(pl.BlockSpec((a.shape[0], tn), lambda j, i, c0=c0: (0, j + c0)))
        args.append(a)
    for a, r0, c0 in rmats:
        in_specs.append(pl.BlockSpec((tm, tn), lambda j, i, r0=r0, c0=c0: (i + r0, j + c0)))
        args.append(a)
    out_specs = [pl.BlockSpec((tm, tn // 128, 128), lambda j, i, r0=r0, c0=c0: (i + r0, j + c0, 0))
                 if len(s.shape) == 3 else
                 pl.BlockSpec((tm, tn), lambda j, i, r0=r0, c0=c0: (i + r0, j + c0)) for s, r0, c0 in outs]
    body = functools.partial(_mm_body, n_x=len(xs), n_w=len(ws), n_kv=len(kvecs), n_cv=len(cvecs),
                             n_rm=len(rmats), n_out=len(outs), prologue=prologue, epilogue=epilogue)
    res = pl.pallas_call(
        body,
        grid=(n_col_tiles, n_row_tiles),
        in_specs=in_specs,
        out_specs=out_specs,
        out_shape=[s for s, _, _ in outs],
        scratch_shapes=[pltpu.VMEM((K, tn), bf16) for _ in ws],
        input_output_aliases=aliases or {},
        name=name,
        compiler_params=_cp(("arbitrary", "arbitrary")),
    )(*args)
    return res


def _pro_id(xs, kvs):
    return xs[0]


def _pro_mul(xs, kvs):
    return (xs[0] * xs[1]).astype(bf16)


def _sds(shape, dtype):
    return jax.ShapeDtypeStruct(shape, dtype)


def _segments_out(h, segs, w, lead, K, prologue):
    tn = 512
    for xs, tm, nrt, h0 in segs:
        h = _mm(name="mixer_out", xs=xs, ws=[(w, lead, 0)], rmats=[(h, h0, 0)],
                outs=[(_sds((ROWS, D), f32), h0, 0)], K=K, tm=tm, tn=tn, n_row_tiles=nrt, n_col_tiles=D // tn,
                prologue=prologue, epilogue=lambda accs, cvs, rms: (rms[0] + accs[0],),
                aliases={len(xs) + 1: 0})[0]
    return h


def _residual_out(h, x_main, x_aux, w, lead, K):
    return _segments_out(h, [([(x_main, 0)], MAIN_TM, ROWS_MAIN // MAIN_TM, 0),
                             ([(x_aux, 0)], AUX, 1, ROWS_MAIN // AUX)], w, lead, K, _pro_id)


def _ffn(h, norm_g, w_in, w_out, layer):
    xn = _rmsnorm(h, norm_g, out_dtype=bf16, tm=768, row_tile0=0, n_row_tiles=ROWS // 768)
    tn = 512
    nct = D_FF // tn
    act = _mm(name="ffn_in", xs=[(xn, 0)], ws=[(w_in, (layer,), 0), (w_in, (layer,), nct)],
              outs=[(_sds((ROWS, D_FF), bf16), 0, 0)],
              K=D, tm=768, tn=tn, n_row_tiles=ROWS // 768, n_col_tiles=nct,
              prologue=_pro_id,
              epilogue=lambda accs, cvs, rms: (jax.nn.silu(accs[0]) * accs[1],))[0]
    h = _mm(name="ffn_out", xs=[(act, 0)], ws=[(w_out, (layer,), 0)], rmats=[(h, 0, 0)],
            outs=[(_sds((ROWS, D), f32), 0, 0)],
            K=D_FF, tm=384, tn=tn, n_row_tiles=ROWS // 384, n_col_tiles=D // tn,
            prologue=_pro_id, epilogue=lambda accs, cvs, rms: (rms[0] + accs[0],),
            aliases={2: 0})[0]
    return h


def _hgrn_lower_bound(logits, j):
    m = jnp.max(logits, axis=0, keepdims=True)
    e = jnp.exp(logits - m)
    p = e / jnp.sum(e, axis=0, keepdims=True)
    cs = p[0:1]
    for r in range(1, j + 1):
        cs = cs + p[r:r + 1]
    return jnp.clip(cs - p[0:1], 0.0, 1.0)


def _hgrn_in(xn, w_in, lb_logits, j):
    tn = 512
    nct = D // tn
    common = dict(K=D, tm=768, tn=tn, n_row_tiles=ROWS // 768, n_col_tiles=nct, prologue=_pro_id)
    o = (_sds((ROWS, D), f32), 0, 0)
    q = _mm(xs=[(xn, 0)], ws=[(w_in, (j,), 0)], outs=[o],
            epilogue=lambda accs, cvs, rms: (jax.nn.silu(accs[0]),), **common)[0]

    def f_epi(accs, cvs, rms):
        f = accs[0]
        lb = _hgrn_lower_bound(cvs[0], j)
        log_f = jax.nn.log_sigmoid(f) + jnp.log1p(lb * jnp.exp(jnp.minimum(-f, GATE_EXP_CLIP)))
        k = (1.0 - lb) * jax.nn.sigmoid(-f)
        return log_f, k

    log_f, k = _mm(xs=[(xn, 0)], ws=[(w_in, (j,), nct)], cvecs=[(lb_logits, 0)], outs=[o, o],
                   epilogue=f_epi, **common)
    v = _mm(xs=[(xn, 0)], ws=[(w_in, (j,), 2 * nct)], outs=[o],
            epilogue=lambda accs, cvs, rms: (accs[0],), **common)[0]
    g = _mm(xs=[(xn, 0)], ws=[(w_in, (j,), 3 * nct)], outs=[o],
            epilogue=lambda accs, cvs, rms: (jax.nn.silu(accs[0]),), **common)[0]
    return q, k, v, log_f, g


def _split3(x):
    hi = x.astype(bf16)
    r = x - hi.astype(f32)
    mid = r.astype(bf16)
    lo = (r - mid.astype(f32)).astype(bf16)
    return hi, mid, lo


def _gla_consts(T):
    t = np.arange(T)[:, None]
    w = np.arange(T)[None, :]
    mats = [w <= t, w > t]
    masks = []
    s = T // 2
    while s >= 1:
        piece = t // (2 * s)
        lower = (t % (2 * s)) >= s
        ref = piece * 2 * s + s
        mats.append(np.where(lower, (w > ref) & (w <= t), (w > t) & (w <= ref)))
        upiece = w // (2 * s)
        ulower = (w % (2 * s)) >= s
        masks.append((piece == upiece) & lower & ~ulower)
        s //= 2
    masks.append(t == w)
    m = np.concatenate(mats, 0)
    return (jnp.asarray(np.concatenate([m, m, m], 1), dtype=bf16), jnp.asarray(np.stack(masks), dtype=f32))


def _gla_block(q_ref, k_ref, v_ref, f_ref, g_ref, gn_ref, dm_ref, pm_ref, o_ref, st_ref, e_s, ql_s, kl_s, T):
    nlev = T.bit_length() - 1
    dn_t = (((1,), (1,)), ((), ()))
    dn_l = (((0,), (0,)), ((), ()))
    e_s[0:(nlev + 2) * T, :] = jnp.dot(dm_ref[...], jnp.concatenate(_split3(f_ref[...]), axis=0),
                                       preferred_element_type=f32)
    q = q_ref[...]
    k = k_ref[...]
    ql_s[0, 0:T, :] = (q * jnp.exp(e_s[0:T, :])).astype(bf16)
    kl_s[0, 0:T, :] = (k * jnp.exp(e_s[T:2 * T, :])).astype(bf16)
    for l in range(nlev):
        ex = jnp.exp(e_s[(l + 2) * T:(l + 3) * T, :])
        ql_s[l + 1, 0:T, :] = (q * ex).astype(bf16)
        kl_s[l + 1, 0:T, :] = (k * ex).astype(bf16)
    ql_s[nlev + 1, 0:T, :] = q.astype(bf16)
    kl_s[nlev + 1, 0:T, :] = k.astype(bf16)

    for h in range(HG_H):
        sl = slice(h * HG_K, (h + 1) * HG_K)
        st = st_ref[h]
        vb = v_ref[:, sl].astype(bf16)
        o = lax.dot_general(ql_s[0, 0:T, sl], st.astype(bf16), dn_t, preferred_element_type=f32)
        att = jnp.zeros((T, T), f32)
        for l in range(nlev + 1):
            a_l = lax.dot_general(ql_s[l + 1, 0:T, sl], kl_s[l + 1, 0:T, sl], dn_t,
                                  preferred_element_type=f32)
            att = att + a_l * pm_ref[l]
        o = o + jnp.dot(att.astype(bf16), vb, preferred_element_type=f32)
        upd = lax.dot_general(vb, kl_s[0, 0:T, sl], dn_l, preferred_element_type=f32)
        st_ref[h] = st * jnp.exp(e_s[T - 1:T, sl]) + upd
        ms = jnp.mean(o * o, axis=-1, keepdims=True)
        on = o * lax.rsqrt(ms + EPS) * gn_ref[:, sl]
        o_ref[:, sl] = (on * g_ref[:, sl]).astype(o_ref.dtype)


def _gla_kernel(qm, km, vm, fm, gm, qa, ka, va, fa, ga, gn_ref, dmm, pmm, dma, pma,
                om_ref, oa_ref, s_ref, st_ref, e_s, ql_s, kl_s):
    c = pl.program_id(1)

    @pl.when(c == 0)
    def _meta():
        st_ref[...] = jnp.zeros_like(st_ref)
        _gla_block(qa, ka, va, fa, ga, gn_ref, dma, pma, oa_ref, st_ref, e_s, ql_s, kl_s, NM)

    @pl.when(c > 0)
    def _main():
        _gla_block(qm, km, vm, fm, gm, gn_ref, dmm, pmm, om_ref, st_ref, e_s, ql_s, kl_s, HG_C)

    @pl.when(c == pl.num_programs(1) - 1)
    def _final():
        for h in range(HG_H):
            s_ref[0, h] = st_ref[h].T


def _hgrn_prompt(q, k, v, log_f, g, gn):
    nc = LP // HG_C
    main_spec = pl.BlockSpec((HG_C, D), lambda b, c: (b * nc + jnp.maximum(c - 1, 0), 0))
    meta_spec = pl.BlockSpec((NM, D), lambda b, c: (ROW_M // NM + b, 0))
    arrs = [q, k, v, log_f, g]
    consts = [*_gla_consts(HG_C), *_gla_consts(NM)]
    nslot = HG_C.bit_length() + 1
    return pl.pallas_call(
        _gla_kernel,
        grid=(NB, nc + 1),
        in_specs=[main_spec] * 5 + [meta_spec] * 5 + [pl.BlockSpec((1, D), lambda b, c: (0, 0))]
                 + [pl.BlockSpec(a.shape, lambda b, c, n=a.ndim: (0,) * n) for a in consts],
        out_specs=[pl.BlockSpec((HG_C, D), lambda b, c: (b * nc + jnp.maximum(c - 1, 0), 0)),
                   pl.BlockSpec((NM, D), lambda b, c: (b, 0)),
                   pl.BlockSpec((1, HG_H, HG_K, HG_K), lambda b, c: (b, 0, 0, 0))],
        out_shape=[_sds((ROWS_MAIN, D), bf16), _sds((NB * NM, D), bf16), _sds((NB, HG_H, HG_K, HG_K), f32)],
        scratch_shapes=[pltpu.VMEM((HG_H, HG_K, HG_K), f32),
                        pltpu.VMEM((nslot * HG_C, D), f32),
                        pltpu.VMEM((nslot, HG_C, D), bf16),
                        pltpu.VMEM((nslot, HG_C, D), bf16)],
        name="hgrn_prompt",
        compiler_params=_cp(("arbitrary", "arbitrary")),
    )(*arrs, *arrs, gn, *consts)


HG_BT = 8


def _hgrn_dec_kernel(s_ref, qT, kT, fT, v_ref, g_ref, gn_ref, e3_ref, so_ref, o_ref, qb_s, kb_s, gb_s):
    def spread(m, dst):
        dst[...] = jnp.dot(jnp.concatenate(_split3(m), axis=1), e3_ref[...], preferred_element_type=f32)

    def per_b(bb, carry):
        spread(qT[bb], qb_s)
        spread(kT[bb], kb_s)
        spread(jnp.exp(fT[bb]), gb_s)
        vall = v_ref[pl.ds(bb, 1), :]
        outs = []
        for h in range(HG_H):
            hs = slice(h * HG_K, (h + 1) * HG_K)
            sn = gb_s[:, hs] * s_ref[bb, h] + kb_s[:, hs] * vall[:, hs]
            so_ref[bb, h] = sn
            o = jnp.sum(qb_s[:, hs] * sn, axis=0, keepdims=True)
            ms = jnp.mean(o * o, axis=-1, keepdims=True)
            outs.append(o * lax.rsqrt(ms + EPS))
        on = jnp.concatenate(outs, axis=1) * gn_ref[...]
        o_ref[pl.ds(bb, 1), :] = on * g_ref[pl.ds(bb, 1), :]
        return carry

    lax.fori_loop(0, HG_BT, per_b, 0)


def _hgrn_sample(state, j, q, k, v, log_f, g, gn):
    def colform(x):
        return x[ROW_S:ROW_S + NS].reshape(NS, HG_H, HG_K).transpose(0, 2, 1)

    head_of_lane = np.arange(D) // HG_K
    sel = (np.arange(HG_H)[:, None] == head_of_lane[None, :])
    e3 = jnp.asarray(np.concatenate([sel, sel, sel], axis=0), dtype=bf16)
    col_spec = pl.BlockSpec((HG_BT, HG_K, HG_H), lambda i: (i, 0, 0))
    row_spec = pl.BlockSpec((HG_BT, D), lambda i: (ROW_S // HG_BT + i, 0))
    return pl.pallas_call(
        _hgrn_dec_kernel,
        grid=(NS // HG_BT,),
        in_specs=[pl.BlockSpec((None, HG_BT, HG_H, HG_K, HG_K), lambda i: (j, i, 0, 0, 0)),
                  col_spec, col_spec, col_spec, row_spec, row_spec,
                  pl.BlockSpec((1, D), lambda i: (0, 0)),
                  pl.BlockSpec((3 * HG_H, D), lambda i: (0, 0))],
        out_specs=[pl.BlockSpec((HG_BT, HG_H, HG_K, HG_K), lambda i: (i, 0, 0, 0)),
                   pl.BlockSpec((HG_BT, D), lambda i: (i, 0))],
        out_shape=[_sds((NS, HG_H, HG_K, HG_K), f32), _sds((NS, D), f32)],
        scratch_shapes=[pltpu.VMEM((HG_K, D), f32)] * 3,
        name="hgrn_decode",
        compiler_params=_cp(("arbitrary",)),
    )(state, colform(q), colform(k), colform(log_f), v, g, gn, e3)


def _aux_rows(sample_rows, meta_rows):
    pad = jnp.zeros((AUX - NS - NB * NM, sample_rows.shape[1]), sample_rows.dtype)
    return jnp.concatenate([sample_rows, meta_rows.astype(sample_rows.dtype), pad], axis=0)


def _hgrn_layer(h, norm_g, state, j, w_in, lb_logits, gnorm, w_out, xn=None):
    if xn is None:
        xn = _rmsnorm(h, norm_g, out_dtype=bf16, tm=768, row_tile0=0, n_row_tiles=ROWS // 768)
    q, k, v, log_f, g = _hgrn_in(xn, w_in, lb_logits, j)
    gn = gnorm[j].reshape(1, D)
    o_main, o_meta, s_p = _hgrn_prompt(q, k, v, log_f, g, gn)
    s_s, o_s = _hgrn_sample(state, j, q, k, v, log_f, g, gn)
    o_aux = _aux_rows(o_s.astype(bf16), o_meta)
    h = _residual_out(h, o_main, o_aux, w_out, (j,), D)
    return h, s_p, s_s


def _lru_in(xn, w_in, j):
    tn = 512
    nct = D // tn
    common = dict(K=D, tm=768, tn=tn, n_row_tiles=ROWS // 768, n_col_tiles=nct, prologue=_pro_id)
    o = (_sds((ROWS, D), f32), 0, 0)
    y = _mm(xs=[(xn, 0)], ws=[(w_in, (j,), 0)], outs=[o],
            epilogue=lambda accs, cvs, rms: (jax.nn.gelu(accs[0], approximate=True),), **common)[0]
    x = _mm(xs=[(xn, 0)], ws=[(w_in, (j,), nct)], outs=[o],
            epilogue=lambda accs, cvs, rms: (accs[0],), **common)[0]
    return y, x


def _lru_gates(xc, gw_ref, gb_ref, lam_ref, n):
    ls = slice(n * LRU_BW, (n + 1) * LRU_BW)
    xb = xc[:, ls].astype(bf16)
    r = jax.nn.sigmoid(jnp.dot(xb, gw_ref[0, n], preferred_element_type=f32) + gb_ref[0:1, ls])
    ig = jax.nn.sigmoid(jnp.dot(xb, gw_ref[1, n], preferred_element_type=f32) + gb_ref[1:2, ls])
    log_a = -LRU_C * r * jax.nn.softplus(-lam_ref[:, ls])
    a = jnp.exp(log_a)
    t = jnp.tanh(log_a)
    mult = jnp.sqrt(jnp.maximum(-2.0 * t / (1.0 - t), 0.0))
    return a, ig, mult


def _lru_conv(cb_ref, cw_ref, x0, x1, x2, x3):
    xc = cb_ref[...] + x0 * cw_ref[0:1, :]
    xc = xc + x1 * cw_ref[1:2, :]
    xc = xc + x2 * cw_ref[2:3, :]
    return xc + x3 * cw_ref[3:4, :]


def _lru_block(x_ref, y_ref, o_ref, cw_ref, cb_ref, gw_s, gb_ref, lam_ref, xbuf, a_s, b_s, hcar, T, first):
    xbuf[8:8 + T, :] = x_ref[...]
    xc = _lru_conv(cb_ref, cw_ref, xbuf[5:5 + T, :], xbuf[6:6 + T, :], xbuf[7:7 + T, :], xbuf[8:8 + T, :])
    xbuf[0:8, :] = xbuf[T:T + 8, :]
    row = lax.broadcasted_iota(jnp.int32, (T, 1), 0)
    for n in range(LRU_NBLK):
        ls = slice(n * LRU_BW, (n + 1) * LRU_BW)
        a, ig, mult = _lru_gates(xc, gw_s, gb_ref, lam_ref, n)
        if first:
            mult = jnp.where(row == 0, 1.0, mult)
        a_s[0:T, ls] = a
        b_s[0:T, ls] = xc[:, ls] * ig * mult

    def step(t, hprev):
        hnew = a_s[pl.ds(t, 1), :] * hprev + b_s[pl.ds(t, 1), :]
        b_s[pl.ds(t, 1), :] = hnew
        return hnew

    hcar[...] = lax.fori_loop(0, T, step, hcar[...], unroll=8)
    o_ref[...] = (b_s[0:T, :] * y_ref[...]).astype(o_ref.dtype)


def _lru_kernel(xm, ym, xa, ya, cw_ref, cb_ref, gw_ref, gb_ref, lam_ref,
                om_ref, oa_ref, hl_ref, cv_ref, gw_s, xbuf, a_s, b_s, hcar):
    c = pl.program_id(1)

    @pl.when((pl.program_id(0) == 0) & (c == 0))
    def _cast():
        gw_s[...] = gw_ref[...].astype(bf16)

    @pl.when(c == 0)
    def _meta():
        xbuf[0:8, :] = jnp.zeros((8, D), f32)
        hcar[...] = jnp.zeros_like(hcar)
        _lru_block(xa, ya, oa_ref, cw_ref, cb_ref, gw_s, gb_ref, lam_ref, xbuf, a_s, b_s, hcar, NM, True)

    @pl.when(c > 0)
    def _main():
        _lru_block(xm, ym, om_ref, cw_ref, cb_ref, gw_s, gb_ref, lam_ref, xbuf, a_s, b_s, hcar, LRU_T, False)

    @pl.when(c == pl.num_programs(1) - 1)
    def _final():
        hl_ref[0] = hcar[...]
        cv_ref[0] = xbuf[5:8, :]


def _lru_prompt(x, y, j, conv_w, conv_b, gate_w, gate_b, lam):
    nc = LP // LRU_T
    main_spec = pl.BlockSpec((LRU_T, D), lambda b, c: (b * nc + jnp.maximum(c - 1, 0), 0))
    meta_spec = pl.BlockSpec((NM, D), lambda b, c: (ROW_M // NM + b, 0))
    return pl.pallas_call(
        _lru_kernel,
        grid=(NB, nc + 1),
        in_specs=[main_spec, main_spec, meta_spec, meta_spec,
                  pl.BlockSpec((None, 4, D), lambda b, c: (j, 0, 0)),
                  pl.BlockSpec((1, D), lambda b, c: (j, 0)),
                  pl.BlockSpec((None, 2, LRU_NBLK, LRU_BW, LRU_BW), lambda b, c: (j, 0, 0, 0, 0)),
                  pl.BlockSpec((None, 2, D), lambda b, c: (j, 0, 0)),
                  pl.BlockSpec((1, D), lambda b, c: (j, 0))],
        out_specs=[pl.BlockSpec((LRU_T, D), lambda b, c: (b * nc + jnp.maximum(c - 1, 0), 0)),
                   pl.BlockSpec((NM, D), lambda b, c: (b, 0)),
                   pl.BlockSpec((1, 1, D), lambda b, c: (b, 0, 0)),
                   pl.BlockSpec((1, 3, D), lambda b, c: (b, 0, 0))],
        out_shape=[_sds((ROWS_MAIN, D), bf16), _sds((NB * NM, D), bf16),
                   _sds((NB, 1, D), f32), _sds((NB, 3, D), f32)],
        scratch_shapes=[pltpu.VMEM((2, LRU_NBLK, LRU_BW, LRU_BW), bf16),
                        pltpu.VMEM((LRU_T + 8, D), f32), pltpu.VMEM((LRU_T, D), f32),
                        pltpu.VMEM((LRU_T, D), f32), pltpu.VMEM((1, D), f32)],
        name="lru_prompt",
        compiler_params=_cp(("arbitrary", "arbitrary")),
    )(x, y, x, y, conv_w, conv_b, gate_w, gate_b, lam)


LRU_ST = 64


def _lru_dec_kernel(x_ref, y_ref, h0_ref, cbuf_ref, cw_ref, cb_ref, gw_ref, gb_ref, lam_ref,
                    o_ref, hn_ref, cn_ref, gw_s):
    gw_s[...] = gw_ref[...].astype(bf16)
    x = x_ref[...]
    b0 = cbuf_ref[:, 0, :]
    b1 = cbuf_ref[:, 1, :]
    b2 = cbuf_ref[:, 2, :]
    xc = _lru_conv(cb_ref, cw_ref, b0, b1, b2, x)
    cn_ref[:, 0, :] = b1
    cn_ref[:, 1, :] = b2
    cn_ref[:, 2, :] = x
    for n in range(LRU_NBLK):
        ls = slice(n * LRU_BW, (n + 1) * LRU_BW)
        a, ig, mult = _lru_gates(xc, gw_s, gb_ref, lam_ref, n)
        hn = a * h0_ref[:, ls] + xc[:, ls] * ig * mult
        hn_ref[:, ls] = hn
        o_ref[:, ls] = (hn * y_ref[:, ls]).astype(o_ref.dtype)


def _lru_sample(x, y, j, h0, cbuf, conv_w, conv_b, gate_w, gate_b, lam):
    row_spec = pl.BlockSpec((LRU_ST, D), lambda i: (ROW_S // LRU_ST + i, 0))
    return pl.pallas_call(
        _lru_dec_kernel,
        grid=(NS // LRU_ST,),
        in_specs=[row_spec, row_spec,
                  pl.BlockSpec((None, LRU_ST, D), lambda i: (j, i, 0)),
                  pl.BlockSpec((None, LRU_ST, 3, D), lambda i: (j, i, 0, 0)),
                  pl.BlockSpec((None, 4, D), lambda i: (j, 0, 0)),
                  pl.BlockSpec((1, D), lambda i: (j, 0)),
                  pl.BlockSpec((None, 2, LRU_NBLK, LRU_BW, LRU_BW), lambda i: (j, 0, 0, 0, 0)),
                  pl.BlockSpec((None, 2, D), lambda i: (j, 0, 0)),
                  pl.BlockSpec((1, D), lambda i: (j, 0))],
        out_specs=[pl.BlockSpec((LRU_ST, D), lambda i: (i, 0)),
                   pl.BlockSpec((LRU_ST, D), lambda i: (i, 0)),
                   pl.BlockSpec((LRU_ST, 3, D), lambda i: (i, 0, 0))],
        out_shape=[_sds((NS, D), bf16), _sds((NS, D), f32), _sds((NS, 3, D), f32)],
        scratch_shapes=[pltpu.VMEM((2, LRU_NBLK, LRU_BW, LRU_BW), bf16)],
        name="lru_decode",
        compiler_params=_cp(("arbitrary",)),
    )(x, y, h0, cbuf, conv_w, conv_b, gate_w, gate_b, lam)


def _lru_layer(h, norm_g, state_h, state_conv, j, w_in, conv_w, conv_b, gate_w, gate_b, lam, w_out):
    xn = _rmsnorm(h, norm_g, out_dtype=bf16, tm=768, row_tile0=0, n_row_tiles=ROWS // 768)
    y, x = _lru_in(xn, w_in, j)
    o_main, o_meta, hl_p, cv_p = _lru_prompt(x, y, j, conv_w, conv_b, gate_w, gate_b, lam)
    o_s, hl_s, cv_s = _lru_sample(x, y, j, state_h, state_conv, conv_w, conv_b, gate_w, gate_b, lam)
    o_aux = _aux_rows(o_s, o_meta)
    h = _residual_out(h, o_main, o_aux, w_out, (j,), D)
    return h, hl_p.reshape(NB, D), hl_s, cv_p, cv_s


def _rwkv_premix_kernel(h_ref, g_ref, mu_ref, sh_ref, *refs):
    x_refs = refs[:6]
    sp_ref, ss_ref, xbuf, meta_last = refs[6:]
    i = pl.program_id(0)
    x = h_ref[...]
    ms = jnp.mean(x * x, axis=-1, keepdims=True)
    xn = x * lax.rsqrt(ms + EPS) * g_ref[...]
    xbuf[8:8 + PM_T, :] = xn
    row = lax.broadcasted_iota(jnp.int32, (PM_T, 1), 0)

    @pl.when(i == 0)
    def _aux():
        xbuf[7:8, :] = jnp.zeros((1, D), f32)
        ss_ref[...] = xn[0:NS]
        for b in range(NB):
            meta_last[b:b + 1, :] = xn[NS + b * NM + NM - 1:NS + b * NM + NM]

    @pl.when(i > 0)
    def _main():
        m = i - 1
        b = m // (LP // PM_T)

        @pl.when(m % (LP // PM_T) == 0)
        def _start():
            xbuf[7:8, :] = meta_last[pl.ds(b, 1), :]

        @pl.when(m % (LP // PM_T) == LP // PM_T - 1)
        def _end():
            sp_ref[pl.ds(b, 1), :] = xn[PM_T - 1:PM_T]

    shifted = xbuf[7:7 + PM_T, :]
    is_meta = (row >= NS) & (row < NS + NB * NM) & ((row - NS) % NM != 0)
    sh_pad = jnp.concatenate([sh_ref[...], jnp.zeros((PM_T - NS, D), f32)], axis=0)
    prev_aux = jnp.where(row < NS, sh_pad, jnp.where(is_meta, shifted, 0.0))
    prev = jnp.where(i == 0, prev_aux, shifted)
    dx = prev - xn
    for n in range(6):
        x_refs[n][...] = (xn + dx * mu_ref[n:n + 1, :]).astype(bf16)
    xbuf[7:8, :] = xn[PM_T - 1:PM_T]


def _rwkv_premix(h, norm_g, mu_j, shift_j):
    nt = ROWS // PM_T
    rows = lambda i: (jnp.where(i == 0, nt - 1, i - 1), 0)
    return pl.pallas_call(
        _rwkv_premix_kernel,
        grid=(nt,),
        in_specs=[pl.BlockSpec((PM_T, D), rows),
                  pl.BlockSpec((1, D), lambda i: (0, 0)),
                  pl.BlockSpec((6, D), lambda i: (0, 0)),
                  pl.BlockSpec((NS, D), lambda i: (0, 0))],
        out_specs=[pl.BlockSpec((PM_T, D), rows)] * 6 + [pl.BlockSpec((NB, D), lambda i: (0, 0)),
                                                         pl.BlockSpec((NS, D), lambda i: (0, 0))],
        out_shape=[_sds((ROWS, D), bf16)] * 6 + [_sds((NB, D), f32), _sds((NS, D), f32)],
        scratch_shapes=[pltpu.VMEM((PM_T + 8, D), f32), pltpu.VMEM((8, D), f32)],
        name="rwkv_premix",
        compiler_params=_cp(("arbitrary",)),
    )(h, norm_g, mu_j, shift_j)


def _rwkv_lora_kernel(xw_ref, xa_ref, xg_ref, w1_ref, w2_ref, a1_ref, a2_ref, g1_ref, g2_ref, w0_ref, a0_ref,
                      d_ref, a_ref, g_ref, w1_s, w2_s, a1_s, a2_s, g1_s, g2_s):
    @pl.when(pl.program_id(0) == 0)
    def _cast():
        for src, dst in ((w1_ref, w1_s), (w2_ref, w2_s), (a1_ref, a1_s), (a2_ref, a2_s), (g1_ref, g1_s),
                         (g2_ref, g2_s)):
            dst[...] = src[...].astype(bf16)

    def mm(u, w_s):
        return jnp.dot(u, w_s[...], preferred_element_type=f32)

    tw = jnp.tanh(mm(xw_ref[...], w1_s)).astype(bf16)
    w = -jax.nn.softplus(-(w0_ref[...] + mm(tw, w2_s))) - 0.5
    d_ref[...] = jnp.exp(-jnp.exp(w)).reshape(d_ref.shape)
    ta = mm(xa_ref[...], a1_s).astype(bf16)
    a_ref[...] = jax.nn.sigmoid(a0_ref[...] + mm(ta, a2_s)).reshape(a_ref.shape)
    tg = jax.nn.sigmoid(mm(xg_ref[...], g1_s)).astype(bf16)
    g_ref[...] = mm(tg, g2_s)


def _rwkv_lora(xw, xa, xg, w1, w2, a1, a2, g1, g2, w0, a0):
    tm = 384
    full = lambda a: pl.BlockSpec(a.shape, lambda i: (0,) * a.ndim)
    row = pl.BlockSpec((tm, D), lambda i: (i, 0))
    row3 = pl.BlockSpec((tm, RW_Q, 128), lambda i: (i, 0, 0))
    ws = [w1, w2, a1, a2, g1, g2]
    return pl.pallas_call(
        _rwkv_lora_kernel,
        grid=(ROWS // tm,),
        in_specs=[row, row, row] + [full(a) for a in ws + [w0, a0]],
        out_specs=[row3, row3, row],
        out_shape=[_sds((ROWS, RW_Q, 128), f32)] * 2 + [_sds((ROWS, D), f32)],
        scratch_shapes=[pltpu.VMEM(a.shape, bf16) for a in ws],
        name="rwkv_lora",
        compiler_params=_cp(("arbitrary",)),
    )(xw, xa, xg, *ws, w0, a0)


def _rwkv_step(t, r_s, k_s, v_s, d_s, a_s, z_s, s_s, tmp, kk_p, ka_p, rk_p, lw_p, lb_p):
    kk_t, d_t, ka_t, k2_t, r_t = tmp
    r = r_s[t]
    k = k_s[t]
    v = v_s[t]
    a = a_s[t]
    kkr = k * kk_p[...]
    nrm = jnp.sqrt(jnp.sum(kkr * kkr, axis=0, keepdims=True))
    kk = kkr / jnp.maximum(nrm, 1e-12)
    k2 = k * (1.0 + (a - 1.0) * ka_p[...])
    kk_t[...] = kk
    d_t[...] = d_s[t]
    ka_t[...] = kk * a
    k2_t[...] = k2
    r_t[...] = r

    sa = -(s_s[0] * kk_t[0:1, :])
    for j in range(1, RW_N):
        sa = sa - s_s[j] * kk_t[j:j + 1, :]
    y = None
    for j in range(RW_N):
        sn = s_s[j] * d_t[j:j + 1, :] + sa * ka_t[j:j + 1, :] + v * k2_t[j:j + 1, :]
        s_s[j] = sn
        yj = sn * r_t[j:j + 1, :]
        y = yj if y is None else y + yj
    mean = jnp.mean(y, axis=0, keepdims=True)
    yc = y - mean
    var = jnp.mean(yc * yc, axis=0, keepdims=True)
    yn = yc * lax.rsqrt(var + RW_LN_EPS) * lw_p[...] + lb_p[...]
    bonus = jnp.sum(r * k2 * rk_p[...], axis=0, keepdims=True) * v
    z_s[t] = yn + bonus


def _rwkv_scan_kernel(r_ref, k_ref, v_ref, d_ref, a_ref, s0_ref, kk_p, ka_p, rk_p, lw_p, lb_p,
                      z_ref, sf_ref, s_s, *tmp):
    c = pl.program_id(1)

    @pl.when(c == 0)
    def _init():
        s_s[...] = s0_ref[...]

    def step(t, carry):
        _rwkv_step(t, r_ref, k_ref, v_ref, d_ref, a_ref, z_ref, s_s, tmp, kk_p, ka_p, rk_p, lw_p, lb_p)
        return carry

    lax.fori_loop(0, r_ref.shape[0], step, 0)

    @pl.when(c == pl.num_programs(1) - 1)
    def _final():
        sf_ref[...] = s_s[...]


RW_Q = D // 128


def _half_transpose(x):
    xt = x.T
    return jnp.concatenate([xt[0:RW_N], xt[RW_N:2 * RW_N]], axis=1)


def _rwkv_prompt_kernel(*refs):
    n_in = 5 * NB
    in_refs = refs[:n_in]
    params = refs[n_in:n_in + 5]
    zmain = refs[n_in + 5:n_in + 5 + NB]
    zmeta = refs[n_in + 5 + NB:n_in + 5 + 2 * NB]
    sf_ref = refs[n_in + 5 + 2 * NB]
    scr = refs[n_in + 6 + 2 * NB:]
    seq, z_s, zo_s, s_s, tmp = scr[:5], scr[5], scr[6], scr[7], scr[8:]
    c = pl.program_id(0)

    @pl.when(c == 0)
    def _init():
        s_s[...] = jnp.zeros_like(s_s)
        z_s[...] = jnp.zeros_like(z_s)

    def load_token(t):
        rows = pl.ds(pl.multiple_of(t * RW_Q, RW_Q), RW_Q)
        for a in range(5):
            tile = jnp.concatenate([in_refs[a * NB + b][rows, :] for b in range(NB)], axis=0)
            seq[a][t] = _half_transpose(tile)

    load_token(0)

    def step(t, carry):
        tp = jnp.maximum(t - 1, 0)
        zo_s[tp] = _half_transpose(z_s[tp])
        _rwkv_step(t, *seq, z_s, s_s, tmp, *params)
        load_token(jnp.minimum(t + 1, RW_TS - 1))
        return carry

    lax.fori_loop(0, RW_TS, step, 0)
    zo_s[RW_TS - 1] = _half_transpose(z_s[RW_TS - 1])

    def write(z_refs):
        for t in range(RW_TS):
            for b in range(NB):
                z_refs[b][t * RW_Q:(t + 1) * RW_Q, :] = zo_s[t, b * RW_Q:(b + 1) * RW_Q, :]

    @pl.when(c == 0)
    def _write_meta():
        write(zmeta)

    @pl.when(c > 0)
    def _write_main():
        write(zmain)

    @pl.when(c == pl.num_programs(0) - 1)
    def _final():
        sf_ref[...] = s_s[...]


def _rwkv_prompt(r, k, v, dec, a, params):
    nc = LP // RW_TS
    blk = RW_TS * RW_Q
    arrs = [x.reshape(ROWS * RW_Q, 128) for x in (r, k, v, dec, a)]
    in_specs, args = [], []
    for x in arrs:
        for b in range(NB):
            in_specs.append(pl.BlockSpec(
                (blk, 128), lambda c, b=b: (jnp.where(c == 0, ROW_M // RW_TS + b, b * nc + c - 1), 0)))
            args.append(x)
    par = pl.BlockSpec((RW_N, 128), lambda c: (0, 0))
    outs = pl.pallas_call(
        _rwkv_prompt_kernel,
        grid=(nc + 1,),
        in_specs=in_specs + [par] * 5,
        out_specs=[pl.BlockSpec((blk, 128), lambda c: (jnp.maximum(c - 1, 0), 0))] * NB
                  + [pl.BlockSpec((blk, 128), lambda c: (0, 0))] * NB
                  + [pl.BlockSpec((RW_N, RW_N, 128), lambda c: (0, 0, 0))],
        out_shape=[_sds((LP * RW_Q, 128), f32)] * NB + [_sds((NM * RW_Q, 128), f32)] * NB
                  + [_sds((RW_N, RW_N, 128), f32)],
        scratch_shapes=[pltpu.VMEM((RW_TS, RW_N, 128), f32)] * 7 + [pltpu.VMEM((RW_N, RW_N, 128), f32)]
                       + [pltpu.VMEM((RW_N, 128), f32)] * 5,
        name="rwkv_prompt",
        compiler_params=_cp(("arbitrary",)),
    )(*args, *params)
    z_main = [o.reshape(LP, RW_Q, 128) for o in outs[:NB]]
    z_meta = [o.reshape(NM, D) for o in outs[NB:2 * NB]]
    return z_main, z_meta, outs[2 * NB]


def _rwkv_scan(rT, kT, vT, dT, aT, s0T, params, tc):
    L, _, lanes = rT.shape
    nl = lanes // 128
    seq = pl.BlockSpec((tc, RW_N, 128), lambda l, c: (c, 0, l))
    st = pl.BlockSpec((RW_N, RW_N, 128), lambda l, c: (0, 0, l))
    par = pl.BlockSpec((RW_N, 128), lambda l, c: (0, l))
    return pl.pallas_call(
        _rwkv_scan_kernel,
        grid=(nl, L // tc),
        in_specs=[seq] * 5 + [st] + [par] * 5,
        out_specs=[seq, st],
        out_shape=[_sds((L, RW_N, lanes), f32), _sds((RW_N, RW_N, lanes), f32)],
        scratch_shapes=[pltpu.VMEM((RW_N, RW_N, 128), f32)] + [pltpu.VMEM((RW_N, 128), f32)] * 5,
        name="rwkv_scan",
        compiler_params=_cp(("arbitrary", "arbitrary")),
    )(rT, kT, vT, dT, aT, s0T, *params)


def _rwkv_layer(h, norm_g, state, shift, j, mu, w_rkv, w0, w1, w2, a0, a1, a2, g1, g2, k_k, k_a, r_k,
                ln_w, ln_b, w_o):
    *xmix, shift_p, shift_s = _rwkv_premix(h, norm_g, mu[j], shift[j])
    tn = 1024
    common = dict(K=D, tm=768, tn=tn, n_row_tiles=ROWS // 768, n_col_tiles=D // tn, prologue=_pro_id,
                  epilogue=lambda accs, cvs, rms: (accs[0],))
    o = (_sds((ROWS, RW_Q, 128), f32), 0, 0)
    r, k, v = [_mm(name="rwkv_rkv", xs=[(xmix[n], 0)], ws=[(w_rkv, (j, n), 0)], outs=[o], **common)[0]
               for n in range(3)]

    def padc(w):
        return jnp.pad(w, ((0, 0), (0, LORA_PAD - w.shape[1])))

    def padr(w):
        return jnp.pad(w, ((0, LORA_PAD - w.shape[0]), (0, 0)))

    dec, a, g = _rwkv_lora(xmix[3], xmix[4], xmix[5], padc(w1[j]), padr(w2[j]), padc(a1[j]), padr(a2[j]),
                           g1[j], g2[j], w0[j].reshape(1, D), a0[j].reshape(1, D))

    def lanes_param(p, reps):
        return jnp.tile(p.reshape(RW_H, RW_N).T, (1, reps))

    def prompt_param(p):
        pt = p.reshape(RW_Q, 2, RW_N).transpose(2, 1, 0)
        return jnp.broadcast_to(pt[:, :, None, :], (RW_N, 2, NB, RW_Q)).reshape(RW_N, 128)

    def params(reps):
        return [lanes_param(k_k[j], reps), lanes_param(k_a[j], reps), lanes_param(r_k[j].reshape(D), reps),
                lanes_param(ln_w[j], reps), lanes_param(ln_b[j], reps)]

    z_main, z_meta, sT = _rwkv_prompt(r, k, v, dec, a, [prompt_param(p) for p in (
        k_k[j], k_a[j], r_k[j].reshape(D), ln_w[j], ln_b[j])])
    s_p = sT.reshape(RW_N, RW_N, 2, NB, RW_Q).transpose(3, 4, 2, 1, 0).reshape(NB, RW_H, RW_N, RW_N)

    def to_step(x):
        return x[ROW_S:ROW_S + NS].reshape(NS, RW_H, RW_N).transpose(2, 0, 1).reshape(1, RW_N, NS * RW_H)

    s0T = state[j].transpose(3, 2, 0, 1).reshape(RW_N, RW_N, NS * RW_H)
    zsT, ssT = _rwkv_scan(to_step(r), to_step(k), to_step(v), to_step(dec), to_step(a), s0T, params(NS), 1)
    s_s = ssT.reshape(RW_N, RW_N, NS, RW_H).transpose(2, 3, 1, 0)
    z_s = zsT.reshape(RW_N, NS, RW_H).transpose(1, 2, 0).reshape(NS, D)

    z_aux = _aux_rows(z_s, jnp.concatenate(z_meta, axis=0))
    tpb = LP // MAIN_TM
    segs = [([(z_main[b], 0), (g, b * tpb)], MAIN_TM, tpb, b * tpb) for b in range(NB)]
    segs.append(([(z_aux, 0), (g, ROWS_MAIN // AUX)], AUX, 1, ROWS_MAIN // AUX))
    h = _segments_out(h, segs, w_o, (j,), D, _pro_mul)
    return h, s_p, s_s, shift_p, shift_s


def kernel(x_prompt, x_sample, state_hgrn, state_lru_h, state_lru_conv, state_rwkv, state_rwkv_shift, meta_tokens, norm_mix, norm_ffn, norm_final, hgrn_w_in, hgrn_lb_logits, hgrn_norm, hgrn_w_out, lru_w_in, lru_conv_w, lru_conv_b, lru_gate_w, lru_gate_b, lru_lambda, lru_w_out, rwkv_mu, rwkv_w_rkv, rwkv_w0, rwkv_w1, rwkv_w2, rwkv_a0, rwkv_a1, rwkv_a2, rwkv_g1, rwkv_g2, rwkv_k_k, rwkv_k_a, rwkv_r_k, rwkv_ln_w, rwkv_ln_b, rwkv_w_o, ffn_w_in, ffn_w_out):
    depth = norm_mix.shape[0]
    x_aux = _aux_rows(x_sample.reshape(NS, D), jnp.tile(meta_tokens, (NB, 1)))
    h, xn0 = _assemble(x_prompt.reshape(ROWS_MAIN, D), x_aux, norm_mix[0].reshape(1, D))
    hg_p, hg_s, lh_p, lh_s, lc_p, lc_s, rw_p, rw_s, rs_p, rs_s = [[] for _ in range(10)]
    for i in range(depth):
        m, j = i % 3, i // 3
        ng = norm_mix[i].reshape(1, D)
        if m == 0:
            h, sp, ss = _hgrn_layer(h, ng, state_hgrn, j, hgrn_w_in, hgrn_lb_logits, hgrn_norm, hgrn_w_out,
                                    xn=xn0 if i == 0 else None)
            hg_p.append(sp)
            hg_s.append(ss)
        elif m == 1:
            h, hp_, hs_, cp_, cs_ = _lru_layer(h, ng, state_lru_h, state_lru_conv, j, lru_w_in, lru_conv_w,
                                               lru_conv_b, lru_gate_w, lru_gate_b, lru_lambda, lru_w_out)
            lh_p.append(hp_)
            lh_s.append(hs_)
            lc_p.append(cp_)
            lc_s.append(cs_)
        else:
            h, sp, ss, shp, shs = _rwkv_layer(h, ng, state_rwkv, state_rwkv_shift, j, rwkv_mu, rwkv_w_rkv,
                                              rwkv_w0, rwkv_w1, rwkv_w2, rwkv_a0, rwkv_a1, rwkv_a2, rwkv_g1,
                                              rwkv_g2, rwkv_k_k, rwkv_k_a, rwkv_r_k, rwkv_ln_w, rwkv_ln_b,
                                              rwkv_w_o)
            rw_p.append(sp)
            rw_s.append(ss)
            rs_p.append(shp)
            rs_s.append(shs)
        h = _ffn(h, norm_ffn[i].reshape(1, D), ffn_w_in, ffn_w_out, i)
    nf = norm_final.reshape(1, D)
    y_main = _rmsnorm(h, nf, out_dtype=f32, tm=512, row_tile0=0, n_row_tiles=ROWS_MAIN // 512)
    y_aux = _rmsnorm(h, nf, out_dtype=f32, tm=AUX, row_tile0=ROWS_MAIN // AUX, n_row_tiles=1)
    y_prompt = y_main.reshape(NB, LP, D)
    y_sample = y_aux[:NS].reshape(NS, 1, D)
    return (y_prompt, y_sample, jnp.stack(hg_p), jnp.stack(hg_s), jnp.stack(lh_p), jnp.stack(lh_s),
            jnp.stack(lc_p), jnp.stack(lc_s), jnp.stack(rw_p), jnp.stack(rw_s), jnp.stack(rs_p), jnp.stack(rs_s))
```

```python
import functools

import jax
import jax.numpy as jnp
import numpy as np
from jax import lax
from jax.experimental import pallas as pl
from jax.experimental.pallas import tpu as pltpu

f32 = jnp.float32
bf16 = jnp.bfloat16

D = 2048
NB = 4
LP = 2048
NM = 16
NS = 128
ROWS_MAIN = NB * LP
ROW_S = ROWS_MAIN
ROW_M = ROW_S + NS
AUX = 256
ROWS = ROWS_MAIN + AUX
EPS = 1e-6
HG_H, HG_K = 16, 128
HG_C = 64
HG_UNROLL = 4
GATE_EXP_CLIP = 60.0
LRU_NBLK, LRU_BW = 8, 256
LRU_C = 8.0
LRU_T = 256
RW_H, RW_N = 32, 64
RW_LN_EPS = 64e-5
RW_TS = 16
PM_T = 256
MAIN_TM = 512
LORA_PAD = 128
D_FF = 5632
VMEM_LIMIT = 56 * 1024 * 1024


def _cp(sem):
    return pltpu.CompilerParams(dimension_semantics=sem, vmem_limit_bytes=VMEM_LIMIT)


def _norm_kernel(h_ref, g_ref, o_ref):
    x = h_ref[...]
    ms = jnp.mean(x * x, axis=-1, keepdims=True)
    o_ref[...] = (x * lax.rsqrt(ms + EPS) * g_ref[...]).astype(o_ref.dtype)


def _rmsnorm(h, g, *, out_dtype, tm, row_tile0, n_row_tiles):
    return pl.pallas_call(
        _norm_kernel,
        grid=(n_row_tiles,),
        in_specs=[pl.BlockSpec((tm, D), lambda i: (i + row_tile0, 0)),
                  pl.BlockSpec((1, D), lambda i: (0, 0))],
        out_specs=pl.BlockSpec((tm, D), lambda i: (i, 0)),
        out_shape=jax.ShapeDtypeStruct((n_row_tiles * tm, D), out_dtype),
        name="rmsnorm",
        compiler_params=_cp(("arbitrary",)),
    )(h, g)


def _assemble_kernel(xm_ref, xa_ref, g_ref, h_ref, xn_ref):
    def emit(x):
        h_ref[...] = x
        ms = jnp.mean(x * x, axis=-1, keepdims=True)
        xn_ref[...] = (x * lax.rsqrt(ms + EPS) * g_ref[...]).astype(xn_ref.dtype)

    @pl.when(pl.program_id(0) < ROWS_MAIN // AUX)
    def _main():
        emit(xm_ref[...])

    @pl.when(pl.program_id(0) == ROWS_MAIN // AUX)
    def _aux():
        emit(xa_ref[...])


def _assemble(x_main, x_aux, g):
    nm = ROWS_MAIN // AUX
    return pl.pallas_call(
        _assemble_kernel,
        grid=(nm + 1,),
        in_specs=[pl.BlockSpec((AUX, D), lambda i: (jnp.minimum(i, nm - 1), 0)),
                  pl.BlockSpec((AUX, D), lambda i: (0, 0)),
                  pl.BlockSpec((1, D), lambda i: (0, 0))],
        out_specs=[pl.BlockSpec((AUX, D), lambda i: (i, 0)), pl.BlockSpec((AUX, D), lambda i: (i, 0))],
        out_shape=[_sds((ROWS, D), f32), _sds((ROWS, D), bf16)],
        name="assemble",
        compiler_params=_cp(("arbitrary",)),
    )(x_main, x_aux, g)


def _mm_body(*refs, n_x, n_w, n_kv, n_cv, n_rm, n_out, prologue, epilogue, row_splits):
    p = 0
    x_refs = refs[p:p + n_x]; p += n_x
    w_refs = refs[p:p + n_w]; p += n_w
    kv_refs = refs[p:p + n_kv]; p += n_kv
    cv_refs = refs[p:p + n_cv]; p += n_cv
    rm_refs = refs[p:p + n_rm]; p += n_rm
    out_refs = refs[p:p + n_out]; p += n_out
    w_scr = refs[p:p + n_w]

    @pl.when(pl.program_id(1) == 0)
    def _cast_weights():
        for w, s in zip(w_refs, w_scr):
            s[...] = w[...].astype(bf16)

    def flat(v):
        return v.reshape(v.shape[0], v.shape[1] * v.shape[2]) if v.ndim == 3 else v

    tm = out_refs[0].shape[0]
    sub = tm // row_splits
    for s0 in range(0, tm, sub):
        rows = slice(s0, s0 + sub)
        x = prologue([flat(r[rows]) for r in x_refs], [r[...] for r in kv_refs])
        accs = [jnp.dot(x, s[...], preferred_element_type=f32) for s in w_scr]
        res = epilogue(accs, [r[...] for r in cv_refs], [r[rows] for r in rm_refs])
        for o, r in zip(out_refs, res):
            o[rows] = r.astype(o.dtype).reshape((sub,) + o.shape[1:])


def _mm(*, name="proj", xs, ws, kvecs=(), cvecs=(), rmats=(), outs, K, tm, tn, n_row_tiles, n_col_tiles,
        prologue, epilogue, aliases=None, row_splits=1):
    in_specs, args = [], []
    for a, r0 in xs:
        if a.ndim == 3:
            in_specs.append(pl.BlockSpec((tm, K // 128, 128), lambda j, i, r0=r0: (i + r0, 0, 0)))
        else:
            in_specs.append(pl.BlockSpec((tm, K), lambda j, i, r0=r0: (i + r0, 0)))
        args.append(a)
    for a, lead, c0 in ws:
        nl = len(lead)
        in_specs.append(pl.BlockSpec((None,) * nl + (K, tn), lambda j, i, lead=lead, c0=c0: lead + (0, j + c0)))
        args.append(a)
    for a in kvecs:
        in_specs.append(pl.BlockSpec(a.shape, lambda j, i: (0, 0)))
        args.append(a)
    for a, c0 in cvecs:
        in_specs.append(pl.BlockSpec((a.shape[0], tn), lambda j, i, c0=c0: (0, j + c0)))
        args.append(a)
    for a, r0, c0 in rmats:
        in_specs.append(pl.BlockSpec((tm, tn), lambda j, i, r0=r0, c0=c0: (i + r0, j + c0)))
        args.append(a)
    out_specs = [pl.BlockSpec((tm, tn // 128, 128), lambda j, i, r0=r0, c0=c0: (i + r0, j + c0, 0))
                 if len(s.shape) == 3 else
                 pl.BlockSpec((tm, tn), lambda j, i, r0=r0, c0=c0: (i + r0, j + c0)) for s, r0, c0 in outs]
    body = functools.partial(_mm_body, n_x=len(xs), n_w=len(ws), n_kv=len(kvecs), n_cv=len(cvecs),
                             n_rm=len(rmats), n_out=len(outs), prologue=prologue, epilogue=epilogue,
                             row_splits=row_splits)
    res = pl.pallas_call(
        body,
        grid=(n_col_tiles, n_row_tiles),
        in_specs=in_specs,
        out_specs=out_specs,
        out_shape=[s for s, _, _ in outs],
        scratch_shapes=[pltpu.VMEM((K, tn), bf16) for _ in ws],
        input_output_aliases=aliases or {},
        name=name,
        compiler_params=_cp(("arbitrary", "arbitrary")),
    )(*args)
    return res


def _pro_id(xs, kvs):
    return xs[0]


def _pro_mul(xs, kvs):
    return (xs[0] * xs[1]).astype(bf16)


def _sds(shape, dtype):
    return jax.ShapeDtypeStruct(shape, dtype)


def _segments_out(h, segs, w, lead, K, prologue):
    tn = 1024
    for xs, tm, nrt, h0 in segs:
        h = _mm(name="mixer_out", xs=xs, ws=[(w, lead, 0)], rmats=[(h, h0, 0)],
                outs=[(_sds((ROWS, D), f32), h0, 0)], K=K, tm=tm, tn=tn, n_row_tiles=nrt, n_col_tiles=D // tn,
                prologue=prologue, epilogue=lambda accs, cvs, rms: (rms[0] + accs[0],),
                aliases={len(xs) + 1: 0})[0]
    return h


def _residual_out(h, x_main, x_aux, w, lead, K):
    return _segments_out(h, [([(x_main, 0)], MAIN_TM, ROWS_MAIN // MAIN_TM, 0),
                             ([(x_aux, 0)], AUX, 1, ROWS_MAIN // AUX)], w, lead, K, _pro_id)


def _ffn(h, norm_g, w_in, w_out, layer):
    xn = _rmsnorm(h, norm_g, out_dtype=bf16, tm=768, row_tile0=0, n_row_tiles=ROWS // 768)
    tn = 512
    nct = D_FF // tn
    act = _mm(name="ffn_in", xs=[(xn, 0)], ws=[(w_in, (layer,), 0), (w_in, (layer,), nct)],
              outs=[(_sds((ROWS, D_FF), bf16), 0, 0)],
              K=D, tm=768, tn=tn, n_row_tiles=ROWS // 768, n_col_tiles=nct,
              prologue=_pro_id,
              epilogue=lambda accs, cvs, rms: (jax.nn.silu(accs[0]) * accs[1],))[0]
    h = _mm(name="ffn_out", xs=[(act, 0)], ws=[(w_out, (layer,), 0)], rmats=[(h, 0, 0)],
            outs=[(_sds((ROWS, D), f32), 0, 0)],
            K=D_FF, tm=384, tn=tn, n_row_tiles=ROWS // 384, n_col_tiles=D // tn,
            prologue=_pro_id, epilogue=lambda accs, cvs, rms: (rms[0] + accs[0],),
            aliases={2: 0})[0]
    return h


def _hgrn_lower_bound(logits, j):
    m = jnp.max(logits, axis=0, keepdims=True)
    e = jnp.exp(logits - m)
    p = e / jnp.sum(e, axis=0, keepdims=True)
    cs = p[0:1]
    for r in range(1, j + 1):
        cs = cs + p[r:r + 1]
    return jnp.clip(cs - p[0:1], 0.0, 1.0)


def _hgrn_in(xn, w_in, lb_logits, j):
    tn = 1024
    nct = D // tn
    common = dict(K=D, tm=768, tn=tn, n_row_tiles=ROWS // 768, n_col_tiles=nct, prologue=_pro_id, row_splits=3)
    o = (_sds((ROWS, D), f32), 0, 0)
    q = _mm(xs=[(xn, 0)], ws=[(w_in, (j,), 0)], outs=[o],
            epilogue=lambda accs, cvs, rms: (jax.nn.silu(accs[0]),), **common)[0]

    def f_epi(accs, cvs, rms):
        f = accs[0]
        lb = _hgrn_lower_bound(cvs[0], j)
        log_f = jax.nn.log_sigmoid(f) + jnp.log1p(lb * jnp.exp(jnp.minimum(-f, GATE_EXP_CLIP)))
        k = (1.0 - lb) * jax.nn.sigmoid(-f)
        return log_f, k

    log_f, k = _mm(xs=[(xn, 0)], ws=[(w_in, (j,), nct)], cvecs=[(lb_logits, 0)], outs=[o, o],
                   epilogue=f_epi, **common)
    v = _mm(xs=[(xn, 0)], ws=[(w_in, (j,), 2 * nct)], outs=[o],
            epilogue=lambda accs, cvs, rms: (accs[0],), **common)[0]
    g = _mm(xs=[(xn, 0)], ws=[(w_in, (j,), 3 * nct)], outs=[o],
            epilogue=lambda accs, cvs, rms: (jax.nn.silu(accs[0]),), **common)[0]
    return q, k, v, log_f, g


def _split3(x):
    hi = x.astype(bf16)
    r = x - hi.astype(f32)
    mid = r.astype(bf16)
    lo = (r - mid.astype(f32)).astype(bf16)
    return hi, mid, lo


def _gla_consts(T):
    t = np.arange(T)[:, None]
    w = np.arange(T)[None, :]
    mats = [w <= t, w > t]
    masks = []
    s = T // 2
    while s >= 1:
        piece = t // (2 * s)
        lower = (t % (2 * s)) >= s
        ref = piece * 2 * s + s
        mats.append(np.where(lower, (w > ref) & (w <= t), (w > t) & (w <= ref)))
        upiece = w // (2 * s)
        ulower = (w % (2 * s)) >= s
        masks.append((piece == upiece) & lower & ~ulower)
        s //= 2
    masks.append(t == w)
    m = np.concatenate(mats, 0)
    return (jnp.asarray(np.concatenate([m, m, m], 1), dtype=bf16), jnp.asarray(np.stack(masks), dtype=f32))


def _gla_block(q_ref, k_ref, v_ref, f_ref, g_ref, gn_ref, dm_ref, pm_ref, o_ref, st_ref, e_s, ql_s, kl_s, T):
    nlev = T.bit_length() - 1
    dn_t = (((1,), (1,)), ((), ()))
    dn_l = (((0,), (0,)), ((), ()))
    e_s[0:(nlev + 2) * T, :] = jnp.dot(dm_ref[...], jnp.concatenate(_split3(f_ref[...]), axis=0),
                                       preferred_element_type=f32)
    q = q_ref[...]
    k = k_ref[...]
    ql_s[0, 0:T, :] = (q * jnp.exp(e_s[0:T, :])).astype(bf16)
    kl_s[0, 0:T, :] = (k * jnp.exp(e_s[T:2 * T, :])).astype(bf16)
    for l in range(nlev):
        ex = jnp.exp(e_s[(l + 2) * T:(l + 3) * T, :])
        ql_s[l + 1, 0:T, :] = (q * ex).astype(bf16)
        kl_s[l + 1, 0:T, :] = (k * ex).astype(bf16)
    ql_s[nlev + 1, 0:T, :] = q.astype(bf16)
    kl_s[nlev + 1, 0:T, :] = k.astype(bf16)

    for h in range(HG_H):
        sl = slice(h * HG_K, (h + 1) * HG_K)
        st = st_ref[h]
        vb = v_ref[:, sl].astype(bf16)
        o = lax.dot_general(ql_s[0, 0:T, sl], st.astype(bf16), dn_t, preferred_element_type=f32)
        att = jnp.zeros((T, T), f32)
        for l in range(nlev + 1):
            a_l = lax.dot_general(ql_s[l + 1, 0:T, sl], kl_s[l + 1, 0:T, sl], dn_t,
                                  preferred_element_type=f32)
            att = att + a_l * pm_ref[l]
        o = o + jnp.dot(att.astype(bf16), vb, preferred_element_type=f32)
        upd = lax.dot_general(vb, kl_s[0, 0:T, sl], dn_l, preferred_element_type=f32)
        st_ref[h] = st * jnp.exp(e_s[T - 1:T, sl]) + upd
        ms = jnp.mean(o * o, axis=-1, keepdims=True)
        on = o * lax.rsqrt(ms + EPS) * gn_ref[:, sl]
        o_ref[:, sl] = (on * g_ref[:, sl]).astype(o_ref.dtype)


def _gla_kernel(qm, km, vm, fm, gm, qa, ka, va, fa, ga, gn_ref, dmm, pmm, dma, pma,
                om_ref, oa_ref, s_ref, st_ref, e_s, ql_s, kl_s):
    c = pl.program_id(1)

    @pl.when(c == 0)
    def _meta():
        st_ref[...] = jnp.zeros_like(st_ref)
        _gla_block(qa, ka, va, fa, ga, gn_ref, dma, pma, oa_ref, st_ref, e_s, ql_s, kl_s, NM)

    @pl.when(c > 0)
    def _main():
        _gla_block(qm, km, vm, fm, gm, gn_ref, dmm, pmm, om_ref, st_ref, e_s, ql_s, kl_s, HG_C)

    @pl.when(c == pl.num_programs(1) - 1)
    def _final():
        for h in range(HG_H):
            s_ref[0, h] = st_ref[h].T


def _hgrn_prompt(q, k, v, log_f, g, gn):
    nc = LP // HG_C
    main_spec = pl.BlockSpec((HG_C, D), lambda b, c: (b * nc + jnp.maximum(c - 1, 0), 0))
    meta_spec = pl.BlockSpec((NM, D), lambda b, c: (ROW_M // NM + b, 0))
    arrs = [q, k, v, log_f, g]
    consts = [*_gla_consts(HG_C), *_gla_consts(NM)]
    nslot = HG_C.bit_length() + 1
    return pl.pallas_call(
        _gla_kernel,
        grid=(NB, nc + 1),
        in_specs=[main_spec] * 5 + [meta_spec] * 5 + [pl.BlockSpec((1, D), lambda b, c: (0, 0))]
                 + [pl.BlockSpec(a.shape, lambda b, c, n=a.ndim: (0,) * n) for a in consts],
        out_specs=[pl.BlockSpec((HG_C, D), lambda b, c: (b * nc + jnp.maximum(c - 1, 0), 0)),
                   pl.BlockSpec((NM, D), lambda b, c: (b, 0)),
                   pl.BlockSpec((1, HG_H, HG_K, HG_K), lambda b, c: (b, 0, 0, 0))],
        out_shape=[_sds((ROWS_MAIN, D), bf16), _sds((NB * NM, D), bf16), _sds((NB, HG_H, HG_K, HG_K), f32)],
        scratch_shapes=[pltpu.VMEM((HG_H, HG_K, HG_K), f32),
                        pltpu.VMEM((nslot * HG_C, D), f32),
                        pltpu.VMEM((nslot, HG_C, D), bf16),
                        pltpu.VMEM((nslot, HG_C, D), bf16)],
        name="hgrn_prompt",
        compiler_params=_cp(("arbitrary", "arbitrary")),
    )(*arrs, *arrs, gn, *consts)


HG_BT = 8


def _hgrn_dec_kernel(s_ref, qT, kT, fT, v_ref, g_ref, gn_ref, e3_ref, *rest):
    so_ref, o_ref, qb_s, kb_s, gb_s = rest[-5:]

    def spread(m, dst):
        dst[...] = jnp.dot(jnp.concatenate(_split3(m), axis=1), e3_ref[...], preferred_element_type=f32)

    def per_b(bb, carry):
        spread(qT[bb], qb_s)
        spread(kT[bb], kb_s)
        spread(jnp.exp(fT[bb]), gb_s)
        vall = v_ref[pl.ds(bb, 1), :]
        outs = []
        for h in range(HG_H):
            hs = slice(h * HG_K, (h + 1) * HG_K)
            sn = gb_s[:, hs] * s_ref[bb, h] + kb_s[:, hs] * vall[:, hs]
            so_ref[bb, h] = sn
            o = jnp.sum(qb_s[:, hs] * sn, axis=0, keepdims=True)
            ms = jnp.mean(o * o, axis=-1, keepdims=True)
            outs.append(o * lax.rsqrt(ms + EPS))
        on = jnp.concatenate(outs, axis=1) * gn_ref[...]
        o_ref[pl.ds(bb, 1), :] = on * g_ref[pl.ds(bb, 1), :]
        return carry

    lax.fori_loop(0, HG_BT, per_b, 0)


def _hgrn_sample(state, j, q, k, v, log_f, g, gn, new_states):
    def colform(x):
        return x[ROW_S:ROW_S + NS].reshape(NS, HG_H, HG_K).transpose(0, 2, 1)

    head_of_lane = np.arange(D) // HG_K
    sel = (np.arange(HG_H)[:, None] == head_of_lane[None, :])
    e3 = jnp.asarray(np.concatenate([sel, sel, sel], axis=0), dtype=bf16)
    col_spec = pl.BlockSpec((HG_BT, HG_K, HG_H), lambda i: (i, 0, 0))
    row_spec = pl.BlockSpec((HG_BT, D), lambda i: (ROW_S // HG_BT + i, 0))
    st_spec = pl.BlockSpec((None, HG_BT, HG_H, HG_K, HG_K), lambda i: (j, i, 0, 0, 0))
    in_specs = [st_spec, col_spec, col_spec, col_spec, row_spec, row_spec,
                pl.BlockSpec((1, D), lambda i: (0, 0)), pl.BlockSpec((3 * HG_H, D), lambda i: (0, 0))]
    args = [state, colform(q), colform(k), colform(log_f), v, g, gn, e3]
    aliases = {}
    if new_states is not None:
        in_specs.append(pl.BlockSpec(memory_space=pl.ANY))
        args.append(new_states)
        aliases = {len(args) - 1: 0}
    return pl.pallas_call(
        _hgrn_dec_kernel,
        grid=(NS // HG_BT,),
        in_specs=in_specs,
        out_specs=[st_spec, pl.BlockSpec((HG_BT, D), lambda i: (i, 0))],
        out_shape=[_sds(state.shape, f32), _sds((NS, D), f32)],
        scratch_shapes=[pltpu.VMEM((HG_K, D), f32)] * 3,
        input_output_aliases=aliases,
        name="hgrn_decode",
        compiler_params=_cp(("arbitrary",)),
    )(*args)


def _aux_rows(sample_rows, meta_rows):
    pad = jnp.zeros((AUX - NS - NB * NM, sample_rows.shape[1]), sample_rows.dtype)
    return jnp.concatenate([sample_rows, meta_rows.astype(sample_rows.dtype), pad], axis=0)


def _hgrn_layer(h, norm_g, state, new_states, j, w_in, lb_logits, gnorm, w_out, xn=None):
    if xn is None:
        xn = _rmsnorm(h, norm_g, out_dtype=bf16, tm=768, row_tile0=0, n_row_tiles=ROWS // 768)
    q, k, v, log_f, g = _hgrn_in(xn, w_in, lb_logits, j)
    gn = gnorm[j].reshape(1, D)
    o_main, o_meta, s_p = _hgrn_prompt(q, k, v, log_f, g, gn)
    s_s, o_s = _hgrn_sample(state, j, q, k, v, log_f, g, gn, new_states)
    o_aux = _aux_rows(o_s.astype(bf16), o_meta)
    h = _residual_out(h, o_main, o_aux, w_out, (j,), D)
    return h, s_p, s_s


def _lru_in(xn, w_in, j):
    tn = 1024
    nct = D // tn
    common = dict(K=D, tm=768, tn=tn, n_row_tiles=ROWS // 768, n_col_tiles=nct, prologue=_pro_id, row_splits=3)
    o = (_sds((ROWS, D), f32), 0, 0)
    y = _mm(xs=[(xn, 0)], ws=[(w_in, (j,), 0)], outs=[o],
            epilogue=lambda accs, cvs, rms: (jax.nn.gelu(accs[0], approximate=True),), **common)[0]
    x = _mm(xs=[(xn, 0)], ws=[(w_in, (j,), nct)], outs=[o],
            epilogue=lambda accs, cvs, rms: (accs[0],), **common)[0]
    return y, x


def _lru_gates(xc, gw_ref, gb_ref, lam_ref, n):
    ls = slice(n * LRU_BW, (n + 1) * LRU_BW)
    xb = xc[:, ls].astype(bf16)
    r = jax.nn.sigmoid(jnp.dot(xb, gw_ref[0, n], preferred_element_type=f32) + gb_ref[0:1, ls])
    ig = jax.nn.sigmoid(jnp.dot(xb, gw_ref[1, n], preferred_element_type=f32) + gb_ref[1:2, ls])
    log_a = -LRU_C * r * jax.nn.softplus(-lam_ref[:, ls])
    a = jnp.exp(log_a)
    mult = jnp.sqrt(jnp.maximum(1.0 - a * a, 0.0))
    return a, ig, mult


def _lru_conv(cb_ref, cw_ref, x0, x1, x2, x3):
    xc = cb_ref[...] + x0 * cw_ref[0:1, :]
    xc = xc + x1 * cw_ref[1:2, :]
    xc = xc + x2 * cw_ref[2:3, :]
    return xc + x3 * cw_ref[3:4, :]


def _lru_block(x_ref, y_ref, o_ref, cw_ref, cb_ref, gw_s, gb_ref, lam_ref, xbuf, a_s, b_s, hcar, T, first):
    xbuf[8:8 + T, :] = x_ref[...]
    xc = _lru_conv(cb_ref, cw_ref, xbuf[5:5 + T, :], xbuf[6:6 + T, :], xbuf[7:7 + T, :], xbuf[8:8 + T, :])
    xbuf[0:8, :] = xbuf[T:T + 8, :]
    row = lax.broadcasted_iota(jnp.int32, (T, 1), 0)
    for n in range(LRU_NBLK):
        ls = slice(n * LRU_BW, (n + 1) * LRU_BW)
        a, ig, mult = _lru_gates(xc, gw_s, gb_ref, lam_ref, n)
        if first:
            mult = jnp.where(row == 0, 1.0, mult)
        a_s[0:T, ls] = a
        b_s[0:T, ls] = xc[:, ls] * ig * mult

    def step(t, hprev):
        hnew = a_s[pl.ds(t, 1), :] * hprev + b_s[pl.ds(t, 1), :]
        b_s[pl.ds(t, 1), :] = hnew
        return hnew

    hcar[...] = lax.fori_loop(0, T, step, hcar[...], unroll=8)
    o_ref[...] = (b_s[0:T, :] * y_ref[...]).astype(o_ref.dtype)


def _lru_kernel(xm, ym, xa, ya, cw_ref, cb_ref, gw_ref, gb_ref, lam_ref,
                om_ref, oa_ref, hl_ref, cv_ref, gw_s, xbuf, a_s, b_s, hcar):
    c = pl.program_id(1)

    @pl.when((pl.program_id(0) == 0) & (c == 0))
    def _cast():
        gw_s[...] = gw_ref[...].astype(bf16)

    @pl.when(c == 0)
    def _meta():
        xbuf[0:8, :] = jnp.zeros((8, D), f32)
        hcar[...] = jnp.zeros_like(hcar)
        _lru_block(xa, ya, oa_ref, cw_ref, cb_ref, gw_s, gb_ref, lam_ref, xbuf, a_s, b_s, hcar, NM, True)

    @pl.when(c > 0)
    def _main():
        _lru_block(xm, ym, om_ref, cw_ref, cb_ref, gw_s, gb_ref, lam_ref, xbuf, a_s, b_s, hcar, LRU_T, False)

    @pl.when(c == pl.num_programs(1) - 1)
    def _final():
        hl_ref[0] = hcar[...]
        cv_ref[0] = xbuf[5:8, :]


def _lru_prompt(x, y, j, conv_w, conv_b, gate_w, gate_b, lam):
    nc = LP // LRU_T
    main_spec = pl.BlockSpec((LRU_T, D), lambda b, c: (b * nc + jnp.maximum(c - 1, 0), 0))
    meta_spec = pl.BlockSpec((NM, D), lambda b, c: (ROW_M // NM + b, 0))
    return pl.pallas_call(
        _lru_kernel,
        grid=(NB, nc + 1),
        in_specs=[main_spec, main_spec, meta_spec, meta_spec,
                  pl.BlockSpec((None, 4, D), lambda b, c: (j, 0, 0)),
                  pl.BlockSpec((1, D), lambda b, c: (j, 0)),
                  pl.BlockSpec((None, 2, LRU_NBLK, LRU_BW, LRU_BW), lambda b, c: (j, 0, 0, 0, 0)),
                  pl.BlockSpec((None, 2, D), lambda b, c: (j, 0, 0)),
                  pl.BlockSpec((1, D), lambda b, c: (j, 0))],
        out_specs=[pl.BlockSpec((LRU_T, D), lambda b, c: (b * nc + jnp.maximum(c - 1, 0), 0)),
                   pl.BlockSpec((NM, D), lambda b, c: (b, 0)),
                   pl.BlockSpec((1, 1, D), lambda b, c: (b, 0, 0)),
                   pl.BlockSpec((1, 3, D), lambda b, c: (b, 0, 0))],
        out_shape=[_sds((ROWS_MAIN, D), bf16), _sds((NB * NM, D), bf16),
                   _sds((NB, 1, D), f32), _sds((NB, 3, D), f32)],
        scratch_shapes=[pltpu.VMEM((2, LRU_NBLK, LRU_BW, LRU_BW), bf16),
                        pltpu.VMEM((LRU_T + 8, D), f32), pltpu.VMEM((LRU_T, D), f32),
                        pltpu.VMEM((LRU_T, D), f32), pltpu.VMEM((1, D), f32)],
        name="lru_prompt",
        compiler_params=_cp(("arbitrary", "arbitrary")),
    )(x, y, x, y, conv_w, conv_b, gate_w, gate_b, lam)


LRU_ST = 64


def _lru_dec_kernel(x_ref, y_ref, h0_ref, cbuf_ref, cw_ref, cb_ref, gw_ref, gb_ref, lam_ref,
                    o_ref, hn_ref, cn_ref, gw_s):
    gw_s[...] = gw_ref[...].astype(bf16)
    x = x_ref[...]
    b0 = cbuf_ref[:, 0, :]
    b1 = cbuf_ref[:, 1, :]
    b2 = cbuf_ref[:, 2, :]
    xc = _lru_conv(cb_ref, cw_ref, b0, b1, b2, x)
    cn_ref[:, 0, :] = b1
    cn_ref[:, 1, :] = b2
    cn_ref[:, 2, :] = x
    for n in range(LRU_NBLK):
        ls = slice(n * LRU_BW, (n + 1) * LRU_BW)
        a, ig, mult = _lru_gates(xc, gw_s, gb_ref, lam_ref, n)
        hn = a * h0_ref[:, ls] + xc[:, ls] * ig * mult
        hn_ref[:, ls] = hn
        o_ref[:, ls] = (hn * y_ref[:, ls]).astype(o_ref.dtype)


def _lru_sample(x, y, j, h0, cbuf, conv_w, conv_b, gate_w, gate_b, lam):
    row_spec = pl.BlockSpec((LRU_ST, D), lambda i: (ROW_S // LRU_ST + i, 0))
    return pl.pallas_call(
        _lru_dec_kernel,
        grid=(NS // LRU_ST,),
        in_specs=[row_spec, row_spec,
                  pl.BlockSpec((None, LRU_ST, D), lambda i: (j, i, 0)),
                  pl.BlockSpec((None, LRU_ST, 3, D), lambda i: (j, i, 0, 0)),
                  pl.BlockSpec((None, 4, D), lambda i: (j, 0, 0)),
                  pl.BlockSpec((1, D), lambda i: (j, 0)),
                  pl.BlockSpec((None, 2, LRU_NBLK, LRU_BW, LRU_BW), lambda i: (j, 0, 0, 0, 0)),
                  pl.BlockSpec((None, 2, D), lambda i: (j, 0, 0)),
                  pl.BlockSpec((1, D), lambda i: (j, 0))],
        out_specs=[pl.BlockSpec((LRU_ST, D), lambda i: (i, 0)),
                   pl.BlockSpec((LRU_ST, D), lambda i: (i, 0)),
                   pl.BlockSpec((LRU_ST, 3, D), lambda i: (i, 0, 0))],
        out_shape=[_sds((NS, D), bf16), _sds((NS, D), f32), _sds((NS, 3, D), f32)],
        scratch_shapes=[pltpu.VMEM((2, LRU_NBLK, LRU_BW, LRU_BW), bf16)],
        name="lru_decode",
        compiler_params=_cp(("arbitrary",)),
    )(x, y, h0, cbuf, conv_w, conv_b, gate_w, gate_b, lam)


def _lru_layer(h, norm_g, state_h, state_conv, j, w_in, conv_w, conv_b, gate_w, gate_b, lam, w_out):
    xn = _rmsnorm(h, norm_g, out_dtype=bf16, tm=768, row_tile0=0, n_row_tiles=ROWS // 768)
    y, x = _lru_in(xn, w_in, j)
    o_main, o_meta, hl_p, cv_p = _lru_prompt(x, y, j, conv_w, conv_b, gate_w, gate_b, lam)
    o_s, hl_s, cv_s = _lru_sample(x, y, j, state_h, state_conv, conv_w, conv_b, gate_w, gate_b, lam)
    o_aux = _aux_rows(o_s, o_meta)
    h = _residual_out(h, o_main, o_aux, w_out, (j,), D)
    return h, hl_p.reshape(NB, D), hl_s, cv_p, cv_s


def _rwkv_premix_kernel(h_ref, g_ref, mu_ref, sh_ref, *refs):
    x_refs = refs[:6]
    sp_ref, ss_ref, xbuf, meta_last = refs[6:]
    i = pl.program_id(0)
    x = h_ref[...]
    ms = jnp.mean(x * x, axis=-1, keepdims=True)
    xn = x * lax.rsqrt(ms + EPS) * g_ref[...]
    xbuf[8:8 + PM_T, :] = xn
    row = lax.broadcasted_iota(jnp.int32, (PM_T, 1), 0)

    @pl.when(i == 0)
    def _aux():
        xbuf[7:8, :] = jnp.zeros((1, D), f32)
        ss_ref[...] = xn[0:NS]
        for b in range(NB):
            meta_last[b:b + 1, :] = xn[NS + b * NM + NM - 1:NS + b * NM + NM]

    @pl.when(i > 0)
    def _main():
        m = i - 1
        b = m // (LP // PM_T)

        @pl.when(m % (LP // PM_T) == 0)
        def _start():
            xbuf[7:8, :] = meta_last[pl.ds(b, 1), :]

        @pl.when(m % (LP // PM_T) == LP // PM_T - 1)
        def _end():
            sp_ref[pl.ds(b, 1), :] = xn[PM_T - 1:PM_T]

    shifted = xbuf[7:7 + PM_T, :]
    is_meta = (row >= NS) & (row < NS + NB * NM) & ((row - NS) % NM != 0)
    sh_pad = jnp.concatenate([sh_ref[...], jnp.zeros((PM_T - NS, D), f32)], axis=0)
    prev_aux = jnp.where(row < NS, sh_pad, jnp.where(is_meta, shifted, 0.0))
    prev = jnp.where(i == 0, prev_aux, shifted)
    dx = prev - xn
    for n in range(6):
        x_refs[n][...] = (xn + dx * mu_ref[n:n + 1, :]).astype(bf16)
    xbuf[7:8, :] = xn[PM_T - 1:PM_T]


def _rwkv_premix(h, norm_g, mu_j, shift_j):
    nt = ROWS // PM_T
    rows = lambda i: (jnp.where(i == 0, nt - 1, i - 1), 0)
    return pl.pallas_call(
        _rwkv_premix_kernel,
        grid=(nt,),
        in_specs=[pl.BlockSpec((PM_T, D), rows),
                  pl.BlockSpec((1, D), lambda i: (0, 0)),
                  pl.BlockSpec((6, D), lambda i: (0, 0)),
                  pl.BlockSpec((NS, D), lambda i: (0, 0))],
        out_specs=[pl.BlockSpec((PM_T, D), rows)] * 6 + [pl.BlockSpec((NB, D), lambda i: (0, 0)),
                                                         pl.BlockSpec((NS, D), lambda i: (0, 0))],
        out_shape=[_sds((ROWS, D), bf16)] * 6 + [_sds((NB, D), f32), _sds((NS, D), f32)],
        scratch_shapes=[pltpu.VMEM((PM_T + 8, D), f32), pltpu.VMEM((8, D), f32)],
        name="rwkv_premix",
        compiler_params=_cp(("arbitrary",)),
    )(h, norm_g, mu_j, shift_j)


def _rwkv_lora_kernel(xw_ref, xa_ref, xg_ref, w1_ref, w2_ref, a1_ref, a2_ref, g1_ref, g2_ref, w0_ref, a0_ref,
                      d_ref, a_ref, g_ref, w1_s, w2_s, a1_s, a2_s, g1_s, g2_s):
    @pl.when(pl.program_id(0) == 0)
    def _cast():
        for src, dst in ((w1_ref, w1_s), (w2_ref, w2_s), (a1_ref, a1_s), (a2_ref, a2_s), (g1_ref, g1_s),
                         (g2_ref, g2_s)):
            dst[...] = src[...].astype(bf16)

    def mm(u, w_s):
        return jnp.dot(u, w_s[...], preferred_element_type=f32)

    tw = jnp.tanh(mm(xw_ref[...], w1_s)).astype(bf16)
    w = -jax.nn.softplus(-(w0_ref[...] + mm(tw, w2_s))) - 0.5
    d_ref[...] = jnp.exp(-jnp.exp(w)).reshape(d_ref.shape)
    ta = mm(xa_ref[...], a1_s).astype(bf16)
    a_ref[...] = jax.nn.sigmoid(a0_ref[...] + mm(ta, a2_s)).reshape(a_ref.shape)
    tg = jax.nn.sigmoid(mm(xg_ref[...], g1_s)).astype(bf16)
    g_ref[...] = mm(tg, g2_s)


def _rwkv_lora(xw, xa, xg, w1, w2, a1, a2, g1, g2, w0, a0):
    tm = 384
    full = lambda a: pl.BlockSpec(a.shape, lambda i: (0,) * a.ndim)
    row = pl.BlockSpec((tm, D), lambda i: (i, 0))
    row3 = pl.BlockSpec((tm, RW_Q, 128), lambda i: (i, 0, 0))
    ws = [w1, w2, a1, a2, g1, g2]
    return pl.pallas_call(
        _rwkv_lora_kernel,
        grid=(ROWS // tm,),
        in_specs=[row, row, row] + [full(a) for a in ws + [w0, a0]],
        out_specs=[row3, row3, row],
        out_shape=[_sds((ROWS, RW_Q, 128), f32)] * 2 + [_sds((ROWS, D), f32)],
        scratch_shapes=[pltpu.VMEM(a.shape, bf16) for a in ws],
        name="rwkv_lora",
        compiler_params=_cp(("arbitrary",)),
    )(xw, xa, xg, *ws, w0, a0)


def _rwkv_step(t, r_s, k_s, v_s, d_s, a_s, z_s, s_s, tmp, kk_p, ka_p, rk_p, lw_p, lb_p):
    kk_t, d_t, ka_t, k2_t, r_t = tmp
    r = r_s[t]
    k = k_s[t]
    v = v_s[t]
    a = a_s[t]
    kkr = k * kk_p[...]
    nrm = jnp.sqrt(jnp.sum(kkr * kkr, axis=0, keepdims=True))
    kk = kkr / jnp.maximum(nrm, 1e-12)
    k2 = k * (1.0 + (a - 1.0) * ka_p[...])
    kk_t[...] = kk
    d_t[...] = d_s[t]
    ka_t[...] = kk * a
    k2_t[...] = k2
    r_t[...] = r

    sa = -(s_s[0] * kk_t[0:1, :])
    for j in range(1, RW_N):
        sa = sa - s_s[j] * kk_t[j:j + 1, :]
    y = None
    for j in range(RW_N):
        sn = s_s[j] * d_t[j:j + 1, :] + sa * ka_t[j:j + 1, :] + v * k2_t[j:j + 1, :]
        s_s[j] = sn
        yj = sn * r_t[j:j + 1, :]
        y = yj if y is None else y + yj
    mean = jnp.mean(y, axis=0, keepdims=True)
    yc = y - mean
    var = jnp.mean(yc * yc, axis=0, keepdims=True)
    yn = yc * lax.rsqrt(var + RW_LN_EPS) * lw_p[...] + lb_p[...]
    bonus = jnp.sum(r * k2 * rk_p[...], axis=0, keepdims=True) * v
    z_s[t] = yn + bonus


def _rwkv_scan_kernel(r_ref, k_ref, v_ref, d_ref, a_ref, s0_ref, kk_p, ka_p, rk_p, lw_p, lb_p,
                      z_ref, sf_ref, s_s, *tmp):
    c = pl.program_id(1)

    @pl.when(c == 0)
    def _init():
        s_s[...] = s0_ref[...]

    def step(t, carry):
        _rwkv_step(t, r_ref, k_ref, v_ref, d_ref, a_ref, z_ref, s_s, tmp, kk_p, ka_p, rk_p, lw_p, lb_p)
        return carry

    lax.fori_loop(0, r_ref.shape[0], step, 0)

    @pl.when(c == pl.num_programs(1) - 1)
    def _final():
        sf_ref[...] = s_s[...]


RW_Q = D // 128


def _half_transpose(x):
    xt = x.T
    return jnp.concatenate([xt[0:RW_N], xt[RW_N:2 * RW_N]], axis=1)


def _rwkv_prompt_kernel(*refs):
    n_in = 5 * NB
    in_refs = refs[:n_in]
    params = refs[n_in:n_in + 5]
    zmain, zmeta, sf_ref = refs[n_in + 5:n_in + 8]
    scr = refs[n_in + 8:]
    seq, z_s, zo_s, s_s, tmp = scr[:5], scr[5], scr[6], scr[7], scr[8:]
    c = pl.program_id(0)

    @pl.when(c == 0)
    def _init():
        s_s[...] = jnp.zeros_like(s_s)
        z_s[...] = jnp.zeros_like(z_s)

    def load_token(t):
        rows = pl.ds(pl.multiple_of(t * RW_Q, RW_Q), RW_Q)
        for a in range(5):
            tile = jnp.concatenate([in_refs[a * NB + b][rows, :] for b in range(NB)], axis=0)
            seq[a][t] = _half_transpose(tile)

    load_token(0)

    def step(t, carry):
        tp = jnp.maximum(t - 1, 0)
        zo_s[tp] = _half_transpose(z_s[tp])
        _rwkv_step(t, *seq, z_s, s_s, tmp, *params)
        load_token(jnp.minimum(t + 1, RW_TS - 1))
        return carry

    lax.fori_loop(0, RW_TS, step, 0)
    zo_s[RW_TS - 1] = _half_transpose(z_s[RW_TS - 1])

    def write(z_ref):
        for t in range(RW_TS):
            for b in range(NB):
                z_ref[b, t * RW_Q:(t + 1) * RW_Q, :] = zo_s[t, b * RW_Q:(b + 1) * RW_Q, :]

    @pl.when(c == 0)
    def _write_meta():
        write(zmeta)

    @pl.when(c > 0)
    def _write_main():
        write(zmain)

    @pl.when(c == pl.num_programs(0) - 1)
    def _final():
        sf_ref[...] = s_s[...]


def _rwkv_prompt(r, k, v, dec, a, params):
    nc = LP // RW_TS
    blk = RW_TS * RW_Q
    arrs = [x.reshape(ROWS * RW_Q, 128) for x in (r, k, v, dec, a)]
    in_specs, args = [], []
    for x in arrs:
        for b in range(NB):
            in_specs.append(pl.BlockSpec(
                (blk, 128), lambda c, b=b: (jnp.where(c == 0, ROW_M // RW_TS + b, b * nc + c - 1), 0)))
            args.append(x)
    par = pl.BlockSpec((RW_N, 128), lambda c: (0, 0))
    outs = pl.pallas_call(
        _rwkv_prompt_kernel,
        grid=(nc + 1,),
        in_specs=in_specs + [par] * 5,
        out_specs=[pl.BlockSpec((NB, blk, 128), lambda c: (0, jnp.maximum(c - 1, 0), 0)),
                   pl.BlockSpec((NB, blk, 128), lambda c: (0, 0, 0)),
                   pl.BlockSpec((RW_N, RW_N, 128), lambda c: (0, 0, 0))],
        out_shape=[_sds((NB, LP * RW_Q, 128), f32), _sds((NB, NM * RW_Q, 128), f32),
                   _sds((RW_N, RW_N, 128), f32)],
        scratch_shapes=[pltpu.VMEM((RW_TS, RW_N, 128), f32)] * 7 + [pltpu.VMEM((RW_N, RW_N, 128), f32)]
                       + [pltpu.VMEM((RW_N, 128), f32)] * 5,
        name="rwkv_prompt",
        compiler_params=_cp(("arbitrary",)),
    )(*args, *params)
    return outs[0].reshape(ROWS_MAIN, RW_Q, 128), outs[1].reshape(NB * NM, D), outs[2]


def _rwkv_scan(rT, kT, vT, dT, aT, s0T, params, tc):
    L, _, lanes = rT.shape
    nl = lanes // 128
    seq = pl.BlockSpec((tc, RW_N, 128), lambda l, c: (c, 0, l))
    st = pl.BlockSpec((RW_N, RW_N, 128), lambda l, c: (0, 0, l))
    par = pl.BlockSpec((RW_N, 128), lambda l, c: (0, l))
    return pl.pallas_call(
        _rwkv_scan_kernel,
        grid=(nl, L // tc),
        in_specs=[seq] * 5 + [st] + [par] * 5,
        out_specs=[seq, st],
        out_shape=[_sds((L, RW_N, lanes), f32), _sds((RW_N, RW_N, lanes), f32)],
        scratch_shapes=[pltpu.VMEM((RW_N, RW_N, 128), f32)] + [pltpu.VMEM((RW_N, 128), f32)] * 5,
        name="rwkv_scan",
        compiler_params=_cp(("arbitrary", "arbitrary")),
    )(rT, kT, vT, dT, aT, s0T, *params)


def _rwkv_layer(h, norm_g, state, shift, j, mu, w_rkv, w0, w1, w2, a0, a1, a2, g1, g2, k_k, k_a, r_k,
                ln_w, ln_b, w_o):
    *xmix, shift_p, shift_s = _rwkv_premix(h, norm_g, mu[j], shift[j])
    tn = 1024
    common = dict(K=D, tm=768, tn=tn, n_row_tiles=ROWS // 768, n_col_tiles=D // tn, prologue=_pro_id,
                  epilogue=lambda accs, cvs, rms: (accs[0],))
    o = (_sds((ROWS, RW_Q, 128), f32), 0, 0)
    r, k, v = [_mm(name="rwkv_rkv", xs=[(xmix[n], 0)], ws=[(w_rkv, (j, n), 0)], outs=[o], **common)[0]
               for n in range(3)]

    def padc(w):
        return jnp.pad(w, ((0, 0), (0, LORA_PAD - w.shape[1])))

    def padr(w):
        return jnp.pad(w, ((0, LORA_PAD - w.shape[0]), (0, 0)))

    dec, a, g = _rwkv_lora(xmix[3], xmix[4], xmix[5], padc(w1[j]), padr(w2[j]), padc(a1[j]), padr(a2[j]),
                           g1[j], g2[j], w0[j].reshape(1, D), a0[j].reshape(1, D))

    def lanes_param(p, reps):
        return jnp.tile(p.reshape(RW_H, RW_N).T, (1, reps))

    def prompt_param(p):
        pt = p.reshape(RW_Q, 2, RW_N).transpose(2, 1, 0)
        return jnp.broadcast_to(pt[:, :, None, :], (RW_N, 2, NB, RW_Q)).reshape(RW_N, 128)

    def params(reps):
        return [lanes_param(k_k[j], reps), lanes_param(k_a[j], reps), lanes_param(r_k[j].reshape(D), reps),
                lanes_param(ln_w[j], reps), lanes_param(ln_b[j], reps)]

    z_main, z_meta, sT = _rwkv_prompt(r, k, v, dec, a, [prompt_param(p) for p in (
        k_k[j], k_a[j], r_k[j].reshape(D), ln_w[j], ln_b[j])])
    s_p = sT.reshape(RW_N, RW_N, 2, NB, RW_Q).transpose(3, 4, 2, 1, 0).reshape(NB, RW_H, RW_N, RW_N)

    def to_step(x):
        return x[ROW_S:ROW_S + NS].reshape(NS, RW_H, RW_N).transpose(2, 0, 1).reshape(1, RW_N, NS * RW_H)

    s0T = state[j].transpose(3, 2, 0, 1).reshape(RW_N, RW_N, NS * RW_H)
    zsT, ssT = _rwkv_scan(to_step(r), to_step(k), to_step(v), to_step(dec), to_step(a), s0T, params(NS), 1)
    s_s = ssT.reshape(RW_N, RW_N, NS, RW_H).transpose(2, 3, 1, 0)
    z_s = zsT.reshape(RW_N, NS, RW_H).transpose(1, 2, 0).reshape(NS, D)

    z_aux = _aux_rows(z_s, z_meta)
    segs = [([(z_main, 0), (g, 0)], MAIN_TM, ROWS_MAIN // MAIN_TM, 0),
            ([(z_aux, 0), (g, ROWS_MAIN // AUX)], AUX, 1, ROWS_MAIN // AUX)]
    h = _segments_out(h, segs, w_o, (j,), D, _pro_mul)
    return h, s_p, s_s, shift_p, shift_s


def kernel(x_prompt, x_sample, state_hgrn, state_lru_h, state_lru_conv, state_rwkv, state_rwkv_shift, meta_tokens, norm_mix, norm_ffn, norm_final, hgrn_w_in, hgrn_lb_logits, hgrn_norm, hgrn_w_out, lru_w_in, lru_conv_w, lru_conv_b, lru_gate_w, lru_gate_b, lru_lambda, lru_w_out, rwkv_mu, rwkv_w_rkv, rwkv_w0, rwkv_w1, rwkv_w2, rwkv_a0, rwkv_a1, rwkv_a2, rwkv_g1, rwkv_g2, rwkv_k_k, rwkv_k_a, rwkv_r_k, rwkv_ln_w, rwkv_ln_b, rwkv_w_o, ffn_w_in, ffn_w_out):
    depth = norm_mix.shape[0]
    x_aux = _aux_rows(x_sample.reshape(NS, D), jnp.tile(meta_tokens, (NB, 1)))
    h, xn0 = _assemble(x_prompt.reshape(ROWS_MAIN, D), x_aux, norm_mix[0].reshape(1, D))
    hg_p, lh_p, lh_s, lc_p, lc_s, rw_p, rw_s, rs_p, rs_s = [[] for _ in range(9)]
    hg_s = None
    for i in range(depth):
        m, j = i % 3, i // 3
        ng = norm_mix[i].reshape(1, D)
        if m == 0:
            h, sp, hg_s = _hgrn_layer(h, ng, state_hgrn, hg_s, j, hgrn_w_in, hgrn_lb_logits, hgrn_norm,
                                      hgrn_w_out, xn=xn0 if i == 0 else None)
            hg_p.append(sp)
        elif m == 1:
            h, hp_, hs_, cp_, cs_ = _lru_layer(h, ng, state_lru_h, state_lru_conv, j, lru_w_in, lru_conv_w,
                                               lru_conv_b, lru_gate_w, lru_gate_b, lru_lambda, lru_w_out)
            lh_p.append(hp_)
            lh_s.append(hs_)
            lc_p.append(cp_)
            lc_s.append(cs_)
        else:
            h, sp, ss, shp, shs = _rwkv_layer(h, ng, state_rwkv, state_rwkv_shift, j, rwkv_mu, rwkv_w_rkv,
                                              rwkv_w0, rwkv_w1, rwkv_w2, rwkv_a0, rwkv_a1, rwkv_a2, rwkv_g1,
                                              rwkv_g2, rwkv_k_k, rwkv_k_a, rwkv_r_k, rwkv_ln_w, rwkv_ln_b,
                                              rwkv_w_o)
            rw_p.append(sp)
            rw_s.append(ss)
            rs_p.append(shp)
            rs_s.append(shs)
        h = _ffn(h, norm_ffn[i].reshape(1, D), ffn_w_in, ffn_w_out, i)
    nf = norm_final.reshape(1, D)
    y_main = _rmsnorm(h, nf, out_dtype=f32, tm=512, row_tile0=0, n_row_tiles=ROWS_MAIN // 512)
    y_aux = _rmsnorm(h, nf, out_dtype=f32, tm=AUX, row_tile0=ROWS_MAIN // AUX, n_row_tiles=1)
    y_prompt = y_main.reshape(NB, LP, D)
    y_sample = y_aux[:NS].reshape(NS, 1, D)
    return (y_prompt, y_sample, jnp.stack(hg_p), hg_s, jnp.stack(lh_p), jnp.stack(lh_s),
            jnp.stack(lc_p), jnp.stack(lc_s), jnp.stack(rw_p), jnp.stack(rw_s), jnp.stack(rs_p), jnp.stack(rs_s))
```

```python
import functools

import jax
import jax.numpy as jnp
import numpy as np
from jax import lax
from jax.experimental import pallas as pl
from jax.experimental.pallas import tpu as pltpu

f32 = jnp.float32
bf16 = jnp.bfloat16

D = 2048
NB = 4
LP = 2048
NM = 16
NS = 128
ROWS_MAIN = NB * LP
ROW_S = ROWS_MAIN
ROW_M = ROW_S + NS
AUX = 256
ROWS = ROWS_MAIN + AUX
EPS = 1e-6
HG_H, HG_K = 16, 128
HG_C = 64
HG_UNROLL = 4
GATE_EXP_CLIP = 60.0
LRU_NBLK, LRU_BW = 8, 256
LRU_C = 8.0
LRU_T = 256
RW_H, RW_N = 32, 64
RW_LN_EPS = 64e-5
RW_TS = 16
PM_T = 256
MAIN_TM = 512
LORA_PAD = 128
D_FF = 5632
VMEM_LIMIT = 56 * 1024 * 1024


def _cp(sem):
    return pltpu.CompilerParams(dimension_semantics=sem, vmem_limit_bytes=VMEM_LIMIT)


def _softplus(x):
    return jnp.maximum(x, 0.0) + jnp.log(1.0 + jnp.exp(-jnp.abs(x)))


def _norm_kernel(h_ref, g_ref, o_ref):
    x = h_ref[...]
    ms = jnp.mean(x * x, axis=-1, keepdims=True)
    o_ref[...] = (x * lax.rsqrt(ms + EPS) * g_ref[...]).astype(o_ref.dtype)


def _rmsnorm(h, g, *, out_dtype, tm, row_tile0, n_row_tiles):
    return pl.pallas_call(
        _norm_kernel,
        grid=(n_row_tiles,),
        in_specs=[pl.BlockSpec((tm, D), lambda i: (i + row_tile0, 0)),
                  pl.BlockSpec((1, D), lambda i: (0, 0))],
        out_specs=pl.BlockSpec((tm, D), lambda i: (i, 0)),
        out_shape=jax.ShapeDtypeStruct((n_row_tiles * tm, D), out_dtype),
        name="rmsnorm",
        compiler_params=_cp(("arbitrary",)),
    )(h, g)


def _assemble_kernel(xm_ref, xa_ref, g_ref, h_ref, xn_ref):
    def emit(x):
        h_ref[...] = x
        ms = jnp.mean(x * x, axis=-1, keepdims=True)
        xn_ref[...] = (x * lax.rsqrt(ms + EPS) * g_ref[...]).astype(xn_ref.dtype)

    @pl.when(pl.program_id(0) < ROWS_MAIN // AUX)
    def _main():
        emit(xm_ref[...])

    @pl.when(pl.program_id(0) == ROWS_MAIN // AUX)
    def _aux():
        emit(xa_ref[...])


def _assemble(x_main, x_aux, g):
    nm = ROWS_MAIN // AUX
    return pl.pallas_call(
        _assemble_kernel,
        grid=(nm + 1,),
        in_specs=[pl.BlockSpec((AUX, D), lambda i: (jnp.minimum(i, nm - 1), 0)),
                  pl.BlockSpec((AUX, D), lambda i: (0, 0)),
                  pl.BlockSpec((1, D), lambda i: (0, 0))],
        out_specs=[pl.BlockSpec((AUX, D), lambda i: (i, 0)), pl.BlockSpec((AUX, D), lambda i: (i, 0))],
        out_shape=[_sds((ROWS, D), f32), _sds((ROWS, D), bf16)],
        name="assemble",
        compiler_params=_cp(("arbitrary",)),
    )(x_main, x_aux, g)


def _mm_body(*refs, n_x, n_w, n_kv, n_cv, n_rm, n_out, prologue, epilogue, row_splits):
    p = 0
    x_refs = refs[p:p + n_x]; p += n_x
    w_refs = refs[p:p + n_w]; p += n_w
    kv_refs = refs[p:p + n_kv]; p += n_kv
    cv_refs = refs[p:p + n_cv]; p += n_cv
    rm_refs = refs[p:p + n_rm]; p += n_rm
    out_refs = refs[p:p + n_out]; p += n_out
    w_scr = refs[p:p + n_w]

    @pl.when(pl.program_id(1) == 0)
    def _cast_weights():
        for w, s in zip(w_refs, w_scr):
            s[...] = w[...].astype(bf16)

    def flat(v):
        return v.reshape(v.shape[0], v.shape[1] * v.shape[2]) if v.ndim == 3 else v

    tm = out_refs[0].shape[0]
    sub = tm // row_splits
    for s0 in range(0, tm, sub):
        rows = slice(s0, s0 + sub)
        x = prologue([flat(r[rows]) for r in x_refs], [r[...] for r in kv_refs])
        accs = [jnp.dot(x, s[...], preferred_element_type=f32) for s in w_scr]
        res = epilogue(accs, [r[...] for r in cv_refs], [r[rows] for r in rm_refs])
        for o, r in zip(out_refs, res):
            o[rows] = r.astype(o.dtype).reshape((sub,) + o.shape[1:])


def _mm(*, name="proj", xs, ws, kvecs=(), cvecs=(), rmats=(), outs, K, tm, tn, n_row_tiles, n_col_tiles,
        prologue, epilogue, aliases=None, row_splits=1):
    in_specs, args = [], []
    for a, r0 in xs:
        if a.ndim == 3:
            in_specs.append(pl.BlockSpec((tm, K // 128, 128), lambda j, i, r0=r0: (i + r0, 0, 0)))
        else:
            in_specs.append(pl.BlockSpec((tm, K), lambda j, i, r0=r0: (i + r0, 0)))
        args.append(a)
    for a, lead, c0 in ws:
        nl = len(lead)
        in_specs.append(pl.BlockSpec((None,) * nl + (K, tn), lambda j, i, lead=lead, c0=c0: lead + (0, j + c0)))
        args.append(a)
    for a in kvecs:
        in_specs.append(pl.BlockSpec(a.shape, lambda j, i: (0, 0)))
        args.append(a)
    for a, c0 in cvecs:
        in_specs.append(pl.BlockSpec((a.shape[0], tn), lambda j, i, c0=c0: (0, j + c0)))
        args.append(a)
    for a, r0, c0 in rmats:
        in_specs.append(pl.BlockSpec((tm, tn), lambda j, i, r0=r0, c0=c0: (i + r0, j + c0)))
        args.append(a)
    out_specs = [pl.BlockSpec((tm, tn // 128, 128), lambda j, i, r0=r0, c0=c0: (i + r0, j + c0, 0))
                 if len(s.shape) == 3 else
                 pl.BlockSpec((tm, tn), lambda j, i, r0=r0, c0=c0: (i + r0, j + c0)) for s, r0, c0 in outs]
    body = functools.partial(_mm_body, n_x=len(xs), n_w=len(ws), n_kv=len(kvecs), n_cv=len(cvecs),
                             n_rm=len(rmats), n_out=len(outs), prologue=prologue, epilogue=epilogue,
                             row_splits=row_splits)
    res = pl.pallas_call(
        body,
        grid=(n_col_tiles, n_row_tiles),
        in_specs=in_specs,
        out_specs=out_specs,
        out_shape=[s for s, _, _ in outs],
        scratch_shapes=[pltpu.VMEM((K, tn), bf16) for _ in ws],
        input_output_aliases=aliases or {},
        name=name,
        compiler_params=_cp(("arbitrary", "arbitrary")),
    )(*args)
    return res


def _pro_id(xs, kvs):
    return xs[0]


def _pro_mul(xs, kvs):
    return (xs[0] * xs[1]).astype(bf16)


def _sds(shape, dtype):
    return jax.ShapeDtypeStruct(shape, dtype)


def _segments_out(h, segs, w, lead, K, prologue):
    tn = 1024
    for xs, tm, nrt, h0 in segs:
        h = _mm(name="mixer_out", xs=xs, ws=[(w, lead, 0)], rmats=[(h, h0, 0)],
                outs=[(_sds((ROWS, D), f32), h0, 0)], K=K, tm=tm, tn=tn, n_row_tiles=nrt, n_col_tiles=D // tn,
                prologue=prologue, epilogue=lambda accs, cvs, rms: (rms[0] + accs[0],),
                aliases={len(xs) + 1: 0})[0]
    return h


def _residual_out(h, x_main, x_aux, w, lead, K):
    return _segments_out(h, [([(x_main, 0)], MAIN_TM, ROWS_MAIN // MAIN_TM, 0),
                             ([(x_aux, 0)], AUX, 1, ROWS_MAIN // AUX)], w, lead, K, _pro_id)


def _ffn(h, norm_g, w_in, w_out, layer):
    xn = _rmsnorm(h, norm_g, out_dtype=bf16, tm=768, row_tile0=0, n_row_tiles=ROWS // 768)
    tn = 512
    nct = D_FF // tn
    act = _mm(name="ffn_in", xs=[(xn, 0)], ws=[(w_in, (layer,), 0), (w_in, (layer,), nct)],
              outs=[(_sds((ROWS, D_FF), bf16), 0, 0)],
              K=D, tm=768, tn=tn, n_row_tiles=ROWS // 768, n_col_tiles=nct,
              prologue=_pro_id,
              epilogue=lambda accs, cvs, rms: (jax.nn.silu(accs[0]) * accs[1],))[0]
    h = _mm(name="ffn_out", xs=[(act, 0)], ws=[(w_out, (layer,), 0)], rmats=[(h, 0, 0)],
            outs=[(_sds((ROWS, D), f32), 0, 0)],
            K=D_FF, tm=384, tn=tn, n_row_tiles=ROWS // 384, n_col_tiles=D // tn,
            prologue=_pro_id, epilogue=lambda accs, cvs, rms: (rms[0] + accs[0],),
            aliases={2: 0})[0]
    return h


def _hgrn_lower_bound(logits, j):
    m = jnp.max(logits, axis=0, keepdims=True)
    e = jnp.exp(logits - m)
    p = e / jnp.sum(e, axis=0, keepdims=True)
    cs = p[0:1]
    for r in range(1, j + 1):
        cs = cs + p[r:r + 1]
    return jnp.clip(cs - p[0:1], 0.0, 1.0)


def _hgrn_in(xn, w_in, lb_logits, j):
    tn = 1024
    nct = D // tn
    common = dict(K=D, tm=768, tn=tn, n_row_tiles=ROWS // 768, n_col_tiles=nct, prologue=_pro_id, row_splits=3)
    o = (_sds((ROWS, D), f32), 0, 0)
    q = _mm(xs=[(xn, 0)], ws=[(w_in, (j,), 0)], outs=[o],
            epilogue=lambda accs, cvs, rms: (jax.nn.silu(accs[0]),), **common)[0]

    def f_epi(accs, cvs, rms):
        f = accs[0]
        lb = _hgrn_lower_bound(cvs[0], j)
        t = jnp.exp(-jnp.abs(f))
        log_f = (-(jnp.maximum(-f, 0.0) + jnp.log(1.0 + t))
                 + jnp.log(1.0 + lb * jnp.exp(jnp.minimum(-f, GATE_EXP_CLIP))))
        k = (1.0 - lb) * (jnp.where(f >= 0.0, t, 1.0) / (1.0 + t))
        return log_f, k

    log_f, k = _mm(xs=[(xn, 0)], ws=[(w_in, (j,), nct)], cvecs=[(lb_logits, 0)], outs=[o, o],
                   epilogue=f_epi, **common)
    v = _mm(xs=[(xn, 0)], ws=[(w_in, (j,), 2 * nct)], outs=[o],
            epilogue=lambda accs, cvs, rms: (accs[0],), **common)[0]
    g = _mm(xs=[(xn, 0)], ws=[(w_in, (j,), 3 * nct)], outs=[o],
            epilogue=lambda accs, cvs, rms: (jax.nn.silu(accs[0]),), **common)[0]
    return q, k, v, log_f, g


def _split3(x):
    hi = x.astype(bf16)
    r = x - hi.astype(f32)
    mid = r.astype(bf16)
    lo = (r - mid.astype(f32)).astype(bf16)
    return hi, mid, lo


def _gla_consts(T):
    t = np.arange(T)[:, None]
    w = np.arange(T)[None, :]
    mats = [w <= t, w > t]
    masks = []
    s = T // 2
    while s >= 1:
        piece = t // (2 * s)
        lower = (t % (2 * s)) >= s
        ref = piece * 2 * s + s
        mats.append(np.where(lower, (w > ref) & (w <= t), (w > t) & (w <= ref)))
        upiece = w // (2 * s)
        ulower = (w % (2 * s)) >= s
        masks.append((piece == upiece) & lower & ~ulower)
        s //= 2
    masks.append(t == w)
    m = np.concatenate(mats, 0)
    eye = np.eye(HG_UNROLL, dtype=bool)
    gmasks = np.stack([np.kron(eye, pm) for pm in masks])
    return (jnp.asarray(np.concatenate([m, m, m], 1), dtype=bf16), jnp.asarray(gmasks, dtype=f32))


def _gla_block(q_ref, k_ref, v_ref, f_ref, g_ref, gn_ref, dm_ref, pm_ref, o_ref, st_ref, e_s, ql_s, kl_s, T):
    nlev = T.bit_length() - 1
    dn_t = (((1,), (1,)), ((), ()))
    dn_l = (((0,), (0,)), ((), ()))
    def factors(ls):
        e_s[0:(nlev + 2) * T, ls] = jnp.dot(dm_ref[...], jnp.concatenate(_split3(f_ref[:, ls]), axis=0),
                                            preferred_element_type=f32)
        q = q_ref[:, ls]
        k = k_ref[:, ls]
        ql_s[0, 0:T, ls] = (q * jnp.exp(e_s[0:T, ls])).astype(bf16)
        kl_s[0, 0:T, ls] = (k * jnp.exp(e_s[T:2 * T, ls])).astype(bf16)
        for l in range(nlev):
            ex = jnp.exp(e_s[(l + 2) * T:(l + 3) * T, ls])
            ql_s[l + 1, 0:T, ls] = (q * ex).astype(bf16)
            kl_s[l + 1, 0:T, ls] = (k * ex).astype(bf16)
        ql_s[nlev + 1, 0:T, ls] = q.astype(bf16)
        kl_s[nlev + 1, 0:T, ls] = k.astype(bf16)

    for h0 in range(0, HG_H, HG_UNROLL):
        factors(slice(h0 * HG_K, (h0 + HG_UNROLL) * HG_K))
        sls = [slice(h * HG_K, (h + 1) * HG_K) for h in range(h0, h0 + HG_UNROLL)]
        att = None
        for l in range(nlev + 1):
            a_l = lax.dot_general(jnp.concatenate([ql_s[l + 1, 0:T, sl] for sl in sls], axis=0),
                                  jnp.concatenate([kl_s[l + 1, 0:T, sl] for sl in sls], axis=0),
                                  dn_t, preferred_element_type=f32) * pm_ref[l]
            att = a_l if att is None else att + a_l
        vbs = [v_ref[:, sl].astype(bf16) for sl in sls]
        o_intra = jnp.dot(att.astype(bf16), jnp.concatenate(vbs, axis=0), preferred_element_type=f32)
        for i, sl in enumerate(sls):
            h = h0 + i
            st = st_ref[h]
            o = (lax.dot_general(ql_s[0, 0:T, sl], st.astype(bf16), dn_t, preferred_element_type=f32)
                 + o_intra[i * T:(i + 1) * T])
            upd = lax.dot_general(vbs[i], kl_s[0, 0:T, sl], dn_l, preferred_element_type=f32)
            st_ref[h] = st * jnp.exp(e_s[T - 1:T, sl]) + upd
            ms = jnp.mean(o * o, axis=-1, keepdims=True)
            on = o * lax.rsqrt(ms + EPS) * gn_ref[:, sl]
            o_ref[:, sl] = (on * g_ref[:, sl]).astype(o_ref.dtype)


def _gla_kernel(qm, km, vm, fm, gm, qa, ka, va, fa, ga, gn_ref, dmm, pmm, dma, pma,
                om_ref, oa_ref, s_ref, st_ref, e_s, ql_s, kl_s):
    c = pl.program_id(1)

    @pl.when(c == 0)
    def _meta():
        st_ref[...] = jnp.zeros_like(st_ref)
        _gla_block(qa, ka, va, fa, ga, gn_ref, dma, pma, oa_ref, st_ref, e_s, ql_s, kl_s, NM)

    @pl.when(c > 0)
    def _main():
        _gla_block(qm, km, vm, fm, gm, gn_ref, dmm, pmm, om_ref, st_ref, e_s, ql_s, kl_s, HG_C)

    @pl.when(c == pl.num_programs(1) - 1)
    def _final():
        for h in range(HG_H):
            s_ref[0, h] = st_ref[h].T


def _hgrn_prompt(q, k, v, log_f, g, gn):
    nc = LP // HG_C
    main_spec = pl.BlockSpec((HG_C, D), lambda b, c: (b * nc + jnp.maximum(c - 1, 0), 0))
    meta_spec = pl.BlockSpec((NM, D), lambda b, c: (ROW_M // NM + b, 0))
    arrs = [q, k, v, log_f, g]
    consts = [*_gla_consts(HG_C), *_gla_consts(NM)]
    nslot = HG_C.bit_length() + 1
    return pl.pallas_call(
        _gla_kernel,
        grid=(NB, nc + 1),
        in_specs=[main_spec] * 5 + [meta_spec] * 5 + [pl.BlockSpec((1, D), lambda b, c: (0, 0))]
                 + [pl.BlockSpec(a.shape, lambda b, c, n=a.ndim: (0,) * n) for a in consts],
        out_specs=[pl.BlockSpec((HG_C, D), lambda b, c: (b * nc + jnp.maximum(c - 1, 0), 0)),
                   pl.BlockSpec((NM, D), lambda b, c: (b, 0)),
                   pl.BlockSpec((1, HG_H, HG_K, HG_K), lambda b, c: (b, 0, 0, 0))],
        out_shape=[_sds((ROWS_MAIN, D), bf16), _sds((NB * NM, D), bf16), _sds((NB, HG_H, HG_K, HG_K), f32)],
        scratch_shapes=[pltpu.VMEM((HG_H, HG_K, HG_K), f32),
                        pltpu.VMEM((nslot * HG_C, D), f32),
                        pltpu.VMEM((nslot, HG_C, D), bf16),
                        pltpu.VMEM((nslot, HG_C, D), bf16)],
        name="hgrn_prompt",
        compiler_params=_cp(("arbitrary", "arbitrary")),
    )(*arrs, *arrs, gn, *consts)


HG_BT = 8


def _hgrn_dec_kernel(s_ref, qT, kT, fT, v_ref, g_ref, gn_ref, e3_ref, *rest):
    so_ref, o_ref, qb_s, kb_s, gb_s = rest[-5:]

    def spread(m, dst):
        dst[...] = jnp.dot(jnp.concatenate(_split3(m), axis=1), e3_ref[...], preferred_element_type=f32)

    def per_b(bb, carry):
        spread(qT[bb], qb_s)
        spread(kT[bb], kb_s)
        spread(jnp.exp(fT[bb]), gb_s)
        vall = v_ref[pl.ds(bb, 1), :]
        outs = []
        for h in range(HG_H):
            hs = slice(h * HG_K, (h + 1) * HG_K)
            sn = gb_s[:, hs] * s_ref[bb, h] + kb_s[:, hs] * vall[:, hs]
            so_ref[bb, h] = sn
            o = jnp.sum(qb_s[:, hs] * sn, axis=0, keepdims=True)
            ms = jnp.mean(o * o, axis=-1, keepdims=True)
            outs.append(o * lax.rsqrt(ms + EPS))
        on = jnp.concatenate(outs, axis=1) * gn_ref[...]
        o_ref[pl.ds(bb, 1), :] = on * g_ref[pl.ds(bb, 1), :]
        return carry

    lax.fori_loop(0, HG_BT, per_b, 0)


def _hgrn_sample(state, j, q, k, v, log_f, g, gn, new_states):
    def colform(x):
        return x[ROW_S:ROW_S + NS].reshape(NS, HG_H, HG_K).transpose(0, 2, 1)

    head_of_lane = np.arange(D) // HG_K
    sel = (np.arange(HG_H)[:, None] == head_of_lane[None, :])
    e3 = jnp.asarray(np.concatenate([sel, sel, sel], axis=0), dtype=bf16)
    col_spec = pl.BlockSpec((HG_BT, HG_K, HG_H), lambda i: (i, 0, 0))
    row_spec = pl.BlockSpec((HG_BT, D), lambda i: (ROW_S // HG_BT + i, 0))
    st_spec = pl.BlockSpec((None, HG_BT, HG_H, HG_K, HG_K), lambda i: (j, i, 0, 0, 0))
    in_specs = [st_spec, col_spec, col_spec, col_spec, row_spec, row_spec,
                pl.BlockSpec((1, D), lambda i: (0, 0)), pl.BlockSpec((3 * HG_H, D), lambda i: (0, 0))]
    args = [state, colform(q), colform(k), colform(log_f), v, g, gn, e3]
    aliases = {}
    if new_states is not None:
        in_specs.append(pl.BlockSpec(memory_space=pl.ANY))
        args.append(new_states)
        aliases = {len(args) - 1: 0}
    return pl.pallas_call(
        _hgrn_dec_kernel,
        grid=(NS // HG_BT,),
        in_specs=in_specs,
        out_specs=[st_spec, pl.BlockSpec((HG_BT, D), lambda i: (i, 0))],
        out_shape=[_sds(state.shape, f32), _sds((NS, D), f32)],
        scratch_shapes=[pltpu.VMEM((HG_K, D), f32)] * 3,
        input_output_aliases=aliases,
        name="hgrn_decode",
        compiler_params=_cp(("arbitrary",)),
    )(*args)


def _aux_rows(sample_rows, meta_rows):
    pad = jnp.zeros((AUX - NS - NB * NM, sample_rows.shape[1]), sample_rows.dtype)
    return jnp.concatenate([sample_rows, meta_rows.astype(sample_rows.dtype), pad], axis=0)


def _hgrn_layer(h, norm_g, state, new_states, j, w_in, lb_logits, gnorm, w_out, xn=None):
    if xn is None:
        xn = _rmsnorm(h, norm_g, out_dtype=bf16, tm=768, row_tile0=0, n_row_tiles=ROWS // 768)
    q, k, v, log_f, g = _hgrn_in(xn, w_in, lb_logits, j)
    gn = gnorm[j].reshape(1, D)
    o_main, o_meta, s_p = _hgrn_prompt(q, k, v, log_f, g, gn)
    s_s, o_s = _hgrn_sample(state, j, q, k, v, log_f, g, gn, new_states)
    o_aux = _aux_rows(o_s.astype(bf16), o_meta)
    h = _residual_out(h, o_main, o_aux, w_out, (j,), D)
    return h, s_p, s_s


def _lru_in(xn, w_in, j):
    tn = 1024
    nct = D // tn
    common = dict(K=D, tm=768, tn=tn, n_row_tiles=ROWS // 768, n_col_tiles=nct, prologue=_pro_id, row_splits=3)
    o = (_sds((ROWS, D), f32), 0, 0)
    y = _mm(xs=[(xn, 0)], ws=[(w_in, (j,), 0)], outs=[o],
            epilogue=lambda accs, cvs, rms: (jax.nn.gelu(accs[0], approximate=True),), **common)[0]
    x = _mm(xs=[(xn, 0)], ws=[(w_in, (j,), nct)], outs=[o],
            epilogue=lambda accs, cvs, rms: (accs[0],), **common)[0]
    return y, x


def _lru_gates(xc, gw_ref, gb_ref, lam_ref, n):
    ls = slice(n * LRU_BW, (n + 1) * LRU_BW)
    xb = xc[:, ls].astype(bf16)
    r = jax.nn.sigmoid(jnp.dot(xb, gw_ref[0, n], preferred_element_type=f32) + gb_ref[0:1, ls])
    ig = jax.nn.sigmoid(jnp.dot(xb, gw_ref[1, n], preferred_element_type=f32) + gb_ref[1:2, ls])
    log_a = -LRU_C * r * _softplus(-lam_ref[:, ls])
    a = jnp.exp(log_a)
    mult = jnp.sqrt(jnp.maximum(1.0 - a * a, 0.0))
    return a, ig, mult


def _lru_conv(cb_ref, cw_ref, x0, x1, x2, x3):
    xc = cb_ref[...] + x0 * cw_ref[0:1, :]
    xc = xc + x1 * cw_ref[1:2, :]
    xc = xc + x2 * cw_ref[2:3, :]
    return xc + x3 * cw_ref[3:4, :]


def _lru_block(x_ref, y_ref, o_ref, cw_ref, cb_ref, gw_s, gb_ref, lam_ref, xbuf, a_s, b_s, hcar, T, first):
    xbuf[8:8 + T, :] = x_ref[...]
    xc = _lru_conv(cb_ref, cw_ref, xbuf[5:5 + T, :], xbuf[6:6 + T, :], xbuf[7:7 + T, :], xbuf[8:8 + T, :])
    xbuf[0:8, :] = xbuf[T:T + 8, :]
    row = lax.broadcasted_iota(jnp.int32, (T, 1), 0)
    for n in range(LRU_NBLK):
        ls = slice(n * LRU_BW, (n + 1) * LRU_BW)
        a, ig, mult = _lru_gates(xc, gw_s, gb_ref, lam_ref, n)
        if first:
            mult = jnp.where(row == 0, 1.0, mult)
        a_s[0:T, ls] = a
        b_s[0:T, ls] = xc[:, ls] * ig * mult

    def step(t, hprev):
        hnew = a_s[pl.ds(t, 1), :] * hprev + b_s[pl.ds(t, 1), :]
        b_s[pl.ds(t, 1), :] = hnew
        return hnew

    hcar[...] = lax.fori_loop(0, T, step, hcar[...], unroll=8)
    o_ref[...] = (b_s[0:T, :] * y_ref[...]).astype(o_ref.dtype)


def _lru_kernel(xm, ym, xa, ya, cw_ref, cb_ref, gw_ref, gb_ref, lam_ref,
                om_ref, oa_ref, hl_ref, cv_ref, gw_s, xbuf, a_s, b_s, hcar):
    c = pl.program_id(1)

    @pl.when((pl.program_id(0) == 0) & (c == 0))
    def _cast():
        gw_s[...] = gw_ref[...].astype(bf16)

    @pl.when(c == 0)
    def _meta():
        xbuf[0:8, :] = jnp.zeros((8, D), f32)
        hcar[...] = jnp.zeros_like(hcar)
        _lru_block(xa, ya, oa_ref, cw_ref, cb_ref, gw_s, gb_ref, lam_ref, xbuf, a_s, b_s, hcar, NM, True)

    @pl.when(c > 0)
    def _main():
        _lru_block(xm, ym, om_ref, cw_ref, cb_ref, gw_s, gb_ref, lam_ref, xbuf, a_s, b_s, hcar, LRU_T, False)

    @pl.when(c == pl.num_programs(1) - 1)
    def _final():
        hl_ref[0] = hcar[...]
        cv_ref[0] = xbuf[5:8, :]


def _lru_prompt(x, y, j, conv_w, conv_b, gate_w, gate_b, lam):
    nc = LP // LRU_T
    main_spec = pl.BlockSpec((LRU_T, D), lambda b, c: (b * nc + jnp.maximum(c - 1, 0), 0))
    meta_spec = pl.BlockSpec((NM, D), lambda b, c: (ROW_M // NM + b, 0))
    return pl.pallas_call(
        _lru_kernel,
        grid=(NB, nc + 1),
        in_specs=[main_spec, main_spec, meta_spec, meta_spec,
                  pl.BlockSpec((None, 4, D), lambda b, c: (j, 0, 0)),
                  pl.BlockSpec((1, D), lambda b, c: (j, 0)),
                  pl.BlockSpec((None, 2, LRU_NBLK, LRU_BW, LRU_BW), lambda b, c: (j, 0, 0, 0, 0)),
                  pl.BlockSpec((None, 2, D), lambda b, c: (j, 0, 0)),
                  pl.BlockSpec((1, D), lambda b, c: (j, 0))],
        out_specs=[pl.BlockSpec((LRU_T, D), lambda b, c: (b * nc + jnp.maximum(c - 1, 0), 0)),
                   pl.BlockSpec((NM, D), lambda b, c: (b, 0)),
                   pl.BlockSpec((1, 1, D), lambda b, c: (b, 0, 0)),
                   pl.BlockSpec((1, 3, D), lambda b, c: (b, 0, 0))],
        out_shape=[_sds((ROWS_MAIN, D), bf16), _sds((NB * NM, D), bf16),
                   _sds((NB, 1, D), f32), _sds((NB, 3, D), f32)],
        scratch_shapes=[pltpu.VMEM((2, LRU_NBLK, LRU_BW, LRU_BW), bf16),
                        pltpu.VMEM((LRU_T + 8, D), f32), pltpu.VMEM((LRU_T, D), f32),
                        pltpu.VMEM((LRU_T, D), f32), pltpu.VMEM((1, D), f32)],
        name="lru_prompt",
        compiler_params=_cp(("arbitrary", "arbitrary")),
    )(x, y, x, y, conv_w, conv_b, gate_w, gate_b, lam)


LRU_ST = 64


def _lru_dec_kernel(x_ref, y_ref, h0_ref, cbuf_ref, cw_ref, cb_ref, gw_ref, gb_ref, lam_ref,
                    o_ref, hn_ref, cn_ref, gw_s):
    gw_s[...] = gw_ref[...].astype(bf16)
    x = x_ref[...]
    b0 = cbuf_ref[:, 0, :]
    b1 = cbuf_ref[:, 1, :]
    b2 = cbuf_ref[:, 2, :]
    xc = _lru_conv(cb_ref, cw_ref, b0, b1, b2, x)
    cn_ref[:, 0, :] = b1
    cn_ref[:, 1, :] = b2
    cn_ref[:, 2, :] = x
    for n in range(LRU_NBLK):
        ls = slice(n * LRU_BW, (n + 1) * LRU_BW)
        a, ig, mult = _lru_gates(xc, gw_s, gb_ref, lam_ref, n)
        hn = a * h0_ref[:, ls] + xc[:, ls] * ig * mult
        hn_ref[:, ls] = hn
        o_ref[:, ls] = (hn * y_ref[:, ls]).astype(o_ref.dtype)


def _lru_sample(x, y, j, h0, cbuf, conv_w, conv_b, gate_w, gate_b, lam):
    row_spec = pl.BlockSpec((LRU_ST, D), lambda i: (ROW_S // LRU_ST + i, 0))
    return pl.pallas_call(
        _lru_dec_kernel,
        grid=(NS // LRU_ST,),
        in_specs=[row_spec, row_spec,
                  pl.BlockSpec((None, LRU_ST, D), lambda i: (j, i, 0)),
                  pl.BlockSpec((None, LRU_ST, 3, D), lambda i: (j, i, 0, 0)),
                  pl.BlockSpec((None, 4, D), lambda i: (j, 0, 0)),
                  pl.BlockSpec((1, D), lambda i: (j, 0)),
                  pl.BlockSpec((None, 2, LRU_NBLK, LRU_BW, LRU_BW), lambda i: (j, 0, 0, 0, 0)),
                  pl.BlockSpec((None, 2, D), lambda i: (j, 0, 0)),
                  pl.BlockSpec((1, D), lambda i: (j, 0))],
        out_specs=[pl.BlockSpec((LRU_ST, D), lambda i: (i, 0)),
                   pl.BlockSpec((LRU_ST, D), lambda i: (i, 0)),
                   pl.BlockSpec((LRU_ST, 3, D), lambda i: (i, 0, 0))],
        out_shape=[_sds((NS, D), bf16), _sds((NS, D), f32), _sds((NS, 3, D), f32)],
        scratch_shapes=[pltpu.VMEM((2, LRU_NBLK, LRU_BW, LRU_BW), bf16)],
        name="lru_decode",
        compiler_params=_cp(("arbitrary",)),
    )(x, y, h0, cbuf, conv_w, conv_b, gate_w, gate_b, lam)


def _lru_layer(h, norm_g, state_h, state_conv, j, w_in, conv_w, conv_b, gate_w, gate_b, lam, w_out):
    xn = _rmsnorm(h, norm_g, out_dtype=bf16, tm=768, row_tile0=0, n_row_tiles=ROWS // 768)
    y, x = _lru_in(xn, w_in, j)
    o_main, o_meta, hl_p, cv_p = _lru_prompt(x, y, j, conv_w, conv_b, gate_w, gate_b, lam)
    o_s, hl_s, cv_s = _lru_sample(x, y, j, state_h, state_conv, conv_w, conv_b, gate_w, gate_b, lam)
    o_aux = _aux_rows(o_s, o_meta)
    h = _residual_out(h, o_main, o_aux, w_out, (j,), D)
    return h, hl_p.reshape(NB, D), hl_s, cv_p, cv_s


def _rwkv_premix_kernel(h_ref, g_ref, mu_ref, sh_ref, *refs):
    x_refs = refs[:6]
    sp_ref, ss_ref, xbuf, meta_last = refs[6:]
    i = pl.program_id(0)
    x = h_ref[...]
    ms = jnp.mean(x * x, axis=-1, keepdims=True)
    xn = x * lax.rsqrt(ms + EPS) * g_ref[...]
    xbuf[8:8 + PM_T, :] = xn
    row = lax.broadcasted_iota(jnp.int32, (PM_T, 1), 0)

    @pl.when(i == 0)
    def _aux():
        xbuf[7:8, :] = jnp.zeros((1, D), f32)
        ss_ref[...] = xn[0:NS]
        for b in range(NB):
            meta_last[b:b + 1, :] = xn[NS + b * NM + NM - 1:NS + b * NM + NM]

    @pl.when(i > 0)
    def _main():
        m = i - 1
        b = m // (LP // PM_T)

        @pl.when(m % (LP // PM_T) == 0)
        def _start():
            xbuf[7:8, :] = meta_last[pl.ds(b, 1), :]

        @pl.when(m % (LP // PM_T) == LP // PM_T - 1)
        def _end():
            sp_ref[pl.ds(b, 1), :] = xn[PM_T - 1:PM_T]

    shifted = xbuf[7:7 + PM_T, :]
    is_meta = (row >= NS) & (row < NS + NB * NM) & ((row - NS) % NM != 0)
    sh_pad = jnp.concatenate([sh_ref[...], jnp.zeros((PM_T - NS, D), f32)], axis=0)
    prev_aux = jnp.where(row < NS, sh_pad, jnp.where(is_meta, shifted, 0.0))
    prev = jnp.where(i == 0, prev_aux, shifted)
    dx = prev - xn
    for n in range(6):
        x_refs[n][...] = (xn + dx * mu_ref[n:n + 1, :]).astype(bf16)
    xbuf[7:8, :] = xn[PM_T - 1:PM_T]


def _rwkv_premix(h, norm_g, mu_j, shift_j):
    nt = ROWS // PM_T
    rows = lambda i: (jnp.where(i == 0, nt - 1, i - 1), 0)
    return pl.pallas_call(
        _rwkv_premix_kernel,
        grid=(nt,),
        in_specs=[pl.BlockSpec((PM_T, D), rows),
                  pl.BlockSpec((1, D), lambda i: (0, 0)),
                  pl.BlockSpec((6, D), lambda i: (0, 0)),
                  pl.BlockSpec((NS, D), lambda i: (0, 0))],
        out_specs=[pl.BlockSpec((PM_T, D), rows)] * 6 + [pl.BlockSpec((NB, D), lambda i: (0, 0)),
                                                         pl.BlockSpec((NS, D), lambda i: (0, 0))],
        out_shape=[_sds((ROWS, D), bf16)] * 6 + [_sds((NB, D), f32), _sds((NS, D), f32)],
        scratch_shapes=[pltpu.VMEM((PM_T + 8, D), f32), pltpu.VMEM((8, D), f32)],
        name="rwkv_premix",
        compiler_params=_cp(("arbitrary",)),
    )(h, norm_g, mu_j, shift_j)


def _rwkv_lora_kernel(xw_ref, xa_ref, xg_ref, w1_ref, w2_ref, a1_ref, a2_ref, g1_ref, g2_ref, w0_ref, a0_ref,
                      d_ref, a_ref, g_ref, w1_s, w2_s, a1_s, a2_s, g1_s, g2_s):
    @pl.when(pl.program_id(0) == 0)
    def _cast():
        for src, dst in ((w1_ref, w1_s), (w2_ref, w2_s), (a1_ref, a1_s), (a2_ref, a2_s), (g1_ref, g1_s),
                         (g2_ref, g2_s)):
            dst[...] = src[...].astype(bf16)

    def mm(u, w_s):
        return jnp.dot(u, w_s[...], preferred_element_type=f32)

    tw = jnp.tanh(mm(xw_ref[...], w1_s)).astype(bf16)
    w = -_softplus(-(w0_ref[...] + mm(tw, w2_s))) - 0.5
    d_ref[...] = jnp.exp(-jnp.exp(w)).reshape(d_ref.shape)
    ta = mm(xa_ref[...], a1_s).astype(bf16)
    a_ref[...] = jax.nn.sigmoid(a0_ref[...] + mm(ta, a2_s)).reshape(a_ref.shape)
    tg = jax.nn.sigmoid(mm(xg_ref[...], g1_s)).astype(bf16)
    g_ref[...] = mm(tg, g2_s)


def _rwkv_lora(xw, xa, xg, w1, w2, a1, a2, g1, g2, w0, a0):
    tm = 384
    full = lambda a: pl.BlockSpec(a.shape, lambda i: (0,) * a.ndim)
    row = pl.BlockSpec((tm, D), lambda i: (i, 0))
    row3 = pl.BlockSpec((tm, RW_Q, 128), lambda i: (i, 0, 0))
    ws = [w1, w2, a1, a2, g1, g2]
    return pl.pallas_call(
        _rwkv_lora_kernel,
        grid=(ROWS // tm,),
        in_specs=[row, row, row] + [full(a) for a in ws + [w0, a0]],
        out_specs=[row3, row3, row],
        out_shape=[_sds((ROWS, RW_Q, 128), f32)] * 2 + [_sds((ROWS, D), f32)],
        scratch_shapes=[pltpu.VMEM(a.shape, bf16) for a in ws],
        name="rwkv_lora",
        compiler_params=_cp(("arbitrary",)),
    )(xw, xa, xg, *ws, w0, a0)


def _rwkv_step(t, r_s, k_s, v_s, d_s, a_s, z_s, s_s, tmp, kk_p, ka_p, rk_p, lw_p, lb_p):
    kk_t, d_t, ka_t, k2_t, r_t = tmp
    r = r_s[t]
    k = k_s[t]
    v = v_s[t]
    a = a_s[t]
    kkr = k * kk_p[...]
    nrm = jnp.sqrt(jnp.sum(kkr * kkr, axis=0, keepdims=True))
    kk = kkr / jnp.maximum(nrm, 1e-12)
    k2 = k * (1.0 + (a - 1.0) * ka_p[...])
    kk_t[...] = kk
    d_t[...] = d_s[t]
    ka_t[...] = kk * a
    k2_t[...] = k2
    r_t[...] = r

    sa = -(s_s[0] * kk_t[0:1, :])
    for j in range(1, RW_N):
        sa = sa - s_s[j] * kk_t[j:j + 1, :]
    y = None
    for j in range(RW_N):
        sn = s_s[j] * d_t[j:j + 1, :] + sa * ka_t[j:j + 1, :] + v * k2_t[j:j + 1, :]
        s_s[j] = sn
        yj = sn * r_t[j:j + 1, :]
        y = yj if y is None else y + yj
    mean = jnp.mean(y, axis=0, keepdims=True)
    yc = y - mean
    var = jnp.mean(yc * yc, axis=0, keepdims=True)
    yn = yc * lax.rsqrt(var + RW_LN_EPS) * lw_p[...] + lb_p[...]
    bonus = jnp.sum(r * k2 * rk_p[...], axis=0, keepdims=True) * v
    z_s[t] = yn + bonus


def _rwkv_scan_kernel(r_ref, k_ref, v_ref, d_ref, a_ref, s0_ref, kk_p, ka_p, rk_p, lw_p, lb_p,
                      z_ref, sf_ref, s_s, *tmp):
    c = pl.program_id(1)

    @pl.when(c == 0)
    def _init():
        s_s[...] = s0_ref[...]

    def step(t, carry):
        _rwkv_step(t, r_ref, k_ref, v_ref, d_ref, a_ref, z_ref, s_s, tmp, kk_p, ka_p, rk_p, lw_p, lb_p)
        return carry

    lax.fori_loop(0, r_ref.shape[0], step, 0)

    @pl.when(c == pl.num_programs(1) - 1)
    def _final():
        sf_ref[...] = s_s[...]


RW_Q = D // 128


def _half_transpose(x):
    xt = x.T
    return jnp.concatenate([xt[0:RW_N], xt[RW_N:2 * RW_N]], axis=1)


def _rwkv_prompt_kernel(*refs):
    n_in = 5 * NB
    in_refs = refs[:n_in]
    params = refs[n_in:n_in + 5]
    zmain, zmeta, sf_ref = refs[n_in + 5:n_in + 8]
    scr = refs[n_in + 8:]
    seq, z_s, zo_s, s_s, tmp = scr[:5], scr[5], scr[6], scr[7], scr[8:]
    c = pl.program_id(0)

    @pl.when(c == 0)
    def _init():
        s_s[...] = jnp.zeros_like(s_s)
        z_s[...] = jnp.zeros_like(z_s)

    def load_token(t):
        rows = pl.ds(pl.multiple_of(t * RW_Q, RW_Q), RW_Q)
        for a in range(5):
            tile = jnp.concatenate([in_refs[a * NB + b][rows, :] for b in range(NB)], axis=0)
            seq[a][t] = _half_transpose(tile)

    load_token(0)

    def step(t, carry):
        tp = jnp.maximum(t - 1, 0)
        zo_s[tp] = _half_transpose(z_s[tp])
        _rwkv_step(t, *seq, z_s, s_s, tmp, *params)
        load_token(jnp.minimum(t + 1, RW_TS - 1))
        return carry

    lax.fori_loop(0, RW_TS, step, 0)
    zo_s[RW_TS - 1] = _half_transpose(z_s[RW_TS - 1])

    def write(z_ref):
        for t in range(RW_TS):
            for b in range(NB):
                z_ref[b, t * RW_Q:(t + 1) * RW_Q, :] = zo_s[t, b * RW_Q:(b + 1) * RW_Q, :]

    @pl.when(c == 0)
    def _write_meta():
        write(zmeta)

    @pl.when(c > 0)
    def _write_main():
        write(zmain)

    @pl.when(c == pl.num_programs(0) - 1)
    def _final():
        sf_ref[...] = s_s[...]


def _rwkv_prompt(r, k, v, dec, a, params):
    nc = LP // RW_TS
    blk = RW_TS * RW_Q
    arrs = [x.reshape(ROWS * RW_Q, 128) for x in (r, k, v, dec, a)]
    in_specs, args = [], []
    for x in arrs:
        for b in range(NB):
            in_specs.append(pl.BlockSpec(
                (blk, 128), lambda c, b=b: (jnp.where(c == 0, ROW_M // RW_TS + b, b * nc + c - 1), 0)))
            args.append(x)
    par = pl.BlockSpec((RW_N, 128), lambda c: (0, 0))
    outs = pl.pallas_call(
        _rwkv_prompt_kernel,
        grid=(nc + 1,),
        in_specs=in_specs + [par] * 5,
        out_specs=[pl.BlockSpec((NB, blk, 128), lambda c: (0, jnp.maximum(c - 1, 0), 0)),
                   pl.BlockSpec((NB, blk, 128), lambda c: (0, 0, 0)),
                   pl.BlockSpec((RW_N, RW_N, 128), lambda c: (0, 0, 0))],
        out_shape=[_sds((NB, LP * RW_Q, 128), f32), _sds((NB, NM * RW_Q, 128), f32),
                   _sds((RW_N, RW_N, 128), f32)],
        scratch_shapes=[pltpu.VMEM((RW_TS, RW_N, 128), f32)] * 7 + [pltpu.VMEM((RW_N, RW_N, 128), f32)]
                       + [pltpu.VMEM((RW_N, 128), f32)] * 5,
        name="rwkv_prompt",
        compiler_params=_cp(("arbitrary",)),
    )(*args, *params)
    return outs[0].reshape(ROWS_MAIN, RW_Q, 128), outs[1].reshape(NB * NM, D), outs[2]


def _rwkv_scan(rT, kT, vT, dT, aT, s0T, params, tc):
    L, _, lanes = rT.shape
    nl = lanes // 128
    seq = pl.BlockSpec((tc, RW_N, 128), lambda l, c: (c, 0, l))
    st = pl.BlockSpec((RW_N, RW_N, 128), lambda l, c: (0, 0, l))
    par = pl.BlockSpec((RW_N, 128), lambda l, c: (0, l))
    return pl.pallas_call(
        _rwkv_scan_kernel,
        grid=(nl, L // tc),
        in_specs=[seq] * 5 + [st] + [par] * 5,
        out_specs=[seq, st],
        out_shape=[_sds((L, RW_N, lanes), f32), _sds((RW_N, RW_N, lanes), f32)],
        scratch_shapes=[pltpu.VMEM((RW_N, RW_N, 128), f32)] + [pltpu.VMEM((RW_N, 128), f32)] * 5,
        name="rwkv_scan",
        compiler_params=_cp(("arbitrary", "arbitrary")),
    )(rT, kT, vT, dT, aT, s0T, *params)


def _rwkv_layer(h, norm_g, state, shift, j, mu, w_rkv, w0, w1, w2, a0, a1, a2, g1, g2, k_k, k_a, r_k,
                ln_w, ln_b, w_o):
    *xmix, shift_p, shift_s = _rwkv_premix(h, norm_g, mu[j], shift[j])
    tn = 1024
    common = dict(K=D, tm=768, tn=tn, n_row_tiles=ROWS // 768, n_col_tiles=D // tn, prologue=_pro_id,
                  epilogue=lambda accs, cvs, rms: (accs[0],))
    o = (_sds((ROWS, RW_Q, 128), f32), 0, 0)
    r, k, v = [_mm(name="rwkv_rkv", xs=[(xmix[n], 0)], ws=[(w_rkv, (j, n), 0)], outs=[o], **common)[0]
               for n in range(3)]

    def padc(w):
        return jnp.pad(w, ((0, 0), (0, LORA_PAD - w.shape[1])))

    def padr(w):
        return jnp.pad(w, ((0, LORA_PAD - w.shape[0]), (0, 0)))

    dec, a, g = _rwkv_lora(xmix[3], xmix[4], xmix[5], padc(w1[j]), padr(w2[j]), padc(a1[j]), padr(a2[j]),
                           g1[j], g2[j], w0[j].reshape(1, D), a0[j].reshape(1, D))

    def lanes_param(p, reps):
        return jnp.tile(p.reshape(RW_H, RW_N).T, (1, reps))

    def prompt_param(p):
        pt = p.reshape(RW_Q, 2, RW_N).transpose(2, 1, 0)
        return jnp.broadcast_to(pt[:, :, None, :], (RW_N, 2, NB, RW_Q)).reshape(RW_N, 128)

    def params(reps):
        return [lanes_param(k_k[j], reps), lanes_param(k_a[j], reps), lanes_param(r_k[j].reshape(D), reps),
                lanes_param(ln_w[j], reps), lanes_param(ln_b[j], reps)]

    z_main, z_meta, sT = _rwkv_prompt(r, k, v, dec, a, [prompt_param(p) for p in (
        k_k[j], k_a[j], r_k[j].reshape(D), ln_w[j], ln_b[j])])
    s_p = sT.reshape(RW_N, RW_N, 2, NB, RW_Q).transpose(3, 4, 2, 1, 0).reshape(NB, RW_H, RW_N, RW_N)

    def to_step(x):
        return x[ROW_S:ROW_S + NS].reshape(NS, RW_H, RW_N).transpose(2, 0, 1).reshape(1, RW_N, NS * RW_H)

    s0T = state[j].transpose(3, 2, 0, 1).reshape(RW_N, RW_N, NS * RW_H)
    zsT, ssT = _rwkv_scan(to_step(r), to_step(k), to_step(v), to_step(dec), to_step(a), s0T, params(NS), 1)
    s_s = ssT.reshape(RW_N, RW_N, NS, RW_H).transpose(2, 3, 1, 0)
    z_s = zsT.reshape(RW_N, NS, RW_H).transpose(1, 2, 0).reshape(NS, D)

    z_aux = _aux_rows(z_s, z_meta)
    segs = [([(z_main, 0), (g, 0)], MAIN_TM, ROWS_MAIN // MAIN_TM, 0),
            ([(z_aux, 0), (g, ROWS_MAIN // AUX)], AUX, 1, ROWS_MAIN // AUX)]
    h = _segments_out(h, segs, w_o, (j,), D, _pro_mul)
    return h, s_p, s_s, shift_p, shift_s


def kernel(x_prompt, x_sample, state_hgrn, state_lru_h, state_lru_conv, state_rwkv, state_rwkv_shift, meta_tokens, norm_mix, norm_ffn, norm_final, hgrn_w_in, hgrn_lb_logits, hgrn_norm, hgrn_w_out, lru_w_in, lru_conv_w, lru_conv_b, lru_gate_w, lru_gate_b, lru_lambda, lru_w_out, rwkv_mu, rwkv_w_rkv, rwkv_w0, rwkv_w1, rwkv_w2, rwkv_a0, rwkv_a1, rwkv_a2, rwkv_g1, rwkv_g2, rwkv_k_k, rwkv_k_a, rwkv_r_k, rwkv_ln_w, rwkv_ln_b, rwkv_w_o, ffn_w_in, ffn_w_out):
    depth = norm_mix.shape[0]
    x_aux = _aux_rows(x_sample.reshape(NS, D), jnp.tile(meta_tokens, (NB, 1)))
    h, xn0 = _assemble(x_prompt.reshape(ROWS_MAIN, D), x_aux, norm_mix[0].reshape(1, D))
    hg_p, lh_p, lh_s, lc_p, lc_s, rw_p, rw_s, rs_p, rs_s = [[] for _ in range(9)]
    hg_s = None
    for i in range(depth):
        m, j = i % 3, i // 3
        ng = norm_mix[i].reshape(1, D)
        if m == 0:
            h, sp, hg_s = _hgrn_layer(h, ng, state_hgrn, hg_s, j, hgrn_w_in, hgrn_lb_logits, hgrn_norm,
                                      hgrn_w_out, xn=xn0 if i == 0 else None)
            hg_p.append(sp)
        elif m == 1:
            h, hp_, hs_, cp_, cs_ = _lru_layer(h, ng, state_lru_h, state_lru_conv, j, lru_w_in, lru_conv_w,
                                               lru_conv_b, lru_gate_w, lru_gate_b, lru_lambda, lru_w_out)
            lh_p.append(hp_)
            lh_s.append(hs_)
            lc_p.append(cp_)
            lc_s.append(cs_)
        else:
            h, sp, ss, shp, shs = _rwkv_layer(h, ng, state_rwkv, state_rwkv_shift, j, rwkv_mu, rwkv_w_rkv,
                                              rwkv_w0, rwkv_w1, rwkv_w2, rwkv_a0, rwkv_a1, rwkv_a2, rwkv_g1,
                                              rwkv_g2, rwkv_k_k, rwkv_k_a, rwkv_r_k, rwkv_ln_w, rwkv_ln_b,
                                              rwkv_w_o)
            rw_p.append(sp)
            rw_s.append(ss)
            rs_p.append(shp)
            rs_s.append(shs)
        h = _ffn(h, norm_ffn[i].reshape(1, D), ffn_w_in, ffn_w_out, i)
    nf = norm_final.reshape(1, D)
    y_main = _rmsnorm(h, nf, out_dtype=f32, tm=512, row_tile0=0, n_row_tiles=ROWS_MAIN // 512)
    y_aux = _rmsnorm(h, nf, out_dtype=f32, tm=AUX, row_tile0=ROWS_MAIN // AUX, n_row_tiles=1)
    y_prompt = y_main.reshape(NB, LP, D)
    y_sample = y_aux[:NS].reshape(NS, 1, D)
    return (y_prompt, y_sample, jnp.stack(hg_p), hg_s, jnp.stack(lh_p), jnp.stack(lh_s),
            jnp.stack(lc_p), jnp.stack(lc_s), jnp.stack(rw_p), jnp.stack(rw_s), jnp.stack(rs_p), jnp.stack(rs_s))
```

```python
import functools

import jax
import jax.numpy as jnp
import numpy as np
from jax import lax
from jax.experimental import pallas as pl
from jax.experimental.pallas import tpu as pltpu

f32 = jnp.float32
bf16 = jnp.bfloat16

D = 2048
NB = 4
LP = 2048
NM = 16
NS = 128
ROWS_MAIN = NB * LP
ROW_S = ROWS_MAIN
ROW_M = ROW_S + NS
AUX = 256
ROWS = ROWS_MAIN + AUX
EPS = 1e-6
HG_H, HG_K = 16, 128
HG_C = 64
HG_UNROLL = 4
GATE_EXP_CLIP = 60.0
LRU_NBLK, LRU_BW = 8, 256
LRU_C = 8.0
LRU_T = 256
RW_H, RW_N = 32, 64
RW_LN_EPS = 64e-5
RW_TS = 16
PM_T = 256
LORA_PAD = 128
D_FF = 5632
VMEM_LIMIT = 56 * 1024 * 1024


def _cp(sem):
    return pltpu.CompilerParams(dimension_semantics=sem, vmem_limit_bytes=VMEM_LIMIT)


def _softplus(x):
    return jnp.maximum(x, 0.0) + jnp.log(1.0 + jnp.exp(-jnp.abs(x)))


def _norm_kernel(h_ref, g_ref, o_ref):
    x = h_ref[...]
    ms = jnp.mean(x * x, axis=-1, keepdims=True)
    o_ref[...] = (x * lax.rsqrt(ms + EPS) * g_ref[...]).astype(o_ref.dtype)


def _rmsnorm(h, g, *, out_dtype, tm, row_tile0, n_row_tiles):
    return pl.pallas_call(
        _norm_kernel,
        grid=(n_row_tiles,),
        in_specs=[pl.BlockSpec((tm, D), lambda i: (i + row_tile0, 0)),
                  pl.BlockSpec((1, D), lambda i: (0, 0))],
        out_specs=pl.BlockSpec((tm, D), lambda i: (i, 0)),
        out_shape=jax.ShapeDtypeStruct((n_row_tiles * tm, D), out_dtype),
        name="rmsnorm",
        compiler_params=_cp(("arbitrary",)),
    )(h, g)


def _assemble_kernel(xm_ref, xa_ref, g_ref, h_ref, xn_ref):
    def emit(x):
        h_ref[...] = x
        ms = jnp.mean(x * x, axis=-1, keepdims=True)
        xn_ref[...] = (x * lax.rsqrt(ms + EPS) * g_ref[...]).astype(xn_ref.dtype)

    @pl.when(pl.program_id(0) < ROWS_MAIN // AUX)
    def _main():
        emit(xm_ref[...])

    @pl.when(pl.program_id(0) == ROWS_MAIN // AUX)
    def _aux():
        emit(xa_ref[...])


def _assemble(x_main, x_aux, g):
    nm = ROWS_MAIN // AUX
    return pl.pallas_call(
        _assemble_kernel,
        grid=(nm + 1,),
        in_specs=[pl.BlockSpec((AUX, D), lambda i: (jnp.minimum(i, nm - 1), 0)),
                  pl.BlockSpec((AUX, D), lambda i: (0, 0)),
                  pl.BlockSpec((1, D), lambda i: (0, 0))],
        out_specs=[pl.BlockSpec((AUX, D), lambda i: (i, 0)), pl.BlockSpec((AUX, D), lambda i: (i, 0))],
        out_shape=[_sds((ROWS, D), f32), _sds((ROWS, D), bf16)],
        name="assemble",
        compiler_params=_cp(("arbitrary",)),
    )(x_main, x_aux, g)


def _mm_body(*refs, n_x, n_w, n_kv, n_cv, n_rm, n_out, prologue, epilogue, row_splits):
    p = 0
    x_refs = refs[p:p + n_x]; p += n_x
    w_refs = refs[p:p + n_w]; p += n_w
    kv_refs = refs[p:p + n_kv]; p += n_kv
    cv_refs = refs[p:p + n_cv]; p += n_cv
    rm_refs = refs[p:p + n_rm]; p += n_rm
    out_refs = refs[p:p + n_out]; p += n_out
    w_scr = refs[p:p + n_w]

    @pl.when(pl.program_id(1) == 0)
    def _cast_weights():
        for w, s in zip(w_refs, w_scr):
            s[...] = w[...].astype(bf16)

    def flat(v):
        return v.reshape(v.shape[0], v.shape[1] * v.shape[2]) if v.ndim == 3 else v

    tm = out_refs[0].shape[0]
    sub = tm // row_splits
    for s0 in range(0, tm, sub):
        rows = slice(s0, s0 + sub)
        x = prologue([flat(r[rows]) for r in x_refs], [r[...] for r in kv_refs])
        accs = [jnp.dot(x, s[...], preferred_element_type=f32) for s in w_scr]
        res = epilogue(accs, [r[...] for r in cv_refs], [r[rows] for r in rm_refs])
        for o, r in zip(out_refs, res):
            o[rows] = r.astype(o.dtype).reshape((sub,) + o.shape[1:])


def _mm(*, name="proj", xs, ws, kvecs=(), cvecs=(), rmats=(), outs, K, tm, tn, n_row_tiles, n_col_tiles,
        prologue, epilogue, aliases=None, row_splits=1):
    in_specs, args = [], []
    for a, r0 in xs:
        if a.ndim == 3:
            in_specs.append(pl.BlockSpec((tm, K // 128, 128), lambda j, i, r0=r0: (i + r0, 0, 0)))
        else:
            in_specs.append(pl.BlockSpec((tm, K), lambda j, i, r0=r0: (i + r0, 0)))
        args.append(a)
    for a, lead, c0 in ws:
        nl = len(lead)
        in_specs.append(pl.BlockSpec((None,) * nl + (K, tn), lambda j, i, lead=lead, c0=c0: lead + (0, j + c0)))
        args.append(a)
    for a in kvecs:
        in_specs.append(pl.BlockSpec(a.shape, lambda j, i: (0, 0)))
        args.append(a)
    for a, c0 in cvecs:
        in_specs.append(pl.BlockSpec((a.shape[0], tn), lambda j, i, c0=c0: (0, j + c0)))
        args.append(a)
    for a, r0, c0 in rmats:
        in_specs.append(pl.BlockSpec((tm, tn), lambda j, i, r0=r0, c0=c0: (i + r0, j + c0)))
        args.append(a)
    out_specs = [pl.BlockSpec((tm, tn // 128, 128), lambda j, i, r0=r0, c0=c0: (i + r0, j + c0, 0))
                 if len(s.shape) == 3 else
                 pl.BlockSpec((tm, tn), lambda j, i, r0=r0, c0=c0: (i + r0, j + c0)) for s, r0, c0 in outs]
    body = functools.partial(_mm_body, n_x=len(xs), n_w=len(ws), n_kv=len(kvecs), n_cv=len(cvecs),
                             n_rm=len(rmats), n_out=len(outs), prologue=prologue, epilogue=epilogue,
                             row_splits=row_splits)
    res = pl.pallas_call(
        body,
        grid=(n_col_tiles, n_row_tiles),
        in_specs=in_specs,
        out_specs=out_specs,
        out_shape=[s for s, _, _ in outs],
        scratch_shapes=[pltpu.VMEM((K, tn), bf16) for _ in ws],
        input_output_aliases=aliases or {},
        name=name,
        compiler_params=_cp(("arbitrary", "arbitrary")),
    )(*args)
    return res


def _pro_id(xs, kvs):
    return xs[0]


def _pro_mul(xs, kvs):
    return (xs[0] * xs[1]).astype(bf16)


def _sds(shape, dtype):
    return jax.ShapeDtypeStruct(shape, dtype)


def _mixer_out_kernel(*refs, n_u, prologue):
    xm_ref, xa_ref = refs[:2]
    u_refs = refs[2:2 + n_u]
    w_ref, h_ref, g_ref, ho_ref, xn_ref, w_s = refs[2 + n_u:]
    i = pl.program_id(0)

    @pl.when(i == 0)
    def _cast():
        w_s[...] = w_ref[...].astype(bf16)

    def run(x_ref):
        x = x_ref[...]
        if x.ndim == 3:
            x = x.reshape(x.shape[0], x.shape[1] * x.shape[2])
        hn = h_ref[...] + jnp.dot(prologue([x] + [u[...] for u in u_refs], []), w_s[...],
                                  preferred_element_type=f32)
        ho_ref[...] = hn
        ms = jnp.mean(hn * hn, axis=-1, keepdims=True)
        xn_ref[...] = (hn * lax.rsqrt(ms + EPS) * g_ref[...]).astype(xn_ref.dtype)

    @pl.when(i < ROWS_MAIN // AUX)
    def _main():
        run(xm_ref)

    @pl.when(i == ROWS_MAIN // AUX)
    def _aux():
        run(xa_ref)


def _mixer_out(h, x_main, x_aux, unified, w, lead, norm_g, prologue):
    nm = ROWS_MAIN // AUX
    def xspec(a, imap):
        if a.ndim == 3:
            return pl.BlockSpec((AUX,) + a.shape[1:], lambda i: (imap(i), 0, 0))
        return pl.BlockSpec((AUX, a.shape[1]), lambda i: (imap(i), 0))
    nl = len(lead)
    row = pl.BlockSpec((AUX, D), lambda i: (i, 0))
    in_specs = ([xspec(x_main, lambda i: jnp.minimum(i, nm - 1)), xspec(x_aux, lambda i: 0)]
                + [xspec(u, lambda i: i) for u in unified]
                + [pl.BlockSpec((None,) * nl + (D, D), lambda i: lead + (0, 0), pipeline_mode=pl.Buffered(1)),
                   row, pl.BlockSpec((1, D), lambda i: (0, 0))])
    return pl.pallas_call(
        functools.partial(_mixer_out_kernel, n_u=len(unified), prologue=prologue),
        grid=(nm + 1,),
        in_specs=in_specs,
        out_specs=[row, row],
        out_shape=[_sds((ROWS, D), f32), _sds((ROWS, D), bf16)],
        scratch_shapes=[pltpu.VMEM((D, D), bf16)],
        input_output_aliases={3 + len(unified): 0},
        name="mixer_out",
        compiler_params=_cp(("arbitrary",)),
    )(x_main, x_aux, *unified, w, h, norm_g)


def _ffn(h, xn, w_in, w_out, layer):
    tn = 512
    nct = D_FF // tn
    act = _mm(name="ffn_in", xs=[(xn, 0)], ws=[(w_in, (layer,), 0), (w_in, (layer,), nct)],
              outs=[(_sds((ROWS, D_FF), bf16), 0, 0)],
              K=D, tm=768, tn=tn, n_row_tiles=ROWS // 768, n_col_tiles=nct,
              prologue=_pro_id,
              epilogue=lambda accs, cvs, rms: (jax.nn.silu(accs[0]) * accs[1],))[0]
    h = _mm(name="ffn_out", xs=[(act, 0)], ws=[(w_out, (layer,), 0)], rmats=[(h, 0, 0)],
            outs=[(_sds((ROWS, D), f32), 0, 0)],
            K=D_FF, tm=768, tn=tn, n_row_tiles=ROWS // 768, n_col_tiles=D // tn,
            prologue=_pro_id, epilogue=lambda accs, cvs, rms: (rms[0] + accs[0],),
            aliases={2: 0})[0]
    return h


def _hgrn_lower_bound(logits, j):
    m = jnp.max(logits, axis=0, keepdims=True)
    e = jnp.exp(logits - m)
    p = e / jnp.sum(e, axis=0, keepdims=True)
    cs = p[0:1]
    for r in range(1, j + 1):
        cs = cs + p[r:r + 1]
    return jnp.clip(cs - p[0:1], 0.0, 1.0)


def _hgrn_in(xn, w_in, lb_logits, j):
    tn = 1024
    nct = D // tn
    common = dict(K=D, tm=768, tn=tn, n_row_tiles=ROWS // 768, n_col_tiles=nct, prologue=_pro_id, row_splits=3)
    o = (_sds((ROWS, D), f32), 0, 0)
    q = _mm(xs=[(xn, 0)], ws=[(w_in, (j,), 0)], outs=[o],
            epilogue=lambda accs, cvs, rms: (jax.nn.silu(accs[0]),), **common)[0]

    def f_epi(accs, cvs, rms):
        f = accs[0]
        lb = _hgrn_lower_bound(cvs[0], j)
        t = jnp.exp(-jnp.abs(f))
        log_f = (-(jnp.maximum(-f, 0.0) + jnp.log(1.0 + t))
                 + jnp.log(1.0 + lb * jnp.exp(jnp.minimum(-f, GATE_EXP_CLIP))))
        k = (1.0 - lb) * (jnp.where(f >= 0.0, t, 1.0) / (1.0 + t))
        return log_f, k

    log_f, k = _mm(xs=[(xn, 0)], ws=[(w_in, (j,), nct)], cvecs=[(lb_logits, 0)], outs=[o, o],
                   epilogue=f_epi, **common)
    v = _mm(xs=[(xn, 0)], ws=[(w_in, (j,), 2 * nct)], outs=[o],
            epilogue=lambda accs, cvs, rms: (accs[0],), **common)[0]
    g = _mm(xs=[(xn, 0)], ws=[(w_in, (j,), 3 * nct)], outs=[o],
            epilogue=lambda accs, cvs, rms: (jax.nn.silu(accs[0]),), **common)[0]
    return q, k, v, log_f, g


def _split3(x):
    hi = x.astype(bf16)
    r = x - hi.astype(f32)
    mid = r.astype(bf16)
    lo = (r - mid.astype(f32)).astype(bf16)
    return hi, mid, lo


def _gla_consts(T):
    t = np.arange(T)[:, None]
    w = np.arange(T)[None, :]
    mats = [w <= t, w > t]
    masks = []
    s = T // 2
    while s >= 1:
        piece = t // (2 * s)
        lower = (t % (2 * s)) >= s
        ref = piece * 2 * s + s
        mats.append(np.where(lower, (w > ref) & (w <= t), (w > t) & (w <= ref)))
        upiece = w // (2 * s)
        ulower = (w % (2 * s)) >= s
        masks.append((piece == upiece) & lower & ~ulower)
        s //= 2
    masks.append(t == w)
    m = np.concatenate(mats, 0)
    eye = np.eye(HG_UNROLL, dtype=bool)
    gmasks = np.stack([np.kron(eye, pm) for pm in masks])
    return (jnp.asarray(np.concatenate([m, m, m], 1), dtype=bf16), jnp.asarray(gmasks, dtype=f32))


def _gla_block(q_ref, k_ref, v_ref, f_ref, g_ref, gn_ref, dm_ref, pm_ref, o_ref, st_ref, e_s, ql_s, kl_s, T):
    nlev = T.bit_length() - 1
    dn_t = (((1,), (1,)), ((), ()))
    dn_l = (((0,), (0,)), ((), ()))
    def factors(ls):
        e_s[0:(nlev + 2) * T, ls] = jnp.dot(dm_ref[...], jnp.concatenate(_split3(f_ref[:, ls]), axis=0),
                                            preferred_element_type=f32)
        q = q_ref[:, ls]
        k = k_ref[:, ls]
        ql_s[0, 0:T, ls] = (q * jnp.exp(e_s[0:T, ls])).astype(bf16)
        kl_s[0, 0:T, ls] = (k * jnp.exp(e_s[T:2 * T, ls])).astype(bf16)
        for l in range(nlev):
            ex = jnp.exp(e_s[(l + 2) * T:(l + 3) * T, ls])
            ql_s[l + 1, 0:T, ls] = (q * ex).astype(bf16)
            kl_s[l + 1, 0:T, ls] = (k * ex).astype(bf16)
        ql_s[nlev + 1, 0:T, ls] = q.astype(bf16)
        kl_s[nlev + 1, 0:T, ls] = k.astype(bf16)

    for h0 in range(0, HG_H, HG_UNROLL):
        factors(slice(h0 * HG_K, (h0 + HG_UNROLL) * HG_K))
        sls = [slice(h * HG_K, (h + 1) * HG_K) for h in range(h0, h0 + HG_UNROLL)]
        att = None
        for l in range(nlev + 1):
            a_l = lax.dot_general(jnp.concatenate([ql_s[l + 1, 0:T, sl] for sl in sls], axis=0),
                                  jnp.concatenate([kl_s[l + 1, 0:T, sl] for sl in sls], axis=0),
                                  dn_t, preferred_element_type=f32) * pm_ref[l]
            att = a_l if att is None else att + a_l
        vbs = [v_ref[:, sl].astype(bf16) for sl in sls]
        o_intra = jnp.dot(att.astype(bf16), jnp.concatenate(vbs, axis=0), preferred_element_type=f32)
        for i, sl in enumerate(sls):
            h = h0 + i
            st = st_ref[h]
            o = (lax.dot_general(ql_s[0, 0:T, sl], st.astype(bf16), dn_t, preferred_element_type=f32)
                 + o_intra[i * T:(i + 1) * T])
            upd = lax.dot_general(vbs[i], kl_s[0, 0:T, sl], dn_l, preferred_element_type=f32)
            st_ref[h] = st * jnp.exp(e_s[T - 1:T, sl]) + upd
            ms = jnp.mean(o * o, axis=-1, keepdims=True)
            on = o * lax.rsqrt(ms + EPS) * gn_ref[:, sl]
            o_ref[:, sl] = (on * g_ref[:, sl]).astype(o_ref.dtype)


def _gla_kernel(qm, km, vm, fm, gm, qa, ka, va, fa, ga, gn_ref, dmm, pmm, dma, pma,
                om_ref, oa_ref, s_ref, st_ref, e_s, ql_s, kl_s):
    c = pl.program_id(1)

    @pl.when(c == 0)
    def _meta():
        st_ref[...] = jnp.zeros_like(st_ref)
        _gla_block(qa, ka, va, fa, ga, gn_ref, dma, pma, oa_ref, st_ref, e_s, ql_s, kl_s, NM)

    @pl.when(c > 0)
    def _main():
        _gla_block(qm, km, vm, fm, gm, gn_ref, dmm, pmm, om_ref, st_ref, e_s, ql_s, kl_s, HG_C)

    @pl.when(c == pl.num_programs(1) - 1)
    def _final():
        for h in range(HG_H):
            s_ref[0, h] = st_ref[h].T


def _hgrn_prompt(q, k, v, log_f, g, gn):
    nc = LP // HG_C
    main_spec = pl.BlockSpec((HG_C, D), lambda b, c: (b * nc + jnp.maximum(c - 1, 0), 0))
    meta_spec = pl.BlockSpec((NM, D), lambda b, c: (ROW_M // NM + b, 0))
    arrs = [q, k, v, log_f, g]
    consts = [*_gla_consts(HG_C), *_gla_consts(NM)]
    nslot = HG_C.bit_length() + 1
    return pl.pallas_call(
        _gla_kernel,
        grid=(NB, nc + 1),
        in_specs=[main_spec] * 5 + [meta_spec] * 5 + [pl.BlockSpec((1, D), lambda b, c: (0, 0))]
                 + [pl.BlockSpec(a.shape, lambda b, c, n=a.ndim: (0,) * n) for a in consts],
        out_specs=[pl.BlockSpec((HG_C, D), lambda b, c: (b * nc + jnp.maximum(c - 1, 0), 0)),
                   pl.BlockSpec((NM, D), lambda b, c: (b, 0)),
                   pl.BlockSpec((1, HG_H, HG_K, HG_K), lambda b, c: (b, 0, 0, 0))],
        out_shape=[_sds((ROWS_MAIN, D), bf16), _sds((NB * NM, D), bf16), _sds((NB, HG_H, HG_K, HG_K), f32)],
        scratch_shapes=[pltpu.VMEM((HG_H, HG_K, HG_K), f32),
                        pltpu.VMEM((nslot * HG_C, D), f32),
                        pltpu.VMEM((nslot, HG_C, D), bf16),
                        pltpu.VMEM((nslot, HG_C, D), bf16)],
        name="hgrn_prompt",
        compiler_params=_cp(("arbitrary", "arbitrary")),
    )(*arrs, *arrs, gn, *consts)


HG_BT = 8
HG_SB = 4


def _hgrn_dec_kernel(s_ref, qT, kT, fT, v_ref, g_ref, gn_ref, e3_ref, *rest, stack_slot):
    so_ref, o_ref, qb_s, kb_s, gb_s = rest[-5:]
    if stack_slot is not None:
        for m, e_ref in enumerate(rest[:-5]):
            so_ref[m] = e_ref[...]
    row0 = pl.program_id(1) * HG_SB

    def spread(m, dst):
        dst[...] = jnp.dot(jnp.concatenate(_split3(m), axis=1), e3_ref[...], preferred_element_type=f32)

    def per_b(bb, carry):
        spread(qT[bb], qb_s)
        spread(kT[bb], kb_s)
        spread(jnp.exp(fT[bb]), gb_s)
        row = pl.ds(row0 + bb, 1)
        vall = v_ref[row, :]
        outs = []
        for h in range(HG_H):
            hs = slice(h * HG_K, (h + 1) * HG_K)
            sn = gb_s[:, hs] * s_ref[bb, h] + kb_s[:, hs] * vall[:, hs]
            if stack_slot is None:
                so_ref[bb, h] = sn
            else:
                so_ref[stack_slot, bb, h] = sn
            o = jnp.sum(qb_s[:, hs] * sn, axis=0, keepdims=True)
            ms = jnp.mean(o * o, axis=-1, keepdims=True)
            outs.append(o * lax.rsqrt(ms + EPS))
        on = jnp.concatenate(outs, axis=1) * gn_ref[...]
        o_ref[row, :] = on * g_ref[row, :]
        return carry

    lax.fori_loop(0, HG_SB, per_b, 0)


def _hgrn_sample(state, j, q, k, v, log_f, g, gn, earlier):
    def colform(x):
        return x[ROW_S:ROW_S + NS].reshape(NS, HG_H, HG_K).transpose(0, 2, 1)

    n_a = state.shape[0]
    last = j == n_a - 1
    head_of_lane = np.arange(D) // HG_K
    sel = (np.arange(HG_H)[:, None] == head_of_lane[None, :])
    e3 = jnp.asarray(np.concatenate([sel, sel, sel], axis=0), dtype=bf16)
    nsb = HG_BT // HG_SB
    col_spec = pl.BlockSpec((HG_SB, HG_K, HG_H), lambda i, s: (i * nsb + s, 0, 0))
    row_spec = pl.BlockSpec((HG_BT, D), lambda i, s: (ROW_S // HG_BT + i, 0))
    new_spec = pl.BlockSpec((HG_SB, HG_H, HG_K, HG_K), lambda i, s: (i * nsb + s, 0, 0, 0))
    in_specs = [pl.BlockSpec((None, HG_SB, HG_H, HG_K, HG_K), lambda i, s: (j, i * nsb + s, 0, 0, 0)),
                col_spec, col_spec, col_spec, row_spec, row_spec,
                pl.BlockSpec((1, D), lambda i, s: (0, 0)), pl.BlockSpec((3 * HG_H, D), lambda i, s: (0, 0))]
    args = [state, colform(q), colform(k), colform(log_f), v, g, gn, e3]
    if last:
        in_specs += [new_spec] * len(earlier)
        args += list(earlier)
        st_out = pl.BlockSpec((n_a, HG_SB, HG_H, HG_K, HG_K), lambda i, s: (0, i * nsb + s, 0, 0, 0))
        st_shape = _sds(state.shape, f32)
    else:
        st_out, st_shape = new_spec, _sds(state.shape[1:], f32)
    return pl.pallas_call(
        functools.partial(_hgrn_dec_kernel, stack_slot=j if last else None),
        grid=(NS // HG_BT, nsb),
        in_specs=in_specs,
        out_specs=[st_out, pl.BlockSpec((HG_BT, D), lambda i, s: (i, 0))],
        out_shape=[st_shape, _sds((NS, D), f32)],
        scratch_shapes=[pltpu.VMEM((HG_K, D), f32)] * 3,
        name="hgrn_decode",
        compiler_params=_cp(("arbitrary", "arbitrary")),
    )(*args)


def _aux_rows(sample_rows, meta_rows):
    pad = jnp.zeros((AUX - NS - NB * NM, sample_rows.shape[1]), sample_rows.dtype)
    return jnp.concatenate([sample_rows, meta_rows.astype(sample_rows.dtype), pad], axis=0)


def _hgrn_layer(h, norm_g, ffn_g, state, earlier, j, w_in, lb_logits, gnorm, w_out, xn=None):
    if xn is None:
        xn = _rmsnorm(h, norm_g, out_dtype=bf16, tm=768, row_tile0=0, n_row_tiles=ROWS // 768)
    q, k, v, log_f, g = _hgrn_in(xn, w_in, lb_logits, j)
    gn = gnorm[j].reshape(1, D)
    o_main, o_meta, s_p = _hgrn_prompt(q, k, v, log_f, g, gn)
    s_s, o_s = _hgrn_sample(state, j, q, k, v, log_f, g, gn, earlier)
    o_aux = _aux_rows(o_s.astype(bf16), o_meta)
    h, xn_f = _mixer_out(h, o_main, o_aux, [], w_out, (j,), ffn_g, _pro_id)
    return (h, xn_f), s_p, s_s


def _lru_in(xn, w_in, j):
    tn = 1024
    nct = D // tn
    common = dict(K=D, tm=768, tn=tn, n_row_tiles=ROWS // 768, n_col_tiles=nct, prologue=_pro_id, row_splits=3)
    o = (_sds((ROWS, D), f32), 0, 0)
    y = _mm(xs=[(xn, 0)], ws=[(w_in, (j,), 0)], outs=[o],
            epilogue=lambda accs, cvs, rms: (jax.nn.gelu(accs[0], approximate=True),), **common)[0]
    x = _mm(xs=[(xn, 0)], ws=[(w_in, (j,), nct)], outs=[o],
            epilogue=lambda accs, cvs, rms: (accs[0],), **common)[0]
    return y, x


def _lru_gates(xc, gw_ref, gb_ref, lam_ref, n):
    ls = slice(n * LRU_BW, (n + 1) * LRU_BW)
    xb = xc[:, ls].astype(bf16)
    r = jax.nn.sigmoid(jnp.dot(xb, gw_ref[0, n], preferred_element_type=f32) + gb_ref[0:1, ls])
    ig = jax.nn.sigmoid(jnp.dot(xb, gw_ref[1, n], preferred_element_type=f32) + gb_ref[1:2, ls])
    log_a = -LRU_C * r * _softplus(-lam_ref[:, ls])
    a = jnp.exp(log_a)
    mult = jnp.sqrt(jnp.maximum(1.0 - a * a, 0.0))
    return a, ig, mult


def _lru_conv(cb_ref, cw_ref, x0, x1, x2, x3):
    xc = cb_ref[...] + x0 * cw_ref[0:1, :]
    xc = xc + x1 * cw_ref[1:2, :]
    xc = xc + x2 * cw_ref[2:3, :]
    return xc + x3 * cw_ref[3:4, :]


def _lru_block(x_ref, y_ref, o_ref, cw_ref, cb_ref, gw_s, gb_ref, lam_ref, xbuf, a_s, b_s, hcar, T, first):
    xbuf[8:8 + T, :] = x_ref[...]
    xc = _lru_conv(cb_ref, cw_ref, xbuf[5:5 + T, :], xbuf[6:6 + T, :], xbuf[7:7 + T, :], xbuf[8:8 + T, :])
    xbuf[0:8, :] = xbuf[T:T + 8, :]
    row = lax.broadcasted_iota(jnp.int32, (T, 1), 0)
    for n in range(LRU_NBLK):
        ls = slice(n * LRU_BW, (n + 1) * LRU_BW)
        a, ig, mult = _lru_gates(xc, gw_s, gb_ref, lam_ref, n)
        if first:
            mult = jnp.where(row == 0, 1.0, mult)
        a_s[0:T, ls] = a
        b_s[0:T, ls] = xc[:, ls] * ig * mult

    def step(t, hprev):
        hnew = a_s[pl.ds(t, 1), :] * hprev + b_s[pl.ds(t, 1), :]
        b_s[pl.ds(t, 1), :] = hnew
        return hnew

    hcar[...] = lax.fori_loop(0, T, step, hcar[...], unroll=8)
    o_ref[...] = (b_s[0:T, :] * y_ref[...]).astype(o_ref.dtype)


def _lru_kernel(xm, ym, xa, ya, cw_ref, cb_ref, gw_ref, gb_ref, lam_ref,
                om_ref, oa_ref, hl_ref, cv_ref, gw_s, xbuf, a_s, b_s, hcar):
    c = pl.program_id(1)

    @pl.when((pl.program_id(0) == 0) & (c == 0))
    def _cast():
        gw_s[...] = gw_ref[...].astype(bf16)

    @pl.when(c == 0)
    def _meta():
        xbuf[0:8, :] = jnp.zeros((8, D), f32)
        hcar[...] = jnp.zeros_like(hcar)
        _lru_block(xa, ya, oa_ref, cw_ref, cb_ref, gw_s, gb_ref, lam_ref, xbuf, a_s, b_s, hcar, NM, True)

    @pl.when(c > 0)
    def _main():
        _lru_block(xm, ym, om_ref, cw_ref, cb_ref, gw_s, gb_ref, lam_ref, xbuf, a_s, b_s, hcar, LRU_T, False)

    @pl.when(c == pl.num_programs(1) - 1)
    def _final():
        hl_ref[0] = hcar[...]
        cv_ref[0] = xbuf[5:8, :]


def _lru_prompt(x, y, j, conv_w, conv_b, gate_w, gate_b, lam):
    nc = LP // LRU_T
    main_spec = pl.BlockSpec((LRU_T, D), lambda b, c: (b * nc + jnp.maximum(c - 1, 0), 0))
    meta_spec = pl.BlockSpec((NM, D), lambda b, c: (ROW_M // NM + b, 0))
    return pl.pallas_call(
        _lru_kernel,
        grid=(NB, nc + 1),
        in_specs=[main_spec, main_spec, meta_spec, meta_spec,
                  pl.BlockSpec((None, 4, D), lambda b, c: (j, 0, 0)),
                  pl.BlockSpec((1, D), lambda b, c: (j, 0)),
                  pl.BlockSpec((None, 2, LRU_NBLK, LRU_BW, LRU_BW), lambda b, c: (j, 0, 0, 0, 0)),
                  pl.BlockSpec((None, 2, D), lambda b, c: (j, 0, 0)),
                  pl.BlockSpec((1, D), lambda b, c: (j, 0))],
        out_specs=[pl.BlockSpec((LRU_T, D), lambda b, c: (b * nc + jnp.maximum(c - 1, 0), 0)),
                   pl.BlockSpec((NM, D), lambda b, c: (b, 0)),
                   pl.BlockSpec((1, 1, D), lambda b, c: (b, 0, 0)),
                   pl.BlockSpec((1, 3, D), lambda b, c: (b, 0, 0))],
        out_shape=[_sds((ROWS_MAIN, D), bf16), _sds((NB * NM, D), bf16),
                   _sds((NB, 1, D), f32), _sds((NB, 3, D), f32)],
        scratch_shapes=[pltpu.VMEM((2, LRU_NBLK, LRU_BW, LRU_BW), bf16),
                        pltpu.VMEM((LRU_T + 8, D), f32), pltpu.VMEM((LRU_T, D), f32),
                        pltpu.VMEM((LRU_T, D), f32), pltpu.VMEM((1, D), f32)],
        name="lru_prompt",
        compiler_params=_cp(("arbitrary", "arbitrary")),
    )(x, y, x, y, conv_w, conv_b, gate_w, gate_b, lam)


LRU_ST = 64


def _lru_dec_kernel(x_ref, y_ref, h0_ref, cbuf_ref, cw_ref, cb_ref, gw_ref, gb_ref, lam_ref,
                    o_ref, hn_ref, cn_ref, gw_s):
    gw_s[...] = gw_ref[...].astype(bf16)
    x = x_ref[...]
    b0 = cbuf_ref[:, 0, :]
    b1 = cbuf_ref[:, 1, :]
    b2 = cbuf_ref[:, 2, :]
    xc = _lru_conv(cb_ref, cw_ref, b0, b1, b2, x)
    cn_ref[:, 0, :] = b1
    cn_ref[:, 1, :] = b2
    cn_ref[:, 2, :] = x
    for n in range(LRU_NBLK):
        ls = slice(n * LRU_BW, (n + 1) * LRU_BW)
        a, ig, mult = _lru_gates(xc, gw_s, gb_ref, lam_ref, n)
        hn = a * h0_ref[:, ls] + xc[:, ls] * ig * mult
        hn_ref[:, ls] = hn
        o_ref[:, ls] = (hn * y_ref[:, ls]).astype(o_ref.dtype)


def _lru_sample(x, y, j, h0, cbuf, conv_w, conv_b, gate_w, gate_b, lam):
    row_spec = pl.BlockSpec((LRU_ST, D), lambda i: (ROW_S // LRU_ST + i, 0))
    return pl.pallas_call(
        _lru_dec_kernel,
        grid=(NS // LRU_ST,),
        in_specs=[row_spec, row_spec,
                  pl.BlockSpec((None, LRU_ST, D), lambda i: (j, i, 0)),
                  pl.BlockSpec((None, LRU_ST, 3, D), lambda i: (j, i, 0, 0)),
                  pl.BlockSpec((None, 4, D), lambda i: (j, 0, 0)),
                  pl.BlockSpec((1, D), lambda i: (j, 0)),
                  pl.BlockSpec((None, 2, LRU_NBLK, LRU_BW, LRU_BW), lambda i: (j, 0, 0, 0, 0)),
                  pl.BlockSpec((None, 2, D), lambda i: (j, 0, 0)),
                  pl.BlockSpec((1, D), lambda i: (j, 0))],
        out_specs=[pl.BlockSpec((LRU_ST, D), lambda i: (i, 0)),
                   pl.BlockSpec((LRU_ST, D), lambda i: (i, 0)),
                   pl.BlockSpec((LRU_ST, 3, D), lambda i: (i, 0, 0))],
        out_shape=[_sds((NS, D), bf16), _sds((NS, D), f32), _sds((NS, 3, D), f32)],
        scratch_shapes=[pltpu.VMEM((2, LRU_NBLK, LRU_BW, LRU_BW), bf16)],
        name="lru_decode",
        compiler_params=_cp(("arbitrary",)),
    )(x, y, h0, cbuf, conv_w, conv_b, gate_w, gate_b, lam)


def _lru_layer(h, norm_g, ffn_g, state_h, state_conv, j, w_in, conv_w, conv_b, gate_w, gate_b, lam, w_out):
    xn = _rmsnorm(h, norm_g, out_dtype=bf16, tm=768, row_tile0=0, n_row_tiles=ROWS // 768)
    y, x = _lru_in(xn, w_in, j)
    o_main, o_meta, hl_p, cv_p = _lru_prompt(x, y, j, conv_w, conv_b, gate_w, gate_b, lam)
    o_s, hl_s, cv_s = _lru_sample(x, y, j, state_h, state_conv, conv_w, conv_b, gate_w, gate_b, lam)
    o_aux = _aux_rows(o_s, o_meta)
    h, xn_f = _mixer_out(h, o_main, o_aux, [], w_out, (j,), ffn_g, _pro_id)
    return (h, xn_f), hl_p.reshape(NB, D), hl_s, cv_p, cv_s


def _rwkv_premix_kernel(h_ref, g_ref, mu_ref, sh_ref, *refs):
    x_refs = refs[:6]
    sp_ref, ss_ref, xbuf, meta_last = refs[6:]
    i = pl.program_id(0)
    x = h_ref[...]
    ms = jnp.mean(x * x, axis=-1, keepdims=True)
    xn = x * lax.rsqrt(ms + EPS) * g_ref[...]
    xbuf[8:8 + PM_T, :] = xn
    row = lax.broadcasted_iota(jnp.int32, (PM_T, 1), 0)

    @pl.when(i == 0)
    def _aux():
        xbuf[7:8, :] = jnp.zeros((1, D), f32)
        ss_ref[...] = xn[0:NS]
        for b in range(NB):
            meta_last[b:b + 1, :] = xn[NS + b * NM + NM - 1:NS + b * NM + NM]

    @pl.when(i > 0)
    def _main():
        m = i - 1
        b = m // (LP // PM_T)

        @pl.when(m % (LP // PM_T) == 0)
        def _start():
            xbuf[7:8, :] = meta_last[pl.ds(b, 1), :]

        @pl.when(m % (LP // PM_T) == LP // PM_T - 1)
        def _end():
            sp_ref[pl.ds(b, 1), :] = xn[PM_T - 1:PM_T]

    shifted = xbuf[7:7 + PM_T, :]
    is_meta = (row >= NS) & (row < NS + NB * NM) & ((row - NS) % NM != 0)
    sh_pad = jnp.concatenate([sh_ref[...], jnp.zeros((PM_T - NS, D), f32)], axis=0)
    prev_aux = jnp.where(row < NS, sh_pad, jnp.where(is_meta, shifted, 0.0))
    prev = jnp.where(i == 0, prev_aux, shifted)
    dx = prev - xn
    for n in range(6):
        x_refs[n][...] = (xn + dx * mu_ref[n:n + 1, :]).astype(bf16)
    xbuf[7:8, :] = xn[PM_T - 1:PM_T]


def _rwkv_premix(h, norm_g, mu_j, shift_j):
    nt = ROWS // PM_T
    rows = lambda i: (jnp.where(i == 0, nt - 1, i - 1), 0)
    return pl.pallas_call(
        _rwkv_premix_kernel,
        grid=(nt,),
        in_specs=[pl.BlockSpec((PM_T, D), rows),
                  pl.BlockSpec((1, D), lambda i: (0, 0)),
                  pl.BlockSpec((6, D), lambda i: (0, 0)),
                  pl.BlockSpec((NS, D), lambda i: (0, 0))],
        out_specs=[pl.BlockSpec((PM_T, D), rows)] * 6 + [pl.BlockSpec((NB, D), lambda i: (0, 0)),
                                                         pl.BlockSpec((NS, D), lambda i: (0, 0))],
        out_shape=[_sds((ROWS, D), bf16)] * 6 + [_sds((NB, D), f32), _sds((NS, D), f32)],
        scratch_shapes=[pltpu.VMEM((PM_T + 8, D), f32), pltpu.VMEM((8, D), f32)],
        name="rwkv_premix",
        compiler_params=_cp(("arbitrary",)),
    )(h, norm_g, mu_j, shift_j)


def _rwkv_lora_kernel(xw_ref, xa_ref, xg_ref, w1_ref, w2_ref, a1_ref, a2_ref, g1_ref, g2_ref, w0_ref, a0_ref,
                      d_ref, a_ref, g_ref, w1_s, w2_s, a1_s, a2_s, g1_s, g2_s):
    @pl.when(pl.program_id(0) == 0)
    def _cast():
        for src, dst in ((w1_ref, w1_s), (w2_ref, w2_s), (a1_ref, a1_s), (a2_ref, a2_s), (g1_ref, g1_s),
                         (g2_ref, g2_s)):
            dst[...] = src[...].astype(bf16)

    def mm(u, w_s):
        return jnp.dot(u, w_s[...], preferred_element_type=f32)

    tw = jnp.tanh(mm(xw_ref[...], w1_s)).astype(bf16)
    x = w0_ref[...] + mm(tw, w2_s)
    d_ref[...] = jnp.exp(-np.float32(np.exp(-0.5)) * jax.nn.sigmoid(x)).reshape(d_ref.shape)
    ta = mm(xa_ref[...], a1_s).astype(bf16)
    a_ref[...] = jax.nn.sigmoid(a0_ref[...] + mm(ta, a2_s)).reshape(a_ref.shape)
    tg = jax.nn.sigmoid(mm(xg_ref[...], g1_s)).astype(bf16)
    g_ref[...] = mm(tg, g2_s)


def _rwkv_lora(xw, xa, xg, w1, w2, a1, a2, g1, g2, w0, a0):
    tm = 384
    full = lambda a: pl.BlockSpec(a.shape, lambda i: (0,) * a.ndim)
    row = pl.BlockSpec((tm, D), lambda i: (i, 0))
    row3 = pl.BlockSpec((tm, RW_Q, 128), lambda i: (i, 0, 0))
    ws = [w1, w2, a1, a2, g1, g2]
    return pl.pallas_call(
        _rwkv_lora_kernel,
        grid=(ROWS // tm,),
        in_specs=[row, row, row] + [full(a) for a in ws + [w0, a0]],
        out_specs=[row3, row3, row],
        out_shape=[_sds((ROWS, RW_Q, 128), f32)] * 2 + [_sds((ROWS, D), f32)],
        scratch_shapes=[pltpu.VMEM(a.shape, bf16) for a in ws],
        name="rwkv_lora",
        compiler_params=_cp(("arbitrary",)),
    )(xw, xa, xg, *ws, w0, a0)


def _rwkv_step(t, r_s, k_s, v_s, d_s, a_s, z_s, s_s, tmp, kk_p, ka_p, rk_p, lw_p, lb_p):
    kk_t, d_t, ka_t, k2_t, r_t = tmp
    r = r_s[t]
    k = k_s[t]
    v = v_s[t]
    a = a_s[t]
    kkr = k * kk_p[...]
    nrm = jnp.sqrt(jnp.sum(kkr * kkr, axis=0, keepdims=True))
    kk = kkr / jnp.maximum(nrm, 1e-12)
    k2 = k * (1.0 + (a - 1.0) * ka_p[...])
    kk_t[...] = kk
    d_t[...] = d_s[t]
    ka_t[...] = kk * a
    k2_t[...] = k2
    r_t[...] = r

    sa = -(s_s[0] * kk_t[0:1, :])
    for j in range(1, RW_N):
        sa = sa - s_s[j] * kk_t[j:j + 1, :]
    y = None
    for j in range(RW_N):
        sn = s_s[j] * d_t[j:j + 1, :] + sa * ka_t[j:j + 1, :] + v * k2_t[j:j + 1, :]
        s_s[j] = sn
        yj = sn * r_t[j:j + 1, :]
        y = yj if y is None else y + yj
    mean = jnp.mean(y, axis=0, keepdims=True)
    yc = y - mean
    var = jnp.mean(yc * yc, axis=0, keepdims=True)
    yn = yc * lax.rsqrt(var + RW_LN_EPS) * lw_p[...] + lb_p[...]
    bonus = jnp.sum(r * k2 * rk_p[...], axis=0, keepdims=True) * v
    z_s[t] = yn + bonus


def _rwkv_scan_kernel(r_ref, k_ref, v_ref, d_ref, a_ref, s0_ref, kk_p, ka_p, rk_p, lw_p, lb_p,
                      z_ref, sf_ref, s_s, *tmp):
    c = pl.program_id(1)

    @pl.when(c == 0)
    def _init():
        s_s[...] = s0_ref[...]

    def step(t, carry):
        _rwkv_step(t, r_ref, k_ref, v_ref, d_ref, a_ref, z_ref, s_s, tmp, kk_p, ka_p, rk_p, lw_p, lb_p)
        return carry

    lax.fori_loop(0, r_ref.shape[0], step, 0)

    @pl.when(c == pl.num_programs(1) - 1)
    def _final():
        sf_ref[...] = s_s[...]


RW_Q = D // 128


def _half_transpose(x):
    xt = x.T
    return jnp.concatenate([xt[0:RW_N], xt[RW_N:2 * RW_N]], axis=1)


def _rwkv_prompt_kernel(*refs):
    n_in = 5 * NB
    in_refs = refs[:n_in]
    params = refs[n_in:n_in + 5]
    zmain, zmeta, sf_ref = refs[n_in + 5:n_in + 8]
    scr = refs[n_in + 8:]
    seq, z_s, zo_s, s_s, tmp = scr[:5], scr[5], scr[6], scr[7], scr[8:]
    c = pl.program_id(0)

    @pl.when(c == 0)
    def _init():
        s_s[...] = jnp.zeros_like(s_s)
        z_s[...] = jnp.zeros_like(z_s)

    def load_token(t):
        rows = pl.ds(pl.multiple_of(t * RW_Q, RW_Q), RW_Q)
        for a in range(5):
            tile = jnp.concatenate([in_refs[a * NB + b][rows, :] for b in range(NB)], axis=0)
            seq[a][t] = _half_transpose(tile)

    load_token(0)

    def step(t, carry):
        tp = jnp.maximum(t - 1, 0)
        zo_s[tp] = _half_transpose(z_s[tp])
        _rwkv_step(t, *seq, z_s, s_s, tmp, *params)
        load_token(jnp.minimum(t + 1, RW_TS - 1))
        return carry

    lax.fori_loop(0, RW_TS, step, 0)
    zo_s[RW_TS - 1] = _half_transpose(z_s[RW_TS - 1])

    def write(z_ref):
        for t in range(RW_TS):
            for b in range(NB):
                z_ref[b, t * RW_Q:(t + 1) * RW_Q, :] = zo_s[t, b * RW_Q:(b + 1) * RW_Q, :]

    @pl.when(c == 0)
    def _write_meta():
        write(zmeta)

    @pl.when(c > 0)
    def _write_main():
        write(zmain)

    @pl.when(c == pl.num_programs(0) - 1)
    def _final():
        sf_ref[...] = s_s[...]


def _rwkv_prompt(r, k, v, dec, a, params):
    nc = LP // RW_TS
    blk = RW_TS * RW_Q
    arrs = [x.reshape(ROWS * RW_Q, 128) for x in (r, k, v, dec, a)]
    in_specs, args = [], []
    for x in arrs:
        for b in range(NB):
            in_specs.append(pl.BlockSpec(
                (blk, 128), lambda c, b=b: (jnp.where(c == 0, ROW_M // RW_TS + b, b * nc + c - 1), 0)))
            args.append(x)
    par = pl.BlockSpec((RW_N, 128), lambda c: (0, 0))
    outs = pl.pallas_call(
        _rwkv_prompt_kernel,
        grid=(nc + 1,),
        in_specs=in_specs + [par] * 5,
        out_specs=[pl.BlockSpec((NB, blk, 128), lambda c: (0, jnp.maximum(c - 1, 0), 0)),
                   pl.BlockSpec((NB, blk, 128), lambda c: (0, 0, 0)),
                   pl.BlockSpec((RW_N, RW_N, 128), lambda c: (0, 0, 0))],
        out_shape=[_sds((NB, LP * RW_Q, 128), f32), _sds((NB, NM * RW_Q, 128), f32),
                   _sds((RW_N, RW_N, 128), f32)],
        scratch_shapes=[pltpu.VMEM((RW_TS, RW_N, 128), f32)] * 7 + [pltpu.VMEM((RW_N, RW_N, 128), f32)]
                       + [pltpu.VMEM((RW_N, 128), f32)] * 5,
        name="rwkv_prompt",
        compiler_params=_cp(("arbitrary",)),
    )(*args, *params)
    return outs[0].reshape(ROWS_MAIN, RW_Q, 128), outs[1].reshape(NB * NM, D), outs[2]


def _rwkv_scan(rT, kT, vT, dT, aT, s0T, params, tc):
    L, _, lanes = rT.shape
    nl = lanes // 128
    seq = pl.BlockSpec((tc, RW_N, 128), lambda l, c: (c, 0, l))
    st = pl.BlockSpec((RW_N, RW_N, 128), lambda l, c: (0, 0, l))
    par = pl.BlockSpec((RW_N, 128), lambda l, c: (0, l))
    return pl.pallas_call(
        _rwkv_scan_kernel,
        grid=(nl, L // tc),
        in_specs=[seq] * 5 + [st] + [par] * 5,
        out_specs=[seq, st],
        out_shape=[_sds((L, RW_N, lanes), f32), _sds((RW_N, RW_N, lanes), f32)],
        scratch_shapes=[pltpu.VMEM((RW_N, RW_N, 128), f32)] + [pltpu.VMEM((RW_N, 128), f32)] * 5,
        name="rwkv_scan",
        compiler_params=_cp(("arbitrary", "arbitrary")),
    )(rT, kT, vT, dT, aT, s0T, *params)


def _rwkv_layer(h, norm_g, ffn_g, state, shift, j, mu, w_rkv, w0, w1, w2, a0, a1, a2, g1, g2, k_k, k_a, r_k,
                ln_w, ln_b, w_o):
    *xmix, shift_p, shift_s = _rwkv_premix(h, norm_g, mu[j], shift[j])
    tn = 1024
    common = dict(K=D, tm=768, tn=tn, n_row_tiles=ROWS // 768, n_col_tiles=D // tn, prologue=_pro_id,
                  epilogue=lambda accs, cvs, rms: (accs[0],))
    o = (_sds((ROWS, RW_Q, 128), f32), 0, 0)
    r, k, v = [_mm(name="rwkv_rkv", xs=[(xmix[n], 0)], ws=[(w_rkv, (j, n), 0)], outs=[o], **common)[0]
               for n in range(3)]

    def padc(w):
        return jnp.pad(w, ((0, 0), (0, LORA_PAD - w.shape[1])))

    def padr(w):
        return jnp.pad(w, ((0, LORA_PAD - w.shape[0]), (0, 0)))

    dec, a, g = _rwkv_lora(xmix[3], xmix[4], xmix[5], padc(w1[j]), padr(w2[j]), padc(a1[j]), padr(a2[j]),
                           g1[j], g2[j], w0[j].reshape(1, D), a0[j].reshape(1, D))

    def lanes_param(p, reps):
        return jnp.tile(p.reshape(RW_H, RW_N).T, (1, reps))

    def prompt_param(p):
        pt = p.reshape(RW_Q, 2, RW_N).transpose(2, 1, 0)
        return jnp.broadcast_to(pt[:, :, None, :], (RW_N, 2, NB, RW_Q)).reshape(RW_N, 128)

    def params(reps):
        return [lanes_param(k_k[j], reps), lanes_param(k_a[j], reps), lanes_param(r_k[j].reshape(D), reps),
                lanes_param(ln_w[j], reps), lanes_param(ln_b[j], reps)]

    z_main, z_meta, sT = _rwkv_prompt(r, k, v, dec, a, [prompt_param(p) for p in (
        k_k[j], k_a[j], r_k[j].reshape(D), ln_w[j], ln_b[j])])
    s_p = sT.reshape(RW_N, RW_N, 2, NB, RW_Q).transpose(3, 4, 2, 1, 0).reshape(NB, RW_H, RW_N, RW_N)

    def to_step(x):
        return x[ROW_S:ROW_S + NS].reshape(NS, RW_H, RW_N).transpose(2, 0, 1).reshape(1, RW_N, NS * RW_H)

    s0T = state[j].transpose(3, 2, 0, 1).reshape(RW_N, RW_N, NS * RW_H)
    zsT, ssT = _rwkv_scan(to_step(r), to_step(k), to_step(v), to_step(dec), to_step(a), s0T, params(NS), 1)
    s_s = ssT.reshape(RW_N, RW_N, NS, RW_H).transpose(2, 3, 1, 0)
    z_s = zsT.reshape(RW_N, NS, RW_H).transpose(1, 2, 0).reshape(NS, D)

    z_aux = _aux_rows(z_s, z_meta)
    h, xn_f = _mixer_out(h, z_main, z_aux, [g], w_o, (j,), ffn_g, _pro_mul)
    return (h, xn_f), s_p, s_s, shift_p, shift_s


def kernel(x_prompt, x_sample, state_hgrn, state_lru_h, state_lru_conv, state_rwkv, state_rwkv_shift, meta_tokens, norm_mix, norm_ffn, norm_final, hgrn_w_in, hgrn_lb_logits, hgrn_norm, hgrn_w_out, lru_w_in, lru_conv_w, lru_conv_b, lru_gate_w, lru_gate_b, lru_lambda, lru_w_out, rwkv_mu, rwkv_w_rkv, rwkv_w0, rwkv_w1, rwkv_w2, rwkv_a0, rwkv_a1, rwkv_a2, rwkv_g1, rwkv_g2, rwkv_k_k, rwkv_k_a, rwkv_r_k, rwkv_ln_w, rwkv_ln_b, rwkv_w_o, ffn_w_in, ffn_w_out):
    depth = norm_mix.shape[0]
    x_aux = _aux_rows(x_sample.reshape(NS, D), jnp.tile(meta_tokens, (NB, 1)))
    h, xn0 = _assemble(x_prompt.reshape(ROWS_MAIN, D), x_aux, norm_mix[0].reshape(1, D))
    hg_p, lh_p, lh_s, lc_p, lc_s, rw_p, rw_s, rs_p, rs_s = [[] for _ in range(9)]
    hg_s = []
    for i in range(depth):
        m, j = i % 3, i // 3
        ng = norm_mix[i].reshape(1, D)
        fg = norm_ffn[i].reshape(1, D)
        if m == 0:
            hx, sp, ss = _hgrn_layer(h, ng, fg, state_hgrn, hg_s, j, hgrn_w_in, hgrn_lb_logits, hgrn_norm,
                                     hgrn_w_out, xn=xn0 if i == 0 else None)
            hg_p.append(sp)
            hg_s.append(ss)
        elif m == 1:
            hx, hp_, hs_, cp_, cs_ = _lru_layer(h, ng, fg, state_lru_h, state_lru_conv, j, lru_w_in,
                                                lru_conv_w, lru_conv_b, lru_gate_w, lru_gate_b, lru_lambda,
                                                lru_w_out)
            lh_p.append(hp_)
            lh_s.append(hs_)
            lc_p.append(cp_)
            lc_s.append(cs_)
        else:
            hx, sp, ss, shp, shs = _rwkv_layer(h, ng, fg, state_rwkv, state_rwkv_shift, j, rwkv_mu, rwkv_w_rkv,
                                               rwkv_w0, rwkv_w1, rwkv_w2, rwkv_a0, rwkv_a1, rwkv_a2, rwkv_g1,
                                               rwkv_g2, rwkv_k_k, rwkv_k_a, rwkv_r_k, rwkv_ln_w, rwkv_ln_b,
                                               rwkv_w_o)
            rw_p.append(sp)
            rw_s.append(ss)
            rs_p.append(shp)
            rs_s.append(shs)
        h = _ffn(*hx, ffn_w_in, ffn_w_out, i)
    nf = norm_final.reshape(1, D)
    y_main = _rmsnorm(h, nf, out_dtype=f32, tm=512, row_tile0=0, n_row_tiles=ROWS_MAIN // 512)
    y_aux = _rmsnorm(h, nf, out_dtype=f32, tm=AUX, row_tile0=ROWS_MAIN // AUX, n_row_tiles=1)
    y_prompt = y_main.reshape(NB, LP, D)
    y_sample = y_aux[:NS].reshape(NS, 1, D)
    return (y_prompt, y_sample, jnp.stack(hg_p), hg_s[-1], jnp.stack(lh_p), jnp.stack(lh_s),
            jnp.stack(lc_p), jnp.stack(lc_s), jnp.stack(rw_p), jnp.stack(rw_s), jnp.stack(rs_p), jnp.stack(rs_s))
```

```python
import functools

import jax
import jax.numpy as jnp
import numpy as np
from jax import lax
from jax.experimental import pallas as pl
from jax.experimental.pallas import tpu as pltpu

f32 = jnp.float32
bf16 = jnp.bfloat16

D = 2048
NB = 4
LP = 2048
NM = 16
NS = 128
ROWS_MAIN = NB * LP
ROW_S = ROWS_MAIN
ROW_M = ROW_S + NS
AUX = 256
ROWS = ROWS_MAIN + AUX
EPS = 1e-6
HG_H, HG_K = 16, 128
HG_C = 64
HG_UNROLL = 4
GATE_EXP_CLIP = 60.0
LRU_NBLK, LRU_BW = 8, 256
LRU_C = 8.0
LRU_T = 256
RW_H, RW_N = 32, 64
RW_LN_EPS = 64e-5
RW_TS = 16
PM_T = 256
LORA_PAD = 128
D_FF = 5632
TM_BIG = 1408
TM_MID = 768
VMEM_LIMIT = 56 * 1024 * 1024


def _cp(sem):
    return pltpu.CompilerParams(dimension_semantics=sem, vmem_limit_bytes=VMEM_LIMIT)


def _softplus(x):
    return jnp.maximum(x, 0.0) + jnp.log(1.0 + jnp.exp(-jnp.abs(x)))


def _norm_kernel(h_ref, g_ref, o_ref):
    x = h_ref[...]
    ms = jnp.mean(x * x, axis=-1, keepdims=True)
    o_ref[...] = (x * lax.rsqrt(ms + EPS) * g_ref[...]).astype(o_ref.dtype)


def _rmsnorm(h, g, *, out_dtype, tm, row_tile0, n_row_tiles):
    return pl.pallas_call(
        _norm_kernel,
        grid=(n_row_tiles,),
        in_specs=[pl.BlockSpec((tm, D), lambda i: (i + row_tile0, 0)),
                  pl.BlockSpec((1, D), lambda i: (0, 0))],
        out_specs=pl.BlockSpec((tm, D), lambda i: (i, 0)),
        out_shape=jax.ShapeDtypeStruct((n_row_tiles * tm, D), out_dtype),
        name="rmsnorm",
        compiler_params=_cp(("arbitrary",)),
    )(h, g)


def _assemble_kernel(xm_ref, xa_ref, g_ref, h_ref, xn_ref):
    def emit(x):
        h_ref[...] = x
        ms = jnp.mean(x * x, axis=-1, keepdims=True)
        xn_ref[...] = (x * lax.rsqrt(ms + EPS) * g_ref[...]).astype(xn_ref.dtype)

    @pl.when(pl.program_id(0) < ROWS_MAIN // AUX)
    def _main():
        emit(xm_ref[...])

    @pl.when(pl.program_id(0) == ROWS_MAIN // AUX)
    def _aux():
        emit(xa_ref[...])


def _assemble(x_main, x_aux, g):
    nm = ROWS_MAIN // AUX
    return pl.pallas_call(
        _assemble_kernel,
        grid=(nm + 1,),
        in_specs=[pl.BlockSpec((AUX, D), lambda i: (jnp.minimum(i, nm - 1), 0)),
                  pl.BlockSpec((AUX, D), lambda i: (0, 0)),
                  pl.BlockSpec((1, D), lambda i: (0, 0))],
        out_specs=[pl.BlockSpec((AUX, D), lambda i: (i, 0)), pl.BlockSpec((AUX, D), lambda i: (i, 0))],
        out_shape=[_sds((ROWS, D), f32), _sds((ROWS, D), bf16)],
        name="assemble",
        compiler_params=_cp(("arbitrary",)),
    )(x_main, x_aux, g)


def _mm_body(*refs, n_x, n_w, n_kv, n_cv, n_rm, n_out, prologue, epilogue, row_splits):
    p = 0
    x_refs = refs[p:p + n_x]; p += n_x
    w_refs = refs[p:p + n_w]; p += n_w
    kv_refs = refs[p:p + n_kv]; p += n_kv
    cv_refs = refs[p:p + n_cv]; p += n_cv
    rm_refs = refs[p:p + n_rm]; p += n_rm
    out_refs = refs[p:p + n_out]; p += n_out
    w_scr = refs[p:p + n_w]

    @pl.when(pl.program_id(1) == 0)
    def _cast_weights():
        for w, s in zip(w_refs, w_scr):
            s[...] = w[...].astype(bf16)

    def flat(v):
        return v.reshape(v.shape[0], v.shape[1] * v.shape[2]) if v.ndim == 3 else v

    tm = out_refs[0].shape[0]
    sub = tm // row_splits
    for s0 in range(0, tm, sub):
        rows = slice(s0, s0 + sub)
        x = prologue([flat(r[rows]) for r in x_refs], [r[...] for r in kv_refs])
        accs = [jnp.dot(x, s[...], preferred_element_type=f32) for s in w_scr]
        res = epilogue(accs, [r[...] for r in cv_refs], [r[rows] for r in rm_refs])
        for o, r in zip(out_refs, res):
            o[rows] = r.astype(o.dtype).reshape((sub,) + o.shape[1:])


def _mm(*, name="proj", xs, ws, kvecs=(), cvecs=(), rmats=(), outs, K, tm, tn, n_row_tiles, n_col_tiles,
        prologue, epilogue, aliases=None, row_splits=1):
    in_specs, args = [], []
    for a, r0 in xs:
        if a.ndim == 3:
            in_specs.append(pl.BlockSpec((tm, K // 128, 128), lambda j, i, r0=r0: (i + r0, 0, 0)))
        else:
            in_specs.append(pl.BlockSpec((tm, K), lambda j, i, r0=r0: (i + r0, 0)))
        args.append(a)
    for a, lead, c0 in ws:
        nl = len(lead)
        in_specs.append(pl.BlockSpec((None,) * nl + (K, tn), lambda j, i, lead=lead, c0=c0: lead + (0, j + c0)))
        args.append(a)
    for a in kvecs:
        in_specs.append(pl.BlockSpec(a.shape, lambda j, i: (0, 0)))
        args.append(a)
    for a, c0 in cvecs:
        in_specs.append(pl.BlockSpec((a.shape[0], tn), lambda j, i, c0=c0: (0, j + c0)))
        args.append(a)
    for a, r0, c0 in rmats:
        in_specs.append(pl.BlockSpec((tm, tn), lambda j, i, r0=r0, c0=c0: (i + r0, j + c0)))
        args.append(a)
    out_specs = [pl.BlockSpec((tm, tn // 128, 128), lambda j, i, r0=r0, c0=c0: (i + r0, j + c0, 0))
                 if len(s.shape) == 3 else
                 pl.BlockSpec((tm, tn), lambda j, i, r0=r0, c0=c0: (i + r0, j + c0)) for s, r0, c0 in outs]
    body = functools.partial(_mm_body, n_x=len(xs), n_w=len(ws), n_kv=len(kvecs), n_cv=len(cvecs),
                             n_rm=len(rmats), n_out=len(outs), prologue=prologue, epilogue=epilogue,
                             row_splits=row_splits)
    res = pl.pallas_call(
        body,
        grid=(n_col_tiles, n_row_tiles),
        in_specs=in_specs,
        out_specs=out_specs,
        out_shape=[s for s, _, _ in outs],
        scratch_shapes=[pltpu.VMEM((K, tn), bf16) for _ in ws],
        input_output_aliases=aliases or {},
        name=name,
        compiler_params=_cp(("arbitrary", "arbitrary")),
    )(*args)
    return res


def _pro_id(xs, kvs):
    return xs[0]


def _pro_mul(xs, kvs):
    return (xs[0] * xs[1]).astype(bf16)


def _sds(shape, dtype):
    return jax.ShapeDtypeStruct(shape, dtype)


def _mixer_out_kernel(*refs, n_u, prologue):
    xm_ref, xa_ref = refs[:2]
    u_refs = refs[2:2 + n_u]
    w_ref, h_ref, g_ref, ho_ref, xn_ref, w_s = refs[2 + n_u:]
    i = pl.program_id(0)

    @pl.when(i == 0)
    def _cast():
        w_s[...] = w_ref[...].astype(bf16)

    def run(x_ref):
        x = x_ref[...]
        if x.ndim == 3:
            x = x.reshape(x.shape[0], x.shape[1] * x.shape[2])
        hn = h_ref[...] + jnp.dot(prologue([x] + [u[...] for u in u_refs], []), w_s[...],
                                  preferred_element_type=f32)
        ho_ref[...] = hn
        ms = jnp.mean(hn * hn, axis=-1, keepdims=True)
        xn_ref[...] = (hn * lax.rsqrt(ms + EPS) * g_ref[...]).astype(xn_ref.dtype)

    @pl.when(i < ROWS_MAIN // AUX)
    def _main():
        run(xm_ref)

    @pl.when(i == ROWS_MAIN // AUX)
    def _aux():
        run(xa_ref)


def _mixer_out(h, x_main, x_aux, unified, w, lead, norm_g, prologue):
    nm = ROWS_MAIN // AUX
    def xspec(a, imap):
        if a.ndim == 3:
            return pl.BlockSpec((AUX,) + a.shape[1:], lambda i: (imap(i), 0, 0))
        return pl.BlockSpec((AUX, a.shape[1]), lambda i: (imap(i), 0))
    nl = len(lead)
    row = pl.BlockSpec((AUX, D), lambda i: (i, 0))
    in_specs = ([xspec(x_main, lambda i: jnp.minimum(i, nm - 1)), xspec(x_aux, lambda i: 0)]
                + [xspec(u, lambda i: i) for u in unified]
                + [pl.BlockSpec((None,) * nl + (D, D), lambda i: lead + (0, 0), pipeline_mode=pl.Buffered(1)),
                   row, pl.BlockSpec((1, D), lambda i: (0, 0))])
    return pl.pallas_call(
        functools.partial(_mixer_out_kernel, n_u=len(unified), prologue=prologue),
        grid=(nm + 1,),
        in_specs=in_specs,
        out_specs=[row, row],
        out_shape=[_sds((ROWS, D), f32), _sds((ROWS, D), bf16)],
        scratch_shapes=[pltpu.VMEM((D, D), bf16)],
        input_output_aliases={3 + len(unified): 0},
        name="mixer_out",
        compiler_params=_cp(("arbitrary",)),
    )(x_main, x_aux, *unified, w, h, norm_g)


def _ffn(h, xn, w_in, w_out, layer):
    tn = 512
    nct = D_FF // tn
    act = _mm(name="ffn_in", xs=[(xn, 0)], ws=[(w_in, (layer,), 0), (w_in, (layer,), nct)],
              outs=[(_sds((ROWS, D_FF), bf16), 0, 0)],
              K=D, tm=TM_BIG, tn=tn, n_row_tiles=ROWS // TM_BIG, n_col_tiles=nct,
              prologue=_pro_id,
              epilogue=lambda accs, cvs, rms: (jax.nn.silu(accs[0]) * accs[1],))[0]
    h = _mm(name="ffn_out", xs=[(act, 0)], ws=[(w_out, (layer,), 0)], rmats=[(h, 0, 0)],
            outs=[(_sds((ROWS, D), f32), 0, 0)],
            K=D_FF, tm=TM_MID, tn=tn, n_row_tiles=ROWS // TM_MID, n_col_tiles=D // tn,
            prologue=_pro_id, epilogue=lambda accs, cvs, rms: (rms[0] + accs[0],),
            aliases={2: 0})[0]
    return h


def _hgrn_lower_bound(logits, j):
    m = jnp.max(logits, axis=0, keepdims=True)
    e = jnp.exp(logits - m)
    p = e / jnp.sum(e, axis=0, keepdims=True)
    cs = p[0:1]
    for r in range(1, j + 1):
        cs = cs + p[r:r + 1]
    return jnp.clip(cs - p[0:1], 0.0, 1.0)


def _hgrn_in(xn, w_in, lb_logits, j):
    tn = 1024
    nct = D // tn
    common = dict(K=D, tm=TM_BIG, tn=tn, n_row_tiles=ROWS // TM_BIG, n_col_tiles=nct, prologue=_pro_id,
                  row_splits=2)
    o = (_sds((ROWS, D), f32), 0, 0)
    q = _mm(xs=[(xn, 0)], ws=[(w_in, (j,), 0)], outs=[o],
            epilogue=lambda accs, cvs, rms: (jax.nn.silu(accs[0]),), **common)[0]

    def f_epi(accs, cvs, rms):
        f = accs[0]
        lb = _hgrn_lower_bound(cvs[0], j)
        t = jnp.exp(-jnp.abs(f))
        log_f = (-(jnp.maximum(-f, 0.0) + jnp.log(1.0 + t))
                 + jnp.log(1.0 + lb * jnp.exp(jnp.minimum(-f, GATE_EXP_CLIP))))
        k = (1.0 - lb) * (jnp.where(f >= 0.0, t, 1.0) / (1.0 + t))
        return log_f, k

    log_f, k = _mm(xs=[(xn, 0)], ws=[(w_in, (j,), nct)], cvecs=[(lb_logits, 0)], outs=[o, o],
                   epilogue=f_epi, **{**common, "tm": TM_MID, "n_row_tiles": ROWS // TM_MID, "row_splits": 3})
    v = _mm(xs=[(xn, 0)], ws=[(w_in, (j,), 2 * nct)], outs=[o],
            epilogue=lambda accs, cvs, rms: (accs[0],), **common)[0]
    g = _mm(xs=[(xn, 0)], ws=[(w_in, (j,), 3 * nct)], outs=[o],
            epilogue=lambda accs, cvs, rms: (jax.nn.silu(accs[0]),), **common)[0]
    return q, k, v, log_f, g


def _split3(x):
    hi = x.astype(bf16)
    r = x - hi.astype(f32)
    mid = r.astype(bf16)
    lo = (r - mid.astype(f32)).astype(bf16)
    return hi, mid, lo


def _gla_consts(T):
    t = np.arange(T)[:, None]
    w = np.arange(T)[None, :]
    mats = [w <= t, w > t]
    masks = []
    s = T // 2
    while s >= 1:
        piece = t // (2 * s)
        lower = (t % (2 * s)) >= s
        ref = piece * 2 * s + s
        mats.append(np.where(lower, (w > ref) & (w <= t), (w > t) & (w <= ref)))
        upiece = w // (2 * s)
        ulower = (w % (2 * s)) >= s
        masks.append((piece == upiece) & lower & ~ulower)
        s //= 2
    masks.append(t == w)
    m = np.concatenate(mats, 0)
    eye = np.eye(HG_UNROLL, dtype=bool)
    gmasks = np.stack([np.kron(eye, pm) for pm in masks])
    return (jnp.asarray(np.concatenate([m, m, m], 1), dtype=bf16), jnp.asarray(gmasks, dtype=f32))


def _gla_block(q_ref, k_ref, v_ref, f_ref, g_ref, gn_ref, dm_ref, pm_ref, o_ref, st_ref, e_s, ql_s, kl_s, T):
    nlev = T.bit_length() - 1
    dn_t = (((1,), (1,)), ((), ()))
    dn_l = (((0,), (0,)), ((), ()))
    def factors(ls):
        e_s[0:(nlev + 2) * T, ls] = jnp.dot(dm_ref[...], jnp.concatenate(_split3(f_ref[:, ls]), axis=0),
                                            preferred_element_type=f32)
        q = q_ref[:, ls]
        k = k_ref[:, ls]
        ql_s[0, 0:T, ls] = (q * jnp.exp(e_s[0:T, ls])).astype(bf16)
        kl_s[0, 0:T, ls] = (k * jnp.exp(e_s[T:2 * T, ls])).astype(bf16)
        for l in range(nlev):
            ex = jnp.exp(e_s[(l + 2) * T:(l + 3) * T, ls])
            ql_s[l + 1, 0:T, ls] = (q * ex).astype(bf16)
            kl_s[l + 1, 0:T, ls] = (k * ex).astype(bf16)
        ql_s[nlev + 1, 0:T, ls] = q.astype(bf16)
        kl_s[nlev + 1, 0:T, ls] = k.astype(bf16)

    for h0 in range(0, HG_H, HG_UNROLL):
        factors(slice(h0 * HG_K, (h0 + HG_UNROLL) * HG_K))
        sls = [slice(h * HG_K, (h + 1) * HG_K) for h in range(h0, h0 + HG_UNROLL)]
        att = None
        for l in range(nlev + 1):
            a_l = lax.dot_general(jnp.concatenate([ql_s[l + 1, 0:T, sl] for sl in sls], axis=0),
                                  jnp.concatenate([kl_s[l + 1, 0:T, sl] for sl in sls], axis=0),
                                  dn_t, preferred_element_type=f32) * pm_ref[l]
            att = a_l if att is None else att + a_l
        vbs = [v_ref[:, sl].astype(bf16) for sl in sls]
        o_intra = jnp.dot(att.astype(bf16), jnp.concatenate(vbs, axis=0), preferred_element_type=f32)
        for p in range(0, HG_UNROLL, 2):
            ha, hb = h0 + p, h0 + p + 1
            sla, slb = sls[p], sls[p + 1]
            st2 = jnp.concatenate([st_ref[ha], st_ref[hb]], axis=0)
            oc = lax.dot_general(jnp.concatenate([ql_s[0, 0:T, sla], ql_s[0, 0:T, slb]], axis=0),
                                 st2.astype(bf16), dn_t, preferred_element_type=f32)
            up = lax.dot_general(jnp.concatenate([vbs[p], vbs[p + 1]], axis=1),
                                 jnp.concatenate([kl_s[0, 0:T, sla], kl_s[0, 0:T, slb]], axis=1),
                                 dn_l, preferred_element_type=f32)
            for i, (h, sl) in enumerate(((ha, sla), (hb, slb))):
                o = (oc[i * T:(i + 1) * T, i * HG_K:(i + 1) * HG_K]
                     + o_intra[(p + i) * T:(p + i + 1) * T])
                st_ref[h] = (st2[i * HG_K:(i + 1) * HG_K] * jnp.exp(e_s[T - 1:T, sl])
                             + up[i * HG_K:(i + 1) * HG_K, i * HG_K:(i + 1) * HG_K])
                ms = jnp.mean(o * o, axis=-1, keepdims=True)
                on = o * lax.rsqrt(ms + EPS) * gn_ref[:, sl]
                o_ref[:, sl] = (on * g_ref[:, sl]).astype(o_ref.dtype)


def _gla_kernel(qm, km, vm, fm, gm, qa, ka, va, fa, ga, gn_ref, dmm, pmm, dma, pma,
                om_ref, oa_ref, s_ref, st_ref, e_s, ql_s, kl_s):
    c = pl.program_id(1)

    @pl.when(c == 0)
    def _meta():
        st_ref[...] = jnp.zeros_like(st_ref)
        _gla_block(qa, ka, va, fa, ga, gn_ref, dma, pma, oa_ref, st_ref, e_s, ql_s, kl_s, NM)

    @pl.when(c > 0)
    def _main():
        _gla_block(qm, km, vm, fm, gm, gn_ref, dmm, pmm, om_ref, st_ref, e_s, ql_s, kl_s, HG_C)

    @pl.when(c == pl.num_programs(1) - 1)
    def _final():
        for h in range(HG_H):
            s_ref[0, h] = st_ref[h].T


def _hgrn_prompt(q, k, v, log_f, g, gn):
    nc = LP // HG_C
    main_spec = pl.BlockSpec((HG_C, D), lambda b, c: (b * nc + jnp.maximum(c - 1, 0), 0))
    meta_spec = pl.BlockSpec((NM, D), lambda b, c: (ROW_M // NM + b, 0))
    arrs = [q, k, v, log_f, g]
    consts = [*_gla_consts(HG_C), *_gla_consts(NM)]
    nslot = HG_C.bit_length() + 1
    return pl.pallas_call(
        _gla_kernel,
        grid=(NB, nc + 1),
        in_specs=[main_spec] * 5 + [meta_spec] * 5 + [pl.BlockSpec((1, D), lambda b, c: (0, 0))]
                 + [pl.BlockSpec(a.shape, lambda b, c, n=a.ndim: (0,) * n) for a in consts],
        out_specs=[pl.BlockSpec((HG_C, D), lambda b, c: (b * nc + jnp.maximum(c - 1, 0), 0)),
                   pl.BlockSpec((NM, D), lambda b, c: (b, 0)),
                   pl.BlockSpec((1, HG_H, HG_K, HG_K), lambda b, c: (b, 0, 0, 0))],
        out_shape=[_sds((ROWS_MAIN, D), bf16), _sds((NB * NM, D), bf16), _sds((NB, HG_H, HG_K, HG_K), f32)],
        scratch_shapes=[pltpu.VMEM((HG_H, HG_K, HG_K), f32),
                        pltpu.VMEM((nslot * HG_C, D), f32),
                        pltpu.VMEM((nslot, HG_C, D), bf16),
                        pltpu.VMEM((nslot, HG_C, D), bf16)],
        name="hgrn_prompt",
        compiler_params=_cp(("arbitrary", "arbitrary")),
    )(*arrs, *arrs, gn, *consts)


HG_BT = 8
HG_SB = 4


def _hgrn_dec_kernel(s_ref, qT, kT, fT, v_ref, g_ref, gn_ref, e3_ref, *rest, stack_slot):
    so_ref, o_ref, qb_s, kb_s, gb_s = rest[-5:]
    if stack_slot is not None:
        for m, e_ref in enumerate(rest[:-5]):
            so_ref[m] = e_ref[...]
    row0 = pl.program_id(1) * HG_SB

    def spread(m, dst):
        dst[...] = jnp.dot(jnp.concatenate(_split3(m), axis=1), e3_ref[...], preferred_element_type=f32)

    def per_b(bb, carry):
        spread(qT[bb], qb_s)
        spread(kT[bb], kb_s)
        spread(jnp.exp(fT[bb]), gb_s)
        row = pl.ds(row0 + bb, 1)
        vall = v_ref[row, :]
        outs = []
        for h in range(HG_H):
            hs = slice(h * HG_K, (h + 1) * HG_K)
            sn = gb_s[:, hs] * s_ref[bb, h] + kb_s[:, hs] * vall[:, hs]
            if stack_slot is None:
                so_ref[bb, h] = sn
            else:
                so_ref[stack_slot, bb, h] = sn
            o = jnp.sum(qb_s[:, hs] * sn, axis=0, keepdims=True)
            ms = jnp.mean(o * o, axis=-1, keepdims=True)
            outs.append(o * lax.rsqrt(ms + EPS))
        on = jnp.concatenate(outs, axis=1) * gn_ref[...]
        o_ref[row, :] = on * g_ref[row, :]
        return carry

    lax.fori_loop(0, HG_SB, per_b, 0)


def _hgrn_sample(state, j, q, k, v, log_f, g, gn, earlier):
    def colform(x):
        return x[ROW_S:ROW_S + NS].reshape(NS, HG_H, HG_K).transpose(0, 2, 1)

    n_a = state.shape[0]
    last = j == n_a - 1
    head_of_lane = np.arange(D) // HG_K
    sel = (np.arange(HG_H)[:, None] == head_of_lane[None, :])
    e3 = jnp.asarray(np.concatenate([sel, sel, sel], axis=0), dtype=bf16)
    nsb = HG_BT // HG_SB
    col_spec = pl.BlockSpec((HG_SB, HG_K, HG_H), lambda i, s: (i * nsb + s, 0, 0))
    row_spec = pl.BlockSpec((HG_BT, D), lambda i, s: (ROW_S // HG_BT + i, 0))
    new_spec = pl.BlockSpec((HG_SB, HG_H, HG_K, HG_K), lambda i, s: (i * nsb + s, 0, 0, 0))
    in_specs = [pl.BlockSpec((None, HG_SB, HG_H, HG_K, HG_K), lambda i, s: (j, i * nsb + s, 0, 0, 0)),
                col_spec, col_spec, col_spec, row_spec, row_spec,
                pl.BlockSpec((1, D), lambda i, s: (0, 0)), pl.BlockSpec((3 * HG_H, D), lambda i, s: (0, 0))]
    args = [state, colform(q), colform(k), colform(log_f), v, g, gn, e3]
    if last:
        in_specs += [new_spec] * len(earlier)
        args += list(earlier)
        st_out = pl.BlockSpec((n_a, HG_SB, HG_H, HG_K, HG_K), lambda i, s: (0, i * nsb + s, 0, 0, 0))
        st_shape = _sds(state.shape, f32)
    else:
        st_out, st_shape = new_spec, _sds(state.shape[1:], f32)
    return pl.pallas_call(
        functools.partial(_hgrn_dec_kernel, stack_slot=j if last else None),
        grid=(NS // HG_BT, nsb),
        in_specs=in_specs,
        out_specs=[st_out, pl.BlockSpec((HG_BT, D), lambda i, s: (i, 0))],
        out_shape=[st_shape, _sds((NS, D), f32)],
        scratch_shapes=[pltpu.VMEM((HG_K, D), f32)] * 3,
        name="hgrn_decode",
        compiler_params=_cp(("arbitrary", "arbitrary")),
    )(*args)


def _aux_rows(sample_rows, meta_rows):
    pad = jnp.zeros((AUX - NS - NB * NM, sample_rows.shape[1]), sample_rows.dtype)
    return jnp.concatenate([sample_rows, meta_rows.astype(sample_rows.dtype), pad], axis=0)


def _hgrn_layer(h, norm_g, ffn_g, state, earlier, j, w_in, lb_logits, gnorm, w_out, xn=None):
    if xn is None:
        xn = _rmsnorm(h, norm_g, out_dtype=bf16, tm=TM_MID, row_tile0=0, n_row_tiles=ROWS // TM_MID)
    q, k, v, log_f, g = _hgrn_in(xn, w_in, lb_logits, j)
    gn = gnorm[j].reshape(1, D)
    o_main, o_meta, s_p = _hgrn_prompt(q, k, v, log_f, g, gn)
    s_s, o_s = _hgrn_sample(state, j, q, k, v, log_f, g, gn, earlier)
    o_aux = _aux_rows(o_s.astype(bf16), o_meta)
    h, xn_f = _mixer_out(h, o_main, o_aux, [], w_out, (j,), ffn_g, _pro_id)
    return (h, xn_f), s_p, s_s


def _lru_in(xn, w_in, j):
    tn = 1024
    nct = D // tn
    common = dict(K=D, tm=TM_BIG, tn=tn, n_row_tiles=ROWS // TM_BIG, n_col_tiles=nct, prologue=_pro_id,
                  row_splits=2)
    o = (_sds((ROWS, D), f32), 0, 0)
    y = _mm(xs=[(xn, 0)], ws=[(w_in, (j,), 0)], outs=[o],
            epilogue=lambda accs, cvs, rms: (jax.nn.gelu(accs[0], approximate=True),), **common)[0]
    x = _mm(xs=[(xn, 0)], ws=[(w_in, (j,), nct)], outs=[o],
            epilogue=lambda accs, cvs, rms: (accs[0],), **common)[0]
    return y, x


def _lru_gates(xc, gw_ref, gb_ref, lam_ref, n):
    ls = slice(n * LRU_BW, (n + 1) * LRU_BW)
    xb = xc[:, ls].astype(bf16)
    r = jax.nn.sigmoid(jnp.dot(xb, gw_ref[0, n], preferred_element_type=f32) + gb_ref[0:1, ls])
    ig = jax.nn.sigmoid(jnp.dot(xb, gw_ref[1, n], preferred_element_type=f32) + gb_ref[1:2, ls])
    log_a = -LRU_C * r * _softplus(-lam_ref[:, ls])
    a = jnp.exp(log_a)
    mult = jnp.sqrt(jnp.maximum(1.0 - a * a, 0.0))
    return a, ig, mult


def _lru_conv(cb_ref, cw_ref, x0, x1, x2, x3):
    xc = cb_ref[...] + x0 * cw_ref[0:1, :]
    xc = xc + x1 * cw_ref[1:2, :]
    xc = xc + x2 * cw_ref[2:3, :]
    return xc + x3 * cw_ref[3:4, :]


def _lru_block(x_ref, y_ref, o_ref, cw_ref, cb_ref, gw_s, gb_ref, lam_ref, xbuf, a_s, b_s, hcar, T, first):
    xbuf[8:8 + T, :] = x_ref[...]
    xc = _lru_conv(cb_ref, cw_ref, xbuf[5:5 + T, :], xbuf[6:6 + T, :], xbuf[7:7 + T, :], xbuf[8:8 + T, :])
    xbuf[0:8, :] = xbuf[T:T + 8, :]
    row = lax.broadcasted_iota(jnp.int32, (T, 1), 0)
    for n in range(LRU_NBLK):
        ls = slice(n * LRU_BW, (n + 1) * LRU_BW)
        a, ig, mult = _lru_gates(xc, gw_s, gb_ref, lam_ref, n)
        if first:
            mult = jnp.where(row == 0, 1.0, mult)
        a_s[0:T, ls] = a
        b_s[0:T, ls] = xc[:, ls] * ig * mult

    def step(t, hprev):
        hnew = a_s[pl.ds(t, 1), :] * hprev + b_s[pl.ds(t, 1), :]
        b_s[pl.ds(t, 1), :] = hnew
        return hnew

    hcar[...] = lax.fori_loop(0, T, step, hcar[...], unroll=8)
    o_ref[...] = (b_s[0:T, :] * y_ref[...]).astype(o_ref.dtype)


def _lru_kernel(xm, ym, xa, ya, cw_ref, cb_ref, gw_ref, gb_ref, lam_ref,
                om_ref, oa_ref, hl_ref, cv_ref, gw_s, xbuf, a_s, b_s, hcar):
    c = pl.program_id(1)

    @pl.when((pl.program_id(0) == 0) & (c == 0))
    def _cast():
        gw_s[...] = gw_ref[...].astype(bf16)

    @pl.when(c == 0)
    def _meta():
        xbuf[0:8, :] = jnp.zeros((8, D), f32)
        hcar[...] = jnp.zeros_like(hcar)
        _lru_block(xa, ya, oa_ref, cw_ref, cb_ref, gw_s, gb_ref, lam_ref, xbuf, a_s, b_s, hcar, NM, True)

    @pl.when(c > 0)
    def _main():
        _lru_block(xm, ym, om_ref, cw_ref, cb_ref, gw_s, gb_ref, lam_ref, xbuf, a_s, b_s, hcar, LRU_T, False)

    @pl.when(c == pl.num_programs(1) - 1)
    def _final():
        hl_ref[0] = hcar[...]
        cv_ref[0] = xbuf[5:8, :]


def _lru_prompt(x, y, j, conv_w, conv_b, gate_w, gate_b, lam):
    nc = LP // LRU_T
    main_spec = pl.BlockSpec((LRU_T, D), lambda b, c: (b * nc + jnp.maximum(c - 1, 0), 0))
    meta_spec = pl.BlockSpec((NM, D), lambda b, c: (ROW_M // NM + b, 0))
    return pl.pallas_call(
        _lru_kernel,
        grid=(NB, nc + 1),
        in_specs=[main_spec, main_spec, meta_spec, meta_spec,
                  pl.BlockSpec((None, 4, D), lambda b, c: (j, 0, 0)),
                  pl.BlockSpec((1, D), lambda b, c: (j, 0)),
                  pl.BlockSpec((None, 2, LRU_NBLK, LRU_BW, LRU_BW), lambda b, c: (j, 0, 0, 0, 0)),
                  pl.BlockSpec((None, 2, D), lambda b, c: (j, 0, 0)),
                  pl.BlockSpec((1, D), lambda b, c: (j, 0))],
        out_specs=[pl.BlockSpec((LRU_T, D), lambda b, c: (b * nc + jnp.maximum(c - 1, 0), 0)),
                   pl.BlockSpec((NM, D), lambda b, c: (b, 0)),
                   pl.BlockSpec((1, 1, D), lambda b, c: (b, 0, 0)),
                   pl.BlockSpec((1, 3, D), lambda b, c: (b, 0, 0))],
        out_shape=[_sds((ROWS_MAIN, D), bf16), _sds((NB * NM, D), bf16),
                   _sds((NB, 1, D), f32), _sds((NB, 3, D), f32)],
        scratch_shapes=[pltpu.VMEM((2, LRU_NBLK, LRU_BW, LRU_BW), bf16),
                        pltpu.VMEM((LRU_T + 8, D), f32), pltpu.VMEM((LRU_T, D), f32),
                        pltpu.VMEM((LRU_T, D), f32), pltpu.VMEM((1, D), f32)],
        name="lru_prompt",
        compiler_params=_cp(("arbitrary", "arbitrary")),
    )(x, y, x, y, conv_w, conv_b, gate_w, gate_b, lam)


LRU_ST = 64


def _lru_dec_kernel(x_ref, y_ref, h0_ref, cbuf_ref, cw_ref, cb_ref, gw_ref, gb_ref, lam_ref,
                    o_ref, hn_ref, cn_ref, gw_s):
    gw_s[...] = gw_ref[...].astype(bf16)
    x = x_ref[...]
    b0 = cbuf_ref[:, 0, :]
    b1 = cbuf_ref[:, 1, :]
    b2 = cbuf_ref[:, 2, :]
    xc = _lru_conv(cb_ref, cw_ref, b0, b1, b2, x)
    cn_ref[:, 0, :] = b1
    cn_ref[:, 1, :] = b2
    cn_ref[:, 2, :] = x
    for n in range(LRU_NBLK):
        ls = slice(n * LRU_BW, (n + 1) * LRU_BW)
        a, ig, mult = _lru_gates(xc, gw_s, gb_ref, lam_ref, n)
        hn = a * h0_ref[:, ls] + xc[:, ls] * ig * mult
        hn_ref[:, ls] = hn
        o_ref[:, ls] = (hn * y_ref[:, ls]).astype(o_ref.dtype)


def _lru_sample(x, y, j, h0, cbuf, conv_w, conv_b, gate_w, gate_b, lam):
    row_spec = pl.BlockSpec((LRU_ST, D), lambda i: (ROW_S // LRU_ST + i, 0))
    return pl.pallas_call(
        _lru_dec_kernel,
        grid=(NS // LRU_ST,),
        in_specs=[row_spec, row_spec,
                  pl.BlockSpec((None, LRU_ST, D), lambda i: (j, i, 0)),
                  pl.BlockSpec((None, LRU_ST, 3, D), lambda i: (j, i, 0, 0)),
                  pl.BlockSpec((None, 4, D), lambda i: (j, 0, 0)),
                  pl.BlockSpec((1, D), lambda i: (j, 0)),
                  pl.BlockSpec((None, 2, LRU_NBLK, LRU_BW, LRU_BW), lambda i: (j, 0, 0, 0, 0)),
                  pl.BlockSpec((None, 2, D), lambda i: (j, 0, 0)),
                  pl.BlockSpec((1, D), lambda i: (j, 0))],
        out_specs=[pl.BlockSpec((LRU_ST, D), lambda i: (i, 0)),
                   pl.BlockSpec((LRU_ST, D), lambda i: (i, 0)),
                   pl.BlockSpec((LRU_ST, 3, D), lambda i: (i, 0, 0))],
        out_shape=[_sds((NS, D), bf16), _sds((NS, D), f32), _sds((NS, 3, D), f32)],
        scratch_shapes=[pltpu.VMEM((2, LRU_NBLK, LRU_BW, LRU_BW), bf16)],
        name="lru_decode",
        compiler_params=_cp(("arbitrary",)),
    )(x, y, h0, cbuf, conv_w, conv_b, gate_w, gate_b, lam)


def _lru_layer(h, norm_g, ffn_g, state_h, state_conv, j, w_in, conv_w, conv_b, gate_w, gate_b, lam, w_out):
    xn = _rmsnorm(h, norm_g, out_dtype=bf16, tm=TM_MID, row_tile0=0, n_row_tiles=ROWS // TM_MID)
    y, x = _lru_in(xn, w_in, j)
    o_main, o_meta, hl_p, cv_p = _lru_prompt(x, y, j, conv_w, conv_b, gate_w, gate_b, lam)
    o_s, hl_s, cv_s = _lru_sample(x, y, j, state_h, state_conv, conv_w, conv_b, gate_w, gate_b, lam)
    o_aux = _aux_rows(o_s, o_meta)
    h, xn_f = _mixer_out(h, o_main, o_aux, [], w_out, (j,), ffn_g, _pro_id)
    return (h, xn_f), hl_p.reshape(NB, D), hl_s, cv_p, cv_s


def _rwkv_premix_kernel(h_ref, g_ref, mu_ref, sh_ref, *refs):
    x_refs = refs[:6]
    sp_ref, ss_ref, xbuf, meta_last = refs[6:]
    i = pl.program_id(0)
    x = h_ref[...]
    ms = jnp.mean(x * x, axis=-1, keepdims=True)
    xn = x * lax.rsqrt(ms + EPS) * g_ref[...]
    xbuf[8:8 + PM_T, :] = xn
    row = lax.broadcasted_iota(jnp.int32, (PM_T, 1), 0)

    @pl.when(i == 0)
    def _aux():
        xbuf[7:8, :] = jnp.zeros((1, D), f32)
        ss_ref[...] = xn[0:NS]
        for b in range(NB):
            meta_last[b:b + 1, :] = xn[NS + b * NM + NM - 1:NS + b * NM + NM]

    @pl.when(i > 0)
    def _main():
        m = i - 1
        b = m // (LP // PM_T)

        @pl.when(m % (LP // PM_T) == 0)
        def _start():
            xbuf[7:8, :] = meta_last[pl.ds(b, 1), :]

        @pl.when(m % (LP // PM_T) == LP // PM_T - 1)
        def _end():
            sp_ref[pl.ds(b, 1), :] = xn[PM_T - 1:PM_T]

    shifted = xbuf[7:7 + PM_T, :]
    is_meta = (row >= NS) & (row < NS + NB * NM) & ((row - NS) % NM != 0)
    sh_pad = jnp.concatenate([sh_ref[...], jnp.zeros((PM_T - NS, D), f32)], axis=0)
    prev_aux = jnp.where(row < NS, sh_pad, jnp.where(is_meta, shifted, 0.0))
    prev = jnp.where(i == 0, prev_aux, shifted)
    dx = prev - xn
    for n in range(6):
        x_refs[n][...] = (xn + dx * mu_ref[n:n + 1, :]).astype(bf16)
    xbuf[7:8, :] = xn[PM_T - 1:PM_T]


def _rwkv_premix(h, norm_g, mu_j, shift_j):
    nt = ROWS // PM_T
    rows = lambda i: (jnp.where(i == 0, nt - 1, i - 1), 0)
    return pl.pallas_call(
        _rwkv_premix_kernel,
        grid=(nt,),
        in_specs=[pl.BlockSpec((PM_T, D), rows),
                  pl.BlockSpec((1, D), lambda i: (0, 0)),
                  pl.BlockSpec((6, D), lambda i: (0, 0)),
                  pl.BlockSpec((NS, D), lambda i: (0, 0))],
        out_specs=[pl.BlockSpec((PM_T, D), rows)] * 6 + [pl.BlockSpec((NB, D), lambda i: (0, 0)),
                                                         pl.BlockSpec((NS, D), lambda i: (0, 0))],
        out_shape=[_sds((ROWS, D), bf16)] * 6 + [_sds((NB, D), f32), _sds((NS, D), f32)],
        scratch_shapes=[pltpu.VMEM((PM_T + 8, D), f32), pltpu.VMEM((8, D), f32)],
        name="rwkv_premix",
        compiler_params=_cp(("arbitrary",)),
    )(h, norm_g, mu_j, shift_j)


def _rwkv_lora_kernel(xw_ref, xa_ref, xg_ref, w1_ref, w2_ref, a1_ref, a2_ref, g1_ref, g2_ref, w0_ref, a0_ref,
                      d_ref, a_ref, g_ref, w1_s, w2_s, a1_s, a2_s, g1_s, g2_s):
    @pl.when(pl.program_id(0) == 0)
    def _cast():
        for src, dst in ((w1_ref, w1_s), (w2_ref, w2_s), (a1_ref, a1_s), (a2_ref, a2_s), (g1_ref, g1_s),
                         (g2_ref, g2_s)):
            dst[...] = src[...].astype(bf16)

    def mm(u, w_s):
        return jnp.dot(u, w_s[...], preferred_element_type=f32)

    tw = jnp.tanh(mm(xw_ref[...], w1_s)).astype(bf16)
    x = w0_ref[...] + mm(tw, w2_s)
    d_ref[...] = jnp.exp(-np.float32(np.exp(-0.5)) * jax.nn.sigmoid(x)).reshape(d_ref.shape)
    ta = mm(xa_ref[...], a1_s).astype(bf16)
    a_ref[...] = jax.nn.sigmoid(a0_ref[...] + mm(ta, a2_s)).reshape(a_ref.shape)
    tg = jax.nn.sigmoid(mm(xg_ref[...], g1_s)).astype(bf16)
    g_ref[...] = mm(tg, g2_s)


def _rwkv_lora(xw, xa, xg, w1, w2, a1, a2, g1, g2, w0, a0):
    tm = 384
    full = lambda a: pl.BlockSpec(a.shape, lambda i: (0,) * a.ndim)
    row = pl.BlockSpec((tm, D), lambda i: (i, 0))
    row3 = pl.BlockSpec((tm, RW_Q, 128), lambda i: (i, 0, 0))
    ws = [w1, w2, a1, a2, g1, g2]
    return pl.pallas_call(
        _rwkv_lora_kernel,
        grid=(ROWS // tm,),
        in_specs=[row, row, row] + [full(a) for a in ws + [w0, a0]],
        out_specs=[row3, row3, row],
        out_shape=[_sds((ROWS, RW_Q, 128), f32)] * 2 + [_sds((ROWS, D), f32)],
        scratch_shapes=[pltpu.VMEM(a.shape, bf16) for a in ws],
        name="rwkv_lora",
        compiler_params=_cp(("arbitrary",)),
    )(xw, xa, xg, *ws, w0, a0)


def _rwkv_step(t, r_s, k_s, v_s, d_s, a_s, z_s, s_s, tmp, kk_p, ka_p, rk_p, lw_p, lb_p):
    kk_t, d_t, ka_t, k2_t, r_t = tmp
    r = r_s[t]
    k = k_s[t]
    v = v_s[t]
    a = a_s[t]
    kkr = k * kk_p[...]
    nrm = jnp.sqrt(jnp.sum(kkr * kkr, axis=0, keepdims=True))
    kk = kkr / jnp.maximum(nrm, 1e-12)
    k2 = k * (1.0 + (a - 1.0) * ka_p[...])
    kk_t[...] = kk
    d_t[...] = d_s[t]
    ka_t[...] = kk * a
    k2_t[...] = k2
    r_t[...] = r

    sa = -(s_s[0] * kk_t[0:1, :])
    for j in range(1, RW_N):
        sa = sa - s_s[j] * kk_t[j:j + 1, :]
    y = None
    for j in range(RW_N):
        sn = s_s[j] * d_t[j:j + 1, :] + sa * ka_t[j:j + 1, :] + v * k2_t[j:j + 1, :]
        s_s[j] = sn
        yj = sn * r_t[j:j + 1, :]
        y = yj if y is None else y + yj
    mean = jnp.mean(y, axis=0, keepdims=True)
    yc = y - mean
    var = jnp.mean(yc * yc, axis=0, keepdims=True)
    yn = yc * lax.rsqrt(var + RW_LN_EPS) * lw_p[...] + lb_p[...]
    bonus = jnp.sum(r * k2 * rk_p[...], axis=0, keepdims=True) * v
    z_s[t] = yn + bonus


def _rwkv_scan_kernel(r_ref, k_ref, v_ref, d_ref, a_ref, s0_ref, kk_p, ka_p, rk_p, lw_p, lb_p,
                      z_ref, sf_ref, s_s, *tmp):
    c = pl.program_id(1)

    @pl.when(c == 0)
    def _init():
        s_s[...] = s0_ref[...]

    def step(t, carry):
        _rwkv_step(t, r_ref, k_ref, v_ref, d_ref, a_ref, z_ref, s_s, tmp, kk_p, ka_p, rk_p, lw_p, lb_p)
        return carry

    lax.fori_loop(0, r_ref.shape[0], step, 0)

    @pl.when(c == pl.num_programs(1) - 1)
    def _final():
        sf_ref[...] = s_s[...]


RW_Q = D // 128


def _half_transpose(x):
    xt = x.T
    return jnp.concatenate([xt[0:RW_N], xt[RW_N:2 * RW_N]], axis=1)


def _rwkv_prompt_kernel(*refs):
    n_in = 5 * NB
    in_refs = refs[:n_in]
    params = refs[n_in:n_in + 5]
    zmain, zmeta, sf_ref = refs[n_in + 5:n_in + 8]
    scr = refs[n_in + 8:]
    seq, z_s, zo_s, s_s, tmp = scr[:5], scr[5], scr[6], scr[7], scr[8:]
    c = pl.program_id(0)

    @pl.when(c == 0)
    def _init():
        s_s[...] = jnp.zeros_like(s_s)
        z_s[...] = jnp.zeros_like(z_s)

    def load_token(t):
        rows = pl.ds(pl.multiple_of(t * RW_Q, RW_Q), RW_Q)
        for a in range(5):
            tile = jnp.concatenate([in_refs[a * NB + b][rows, :] for b in range(NB)], axis=0)
            seq[a][t] = _half_transpose(tile)

    load_token(0)

    def step(t, carry):
        tp = jnp.maximum(t - 1, 0)
        zo_s[tp] = _half_transpose(z_s[tp])
        _rwkv_step(t, *seq, z_s, s_s, tmp, *params)
        load_token(jnp.minimum(t + 1, RW_TS - 1))
        return carry

    lax.fori_loop(0, RW_TS, step, 0)
    zo_s[RW_TS - 1] = _half_transpose(z_s[RW_TS - 1])

    def write(z_ref):
        for t in range(RW_TS):
            for b in range(NB):
                z_ref[b, t * RW_Q:(t + 1) * RW_Q, :] = zo_s[t, b * RW_Q:(b + 1) * RW_Q, :]

    @pl.when(c == 0)
    def _write_meta():
        write(zmeta)

    @pl.when(c > 0)
    def _write_main():
        write(zmain)

    @pl.when(c == pl.num_programs(0) - 1)
    def _final():
        sf_ref[...] = s_s[...]


def _rwkv_prompt(r, k, v, dec, a, params):
    nc = LP // RW_TS
    blk = RW_TS * RW_Q
    arrs = [x.reshape(ROWS * RW_Q, 128) for x in (r, k, v, dec, a)]
    in_specs, args = [], []
    for x in arrs:
        for b in range(NB):
            in_specs.append(pl.BlockSpec(
                (blk, 128), lambda c, b=b: (jnp.where(c == 0, ROW_M // RW_TS + b, b * nc + c - 1), 0)))
            args.append(x)
    par = pl.BlockSpec((RW_N, 128), lambda c: (0, 0))
    outs = pl.pallas_call(
        _rwkv_prompt_kernel,
        grid=(nc + 1,),
        in_specs=in_specs + [par] * 5,
        out_specs=[pl.BlockSpec((NB, blk, 128), lambda c: (0, jnp.maximum(c - 1, 0), 0)),
                   pl.BlockSpec((NB, blk, 128), lambda c: (0, 0, 0)),
                   pl.BlockSpec((RW_N, RW_N, 128), lambda c: (0, 0, 0))],
        out_shape=[_sds((NB, LP * RW_Q, 128), f32), _sds((NB, NM * RW_Q, 128), f32),
                   _sds((RW_N, RW_N, 128), f32)],
        scratch_shapes=[pltpu.VMEM((RW_TS, RW_N, 128), f32)] * 7 + [pltpu.VMEM((RW_N, RW_N, 128), f32)]
                       + [pltpu.VMEM((RW_N, 128), f32)] * 5,
        name="rwkv_prompt",
        compiler_params=_cp(("arbitrary",)),
    )(*args, *params)
    return outs[0].reshape(ROWS_MAIN, RW_Q, 128), outs[1].reshape(NB * NM, D), outs[2]


def _rwkv_scan(rT, kT, vT, dT, aT, s0T, params, tc):
    L, _, lanes = rT.shape
    nl = lanes // 128
    seq = pl.BlockSpec((tc, RW_N, 128), lambda l, c: (c, 0, l))
    st = pl.BlockSpec((RW_N, RW_N, 128), lambda l, c: (0, 0, l))
    par = pl.BlockSpec((RW_N, 128), lambda l, c: (0, l))
    return pl.pallas_call(
        _rwkv_scan_kernel,
        grid=(nl, L // tc),
        in_specs=[seq] * 5 + [st] + [par] * 5,
        out_specs=[seq, st],
        out_shape=[_sds((L, RW_N, lanes), f32), _sds((RW_N, RW_N, lanes), f32)],
        scratch_shapes=[pltpu.VMEM((RW_N, RW_N, 128), f32)] + [pltpu.VMEM((RW_N, 128), f32)] * 5,
        name="rwkv_scan",
        compiler_params=_cp(("arbitrary", "arbitrary")),
    )(rT, kT, vT, dT, aT, s0T, *params)


def _rwkv_layer(h, norm_g, ffn_g, state, shift, j, mu, w_rkv, w0, w1, w2, a0, a1, a2, g1, g2, k_k, k_a, r_k,
                ln_w, ln_b, w_o):
    *xmix, shift_p, shift_s = _rwkv_premix(h, norm_g, mu[j], shift[j])
    tn = 1024
    common = dict(K=D, tm=TM_BIG, tn=tn, n_row_tiles=ROWS // TM_BIG, n_col_tiles=D // tn, prologue=_pro_id,
                  epilogue=lambda accs, cvs, rms: (accs[0],))
    o = (_sds((ROWS, RW_Q, 128), f32), 0, 0)
    r, k, v = [_mm(name="rwkv_rkv", xs=[(xmix[n], 0)], ws=[(w_rkv, (j, n), 0)], outs=[o], **common)[0]
               for n in range(3)]

    def padc(w):
        return jnp.pad(w, ((0, 0), (0, LORA_PAD - w.shape[1])))

    def padr(w):
        return jnp.pad(w, ((0, LORA_PAD - w.shape[0]), (0, 0)))

    dec, a, g = _rwkv_lora(xmix[3], xmix[4], xmix[5], padc(w1[j]), padr(w2[j]), padc(a1[j]), padr(a2[j]),
                           g1[j], g2[j], w0[j].reshape(1, D), a0[j].reshape(1, D))

    def lanes_param(p, reps):
        return jnp.tile(p.reshape(RW_H, RW_N).T, (1, reps))

    def prompt_param(p):
        pt = p.reshape(RW_Q, 2, RW_N).transpose(2, 1, 0)
        return jnp.broadcast_to(pt[:, :, None, :], (RW_N, 2, NB, RW_Q)).reshape(RW_N, 128)

    def params(reps):
        return [lanes_param(k_k[j], reps), lanes_param(k_a[j], reps), lanes_param(r_k[j].reshape(D), reps),
                lanes_param(ln_w[j], reps), lanes_param(ln_b[j], reps)]

    z_main, z_meta, sT = _rwkv_prompt(r, k, v, dec, a, [prompt_param(p) for p in (
        k_k[j], k_a[j], r_k[j].reshape(D), ln_w[j], ln_b[j])])
    s_p = sT.reshape(RW_N, RW_N, 2, NB, RW_Q).transpose(3, 4, 2, 1, 0).reshape(NB, RW_H, RW_N, RW_N)

    def to_step(x):
        return x[ROW_S:ROW_S + NS].reshape(NS, RW_H, RW_N).transpose(2, 0, 1).reshape(1, RW_N, NS * RW_H)

    s0T = state[j].transpose(3, 2, 0, 1).reshape(RW_N, RW_N, NS * RW_H)
    zsT, ssT = _rwkv_scan(to_step(r), to_step(k), to_step(v), to_step(dec), to_step(a), s0T, params(NS), 1)
    s_s = ssT.reshape(RW_N, RW_N, NS, RW_H).transpose(2, 3, 1, 0)
    z_s = zsT.reshape(RW_N, NS, RW_H).transpose(1, 2, 0).reshape(NS, D)

    z_aux = _aux_rows(z_s, z_meta)
    h, xn_f = _mixer_out(h, z_main, z_aux, [g], w_o, (j,), ffn_g, _pro_mul)
    return (h, xn_f), s_p, s_s, shift_p, shift_s


def kernel(x_prompt, x_sample, state_hgrn, state_lru_h, state_lru_conv, state_rwkv, state_rwkv_shift, meta_tokens, norm_mix, norm_ffn, norm_final, hgrn_w_in, hgrn_lb_logits, hgrn_norm, hgrn_w_out, lru_w_in, lru_conv_w, lru_conv_b, lru_gate_w, lru_gate_b, lru_lambda, lru_w_out, rwkv_mu, rwkv_w_rkv, rwkv_w0, rwkv_w1, rwkv_w2, rwkv_a0, rwkv_a1, rwkv_a2, rwkv_g1, rwkv_g2, rwkv_k_k, rwkv_k_a, rwkv_r_k, rwkv_ln_w, rwkv_ln_b, rwkv_w_o, ffn_w_in, ffn_w_out):
    depth = norm_mix.shape[0]
    x_aux = _aux_rows(x_sample.reshape(NS, D), jnp.tile(meta_tokens, (NB, 1)))
    h, xn0 = _assemble(x_prompt.reshape(ROWS_MAIN, D), x_aux, norm_mix[0].reshape(1, D))
    hg_p, lh_p, lh_s, lc_p, lc_s, rw_p, rw_s, rs_p, rs_s = [[] for _ in range(9)]
    hg_s = []
    for i in range(depth):
        m, j = i % 3, i // 3
        ng = norm_mix[i].reshape(1, D)
        fg = norm_ffn[i].reshape(1, D)
        if m == 0:
            hx, sp, ss = _hgrn_layer(h, ng, fg, state_hgrn, hg_s, j, hgrn_w_in, hgrn_lb_logits, hgrn_norm,
                                     hgrn_w_out, xn=xn0 if i == 0 else None)
            hg_p.append(sp)
            hg_s.append(ss)
        elif m == 1:
            hx, hp_, hs_, cp_, cs_ = _lru_layer(h, ng, fg, state_lru_h, state_lru_conv, j, lru_w_in,
                                                lru_conv_w, lru_conv_b, lru_gate_w, lru_gate_b, lru_lambda,
                                                lru_w_out)
            lh_p.append(hp_)
            lh_s.append(hs_)
            lc_p.append(cp_)
            lc_s.append(cs_)
        else:
            hx, sp, ss, shp, shs = _rwkv_layer(h, ng, fg, state_rwkv, state_rwkv_shift, j, rwkv_mu, rwkv_w_rkv,
                                               rwkv_w0, rwkv_w1, rwkv_w2, rwkv_a0, rwkv_a1, rwkv_a2, rwkv_g1,
                                               rwkv_g2, rwkv_k_k, rwkv_k_a, rwkv_r_k, rwkv_ln_w, rwkv_ln_b,
                                               rwkv_w_o)
            rw_p.append(sp)
            rw_s.append(ss)
            rs_p.append(shp)
            rs_s.append(shs)
        h = _ffn(*hx, ffn_w_in, ffn_w_out, i)
    nf = norm_final.reshape(1, D)
    y_main = _rmsnorm(h, nf, out_dtype=f32, tm=512, row_tile0=0, n_row_tiles=ROWS_MAIN // 512)
    y_aux = _rmsnorm(h, nf, out_dtype=f32, tm=AUX, row_tile0=ROWS_MAIN // AUX, n_row_tiles=1)
    y_prompt = y_main.reshape(NB, LP, D)
    y_sample = y_aux[:NS].reshape(NS, 1, D)
    return (y_prompt, y_sample, jnp.stack(hg_p), hg_s[-1], jnp.stack(lh_p), jnp.stack(lh_s),
            jnp.stack(lc_p), jnp.stack(lc_s), jnp.stack(rw_p), jnp.stack(rw_s), jnp.stack(rs_p), jnp.stack(rs_s))
```

```python
import functools

import jax
import jax.numpy as jnp
import numpy as np
from jax import lax
from jax.experimental import pallas as pl
from jax.experimental.pallas import tpu as pltpu

f32 = jnp.float32
bf16 = jnp.bfloat16

D = 2048
NB = 4
LP = 2048
NM = 16
NS = 128
ROWS_MAIN = NB * LP
ROW_S = ROWS_MAIN
ROW_M = ROW_S + NS
AUX = 256
ROWS = ROWS_MAIN + AUX
EPS = 1e-6
HG_H, HG_K = 16, 128
HG_C = 64
HG_UNROLL = 4
GATE_EXP_CLIP = 60.0
LRU_NBLK, LRU_BW = 8, 256
LRU_C = 8.0
LRU_T = 256
RW_H, RW_N = 32, 64
RW_LN_EPS = 64e-5
RW_TS = 16
PM_T = 256
LORA_PAD = 128
D_FF = 5632
TM_BIG = 1408
TM_MID = 768
VMEM_LIMIT = 56 * 1024 * 1024


def _cp(sem):
    return pltpu.CompilerParams(dimension_semantics=sem, vmem_limit_bytes=VMEM_LIMIT)


def _softplus(x):
    return jnp.maximum(x, 0.0) + jnp.log(1.0 + jnp.exp(-jnp.abs(x)))


def _norm_kernel(h_ref, g_ref, o_ref):
    x = h_ref[...]
    ms = jnp.mean(x * x, axis=-1, keepdims=True)
    o_ref[...] = (x * lax.rsqrt(ms + EPS) * g_ref[...]).astype(o_ref.dtype)


def _rmsnorm(h, g, *, out_dtype, tm, row_tile0, n_row_tiles):
    return pl.pallas_call(
        _norm_kernel,
        grid=(n_row_tiles,),
        in_specs=[pl.BlockSpec((tm, D), lambda i: (i + row_tile0, 0)),
                  pl.BlockSpec((1, D), lambda i: (0, 0))],
        out_specs=pl.BlockSpec((tm, D), lambda i: (i, 0)),
        out_shape=jax.ShapeDtypeStruct((n_row_tiles * tm, D), out_dtype),
        name="rmsnorm",
        compiler_params=_cp(("arbitrary",)),
    )(h, g)


def _assemble_kernel(xm_ref, xa_ref, g_ref, h_ref, xn_ref):
    def emit(x):
        h_ref[...] = x
        ms = jnp.mean(x * x, axis=-1, keepdims=True)
        xn_ref[...] = (x * lax.rsqrt(ms + EPS) * g_ref[...]).astype(xn_ref.dtype)

    @pl.when(pl.program_id(0) < ROWS_MAIN // AUX)
    def _main():
        emit(xm_ref[...])

    @pl.when(pl.program_id(0) == ROWS_MAIN // AUX)
    def _aux():
        emit(xa_ref[...])


def _assemble(x_main, x_aux, g):
    nm = ROWS_MAIN // AUX
    return pl.pallas_call(
        _assemble_kernel,
        grid=(nm + 1,),
        in_specs=[pl.BlockSpec((AUX, D), lambda i: (jnp.minimum(i, nm - 1), 0)),
                  pl.BlockSpec((AUX, D), lambda i: (0, 0)),
                  pl.BlockSpec((1, D), lambda i: (0, 0))],
        out_specs=[pl.BlockSpec((AUX, D), lambda i: (i, 0)), pl.BlockSpec((AUX, D), lambda i: (i, 0))],
        out_shape=[_sds((ROWS, D), f32), _sds((ROWS, D), bf16)],
        name="assemble",
        compiler_params=_cp(("arbitrary",)),
    )(x_main, x_aux, g)


def _mm_body(*refs, n_x, n_w, n_kv, n_cv, n_rm, n_out, prologue, epilogue, row_splits):
    p = 0
    x_refs = refs[p:p + n_x]; p += n_x
    w_refs = refs[p:p + n_w]; p += n_w
    kv_refs = refs[p:p + n_kv]; p += n_kv
    cv_refs = refs[p:p + n_cv]; p += n_cv
    rm_refs = refs[p:p + n_rm]; p += n_rm
    out_refs = refs[p:p + n_out]; p += n_out
    w_scr = refs[p:p + n_w]

    @pl.when(pl.program_id(1) == 0)
    def _cast_weights():
        for w, s in zip(w_refs, w_scr):
            s[...] = w[...].astype(bf16)

    def flat(v):
        return v.reshape(v.shape[0], v.shape[1] * v.shape[2]) if v.ndim == 3 else v

    tm = out_refs[0].shape[0]
    sub = tm // row_splits
    for s0 in range(0, tm, sub):
        rows = slice(s0, s0 + sub)
        x = prologue([flat(r[rows]) for r in x_refs], [r[...] for r in kv_refs])
        accs = [jnp.dot(x, s[...], preferred_element_type=f32) for s in w_scr]
        res = epilogue(accs, [r[...] for r in cv_refs], [r[rows] for r in rm_refs])
        for o, r in zip(out_refs, res):
            o[rows] = r.astype(o.dtype).reshape((sub,) + o.shape[1:])


def _mm(*, name="proj", xs, ws, kvecs=(), cvecs=(), rmats=(), outs, K, tm, tn, n_row_tiles, n_col_tiles,
        prologue, epilogue, aliases=None, row_splits=1):
    in_specs, args = [], []
    for a, r0 in xs:
        if a.ndim == 3:
            in_specs.append(pl.BlockSpec((tm, K // 128, 128), lambda j, i, r0=r0: (i + r0, 0, 0)))
        else:
            in_specs.append(pl.BlockSpec((tm, K), lambda j, i, r0=r0: (i + r0, 0)))
        args.append(a)
    for a, lead, c0 in ws:
        nl = len(lead)
        in_specs.append(pl.BlockSpec((None,) * nl + (K, tn), lambda j, i, lead=lead, c0=c0: lead + (0, j + c0)))
        args.append(a)
    for a in kvecs:
        in_specs.append(pl.BlockSpec(a.shape, lambda j, i: (0, 0)))
        args.append(a)
    for a, c0 in cvecs:
        in_specs.append(pl.BlockSpec((a.shape[0], tn), lambda j, i, c0=c0: (0, j + c0)))
        args.append(a)
    for a, r0, c0 in rmats:
        in_specs.append(pl.BlockSpec((tm, tn), lambda j, i, r0=r0, c0=c0: (i + r0, j + c0)))
        args.append(a)
    out_specs = [pl.BlockSpec((tm, tn // 128, 128), lambda j, i, r0=r0, c0=c0: (i + r0, j + c0, 0))
                 if len(s.shape) == 3 else
                 pl.BlockSpec((tm, tn), lambda j, i, r0=r0, c0=c0: (i + r0, j + c0)) for s, r0, c0 in outs]
    body = functools.partial(_mm_body, n_x=len(xs), n_w=len(ws), n_kv=len(kvecs), n_cv=len(cvecs),
                             n_rm=len(rmats), n_out=len(outs), prologue=prologue, epilogue=epilogue,
                             row_splits=row_splits)
    res = pl.pallas_call(
        body,
        grid=(n_col_tiles, n_row_tiles),
        in_specs=in_specs,
        out_specs=out_specs,
        out_shape=[s for s, _, _ in outs],
        scratch_shapes=[pltpu.VMEM((K, tn), bf16) for _ in ws],
        input_output_aliases=aliases or {},
        name=name,
        compiler_params=_cp(("arbitrary", "arbitrary")),
    )(*args)
    return res


def _pro_id(xs, kvs):
    return xs[0]


def _pro_mul(xs, kvs):
    return (xs[0] * xs[1]).astype(bf16)


def _sds(shape, dtype):
    return jax.ShapeDtypeStruct(shape, dtype)


def _mixer_out_kernel(*refs, n_u, prologue):
    xm_ref, xa_ref = refs[:2]
    u_refs = refs[2:2 + n_u]
    w_ref, h_ref, g_ref, ho_ref, xn_ref, w_s = refs[2 + n_u:]
    i = pl.program_id(0)

    @pl.when(i == 0)
    def _cast():
        w_s[...] = w_ref[...].astype(bf16)

    def run(x_ref):
        x = x_ref[...]
        if x.ndim == 3:
            x = x.reshape(x.shape[0], x.shape[1] * x.shape[2])
        hn = h_ref[...] + jnp.dot(prologue([x] + [u[...] for u in u_refs], []), w_s[...],
                                  preferred_element_type=f32)
        ho_ref[...] = hn
        ms = jnp.mean(hn * hn, axis=-1, keepdims=True)
        xn_ref[...] = (hn * lax.rsqrt(ms + EPS) * g_ref[...]).astype(xn_ref.dtype)

    @pl.when(i < ROWS_MAIN // AUX)
    def _main():
        run(xm_ref)

    @pl.when(i == ROWS_MAIN // AUX)
    def _aux():
        run(xa_ref)


def _mixer_out(h, x_main, x_aux, unified, w, lead, norm_g, prologue):
    nm = ROWS_MAIN // AUX
    def xspec(a, imap):
        if a.ndim == 3:
            return pl.BlockSpec((AUX,) + a.shape[1:], lambda i: (imap(i), 0, 0))
        return pl.BlockSpec((AUX, a.shape[1]), lambda i: (imap(i), 0))
    nl = len(lead)
    row = pl.BlockSpec((AUX, D), lambda i: (i, 0))
    in_specs = ([xspec(x_main, lambda i: jnp.minimum(i, nm - 1)), xspec(x_aux, lambda i: 0)]
                + [xspec(u, lambda i: i) for u in unified]
                + [pl.BlockSpec((None,) * nl + (D, D), lambda i: lead + (0, 0), pipeline_mode=pl.Buffered(1)),
                   row, pl.BlockSpec((1, D), lambda i: (0, 0))])
    return pl.pallas_call(
        functools.partial(_mixer_out_kernel, n_u=len(unified), prologue=prologue),
        grid=(nm + 1,),
        in_specs=in_specs,
        out_specs=[row, row],
        out_shape=[_sds((ROWS, D), f32), _sds((ROWS, D), bf16)],
        scratch_shapes=[pltpu.VMEM((D, D), bf16)],
        input_output_aliases={3 + len(unified): 0},
        name="mixer_out",
        compiler_params=_cp(("arbitrary",)),
    )(x_main, x_aux, *unified, w, h, norm_g)


def _ffn(h, xn, w_in, w_out, layer):
    tn = 512
    nct = D_FF // tn
    act = _mm(name="ffn_in", xs=[(xn, 0)], ws=[(w_in, (layer,), 0), (w_in, (layer,), nct)],
              outs=[(_sds((ROWS, D_FF), bf16), 0, 0)],
              K=D, tm=TM_BIG, tn=tn, n_row_tiles=ROWS // TM_BIG, n_col_tiles=nct,
              prologue=_pro_id,
              epilogue=lambda accs, cvs, rms: (jax.nn.silu(accs[0]) * accs[1],))[0]
    h = _mm(name="ffn_out", xs=[(act, 0)], ws=[(w_out, (layer,), 0)], rmats=[(h, 0, 0)],
            outs=[(_sds((ROWS, D), f32), 0, 0)],
            K=D_FF, tm=TM_MID, tn=tn, n_row_tiles=ROWS // TM_MID, n_col_tiles=D // tn,
            prologue=_pro_id, epilogue=lambda accs, cvs, rms: (rms[0] + accs[0],),
            aliases={2: 0})[0]
    return h


def _hgrn_lower_bound(logits, j):
    m = jnp.max(logits, axis=0, keepdims=True)
    e = jnp.exp(logits - m)
    p = e / jnp.sum(e, axis=0, keepdims=True)
    cs = p[0:1]
    for r in range(1, j + 1):
        cs = cs + p[r:r + 1]
    return jnp.clip(cs - p[0:1], 0.0, 1.0)


def _hgrn_in(xn, w_in, lb_logits, j):
    tn = 1024
    nct = D // tn
    common = dict(K=D, tm=TM_BIG, tn=tn, n_row_tiles=ROWS // TM_BIG, n_col_tiles=nct, prologue=_pro_id,
                  row_splits=2)
    o = (_sds((ROWS, D), f32), 0, 0)
    q = _mm(xs=[(xn, 0)], ws=[(w_in, (j,), 0)], outs=[o],
            epilogue=lambda accs, cvs, rms: (jax.nn.silu(accs[0]),), **common)[0]

    def f_epi(accs, cvs, rms):
        f = accs[0]
        lb = _hgrn_lower_bound(cvs[0], j)
        t = jnp.exp(-jnp.abs(f))
        log_f = (-(jnp.maximum(-f, 0.0) + jnp.log(1.0 + t))
                 + jnp.log(1.0 + lb * jnp.exp(jnp.minimum(-f, GATE_EXP_CLIP))))
        k = (1.0 - lb) * (jnp.where(f >= 0.0, t, 1.0) / (1.0 + t))
        return log_f, k

    log_f, k = _mm(xs=[(xn, 0)], ws=[(w_in, (j,), nct)], cvecs=[(lb_logits, 0)], outs=[o, o],
                   epilogue=f_epi, **{**common, "tm": TM_MID, "n_row_tiles": ROWS // TM_MID, "row_splits": 3})
    v = _mm(xs=[(xn, 0)], ws=[(w_in, (j,), 2 * nct)], outs=[o],
            epilogue=lambda accs, cvs, rms: (accs[0],), **common)[0]
    g = _mm(xs=[(xn, 0)], ws=[(w_in, (j,), 3 * nct)], outs=[o],
            epilogue=lambda accs, cvs, rms: (jax.nn.silu(accs[0]),), **common)[0]
    return q, k, v, log_f, g


def _split3(x):
    hi = x.astype(bf16)
    r = x - hi.astype(f32)
    mid = r.astype(bf16)
    lo = (r - mid.astype(f32)).astype(bf16)
    return hi, mid, lo


def _gla_consts(T):
    t = np.arange(T)[:, None]
    w = np.arange(T)[None, :]
    mats = [w <= t, w > t]
    masks = []
    s = T // 2
    while s >= 1:
        piece = t // (2 * s)
        lower = (t % (2 * s)) >= s
        ref = piece * 2 * s + s
        mats.append(np.where(lower, (w > ref) & (w <= t), (w > t) & (w <= ref)))
        upiece = w // (2 * s)
        ulower = (w % (2 * s)) >= s
        masks.append((piece == upiece) & lower & ~ulower)
        s //= 2
    masks.append(t == w)
    m = np.concatenate(mats, 0)
    eye = np.eye(HG_UNROLL, dtype=bool)
    gmasks = np.stack([np.kron(eye, pm) for pm in masks])
    return (jnp.asarray(np.concatenate([m, m, m], 1), dtype=bf16), jnp.asarray(gmasks, dtype=f32))


def _gla_block(q_ref, k_ref, v_ref, f_ref, g_ref, gn_ref, dm_ref, pm_ref, o_ref, st_ref, e_s, ql_s, kl_s, T):
    nlev = T.bit_length() - 1
    dn_t = (((1,), (1,)), ((), ()))
    dn_l = (((0,), (0,)), ((), ()))
    def factors(ls):
        e_s[0:(nlev + 2) * T, ls] = jnp.dot(dm_ref[...], jnp.concatenate(_split3(f_ref[:, ls]), axis=0),
                                            preferred_element_type=f32)
        q = q_ref[:, ls]
        k = k_ref[:, ls]
        ql_s[0, 0:T, ls] = (q * jnp.exp(e_s[0:T, ls])).astype(bf16)
        kl_s[0, 0:T, ls] = (k * jnp.exp(e_s[T:2 * T, ls])).astype(bf16)
        for l in range(nlev):
            ex = jnp.exp(e_s[(l + 2) * T:(l + 3) * T, ls])
            ql_s[l + 1, 0:T, ls] = (q * ex).astype(bf16)
            kl_s[l + 1, 0:T, ls] = (k * ex).astype(bf16)
        ql_s[nlev + 1, 0:T, ls] = q.astype(bf16)
        kl_s[nlev + 1, 0:T, ls] = k.astype(bf16)

    for h0 in range(0, HG_H, HG_UNROLL):
        factors(slice(h0 * HG_K, (h0 + HG_UNROLL) * HG_K))
        sls = [slice(h * HG_K, (h + 1) * HG_K) for h in range(h0, h0 + HG_UNROLL)]
        att = None
        for l in range(nlev + 1):
            a_l = lax.dot_general(jnp.concatenate([ql_s[l + 1, 0:T, sl] for sl in sls], axis=0),
                                  jnp.concatenate([kl_s[l + 1, 0:T, sl] for sl in sls], axis=0),
                                  dn_t, preferred_element_type=f32) * pm_ref[l]
            att = a_l if att is None else att + a_l
        vbs = [v_ref[:, sl].astype(bf16) for sl in sls]
        o_intra = jnp.dot(att.astype(bf16), jnp.concatenate(vbs, axis=0), preferred_element_type=f32)
        for p in range(0, HG_UNROLL, 2):
            ha, hb = h0 + p, h0 + p + 1
            sla, slb = sls[p], sls[p + 1]
            st2 = jnp.concatenate([st_ref[ha], st_ref[hb]], axis=0)
            oc = lax.dot_general(jnp.concatenate([ql_s[0, 0:T, sla], ql_s[0, 0:T, slb]], axis=0),
                                 st2.astype(bf16), dn_t, preferred_element_type=f32)
            up = lax.dot_general(jnp.concatenate([vbs[p], vbs[p + 1]], axis=1),
                                 jnp.concatenate([kl_s[0, 0:T, sla], kl_s[0, 0:T, slb]], axis=1),
                                 dn_l, preferred_element_type=f32)
            for i, (h, sl) in enumerate(((ha, sla), (hb, slb))):
                o = (oc[i * T:(i + 1) * T, i * HG_K:(i + 1) * HG_K]
                     + o_intra[(p + i) * T:(p + i + 1) * T])
                st_ref[h] = (st2[i * HG_K:(i + 1) * HG_K] * jnp.exp(e_s[T - 1:T, sl])
                             + up[i * HG_K:(i + 1) * HG_K, i * HG_K:(i + 1) * HG_K])
                ms = jnp.mean(o * o, axis=-1, keepdims=True)
                on = o * lax.rsqrt(ms + EPS) * gn_ref[:, sl]
                o_ref[:, sl] = (on * g_ref[:, sl]).astype(o_ref.dtype)


def _gla_kernel(qm, km, vm, fm, gm, qa, ka, va, fa, ga, gn_ref, dmm, pmm, dma, pma,
                om_ref, oa_ref, s_ref, st_ref, e_s, ql_s, kl_s):
    c = pl.program_id(1)

    @pl.when(c == 0)
    def _meta():
        st_ref[...] = jnp.zeros_like(st_ref)
        _gla_block(qa, ka, va, fa, ga, gn_ref, dma, pma, oa_ref, st_ref, e_s, ql_s, kl_s, NM)

    @pl.when(c > 0)
    def _main():
        _gla_block(qm, km, vm, fm, gm, gn_ref, dmm, pmm, om_ref, st_ref, e_s, ql_s, kl_s, HG_C)

    @pl.when(c == pl.num_programs(1) - 1)
    def _final():
        for h in range(HG_H):
            s_ref[0, h] = st_ref[h].T


def _hgrn_prompt(q, k, v, log_f, g, gn):
    nc = LP // HG_C
    main_spec = pl.BlockSpec((HG_C, D), lambda b, c: (b * nc + jnp.maximum(c - 1, 0), 0))
    meta_spec = pl.BlockSpec((NM, D), lambda b, c: (ROW_M // NM + b, 0))
    arrs = [q, k, v, log_f, g]
    consts = [*_gla_consts(HG_C), *_gla_consts(NM)]
    nslot = HG_C.bit_length() + 1
    return pl.pallas_call(
        _gla_kernel,
        grid=(NB, nc + 1),
        in_specs=[main_spec] * 5 + [meta_spec] * 5 + [pl.BlockSpec((1, D), lambda b, c: (0, 0))]
                 + [pl.BlockSpec(a.shape, lambda b, c, n=a.ndim: (0,) * n) for a in consts],
        out_specs=[pl.BlockSpec((HG_C, D), lambda b, c: (b * nc + jnp.maximum(c - 1, 0), 0)),
                   pl.BlockSpec((NM, D), lambda b, c: (b, 0)),
                   pl.BlockSpec((1, HG_H, HG_K, HG_K), lambda b, c: (b, 0, 0, 0))],
        out_shape=[_sds((ROWS_MAIN, D), bf16), _sds((NB * NM, D), bf16), _sds((NB, HG_H, HG_K, HG_K), f32)],
        scratch_shapes=[pltpu.VMEM((HG_H, HG_K, HG_K), f32),
                        pltpu.VMEM((nslot * HG_C, D), f32),
                        pltpu.VMEM((nslot, HG_C, D), bf16),
                        pltpu.VMEM((nslot, HG_C, D), bf16)],
        name="hgrn_prompt",
        compiler_params=_cp(("arbitrary", "arbitrary")),
    )(*arrs, *arrs, gn, *consts)


HG_BT = 8
HG_SB = 4


def _hgrn_dec_kernel(s_ref, qT, kT, fT, v_ref, g_ref, gn_ref, e3_ref, *rest, stack_slot):
    so_ref, o_ref, qb_s, kb_s, gb_s = rest[-5:]
    if stack_slot is not None:
        for m, e_ref in enumerate(rest[:-5]):
            so_ref[m] = e_ref[...]
    row0 = pl.program_id(1) * HG_SB

    def spread(m, dst):
        dst[...] = jnp.dot(jnp.concatenate(_split3(m), axis=1), e3_ref[...], preferred_element_type=f32)

    def per_b(bb, carry):
        spread(qT[bb], qb_s)
        spread(kT[bb], kb_s)
        spread(jnp.exp(fT[bb]), gb_s)
        row = pl.ds(row0 + bb, 1)
        vall = v_ref[row, :]
        outs = []
        for h in range(HG_H):
            hs = slice(h * HG_K, (h + 1) * HG_K)
            sn = gb_s[:, hs] * s_ref[bb, h] + kb_s[:, hs] * vall[:, hs]
            if stack_slot is None:
                so_ref[bb, h] = sn
            else:
                so_ref[stack_slot, bb, h] = sn
            o = jnp.sum(qb_s[:, hs] * sn, axis=0, keepdims=True)
            ms = jnp.mean(o * o, axis=-1, keepdims=True)
            outs.append(o * lax.rsqrt(ms + EPS))
        on = jnp.concatenate(outs, axis=1) * gn_ref[...]
        o_ref[row, :] = on * g_ref[row, :]
        return carry

    lax.fori_loop(0, HG_SB, per_b, 0)


def _hgrn_sample(state, j, q, k, v, log_f, g, gn, earlier):
    def colform(x):
        return x[ROW_S:ROW_S + NS].reshape(NS, HG_H, HG_K).transpose(0, 2, 1)

    n_a = state.shape[0]
    last = j == n_a - 1
    head_of_lane = np.arange(D) // HG_K
    sel = (np.arange(HG_H)[:, None] == head_of_lane[None, :])
    e3 = jnp.asarray(np.concatenate([sel, sel, sel], axis=0), dtype=bf16)
    nsb = HG_BT // HG_SB
    col_spec = pl.BlockSpec((HG_SB, HG_K, HG_H), lambda i, s: (i * nsb + s, 0, 0))
    row_spec = pl.BlockSpec((HG_BT, D), lambda i, s: (ROW_S // HG_BT + i, 0))
    new_spec = pl.BlockSpec((HG_SB, HG_H, HG_K, HG_K), lambda i, s: (i * nsb + s, 0, 0, 0))
    in_specs = [pl.BlockSpec((None, HG_SB, HG_H, HG_K, HG_K), lambda i, s: (j, i * nsb + s, 0, 0, 0)),
                col_spec, col_spec, col_spec, row_spec, row_spec,
                pl.BlockSpec((1, D), lambda i, s: (0, 0)), pl.BlockSpec((3 * HG_H, D), lambda i, s: (0, 0))]
    args = [state, colform(q), colform(k), colform(log_f), v, g, gn, e3]
    if last:
        in_specs += [new_spec] * len(earlier)
        args += list(earlier)
        st_out = pl.BlockSpec((n_a, HG_SB, HG_H, HG_K, HG_K), lambda i, s: (0, i * nsb + s, 0, 0, 0))
        st_shape = _sds(state.shape, f32)
    else:
        st_out, st_shape = new_spec, _sds(state.shape[1:], f32)
    return pl.pallas_call(
        functools.partial(_hgrn_dec_kernel, stack_slot=j if last else None),
        grid=(NS // HG_BT, nsb),
        in_specs=in_specs,
        out_specs=[st_out, pl.BlockSpec((HG_BT, D), lambda i, s: (i, 0))],
        out_shape=[st_shape, _sds((NS, D), f32)],
        scratch_shapes=[pltpu.VMEM((HG_K, D), f32)] * 3,
        name="hgrn_decode",
        compiler_params=_cp(("arbitrary", "arbitrary")),
    )(*args)


def _aux_rows(sample_rows, meta_rows):
    pad = jnp.zeros((AUX - NS - NB * NM, sample_rows.shape[1]), sample_rows.dtype)
    return jnp.concatenate([sample_rows, meta_rows.astype(sample_rows.dtype), pad], axis=0)


def _hgrn_layer(h, norm_g, ffn_g, state, earlier, j, w_in, lb_logits, gnorm, w_out, xn=None):
    if xn is None:
        xn = _rmsnorm(h, norm_g, out_dtype=bf16, tm=TM_MID, row_tile0=0, n_row_tiles=ROWS // TM_MID)
    q, k, v, log_f, g = _hgrn_in(xn, w_in, lb_logits, j)
    gn = gnorm[j].reshape(1, D)
    o_main, o_meta, s_p = _hgrn_prompt(q, k, v, log_f, g, gn)
    s_s, o_s = _hgrn_sample(state, j, q, k, v, log_f, g, gn, earlier)
    o_aux = _aux_rows(o_s.astype(bf16), o_meta)
    h, xn_f = _mixer_out(h, o_main, o_aux, [], w_out, (j,), ffn_g, _pro_id)
    return (h, xn_f), s_p, s_s


def _lru_in(xn, w_in, j):
    tn = 1024
    nct = D // tn
    common = dict(K=D, tm=TM_BIG, tn=tn, n_row_tiles=ROWS // TM_BIG, n_col_tiles=nct, prologue=_pro_id,
                  row_splits=2)
    o = (_sds((ROWS, D), f32), 0, 0)
    y = _mm(xs=[(xn, 0)], ws=[(w_in, (j,), 0)], outs=[o],
            epilogue=lambda accs, cvs, rms: (jax.nn.gelu(accs[0], approximate=True),), **common)[0]
    x = _mm(xs=[(xn, 0)], ws=[(w_in, (j,), nct)], outs=[o],
            epilogue=lambda accs, cvs, rms: (accs[0],), **common)[0]
    return y, x


def _lru_gates(xc, gw_ref, gb_ref, lam_ref, n):
    ls = slice(n * LRU_BW, (n + 1) * LRU_BW)
    xb = xc[:, ls].astype(bf16)
    r = jax.nn.sigmoid(jnp.dot(xb, gw_ref[0, n], preferred_element_type=f32) + gb_ref[0:1, ls])
    ig = jax.nn.sigmoid(jnp.dot(xb, gw_ref[1, n], preferred_element_type=f32) + gb_ref[1:2, ls])
    log_a = -LRU_C * r * _softplus(-lam_ref[:, ls])
    a = jnp.exp(log_a)
    mult = jnp.sqrt(jnp.maximum(1.0 - a * a, 0.0))
    return a, ig, mult


def _lru_conv(cb_ref, cw_ref, x0, x1, x2, x3):
    xc = cb_ref[...] + x0 * cw_ref[0:1, :]
    xc = xc + x1 * cw_ref[1:2, :]
    xc = xc + x2 * cw_ref[2:3, :]
    return xc + x3 * cw_ref[3:4, :]


def _lru_block(x_ref, y_ref, o_ref, cw_ref, cb_ref, gw_s, gb_ref, lam_ref, xbuf, a_s, b_s, hcar, T, first):
    xbuf[8:8 + T, :] = x_ref[...]
    xc = _lru_conv(cb_ref, cw_ref, xbuf[5:5 + T, :], xbuf[6:6 + T, :], xbuf[7:7 + T, :], xbuf[8:8 + T, :])
    xbuf[0:8, :] = xbuf[T:T + 8, :]
    row = lax.broadcasted_iota(jnp.int32, (T, 1), 0)
    for n in range(LRU_NBLK):
        ls = slice(n * LRU_BW, (n + 1) * LRU_BW)
        a, ig, mult = _lru_gates(xc, gw_s, gb_ref, lam_ref, n)
        if first:
            mult = jnp.where(row == 0, 1.0, mult)
        a_s[0:T, ls] = a
        b_s[0:T, ls] = xc[:, ls] * ig * mult

    def step(t, hprev):
        hnew = a_s[pl.ds(t, 1), :] * hprev + b_s[pl.ds(t, 1), :]
        b_s[pl.ds(t, 1), :] = hnew
        return hnew

    hcar[...] = lax.fori_loop(0, T, step, hcar[...], unroll=8)
    o_ref[...] = (b_s[0:T, :] * y_ref[...]).astype(o_ref.dtype)


def _lru_kernel(xm, ym, xa, ya, cw_ref, cb_ref, gw_ref, gb_ref, lam_ref,
                om_ref, oa_ref, hl_ref, cv_ref, gw_s, xbuf, a_s, b_s, hcar):
    c = pl.program_id(1)

    @pl.when((pl.program_id(0) == 0) & (c == 0))
    def _cast():
        gw_s[...] = gw_ref[...].astype(bf16)

    @pl.when(c == 0)
    def _meta():
        xbuf[0:8, :] = jnp.zeros((8, D), f32)
        hcar[...] = jnp.zeros_like(hcar)
        _lru_block(xa, ya, oa_ref, cw_ref, cb_ref, gw_s, gb_ref, lam_ref, xbuf, a_s, b_s, hcar, NM, True)

    @pl.when(c > 0)
    def _main():
        _lru_block(xm, ym, om_ref, cw_ref, cb_ref, gw_s, gb_ref, lam_ref, xbuf, a_s, b_s, hcar, LRU_T, False)

    @pl.when(c == pl.num_programs(1) - 1)
    def _final():
        hl_ref[0] = hcar[...]
        cv_ref[0] = xbuf[5:8, :]


def _lru_prompt(x, y, j, conv_w, conv_b, gate_w, gate_b, lam):
    nc = LP // LRU_T
    main_spec = pl.BlockSpec((LRU_T, D), lambda b, c: (b * nc + jnp.maximum(c - 1, 0), 0))
    meta_spec = pl.BlockSpec((NM, D), lambda b, c: (ROW_M // NM + b, 0))
    return pl.pallas_call(
        _lru_kernel,
        grid=(NB, nc + 1),
        in_specs=[main_spec, main_spec, meta_spec, meta_spec,
                  pl.BlockSpec((None, 4, D), lambda b, c: (j, 0, 0)),
                  pl.BlockSpec((1, D), lambda b, c: (j, 0)),
                  pl.BlockSpec((None, 2, LRU_NBLK, LRU_BW, LRU_BW), lambda b, c: (j, 0, 0, 0, 0)),
                  pl.BlockSpec((None, 2, D), lambda b, c: (j, 0, 0)),
                  pl.BlockSpec((1, D), lambda b, c: (j, 0))],
        out_specs=[pl.BlockSpec((LRU_T, D), lambda b, c: (b * nc + jnp.maximum(c - 1, 0), 0)),
                   pl.BlockSpec((NM, D), lambda b, c: (b, 0)),
                   pl.BlockSpec((1, 1, D), lambda b, c: (b, 0, 0)),
                   pl.BlockSpec((1, 3, D), lambda b, c: (b, 0, 0))],
        out_shape=[_sds((ROWS_MAIN, D), bf16), _sds((NB * NM, D), bf16),
                   _sds((NB, 1, D), f32), _sds((NB, 3, D), f32)],
        scratch_shapes=[pltpu.VMEM((2, LRU_NBLK, LRU_BW, LRU_BW), bf16),
                        pltpu.VMEM((LRU_T + 8, D), f32), pltpu.VMEM((LRU_T, D), f32),
                        pltpu.VMEM((LRU_T, D), f32), pltpu.VMEM((1, D), f32)],
        name="lru_prompt",
        compiler_params=_cp(("arbitrary", "arbitrary")),
    )(x, y, x, y, conv_w, conv_b, gate_w, gate_b, lam)


LRU_ST = 64


def _lru_dec_kernel(x_ref, y_ref, h0_ref, cbuf_ref, cw_ref, cb_ref, gw_ref, gb_ref, lam_ref,
                    o_ref, hn_ref, cn_ref, gw_s):
    gw_s[...] = gw_ref[...].astype(bf16)
    x = x_ref[...]
    b0 = cbuf_ref[:, 0, :]
    b1 = cbuf_ref[:, 1, :]
    b2 = cbuf_ref[:, 2, :]
    xc = _lru_conv(cb_ref, cw_ref, b0, b1, b2, x)
    cn_ref[:, 0, :] = b1
    cn_ref[:, 1, :] = b2
    cn_ref[:, 2, :] = x
    for n in range(LRU_NBLK):
        ls = slice(n * LRU_BW, (n + 1) * LRU_BW)
        a, ig, mult = _lru_gates(xc, gw_s, gb_ref, lam_ref, n)
        hn = a * h0_ref[:, ls] + xc[:, ls] * ig * mult
        hn_ref[:, ls] = hn
        o_ref[:, ls] = (hn * y_ref[:, ls]).astype(o_ref.dtype)


def _lru_sample(x, y, j, h0, cbuf, conv_w, conv_b, gate_w, gate_b, lam):
    row_spec = pl.BlockSpec((LRU_ST, D), lambda i: (ROW_S // LRU_ST + i, 0))
    return pl.pallas_call(
        _lru_dec_kernel,
        grid=(NS // LRU_ST,),
        in_specs=[row_spec, row_spec,
                  pl.BlockSpec((None, LRU_ST, D), lambda i: (j, i, 0)),
                  pl.BlockSpec((None, LRU_ST, 3, D), lambda i: (j, i, 0, 0)),
                  pl.BlockSpec((None, 4, D), lambda i: (j, 0, 0)),
                  pl.BlockSpec((1, D), lambda i: (j, 0)),
                  pl.BlockSpec((None, 2, LRU_NBLK, LRU_BW, LRU_BW), lambda i: (j, 0, 0, 0, 0)),
                  pl.BlockSpec((None, 2, D), lambda i: (j, 0, 0)),
                  pl.BlockSpec((1, D), lambda i: (j, 0))],
        out_specs=[pl.BlockSpec((LRU_ST, D), lambda i: (i, 0)),
                   pl.BlockSpec((LRU_ST, D), lambda i: (i, 0)),
                   pl.BlockSpec((LRU_ST, 3, D), lambda i: (i, 0, 0))],
        out_shape=[_sds((NS, D), bf16), _sds((NS, D), f32), _sds((NS, 3, D), f32)],
        scratch_shapes=[pltpu.VMEM((2, LRU_NBLK, LRU_BW, LRU_BW), bf16)],
        name="lru_decode",
        compiler_params=_cp(("arbitrary",)),
    )(x, y, h0, cbuf, conv_w, conv_b, gate_w, gate_b, lam)


def _lru_layer(h, norm_g, ffn_g, state_h, state_conv, j, w_in, conv_w, conv_b, gate_w, gate_b, lam, w_out):
    xn = _rmsnorm(h, norm_g, out_dtype=bf16, tm=TM_MID, row_tile0=0, n_row_tiles=ROWS // TM_MID)
    y, x = _lru_in(xn, w_in, j)
    o_main, o_meta, hl_p, cv_p = _lru_prompt(x, y, j, conv_w, conv_b, gate_w, gate_b, lam)
    o_s, hl_s, cv_s = _lru_sample(x, y, j, state_h, state_conv, conv_w, conv_b, gate_w, gate_b, lam)
    o_aux = _aux_rows(o_s, o_meta)
    h, xn_f = _mixer_out(h, o_main, o_aux, [], w_out, (j,), ffn_g, _pro_id)
    return (h, xn_f), hl_p.reshape(NB, D), hl_s, cv_p, cv_s


def _rwkv_premix_kernel(h_ref, g_ref, mu_ref, sh_ref, *refs):
    x_refs = refs[:6]
    sp_ref, ss_ref, xbuf, meta_last = refs[6:]
    i = pl.program_id(0)
    x = h_ref[...]
    ms = jnp.mean(x * x, axis=-1, keepdims=True)
    xn = x * lax.rsqrt(ms + EPS) * g_ref[...]
    xbuf[8:8 + PM_T, :] = xn
    row = lax.broadcasted_iota(jnp.int32, (PM_T, 1), 0)

    @pl.when(i == 0)
    def _aux():
        xbuf[7:8, :] = jnp.zeros((1, D), f32)
        ss_ref[...] = xn[0:NS]
        for b in range(NB):
            meta_last[b:b + 1, :] = xn[NS + b * NM + NM - 1:NS + b * NM + NM]

    @pl.when(i > 0)
    def _main():
        m = i - 1
        b = m // (LP // PM_T)

        @pl.when(m % (LP // PM_T) == 0)
        def _start():
            xbuf[7:8, :] = meta_last[pl.ds(b, 1), :]

        @pl.when(m % (LP // PM_T) == LP // PM_T - 1)
        def _end():
            sp_ref[pl.ds(b, 1), :] = xn[PM_T - 1:PM_T]

    shifted = xbuf[7:7 + PM_T, :]
    is_meta = (row >= NS) & (row < NS + NB * NM) & ((row - NS) % NM != 0)
    sh_pad = jnp.concatenate([sh_ref[...], jnp.zeros((PM_T - NS, D), f32)], axis=0)
    prev_aux = jnp.where(row < NS, sh_pad, jnp.where(is_meta, shifted, 0.0))
    prev = jnp.where(i == 0, prev_aux, shifted)
    dx = prev - xn
    for n in range(6):
        x_refs[n][...] = (xn + dx * mu_ref[n:n + 1, :]).astype(bf16)
    xbuf[7:8, :] = xn[PM_T - 1:PM_T]


def _rwkv_premix(h, norm_g, mu_j, shift_j):
    nt = ROWS // PM_T
    rows = lambda i: (jnp.where(i == 0, nt - 1, i - 1), 0)
    return pl.pallas_call(
        _rwkv_premix_kernel,
        grid=(nt,),
        in_specs=[pl.BlockSpec((PM_T, D), rows),
                  pl.BlockSpec((1, D), lambda i: (0, 0)),
                  pl.BlockSpec((6, D), lambda i: (0, 0)),
                  pl.BlockSpec((NS, D), lambda i: (0, 0))],
        out_specs=[pl.BlockSpec((PM_T, D), rows)] * 6 + [pl.BlockSpec((NB, D), lambda i: (0, 0)),
                                                         pl.BlockSpec((NS, D), lambda i: (0, 0))],
        out_shape=[_sds((ROWS, D), bf16)] * 6 + [_sds((NB, D), f32), _sds((NS, D), f32)],
        scratch_shapes=[pltpu.VMEM((PM_T + 8, D), f32), pltpu.VMEM((8, D), f32)],
        name="rwkv_premix",
        compiler_params=_cp(("arbitrary",)),
    )(h, norm_g, mu_j, shift_j)


def _rwkv_lora_kernel(xw_ref, xa_ref, xg_ref, w1_ref, w2_ref, a1_ref, a2_ref, g1_ref, g2_ref, w0_ref, a0_ref,
                      d_ref, a_ref, g_ref, w1_s, w2_s, a1_s, a2_s, g1_s, g2_s):
    @pl.when(pl.program_id(0) == 0)
    def _cast():
        for src, dst in ((w1_ref, w1_s), (w2_ref, w2_s), (a1_ref, a1_s), (a2_ref, a2_s), (g1_ref, g1_s),
                         (g2_ref, g2_s)):
            dst[...] = src[...].astype(bf16)

    def mm(u, w_s):
        return jnp.dot(u, w_s[...], preferred_element_type=f32)

    tw = jnp.tanh(mm(xw_ref[...], w1_s)).astype(bf16)
    x = w0_ref[...] + mm(tw, w2_s)
    d_ref[...] = jnp.exp(-np.float32(np.exp(-0.5)) * jax.nn.sigmoid(x)).reshape(d_ref.shape)
    ta = mm(xa_ref[...], a1_s).astype(bf16)
    a_ref[...] = jax.nn.sigmoid(a0_ref[...] + mm(ta, a2_s)).reshape(a_ref.shape)
    tg = jax.nn.sigmoid(mm(xg_ref[...], g1_s)).astype(bf16)
    g_ref[...] = mm(tg, g2_s)


def _rwkv_lora(xw, xa, xg, w1, w2, a1, a2, g1, g2, w0, a0):
    tm = 384
    full = lambda a: pl.BlockSpec(a.shape, lambda i: (0,) * a.ndim)
    row = pl.BlockSpec((tm, D), lambda i: (i, 0))
    row3 = pl.BlockSpec((tm, RW_Q, 128), lambda i: (i, 0, 0))
    ws = [w1, w2, a1, a2, g1, g2]
    return pl.pallas_call(
        _rwkv_lora_kernel,
        grid=(ROWS // tm,),
        in_specs=[row, row, row] + [full(a) for a in ws + [w0, a0]],
        out_specs=[row3, row3, row],
        out_shape=[_sds((ROWS, RW_Q, 128), f32)] * 2 + [_sds((ROWS, D), f32)],
        scratch_shapes=[pltpu.VMEM(a.shape, bf16) for a in ws],
        name="rwkv_lora",
        compiler_params=_cp(("arbitrary",)),
    )(xw, xa, xg, *ws, w0, a0)


def _rwkv_prep(t, r_s, k_s, d_s, a_s, tmp, kk_p, ka_p):
    kk_t, d_t, ka_t, k2_t, r_t = tmp
    k = k_s[t]
    a = a_s[t]
    kkr = k * kk_p[...]
    nrm = jnp.sqrt(jnp.sum(kkr * kkr, axis=0, keepdims=True))
    kk = kkr / jnp.maximum(nrm, 1e-12)
    kk_t[...] = kk
    d_t[...] = d_s[t]
    ka_t[...] = kk * a
    k2_t[...] = k * (1.0 + (a - 1.0) * ka_p[...])
    r_t[...] = r_s[t]


def _rwkv_body(v, s_s, tmp):
    kk_t, d_t, ka_t, k2_t, r_t = tmp
    sa = -(s_s[0] * kk_t[0:1, :])
    for j in range(1, RW_N):
        sa = sa - s_s[j] * kk_t[j:j + 1, :]
    y = None
    for j in range(RW_N):
        sn = s_s[j] * d_t[j:j + 1, :] + sa * ka_t[j:j + 1, :] + v * k2_t[j:j + 1, :]
        s_s[j] = sn
        yj = sn * r_t[j:j + 1, :]
        y = yj if y is None else y + yj
    return y


def _rwkv_tail(y, v, tmp, rk_p, lw_p, lb_p):
    _, _, _, k2_t, r_t = tmp
    mean = jnp.mean(y, axis=0, keepdims=True)
    yc = y - mean
    var = jnp.mean(yc * yc, axis=0, keepdims=True)
    yn = yc * lax.rsqrt(var + RW_LN_EPS) * lw_p[...] + lb_p[...]
    bonus = jnp.sum(r_t[...] * k2_t[...] * rk_p[...], axis=0, keepdims=True) * v
    return yn + bonus


def _rwkv_step(t, r_s, k_s, v_s, d_s, a_s, z_s, s_s, tmp, kk_p, ka_p, rk_p, lw_p, lb_p):
    _rwkv_prep(t, r_s, k_s, d_s, a_s, tmp, kk_p, ka_p)
    v = v_s[t]
    z_s[t] = _rwkv_tail(_rwkv_body(v, s_s, tmp), v, tmp, rk_p, lw_p, lb_p)


RW_SB = 4


def _hi_lo(x):
    hi = x.astype(bf16)
    return hi, (x - hi.astype(f32)).astype(bf16)


def _rwkv_dec_kernel(s_ref, r_ref, k_ref, v_ref, d_ref, a_ref, kk_p, ka_p, rk_p, lw_p, lb_p, so_ref, z_ref):
    dn_t = (((1,), (1,)), ((), ()))
    dn_l = (((0,), (0,)), ((), ()))
    pre = []
    for bb in range(RW_SB):
        r = r_ref[bb]
        k = k_ref[bb]
        a = a_ref[bb]
        kkr = k * kk_p[...]
        nrm = jnp.sqrt(jnp.sum(kkr * kkr, axis=-1, keepdims=True))
        kk = kkr / jnp.maximum(nrm, 1e-12)
        k2 = k * (1.0 + (a - 1.0) * ka_p[...])
        pre.append(dict(r=r, k2=k2, v=v_ref[bb], d=d_ref[bb], nkk=(-kk).astype(bf16), rb=r.astype(bf16),
                        v2=_hi_lo(v_ref[bb]), ka2=_hi_lo(kk * a), k22=_hi_lo(k2)))
    units = [(bb, h) for bb in range(RW_SB) for h in range(RW_H)]
    ss = [s_ref[bb, h] for bb, h in units]
    sas = [lax.dot_general(pre[bb]["nkk"][h:h + 1], s.astype(bf16), dn_t, preferred_element_type=f32)
           for (bb, h), s in zip(units, ss)]
    us = []
    for (bb, h), sa in zip(units, sas):
        hh = slice(h, h + 1)
        p = pre[bb]
        sa_hi, sa_lo = _hi_lo(sa)
        lhs = jnp.concatenate([sa_hi, sa_hi, sa_lo, p["v2"][0][hh], p["v2"][0][hh], p["v2"][1][hh]], axis=0)
        rhs = jnp.concatenate([p["ka2"][0][hh], p["ka2"][1][hh], p["ka2"][0][hh],
                               p["k22"][0][hh], p["k22"][1][hh], p["k22"][0][hh]], axis=0)
        us.append(lax.dot_general(lhs, rhs, dn_l, preferred_element_type=f32))
    sns = [s * pre[bb]["d"][h:h + 1] + u for (bb, h), s, u in zip(units, ss, us)]
    for (bb, h), sn in zip(units, sns):
        so_ref[bb, h] = sn
    ys = [lax.dot_general(pre[bb]["rb"][h:h + 1], sn.astype(bf16), dn_t, preferred_element_type=f32)
          for (bb, h), sn in zip(units, sns)]
    for bb in range(RW_SB):
        p = pre[bb]
        y = jnp.concatenate(ys[bb * RW_H:(bb + 1) * RW_H], axis=0)
        mean = jnp.mean(y, axis=-1, keepdims=True)
        yc = y - mean
        var = jnp.mean(yc * yc, axis=-1, keepdims=True)
        yn = yc * lax.rsqrt(var + RW_LN_EPS) * lw_p[...] + lb_p[...]
        bonus = jnp.sum(p["r"] * p["k2"] * rk_p[...], axis=-1, keepdims=True) * p["v"]
        z_ref[bb] = yn + bonus


def _rwkv_sample(state, j, r, k, v, dec, a, params):
    def heads(x):
        return x[ROW_S:ROW_S + NS].reshape(NS, RW_H, RW_N)

    vec = pl.BlockSpec((RW_SB, RW_H, RW_N), lambda i: (i, 0, 0))
    par = pl.BlockSpec((RW_H, RW_N), lambda i: (0, 0))
    s_new, z = pl.pallas_call(
        _rwkv_dec_kernel,
        grid=(NS // RW_SB,),
        in_specs=[pl.BlockSpec((None, RW_SB, RW_H, RW_N, RW_N), lambda i: (j, i, 0, 0, 0))] + [vec] * 5 + [par] * 5,
        out_specs=[pl.BlockSpec((RW_SB, RW_H, RW_N, RW_N), lambda i: (i, 0, 0, 0)), vec],
        out_shape=[_sds((NS, RW_H, RW_N, RW_N), f32), _sds((NS, RW_H, RW_N), f32)],
        name="rwkv_decode",
        compiler_params=_cp(("arbitrary",)),
    )(state, heads(r), heads(k), heads(v), heads(dec), heads(a), *params)
    return s_new, z.reshape(NS, D)


RW_Q = D // 128


def _half_transpose(x):
    xt = x.T
    return jnp.concatenate([xt[0:RW_N], xt[RW_N:2 * RW_N]], axis=1)


def _rwkv_prompt_kernel(*refs):
    n_in = 5 * NB
    in_refs = refs[:n_in]
    params = refs[n_in:n_in + 5]
    zmain, zmeta, sf_ref = refs[n_in + 5:n_in + 8]
    scr = refs[n_in + 8:]
    seq, z_s, zo_s, s_s, tmp = scr[:5], scr[5], scr[6], scr[7], scr[8:]
    c = pl.program_id(0)

    @pl.when(c == 0)
    def _init():
        s_s[...] = jnp.zeros_like(s_s)
        z_s[...] = jnp.zeros_like(z_s)

    def load_token(t):
        rows = pl.ds(pl.multiple_of(t * RW_Q, RW_Q), RW_Q)
        for a in range(5):
            tile = jnp.concatenate([in_refs[a * NB + b][rows, :] for b in range(NB)], axis=0)
            seq[a][t] = _half_transpose(tile)

    load_token(0)

    def step(t, carry):
        tp = jnp.maximum(t - 1, 0)
        zo_s[tp] = _half_transpose(z_s[tp])
        _rwkv_step(t, *seq, z_s, s_s, tmp, *params)
        load_token(jnp.minimum(t + 1, RW_TS - 1))
        return carry

    lax.fori_loop(0, RW_TS, step, 0)
    zo_s[RW_TS - 1] = _half_transpose(z_s[RW_TS - 1])

    def write(z_ref):
        for t in range(RW_TS):
            for b in range(NB):
                z_ref[b, t * RW_Q:(t + 1) * RW_Q, :] = zo_s[t, b * RW_Q:(b + 1) * RW_Q, :]

    @pl.when(c == 0)
    def _write_meta():
        write(zmeta)

    @pl.when(c > 0)
    def _write_main():
        write(zmain)

    @pl.when(c == pl.num_programs(0) - 1)
    def _final():
        sf_ref[...] = s_s[...]


def _rwkv_prompt(r, k, v, dec, a, params):
    nc = LP // RW_TS
    blk = RW_TS * RW_Q
    arrs = [x.reshape(ROWS * RW_Q, 128) for x in (r, k, v, dec, a)]
    in_specs, args = [], []
    for x in arrs:
        for b in range(NB):
            in_specs.append(pl.BlockSpec(
                (blk, 128), lambda c, b=b: (jnp.where(c == 0, ROW_M // RW_TS + b, b * nc + c - 1), 0)))
            args.append(x)
    par = pl.BlockSpec((RW_N, 128), lambda c: (0, 0))
    outs = pl.pallas_call(
        _rwkv_prompt_kernel,
        grid=(nc + 1,),
        in_specs=in_specs + [par] * 5,
        out_specs=[pl.BlockSpec((NB, blk, 128), lambda c: (0, jnp.maximum(c - 1, 0), 0)),
                   pl.BlockSpec((NB, blk, 128), lambda c: (0, 0, 0)),
                   pl.BlockSpec((RW_N, RW_N, 128), lambda c: (0, 0, 0))],
        out_shape=[_sds((NB, LP * RW_Q, 128), f32), _sds((NB, NM * RW_Q, 128), f32),
                   _sds((RW_N, RW_N, 128), f32)],
        scratch_shapes=[pltpu.VMEM((RW_TS, RW_N, 128), f32)] * 7 + [pltpu.VMEM((RW_N, RW_N, 128), f32)]
                       + [pltpu.VMEM((RW_N, 128), f32)] * 5,
        name="rwkv_prompt",
        compiler_params=_cp(("arbitrary",)),
    )(*args, *params)
    return outs[0].reshape(ROWS_MAIN, RW_Q, 128), outs[1].reshape(NB * NM, D), outs[2]


def _rwkv_layer(h, norm_g, ffn_g, state, shift, j, mu, w_rkv, w0, w1, w2, a0, a1, a2, g1, g2, k_k, k_a, r_k,
                ln_w, ln_b, w_o):
    *xmix, shift_p, shift_s = _rwkv_premix(h, norm_g, mu[j], shift[j])
    tn = 1024
    common = dict(K=D, tm=TM_BIG, tn=tn, n_row_tiles=ROWS // TM_BIG, n_col_tiles=D // tn, prologue=_pro_id,
                  epilogue=lambda accs, cvs, rms: (accs[0],))
    o = (_sds((ROWS, RW_Q, 128), f32), 0, 0)
    r, k, v = [_mm(name="rwkv_rkv", xs=[(xmix[n], 0)], ws=[(w_rkv, (j, n), 0)], outs=[o], **common)[0]
               for n in range(3)]

    def padc(w):
        return jnp.pad(w, ((0, 0), (0, LORA_PAD - w.shape[1])))

    def padr(w):
        return jnp.pad(w, ((0, LORA_PAD - w.shape[0]), (0, 0)))

    dec, a, g = _rwkv_lora(xmix[3], xmix[4], xmix[5], padc(w1[j]), padr(w2[j]), padc(a1[j]), padr(a2[j]),
                           g1[j], g2[j], w0[j].reshape(1, D), a0[j].reshape(1, D))

    def prompt_param(p):
        pt = p.reshape(RW_Q, 2, RW_N).transpose(2, 1, 0)
        return jnp.broadcast_to(pt[:, :, None, :], (RW_N, 2, NB, RW_Q)).reshape(RW_N, 128)

    head_params = (k_k[j], k_a[j], r_k[j].reshape(D), ln_w[j], ln_b[j])
    z_main, z_meta, sT = _rwkv_prompt(r, k, v, dec, a, [prompt_param(p) for p in head_params])
    s_p = sT.reshape(RW_N, RW_N, 2, NB, RW_Q).transpose(3, 4, 2, 1, 0).reshape(NB, RW_H, RW_N, RW_N)
    s_s, z_s = _rwkv_sample(state, j, r, k, v, dec, a, [p.reshape(RW_H, RW_N) for p in head_params])

    z_aux = _aux_rows(z_s, z_meta)
    h, xn_f = _mixer_out(h, z_main, z_aux, [g], w_o, (j,), ffn_g, _pro_mul)
    return (h, xn_f), s_p, s_s, shift_p, shift_s


def kernel(x_prompt, x_sample, state_hgrn, state_lru_h, state_lru_conv, state_rwkv, state_rwkv_shift, meta_tokens, norm_mix, norm_ffn, norm_final, hgrn_w_in, hgrn_lb_logits, hgrn_norm, hgrn_w_out, lru_w_in, lru_conv_w, lru_conv_b, lru_gate_w, lru_gate_b, lru_lambda, lru_w_out, rwkv_mu, rwkv_w_rkv, rwkv_w0, rwkv_w1, rwkv_w2, rwkv_a0, rwkv_a1, rwkv_a2, rwkv_g1, rwkv_g2, rwkv_k_k, rwkv_k_a, rwkv_r_k, rwkv_ln_w, rwkv_ln_b, rwkv_w_o, ffn_w_in, ffn_w_out):
    depth = norm_mix.shape[0]
    x_aux = _aux_rows(x_sample.reshape(NS, D), jnp.tile(meta_tokens, (NB, 1)))
    h, xn0 = _assemble(x_prompt.reshape(ROWS_MAIN, D), x_aux, norm_mix[0].reshape(1, D))
    hg_p, lh_p, lh_s, lc_p, lc_s, rw_p, rw_s, rs_p, rs_s = [[] for _ in range(9)]
    hg_s = []
    for i in range(depth):
        m, j = i % 3, i // 3
        ng = norm_mix[i].reshape(1, D)
        fg = norm_ffn[i].reshape(1, D)
        if m == 0:
            hx, sp, ss = _hgrn_layer(h, ng, fg, state_hgrn, hg_s, j, hgrn_w_in, hgrn_lb_logits, hgrn_norm,
                                     hgrn_w_out, xn=xn0 if i == 0 else None)
            hg_p.append(sp)
            hg_s.append(ss)
        elif m == 1:
            hx, hp_, hs_, cp_, cs_ = _lru_layer(h, ng, fg, state_lru_h, state_lru_conv, j, lru_w_in,
                                                lru_conv_w, lru_conv_b, lru_gate_w, lru_gate_b, lru_lambda,
                                                lru_w_out)
            lh_p.append(hp_)
            lh_s.append(hs_)
            lc_p.append(cp_)
            lc_s.append(cs_)
        else:
            hx, sp, ss, shp, shs = _rwkv_layer(h, ng, fg, state_rwkv, state_rwkv_shift, j, rwkv_mu, rwkv_w_rkv,
                                               rwkv_w0, rwkv_w1, rwkv_w2, rwkv_a0, rwkv_a1, rwkv_a2, rwkv_g1,
                                               rwkv_g2, rwkv_k_k, rwkv_k_a, rwkv_r_k, rwkv_ln_w, rwkv_ln_b,
                                               rwkv_w_o)
            rw_p.append(sp)
            rw_s.append(ss)
            rs_p.append(shp)
            rs_s.append(shs)
        h = _ffn(*hx, ffn_w_in, ffn_w_out, i)
    nf = norm_final.reshape(1, D)
    y_main = _rmsnorm(h, nf, out_dtype=f32, tm=512, row_tile0=0, n_row_tiles=ROWS_MAIN // 512)
    y_aux = _rmsnorm(h, nf, out_dtype=f32, tm=AUX, row_tile0=ROWS_MAIN // AUX, n_row_tiles=1)
    y_prompt = y_main.reshape(NB, LP, D)
    y_sample = y_aux[:NS].reshape(NS, 1, D)
    return (y_prompt, y_sample, jnp.stack(hg_p), hg_s[-1], jnp.stack(lh_p), jnp.stack(lh_s),
            jnp.stack(lc_p), jnp.stack(lc_s), jnp.stack(rw_p), jnp.stack(rw_s), jnp.stack(rs_p), jnp.stack(rs_s))
```

```python
import functools

import jax
import jax.numpy as jnp
import numpy as np
from jax import lax
from jax.experimental import pallas as pl
from jax.experimental.pallas import tpu as pltpu

f32 = jnp.float32
bf16 = jnp.bfloat16

D = 2048
NB = 4
LP = 2048
NM = 16
NS = 128
ROWS_MAIN = NB * LP
ROW_S = ROWS_MAIN
ROW_M = ROW_S + NS
AUX = 256
ROWS = ROWS_MAIN + AUX
EPS = 1e-6
HG_H, HG_K = 16, 128
HG_C = 64
HG_UNROLL = 4
GATE_EXP_CLIP = 60.0
LRU_NBLK, LRU_BW = 8, 256
LRU_C = 8.0
LRU_T = 256
RW_H, RW_N = 32, 64
RW_LN_EPS = 64e-5
RW_TS = 16
PM_T = 256
LORA_PAD = 128
D_FF = 5632
TM_BIG = 1408
TM_MID = 768
VMEM_LIMIT = 56 * 1024 * 1024


def _cp(sem):
    return pltpu.CompilerParams(dimension_semantics=sem, vmem_limit_bytes=VMEM_LIMIT)


def _softplus(x):
    return jnp.maximum(x, 0.0) + jnp.log(1.0 + jnp.exp(-jnp.abs(x)))


def _norm_kernel(h_ref, g_ref, o_ref):
    x = h_ref[...]
    ms = jnp.mean(x * x, axis=-1, keepdims=True)
    o_ref[...] = (x * lax.rsqrt(ms + EPS) * g_ref[...]).astype(o_ref.dtype)


def _rmsnorm(h, g, *, out_dtype, tm, row_tile0, n_row_tiles):
    return pl.pallas_call(
        _norm_kernel,
        grid=(n_row_tiles,),
        in_specs=[pl.BlockSpec((tm, D), lambda i: (i + row_tile0, 0)),
                  pl.BlockSpec((1, D), lambda i: (0, 0))],
        out_specs=pl.BlockSpec((tm, D), lambda i: (i, 0)),
        out_shape=jax.ShapeDtypeStruct((n_row_tiles * tm, D), out_dtype),
        name="rmsnorm",
        compiler_params=_cp(("arbitrary",)),
    )(h, g)


def _assemble_kernel(xm_ref, xa_ref, g_ref, h_ref, xn_ref):
    def emit(x):
        h_ref[...] = x
        ms = jnp.mean(x * x, axis=-1, keepdims=True)
        xn_ref[...] = (x * lax.rsqrt(ms + EPS) * g_ref[...]).astype(xn_ref.dtype)

    @pl.when(pl.program_id(0) < ROWS_MAIN // AUX)
    def _main():
        emit(xm_ref[...])

    @pl.when(pl.program_id(0) == ROWS_MAIN // AUX)
    def _aux():
        emit(xa_ref[...])


def _assemble(x_main, x_aux, g):
    nm = ROWS_MAIN // AUX
    return pl.pallas_call(
        _assemble_kernel,
        grid=(nm + 1,),
        in_specs=[pl.BlockSpec((AUX, D), lambda i: (jnp.minimum(i, nm - 1), 0)),
                  pl.BlockSpec((AUX, D), lambda i: (0, 0)),
                  pl.BlockSpec((1, D), lambda i: (0, 0))],
        out_specs=[pl.BlockSpec((AUX, D), lambda i: (i, 0)), pl.BlockSpec((AUX, D), lambda i: (i, 0))],
        out_shape=[_sds((ROWS, D), f32), _sds((ROWS, D), bf16)],
        name="assemble",
        compiler_params=_cp(("arbitrary",)),
    )(x_main, x_aux, g)


def _mm_body(*refs, n_x, n_w, n_kv, n_cv, n_rm, n_out, prologue, epilogue, row_splits):
    p = 0
    x_refs = refs[p:p + n_x]; p += n_x
    w_refs = refs[p:p + n_w]; p += n_w
    kv_refs = refs[p:p + n_kv]; p += n_kv
    cv_refs = refs[p:p + n_cv]; p += n_cv
    rm_refs = refs[p:p + n_rm]; p += n_rm
    out_refs = refs[p:p + n_out]; p += n_out
    w_scr = refs[p:p + n_w]

    @pl.when(pl.program_id(1) == 0)
    def _cast_weights():
        for w, s in zip(w_refs, w_scr):
            s[...] = w[...].astype(bf16)

    def flat(v):
        return v.reshape(v.shape[0], v.shape[1] * v.shape[2]) if v.ndim == 3 else v

    tm = out_refs[0].shape[0]
    sub = tm // row_splits
    for s0 in range(0, tm, sub):
        rows = slice(s0, s0 + sub)
        x = prologue([flat(r[rows]) for r in x_refs], [r[...] for r in kv_refs])
        accs = [jnp.dot(x, s[...], preferred_element_type=f32) for s in w_scr]
        res = epilogue(accs, [r[...] for r in cv_refs], [r[rows] for r in rm_refs])
        for o, r in zip(out_refs, res):
            o[rows] = r.astype(o.dtype).reshape((sub,) + o.shape[1:])


def _mm(*, name="proj", xs, ws, kvecs=(), cvecs=(), rmats=(), outs, K, tm, tn, n_row_tiles, n_col_tiles,
        prologue, epilogue, aliases=None, row_splits=1):
    in_specs, args = [], []
    for a, r0 in xs:
        if a.ndim == 3:
            in_specs.append(pl.BlockSpec((tm, K // 128, 128), lambda j, i, r0=r0: (i + r0, 0, 0)))
        else:
            in_specs.append(pl.BlockSpec((tm, K), lambda j, i, r0=r0: (i + r0, 0)))
        args.append(a)
    for a, lead, c0 in ws:
        nl = len(lead)
        in_specs.append(pl.BlockSpec((None,) * nl + (K, tn), lambda j, i, lead=lead, c0=c0: lead + (0, j + c0)))
        args.append(a)
    for a in kvecs:
        in_specs.append(pl.BlockSpec(a.shape, lambda j, i: (0, 0)))
        args.append(a)
    for a, c0 in cvecs:
        in_specs.append(pl.BlockSpec((a.shape[0], tn), lambda j, i, c0=c0: (0, j + c0)))
        args.append(a)
    for a, r0, c0 in rmats:
        in_specs.append(pl.BlockSpec((tm, tn), lambda j, i, r0=r0, c0=c0: (i + r0, j + c0)))
        args.append(a)
    out_specs = [pl.BlockSpec((tm, tn // 128, 128), lambda j, i, r0=r0, c0=c0: (i + r0, j + c0, 0))
                 if len(s.shape) == 3 else
                 pl.BlockSpec((tm, tn), lambda j, i, r0=r0, c0=c0: (i + r0, j + c0)) for s, r0, c0 in outs]
    body = functools.partial(_mm_body, n_x=len(xs), n_w=len(ws), n_kv=len(kvecs), n_cv=len(cvecs),
                             n_rm=len(rmats), n_out=len(outs), prologue=prologue, epilogue=epilogue,
                             row_splits=row_splits)
    res = pl.pallas_call(
        body,
        grid=(n_col_tiles, n_row_tiles),
        in_specs=in_specs,
        out_specs=out_specs,
        out_shape=[s for s, _, _ in outs],
        scratch_shapes=[pltpu.VMEM((K, tn), bf16) for _ in ws],
        input_output_aliases=aliases or {},
        name=name,
        compiler_params=_cp(("arbitrary", "arbitrary")),
    )(*args)
    return res


def _pro_id(xs, kvs):
    return xs[0]


def _pro_mul(xs, kvs):
    return (xs[0] * xs[1]).astype(bf16)


def _sds(shape, dtype):
    return jax.ShapeDtypeStruct(shape, dtype)


def _mixer_out_kernel(*refs, n_u, prologue):
    xm_ref, xa_ref = refs[:2]
    u_refs = refs[2:2 + n_u]
    w_ref, h_ref, g_ref, ho_ref, xn_ref, w_s = refs[2 + n_u:]
    i = pl.program_id(0)

    @pl.when(i == 0)
    def _cast():
        w_s[...] = w_ref[...].astype(bf16)

    def run(x_ref):
        x = x_ref[...]
        if x.ndim == 3:
            x = x.reshape(x.shape[0], x.shape[1] * x.shape[2])
        hn = h_ref[...] + jnp.dot(prologue([x] + [u[...] for u in u_refs], []), w_s[...],
                                  preferred_element_type=f32)
        ho_ref[...] = hn
        ms = jnp.mean(hn * hn, axis=-1, keepdims=True)
        xn_ref[...] = (hn * lax.rsqrt(ms + EPS) * g_ref[...]).astype(xn_ref.dtype)

    @pl.when(i < ROWS_MAIN // AUX)
    def _main():
        run(xm_ref)

    @pl.when(i == ROWS_MAIN // AUX)
    def _aux():
        run(xa_ref)


def _mixer_out(h, x_main, x_aux, unified, w, lead, norm_g, prologue):
    nm = ROWS_MAIN // AUX
    def xspec(a, imap):
        if a.ndim == 3:
            return pl.BlockSpec((AUX,) + a.shape[1:], lambda i: (imap(i), 0, 0))
        return pl.BlockSpec((AUX, a.shape[1]), lambda i: (imap(i), 0))
    nl = len(lead)
    row = pl.BlockSpec((AUX, D), lambda i: (i, 0))
    in_specs = ([xspec(x_main, lambda i: jnp.minimum(i, nm - 1)), xspec(x_aux, lambda i: 0)]
                + [xspec(u, lambda i: i) for u in unified]
                + [pl.BlockSpec((None,) * nl + (D, D), lambda i: lead + (0, 0), pipeline_mode=pl.Buffered(1)),
                   row, pl.BlockSpec((1, D), lambda i: (0, 0))])
    return pl.pallas_call(
        functools.partial(_mixer_out_kernel, n_u=len(unified), prologue=prologue),
        grid=(nm + 1,),
        in_specs=in_specs,
        out_specs=[row, row],
        out_shape=[_sds((ROWS, D), f32), _sds((ROWS, D), bf16)],
        scratch_shapes=[pltpu.VMEM((D, D), bf16)],
        input_output_aliases={3 + len(unified): 0},
        name="mixer_out",
        compiler_params=_cp(("arbitrary",)),
    )(x_main, x_aux, *unified, w, h, norm_g)


def _ffn(h, xn, w_in, w_out, layer):
    tn = 512
    nct = D_FF // tn
    act = _mm(name="ffn_in", xs=[(xn, 0)], ws=[(w_in, (layer,), 0), (w_in, (layer,), nct)],
              outs=[(_sds((ROWS, D_FF), bf16), 0, 0)],
              K=D, tm=TM_BIG, tn=tn, n_row_tiles=ROWS // TM_BIG, n_col_tiles=nct,
              prologue=_pro_id,
              epilogue=lambda accs, cvs, rms: (jax.nn.silu(accs[0]) * accs[1],))[0]
    h = _mm(name="ffn_out", xs=[(act, 0)], ws=[(w_out, (layer,), 0)], rmats=[(h, 0, 0)],
            outs=[(_sds((ROWS, D), f32), 0, 0)],
            K=D_FF, tm=TM_MID, tn=tn, n_row_tiles=ROWS // TM_MID, n_col_tiles=D // tn,
            prologue=_pro_id, epilogue=lambda accs, cvs, rms: (rms[0] + accs[0],),
            aliases={2: 0})[0]
    return h


def _hgrn_lower_bound(logits, j):
    m = jnp.max(logits, axis=0, keepdims=True)
    e = jnp.exp(logits - m)
    p = e / jnp.sum(e, axis=0, keepdims=True)
    cs = p[0:1]
    for r in range(1, j + 1):
        cs = cs + p[r:r + 1]
    return jnp.clip(cs - p[0:1], 0.0, 1.0)


def _hgrn_in(xn, w_in, lb_logits, j):
    tn = 1024
    nct = D // tn
    common = dict(K=D, tm=TM_BIG, tn=tn, n_row_tiles=ROWS // TM_BIG, n_col_tiles=nct, prologue=_pro_id,
                  row_splits=2)
    o = (_sds((ROWS, D), f32), 0, 0)
    q = _mm(xs=[(xn, 0)], ws=[(w_in, (j,), 0)], outs=[o],
            epilogue=lambda accs, cvs, rms: (jax.nn.silu(accs[0]),), **common)[0]

    def f_epi(accs, cvs, rms):
        f = accs[0]
        lb = _hgrn_lower_bound(cvs[0], j)
        t = jnp.exp(-jnp.abs(f))
        log_f = (-(jnp.maximum(-f, 0.0) + jnp.log(1.0 + t))
                 + jnp.log(1.0 + lb * jnp.exp(jnp.minimum(-f, GATE_EXP_CLIP))))
        k = (1.0 - lb) * (jnp.where(f >= 0.0, t, 1.0) / (1.0 + t))
        return log_f, k

    log_f, k = _mm(xs=[(xn, 0)], ws=[(w_in, (j,), nct)], cvecs=[(lb_logits, 0)], outs=[o, o],
                   epilogue=f_epi, **{**common, "tm": TM_MID, "n_row_tiles": ROWS // TM_MID, "row_splits": 3})
    v = _mm(xs=[(xn, 0)], ws=[(w_in, (j,), 2 * nct)], outs=[o],
            epilogue=lambda accs, cvs, rms: (accs[0],), **common)[0]
    g = _mm(xs=[(xn, 0)], ws=[(w_in, (j,), 3 * nct)], outs=[o],
            epilogue=lambda accs, cvs, rms: (jax.nn.silu(accs[0]),), **common)[0]
    return q, k, v, log_f, g


def _split3(x):
    hi = x.astype(bf16)
    r = x - hi.astype(f32)
    mid = r.astype(bf16)
    lo = (r - mid.astype(f32)).astype(bf16)
    return hi, mid, lo


def _gla_consts(T):
    t = np.arange(T)[:, None]
    w = np.arange(T)[None, :]
    mats = [w <= t, w > t]
    masks = []
    s = T // 2
    while s >= 1:
        piece = t // (2 * s)
        lower = (t % (2 * s)) >= s
        ref = piece * 2 * s + s
        mats.append(np.where(lower, (w > ref) & (w <= t), (w > t) & (w <= ref)))
        upiece = w // (2 * s)
        ulower = (w % (2 * s)) >= s
        masks.append((piece == upiece) & lower & ~ulower)
        s //= 2
    masks.append(t == w)
    m = np.concatenate(mats, 0)
    eye = np.eye(HG_UNROLL, dtype=bool)
    gmasks = np.stack([np.kron(eye, pm) for pm in masks])
    return (jnp.asarray(np.concatenate([m, m, m], 1), dtype=bf16), jnp.asarray(gmasks, dtype=f32))


def _gla_block(q_ref, k_ref, v_ref, f_ref, g_ref, gn_ref, dm_ref, pm_ref, o_ref, st_ref, e_s, ql_s, kl_s, T):
    nlev = T.bit_length() - 1
    dn_t = (((1,), (1,)), ((), ()))
    dn_l = (((0,), (0,)), ((), ()))
    def factors(ls):
        e_s[0:(nlev + 2) * T, ls] = jnp.dot(dm_ref[...], jnp.concatenate(_split3(f_ref[:, ls]), axis=0),
                                            preferred_element_type=f32)
        q = q_ref[:, ls]
        k = k_ref[:, ls]
        ql_s[0, 0:T, ls] = (q * jnp.exp(e_s[0:T, ls])).astype(bf16)
        kl_s[0, 0:T, ls] = (k * jnp.exp(e_s[T:2 * T, ls])).astype(bf16)
        for l in range(nlev):
            ex = jnp.exp(e_s[(l + 2) * T:(l + 3) * T, ls])
            ql_s[l + 1, 0:T, ls] = (q * ex).astype(bf16)
            kl_s[l + 1, 0:T, ls] = (k * ex).astype(bf16)
        ql_s[nlev + 1, 0:T, ls] = q.astype(bf16)
        kl_s[nlev + 1, 0:T, ls] = k.astype(bf16)

    for h0 in range(0, HG_H, HG_UNROLL):
        factors(slice(h0 * HG_K, (h0 + HG_UNROLL) * HG_K))
        sls = [slice(h * HG_K, (h + 1) * HG_K) for h in range(h0, h0 + HG_UNROLL)]
        att = None
        for l in range(nlev + 1):
            a_l = lax.dot_general(jnp.concatenate([ql_s[l + 1, 0:T, sl] for sl in sls], axis=0),
                                  jnp.concatenate([kl_s[l + 1, 0:T, sl] for sl in sls], axis=0),
                                  dn_t, preferred_element_type=f32) * pm_ref[l]
            att = a_l if att is None else att + a_l
        vbs = [v_ref[:, sl].astype(bf16) for sl in sls]
        o_intra = jnp.dot(att.astype(bf16), jnp.concatenate(vbs, axis=0), preferred_element_type=f32)
        for p in range(0, HG_UNROLL, 2):
            ha, hb = h0 + p, h0 + p + 1
            sla, slb = sls[p], sls[p + 1]
            st2 = jnp.concatenate([st_ref[ha], st_ref[hb]], axis=0)
            oc = lax.dot_general(jnp.concatenate([ql_s[0, 0:T, sla], ql_s[0, 0:T, slb]], axis=0),
                                 st2.astype(bf16), dn_t, preferred_element_type=f32)
            up = lax.dot_general(jnp.concatenate([vbs[p], vbs[p + 1]], axis=1),
                                 jnp.concatenate([kl_s[0, 0:T, sla], kl_s[0, 0:T, slb]], axis=1),
                                 dn_l, preferred_element_type=f32)
            for i, (h, sl) in enumerate(((ha, sla), (hb, slb))):
                o = (oc[i * T:(i + 1) * T, i * HG_K:(i + 1) * HG_K]
                     + o_intra[(p + i) * T:(p + i + 1) * T])
                st_ref[h] = (st2[i * HG_K:(i + 1) * HG_K] * jnp.exp(e_s[T - 1:T, sl])
                             + up[i * HG_K:(i + 1) * HG_K, i * HG_K:(i + 1) * HG_K])
                ms = jnp.mean(o * o, axis=-1, keepdims=True)
                on = o * lax.rsqrt(ms + EPS) * gn_ref[:, sl]
                o_ref[:, sl] = (on * g_ref[:, sl]).astype(o_ref.dtype)


def _gla_kernel(qm, km, vm, fm, gm, qa, ka, va, fa, ga, gn_ref, dmm, pmm, dma, pma,
                om_ref, oa_ref, s_ref, st_ref, e_s, ql_s, kl_s):
    c = pl.program_id(1)

    @pl.when(c == 0)
    def _meta():
        st_ref[...] = jnp.zeros_like(st_ref)
        _gla_block(qa, ka, va, fa, ga, gn_ref, dma, pma, oa_ref, st_ref, e_s, ql_s, kl_s, NM)

    @pl.when(c > 0)
    def _main():
        _gla_block(qm, km, vm, fm, gm, gn_ref, dmm, pmm, om_ref, st_ref, e_s, ql_s, kl_s, HG_C)

    @pl.when(c == pl.num_programs(1) - 1)
    def _final():
        for h in range(HG_H):
            s_ref[0, h] = st_ref[h].T


def _hgrn_prompt(q, k, v, log_f, g, gn):
    nc = LP // HG_C
    main_spec = pl.BlockSpec((HG_C, D), lambda b, c: (b * nc + jnp.maximum(c - 1, 0), 0))
    meta_spec = pl.BlockSpec((NM, D), lambda b, c: (ROW_M // NM + b, 0))
    arrs = [q, k, v, log_f, g]
    consts = [*_gla_consts(HG_C), *_gla_consts(NM)]
    nslot = HG_C.bit_length() + 1
    return pl.pallas_call(
        _gla_kernel,
        grid=(NB, nc + 1),
        in_specs=[main_spec] * 5 + [meta_spec] * 5 + [pl.BlockSpec((1, D), lambda b, c: (0, 0))]
                 + [pl.BlockSpec(a.shape, lambda b, c, n=a.ndim: (0,) * n) for a in consts],
        out_specs=[pl.BlockSpec((HG_C, D), lambda b, c: (b * nc + jnp.maximum(c - 1, 0), 0)),
                   pl.BlockSpec((NM, D), lambda b, c: (b, 0)),
                   pl.BlockSpec((1, HG_H, HG_K, HG_K), lambda b, c: (b, 0, 0, 0))],
        out_shape=[_sds((ROWS_MAIN, D), bf16), _sds((NB * NM, D), bf16), _sds((NB, HG_H, HG_K, HG_K), f32)],
        scratch_shapes=[pltpu.VMEM((HG_H, HG_K, HG_K), f32),
                        pltpu.VMEM((nslot * HG_C, D), f32),
                        pltpu.VMEM((nslot, HG_C, D), bf16),
                        pltpu.VMEM((nslot, HG_C, D), bf16)],
        name="hgrn_prompt",
        compiler_params=_cp(("arbitrary", "arbitrary")),
    )(*arrs, *arrs, gn, *consts)


HG_BT = 8
HG_SB = 4


def _hgrn_dec_kernel(s_ref, qT, kT, fT, v_ref, g_ref, gn_ref, e3_ref, *rest, stack_slot):
    so_ref, o_ref, qb_s, kb_s, gb_s = rest[-5:]
    if stack_slot is not None:
        for m, e_ref in enumerate(rest[:-5]):
            so_ref[m] = e_ref[...]
    row0 = pl.program_id(1) * HG_SB

    def spread(m, dst):
        dst[...] = jnp.dot(jnp.concatenate(_split3(m), axis=1), e3_ref[...], preferred_element_type=f32)

    def per_b(bb, carry):
        spread(qT[bb], qb_s)
        spread(kT[bb], kb_s)
        spread(jnp.exp(fT[bb]), gb_s)
        row = pl.ds(row0 + bb, 1)
        vall = v_ref[row, :]
        outs = []
        for h in range(HG_H):
            hs = slice(h * HG_K, (h + 1) * HG_K)
            sn = gb_s[:, hs] * s_ref[bb, h] + kb_s[:, hs] * vall[:, hs]
            if stack_slot is None:
                so_ref[bb, h] = sn
            else:
                so_ref[stack_slot, bb, h] = sn
            o = jnp.sum(qb_s[:, hs] * sn, axis=0, keepdims=True)
            ms = jnp.mean(o * o, axis=-1, keepdims=True)
            outs.append(o * lax.rsqrt(ms + EPS))
        on = jnp.concatenate(outs, axis=1) * gn_ref[...]
        o_ref[row, :] = on * g_ref[row, :]
        return carry

    lax.fori_loop(0, HG_SB, per_b, 0)


def _hgrn_sample(state, j, q, k, v, log_f, g, gn, earlier):
    def colform(x):
        return x[ROW_S:ROW_S + NS].reshape(NS, HG_H, HG_K).transpose(0, 2, 1)

    n_a = state.shape[0]
    last = j == n_a - 1
    head_of_lane = np.arange(D) // HG_K
    sel = (np.arange(HG_H)[:, None] == head_of_lane[None, :])
    e3 = jnp.asarray(np.concatenate([sel, sel, sel], axis=0), dtype=bf16)
    nsb = HG_BT // HG_SB
    col_spec = pl.BlockSpec((HG_SB, HG_K, HG_H), lambda i, s: (i * nsb + s, 0, 0))
    row_spec = pl.BlockSpec((HG_BT, D), lambda i, s: (ROW_S // HG_BT + i, 0))
    new_spec = pl.BlockSpec((HG_SB, HG_H, HG_K, HG_K), lambda i, s: (i * nsb + s, 0, 0, 0))
    in_specs = [pl.BlockSpec((None, HG_SB, HG_H, HG_K, HG_K), lambda i, s: (j, i * nsb + s, 0, 0, 0)),
                col_spec, col_spec, col_spec, row_spec, row_spec,
                pl.BlockSpec((1, D), lambda i, s: (0, 0)), pl.BlockSpec((3 * HG_H, D), lambda i, s: (0, 0))]
    args = [state, colform(q), colform(k), colform(log_f), v, g, gn, e3]
    if last:
        in_specs += [new_spec] * len(earlier)
        args += list(earlier)
        st_out = pl.BlockSpec((n_a, HG_SB, HG_H, HG_K, HG_K), lambda i, s: (0, i * nsb + s, 0, 0, 0))
        st_shape = _sds(state.shape, f32)
    else:
        st_out, st_shape = new_spec, _sds(state.shape[1:], f32)
    return pl.pallas_call(
        functools.partial(_hgrn_dec_kernel, stack_slot=j if last else None),
        grid=(NS // HG_BT, nsb),
        in_specs=in_specs,
        out_specs=[st_out, pl.BlockSpec((HG_BT, D), lambda i, s: (i, 0))],
        out_shape=[st_shape, _sds((NS, D), f32)],
        scratch_shapes=[pltpu.VMEM((HG_K, D), f32)] * 3,
        name="hgrn_decode",
        compiler_params=_cp(("arbitrary", "arbitrary")),
    )(*args)


def _aux_rows(sample_rows, meta_rows):
    pad = jnp.zeros((AUX - NS - NB * NM, sample_rows.shape[1]), sample_rows.dtype)
    return jnp.concatenate([sample_rows, meta_rows.astype(sample_rows.dtype), pad], axis=0)


def _hgrn_layer(h, norm_g, ffn_g, state, earlier, j, w_in, lb_logits, gnorm, w_out, xn=None):
    if xn is None:
        xn = _rmsnorm(h, norm_g, out_dtype=bf16, tm=TM_MID, row_tile0=0, n_row_tiles=ROWS // TM_MID)
    q, k, v, log_f, g = _hgrn_in(xn, w_in, lb_logits, j)
    gn = gnorm[j].reshape(1, D)
    o_main, o_meta, s_p = _hgrn_prompt(q, k, v, log_f, g, gn)
    s_s, o_s = _hgrn_sample(state, j, q, k, v, log_f, g, gn, earlier)
    o_aux = _aux_rows(o_s.astype(bf16), o_meta)
    h, xn_f = _mixer_out(h, o_main, o_aux, [], w_out, (j,), ffn_g, _pro_id)
    return (h, xn_f), s_p, s_s


def _lru_in(xn, w_in, j):
    tn = 1024
    nct = D // tn
    common = dict(K=D, tm=TM_BIG, tn=tn, n_row_tiles=ROWS // TM_BIG, n_col_tiles=nct, prologue=_pro_id,
                  row_splits=2)
    o = (_sds((ROWS, D), f32), 0, 0)
    y = _mm(xs=[(xn, 0)], ws=[(w_in, (j,), 0)], outs=[o],
            epilogue=lambda accs, cvs, rms: (jax.nn.gelu(accs[0], approximate=True),), **common)[0]
    x = _mm(xs=[(xn, 0)], ws=[(w_in, (j,), nct)], outs=[o],
            epilogue=lambda accs, cvs, rms: (accs[0],), **common)[0]
    return y, x


def _lru_gates(xc, gw_ref, gb_ref, lam_ref, n):
    ls = slice(n * LRU_BW, (n + 1) * LRU_BW)
    xb = xc[:, ls].astype(bf16)
    r = jax.nn.sigmoid(jnp.dot(xb, gw_ref[0, n], preferred_element_type=f32) + gb_ref[0:1, ls])
    ig = jax.nn.sigmoid(jnp.dot(xb, gw_ref[1, n], preferred_element_type=f32) + gb_ref[1:2, ls])
    log_a = -LRU_C * r * _softplus(-lam_ref[:, ls])
    a = jnp.exp(log_a)
    mult = jnp.sqrt(jnp.maximum(1.0 - a * a, 0.0))
    return a, ig, mult


def _lru_conv(cb_ref, cw_ref, x0, x1, x2, x3):
    xc = cb_ref[...] + x0 * cw_ref[0:1, :]
    xc = xc + x1 * cw_ref[1:2, :]
    xc = xc + x2 * cw_ref[2:3, :]
    return xc + x3 * cw_ref[3:4, :]


def _lru_block(x_ref, y_ref, o_ref, cw_ref, cb_ref, gw_s, gb_ref, lam_ref, xbuf, a_s, b_s, hcar, T, first):
    xbuf[8:8 + T, :] = x_ref[...]
    xc = _lru_conv(cb_ref, cw_ref, xbuf[5:5 + T, :], xbuf[6:6 + T, :], xbuf[7:7 + T, :], xbuf[8:8 + T, :])
    xbuf[0:8, :] = xbuf[T:T + 8, :]
    row = lax.broadcasted_iota(jnp.int32, (T, 1), 0)
    for n in range(LRU_NBLK):
        ls = slice(n * LRU_BW, (n + 1) * LRU_BW)
        a, ig, mult = _lru_gates(xc, gw_s, gb_ref, lam_ref, n)
        if first:
            mult = jnp.where(row == 0, 1.0, mult)
        a_s[0:T, ls] = a
        b_s[0:T, ls] = xc[:, ls] * ig * mult

    def step(t, hprev):
        hnew = a_s[pl.ds(t, 1), :] * hprev + b_s[pl.ds(t, 1), :]
        b_s[pl.ds(t, 1), :] = hnew
        return hnew

    hcar[...] = lax.fori_loop(0, T, step, hcar[...], unroll=8)
    o_ref[...] = (b_s[0:T, :] * y_ref[...]).astype(o_ref.dtype)


def _lru_kernel(xm, ym, xa, ya, cw_ref, cb_ref, gw_ref, gb_ref, lam_ref,
                om_ref, oa_ref, hl_ref, cv_ref, gw_s, xbuf, a_s, b_s, hcar):
    c = pl.program_id(1)

    @pl.when((pl.program_id(0) == 0) & (c == 0))
    def _cast():
        gw_s[...] = gw_ref[...].astype(bf16)

    @pl.when(c == 0)
    def _meta():
        xbuf[0:8, :] = jnp.zeros((8, D), f32)
        hcar[...] = jnp.zeros_like(hcar)
        _lru_block(xa, ya, oa_ref, cw_ref, cb_ref, gw_s, gb_ref, lam_ref, xbuf, a_s, b_s, hcar, NM, True)

    @pl.when(c > 0)
    def _main():
        _lru_block(xm, ym, om_ref, cw_ref, cb_ref, gw_s, gb_ref, lam_ref, xbuf, a_s, b_s, hcar, LRU_T, False)

    @pl.when(c == pl.num_programs(1) - 1)
    def _final():
        hl_ref[0] = hcar[...]
        cv_ref[0] = xbuf[5:8, :]


def _lru_prompt(x, y, j, conv_w, conv_b, gate_w, gate_b, lam):
    nc = LP // LRU_T
    main_spec = pl.BlockSpec((LRU_T, D), lambda b, c: (b * nc + jnp.maximum(c - 1, 0), 0))
    meta_spec = pl.BlockSpec((NM, D), lambda b, c: (ROW_M // NM + b, 0))
    return pl.pallas_call(
        _lru_kernel,
        grid=(NB, nc + 1),
        in_specs=[main_spec, main_spec, meta_spec, meta_spec,
                  pl.BlockSpec((None, 4, D), lambda b, c: (j, 0, 0)),
                  pl.BlockSpec((1, D), lambda b, c: (j, 0)),
                  pl.BlockSpec((None, 2, LRU_NBLK, LRU_BW, LRU_BW), lambda b, c: (j, 0, 0, 0, 0)),
                  pl.BlockSpec((None, 2, D), lambda b, c: (j, 0, 0)),
                  pl.BlockSpec((1, D), lambda b, c: (j, 0))],
        out_specs=[pl.BlockSpec((LRU_T, D), lambda b, c: (b * nc + jnp.maximum(c - 1, 0), 0)),
                   pl.BlockSpec((NM, D), lambda b, c: (b, 0)),
                   pl.BlockSpec((1, 1, D), lambda b, c: (b, 0, 0)),
                   pl.BlockSpec((1, 3, D), lambda b, c: (b, 0, 0))],
        out_shape=[_sds((ROWS_MAIN, D), bf16), _sds((NB * NM, D), bf16),
                   _sds((NB, 1, D), f32), _sds((NB, 3, D), f32)],
        scratch_shapes=[pltpu.VMEM((2, LRU_NBLK, LRU_BW, LRU_BW), bf16),
                        pltpu.VMEM((LRU_T + 8, D), f32), pltpu.VMEM((LRU_T, D), f32),
                        pltpu.VMEM((LRU_T, D), f32), pltpu.VMEM((1, D), f32)],
        name="lru_prompt",
        compiler_params=_cp(("arbitrary", "arbitrary")),
    )(x, y, x, y, conv_w, conv_b, gate_w, gate_b, lam)


LRU_ST = 64


def _lru_dec_kernel(x_ref, y_ref, h0_ref, cbuf_ref, cw_ref, cb_ref, gw_ref, gb_ref, lam_ref,
                    o_ref, hn_ref, cn_ref, gw_s):
    gw_s[...] = gw_ref[...].astype(bf16)
    x = x_ref[...]
    b0 = cbuf_ref[:, 0, :]
    b1 = cbuf_ref[:, 1, :]
    b2 = cbuf_ref[:, 2, :]
    xc = _lru_conv(cb_ref, cw_ref, b0, b1, b2, x)
    cn_ref[:, 0, :] = b1
    cn_ref[:, 1, :] = b2
    cn_ref[:, 2, :] = x
    for n in range(LRU_NBLK):
        ls = slice(n * LRU_BW, (n + 1) * LRU_BW)
        a, ig, mult = _lru_gates(xc, gw_s, gb_ref, lam_ref, n)
        hn = a * h0_ref[:, ls] + xc[:, ls] * ig * mult
        hn_ref[:, ls] = hn
        o_ref[:, ls] = (hn * y_ref[:, ls]).astype(o_ref.dtype)


def _lru_sample(x, y, j, h0, cbuf, conv_w, conv_b, gate_w, gate_b, lam):
    row_spec = pl.BlockSpec((LRU_ST, D), lambda i: (ROW_S // LRU_ST + i, 0))
    return pl.pallas_call(
        _lru_dec_kernel,
        grid=(NS // LRU_ST,),
        in_specs=[row_spec, row_spec,
                  pl.BlockSpec((None, LRU_ST, D), lambda i: (j, i, 0)),
                  pl.BlockSpec((None, LRU_ST, 3, D), lambda i: (j, i, 0, 0)),
                  pl.BlockSpec((None, 4, D), lambda i: (j, 0, 0)),
                  pl.BlockSpec((1, D), lambda i: (j, 0)),
                  pl.BlockSpec((None, 2, LRU_NBLK, LRU_BW, LRU_BW), lambda i: (j, 0, 0, 0, 0)),
                  pl.BlockSpec((None, 2, D), lambda i: (j, 0, 0)),
                  pl.BlockSpec((1, D), lambda i: (j, 0))],
        out_specs=[pl.BlockSpec((LRU_ST, D), lambda i: (i, 0)),
                   pl.BlockSpec((LRU_ST, D), lambda i: (i, 0)),
                   pl.BlockSpec((LRU_ST, 3, D), lambda i: (i, 0, 0))],
        out_shape=[_sds((NS, D), bf16), _sds((NS, D), f32), _sds((NS, 3, D), f32)],
        scratch_shapes=[pltpu.VMEM((2, LRU_NBLK, LRU_BW, LRU_BW), bf16)],
        name="lru_decode",
        compiler_params=_cp(("arbitrary",)),
    )(x, y, h0, cbuf, conv_w, conv_b, gate_w, gate_b, lam)


def _lru_layer(h, norm_g, ffn_g, state_h, state_conv, j, w_in, conv_w, conv_b, gate_w, gate_b, lam, w_out):
    xn = _rmsnorm(h, norm_g, out_dtype=bf16, tm=TM_MID, row_tile0=0, n_row_tiles=ROWS // TM_MID)
    y, x = _lru_in(xn, w_in, j)
    o_main, o_meta, hl_p, cv_p = _lru_prompt(x, y, j, conv_w, conv_b, gate_w, gate_b, lam)
    o_s, hl_s, cv_s = _lru_sample(x, y, j, state_h, state_conv, conv_w, conv_b, gate_w, gate_b, lam)
    o_aux = _aux_rows(o_s, o_meta)
    h, xn_f = _mixer_out(h, o_main, o_aux, [], w_out, (j,), ffn_g, _pro_id)
    return (h, xn_f), hl_p.reshape(NB, D), hl_s, cv_p, cv_s


def _rwkv_premix_kernel(h_ref, g_ref, mu_ref, sh_ref, *refs):
    x_refs = refs[:6]
    sp_ref, ss_ref, xbuf, meta_last = refs[6:]
    i = pl.program_id(0)
    x = h_ref[...]
    ms = jnp.mean(x * x, axis=-1, keepdims=True)
    xn = x * lax.rsqrt(ms + EPS) * g_ref[...]
    xbuf[8:8 + PM_T, :] = xn
    row = lax.broadcasted_iota(jnp.int32, (PM_T, 1), 0)

    @pl.when(i == 0)
    def _aux():
        xbuf[7:8, :] = jnp.zeros((1, D), f32)
        ss_ref[...] = xn[0:NS]
        for b in range(NB):
            meta_last[b:b + 1, :] = xn[NS + b * NM + NM - 1:NS + b * NM + NM]

    @pl.when(i > 0)
    def _main():
        m = i - 1
        b = m // (LP // PM_T)

        @pl.when(m % (LP // PM_T) == 0)
        def _start():
            xbuf[7:8, :] = meta_last[pl.ds(b, 1), :]

        @pl.when(m % (LP // PM_T) == LP // PM_T - 1)
        def _end():
            sp_ref[pl.ds(b, 1), :] = xn[PM_T - 1:PM_T]

    shifted = xbuf[7:7 + PM_T, :]
    is_meta = (row >= NS) & (row < NS + NB * NM) & ((row - NS) % NM != 0)
    sh_pad = jnp.concatenate([sh_ref[...], jnp.zeros((PM_T - NS, D), f32)], axis=0)
    prev_aux = jnp.where(row < NS, sh_pad, jnp.where(is_meta, shifted, 0.0))
    prev = jnp.where(i == 0, prev_aux, shifted)
    dx = prev - xn
    for n in range(6):
        x_refs[n][...] = (xn + dx * mu_ref[n:n + 1, :]).astype(bf16)
    xbuf[7:8, :] = xn[PM_T - 1:PM_T]


def _rwkv_premix(h, norm_g, mu_j, shift_j):
    nt = ROWS // PM_T
    rows = lambda i: (jnp.where(i == 0, nt - 1, i - 1), 0)
    return pl.pallas_call(
        _rwkv_premix_kernel,
        grid=(nt,),
        in_specs=[pl.BlockSpec((PM_T, D), rows),
                  pl.BlockSpec((1, D), lambda i: (0, 0)),
                  pl.BlockSpec((6, D), lambda i: (0, 0)),
                  pl.BlockSpec((NS, D), lambda i: (0, 0))],
        out_specs=[pl.BlockSpec((PM_T, D), rows)] * 6 + [pl.BlockSpec((NB, D), lambda i: (0, 0)),
                                                         pl.BlockSpec((NS, D), lambda i: (0, 0))],
        out_shape=[_sds((ROWS, D), bf16)] * 6 + [_sds((NB, D), f32), _sds((NS, D), f32)],
        scratch_shapes=[pltpu.VMEM((PM_T + 8, D), f32), pltpu.VMEM((8, D), f32)],
        name="rwkv_premix",
        compiler_params=_cp(("arbitrary",)),
    )(h, norm_g, mu_j, shift_j)


def _rwkv_lora_kernel(xw_ref, xa_ref, xg_ref, w1_ref, w2_ref, a1_ref, a2_ref, g1_ref, g2_ref, w0_ref, a0_ref,
                      d_ref, a_ref, g_ref, w1_s, w2_s, a1_s, a2_s, g1_s, g2_s):
    @pl.when(pl.program_id(0) == 0)
    def _cast():
        for src, dst in ((w1_ref, w1_s), (w2_ref, w2_s), (a1_ref, a1_s), (a2_ref, a2_s), (g1_ref, g1_s),
                         (g2_ref, g2_s)):
            dst[...] = src[...].astype(bf16)

    def mm(u, w_s):
        return jnp.dot(u, w_s[...], preferred_element_type=f32)

    tw = jnp.tanh(mm(xw_ref[...], w1_s)).astype(bf16)
    x = w0_ref[...] + mm(tw, w2_s)
    d_ref[...] = jnp.exp(-np.float32(np.exp(-0.5)) * jax.nn.sigmoid(x)).reshape(d_ref.shape)
    ta = mm(xa_ref[...], a1_s).astype(bf16)
    a_ref[...] = jax.nn.sigmoid(a0_ref[...] + mm(ta, a2_s)).reshape(a_ref.shape)
    tg = jax.nn.sigmoid(mm(xg_ref[...], g1_s)).astype(bf16)
    g_ref[...] = mm(tg, g2_s)


def _rwkv_lora(xw, xa, xg, w1, w2, a1, a2, g1, g2, w0, a0):
    tm = 384
    full = lambda a: pl.BlockSpec(a.shape, lambda i: (0,) * a.ndim)
    row = pl.BlockSpec((tm, D), lambda i: (i, 0))
    row3 = pl.BlockSpec((tm, RW_Q, 128), lambda i: (i, 0, 0))
    ws = [w1, w2, a1, a2, g1, g2]
    return pl.pallas_call(
        _rwkv_lora_kernel,
        grid=(ROWS // tm,),
        in_specs=[row, row, row] + [full(a) for a in ws + [w0, a0]],
        out_specs=[row3, row3, row],
        out_shape=[_sds((ROWS, RW_Q, 128), f32)] * 2 + [_sds((ROWS, D), f32)],
        scratch_shapes=[pltpu.VMEM(a.shape, bf16) for a in ws],
        name="rwkv_lora",
        compiler_params=_cp(("arbitrary",)),
    )(xw, xa, xg, *ws, w0, a0)


def _rwkv_prep(t, r_s, k_s, d_s, a_s, tmp, kk_p, ka_p):
    kk_t, d_t, ka_t, k2_t, r_t = tmp
    k = k_s[t]
    a = a_s[t]
    kkr = k * kk_p[...]
    nrm = jnp.sqrt(jnp.sum(kkr * kkr, axis=0, keepdims=True))
    kk = kkr / jnp.maximum(nrm, 1e-12)
    kk_t[...] = kk
    d_t[...] = d_s[t]
    ka_t[...] = kk * a
    k2_t[...] = k * (1.0 + (a - 1.0) * ka_p[...])
    r_t[...] = r_s[t]


def _rwkv_body(v, s_s, tmp):
    kk_t, d_t, ka_t, k2_t, r_t = tmp
    sa = -(s_s[0] * kk_t[0:1, :])
    for j in range(1, RW_N):
        sa = sa - s_s[j] * kk_t[j:j + 1, :]
    y = None
    for j in range(RW_N):
        sn = s_s[j] * d_t[j:j + 1, :] + sa * ka_t[j:j + 1, :] + v * k2_t[j:j + 1, :]
        s_s[j] = sn
        yj = sn * r_t[j:j + 1, :]
        y = yj if y is None else y + yj
    return y


def _rwkv_tail(y, v, tmp, rk_p, lw_p, lb_p):
    _, _, _, k2_t, r_t = tmp
    mean = jnp.mean(y, axis=0, keepdims=True)
    yc = y - mean
    var = jnp.mean(yc * yc, axis=0, keepdims=True)
    yn = yc * lax.rsqrt(var + RW_LN_EPS) * lw_p[...] + lb_p[...]
    bonus = jnp.sum(r_t[...] * k2_t[...] * rk_p[...], axis=0, keepdims=True) * v
    return yn + bonus


def _rwkv_step(t, r_s, k_s, v_s, d_s, a_s, z_s, s_s, tmp, kk_p, ka_p, rk_p, lw_p, lb_p):
    _rwkv_prep(t, r_s, k_s, d_s, a_s, tmp, kk_p, ka_p)
    v = v_s[t]
    z_s[t] = _rwkv_tail(_rwkv_body(v, s_s, tmp), v, tmp, rk_p, lw_p, lb_p)


RW_SB = 4


def _hi_lo(x):
    hi = x.astype(bf16)
    return hi, (x - hi.astype(f32)).astype(bf16)


def _rwkv_dec_kernel(s_ref, r_ref, k_ref, v_ref, d_ref, a_ref, kk_p, ka_p, rk_p, lw_p, lb_p, so_ref, z_ref):
    dn_t = (((1,), (1,)), ((), ()))
    dn_l = (((0,), (0,)), ((), ()))
    pre = []
    for bb in range(RW_SB):
        r = r_ref[bb]
        k = k_ref[bb]
        a = a_ref[bb]
        kkr = k * kk_p[...]
        nrm = jnp.sqrt(jnp.sum(kkr * kkr, axis=-1, keepdims=True))
        kk = kkr / jnp.maximum(nrm, 1e-12)
        k2 = k * (1.0 + (a - 1.0) * ka_p[...])
        pre.append(dict(r=r, k2=k2, v=v_ref[bb], d=d_ref[bb], nkk=(-kk).astype(bf16), rb=r.astype(bf16),
                        v2=_hi_lo(v_ref[bb]), ka2=_hi_lo(kk * a), k22=_hi_lo(k2)))
    units = [(bb, h) for bb in range(RW_SB) for h in range(RW_H)]
    ss = [s_ref[bb, h] for bb, h in units]
    sas = [lax.dot_general(pre[bb]["nkk"][h:h + 1], s.astype(bf16), dn_t, preferred_element_type=f32)
           for (bb, h), s in zip(units, ss)]
    us = []
    for (bb, h), sa in zip(units, sas):
        hh = slice(h, h + 1)
        p = pre[bb]
        sa_hi, sa_lo = _hi_lo(sa)
        lhs = jnp.concatenate([sa_hi, sa_hi, sa_lo, p["v2"][0][hh], p["v2"][0][hh], p["v2"][1][hh]], axis=0)
        rhs = jnp.concatenate([p["ka2"][0][hh], p["ka2"][1][hh], p["ka2"][0][hh],
                               p["k22"][0][hh], p["k22"][1][hh], p["k22"][0][hh]], axis=0)
        us.append(lax.dot_general(lhs, rhs, dn_l, preferred_element_type=f32))
    sns = [s * pre[bb]["d"][h:h + 1] + u for (bb, h), s, u in zip(units, ss, us)]
    for (bb, h), sn in zip(units, sns):
        so_ref[bb, h] = sn
    ys = [lax.dot_general(pre[bb]["rb"][h:h + 1], sn.astype(bf16), dn_t, preferred_element_type=f32)
          for (bb, h), sn in zip(units, sns)]
    for bb in range(RW_SB):
        p = pre[bb]
        y = jnp.concatenate(ys[bb * RW_H:(bb + 1) * RW_H], axis=0)
        mean = jnp.mean(y, axis=-1, keepdims=True)
        yc = y - mean
        var = jnp.mean(yc * yc, axis=-1, keepdims=True)
        yn = yc * lax.rsqrt(var + RW_LN_EPS) * lw_p[...] + lb_p[...]
        bonus = jnp.sum(p["r"] * p["k2"] * rk_p[...], axis=-1, keepdims=True) * p["v"]
        z_ref[bb] = yn + bonus


def _rwkv_sample(state, j, r, k, v, dec, a, params):
    def heads(x):
        return x[ROW_S:ROW_S + NS].reshape(NS, RW_H, RW_N)

    vec = pl.BlockSpec((RW_SB, RW_H, RW_N), lambda i: (i, 0, 0))
    par = pl.BlockSpec((RW_H, RW_N), lambda i: (0, 0))
    s_new, z = pl.pallas_call(
        _rwkv_dec_kernel,
        grid=(NS // RW_SB,),
        in_specs=[pl.BlockSpec((None, RW_SB, RW_H, RW_N, RW_N), lambda i: (j, i, 0, 0, 0))] + [vec] * 5 + [par] * 5,
        out_specs=[pl.BlockSpec((RW_SB, RW_H, RW_N, RW_N), lambda i: (i, 0, 0, 0)), vec],
        out_shape=[_sds((NS, RW_H, RW_N, RW_N), f32), _sds((NS, RW_H, RW_N), f32)],
        name="rwkv_decode",
        compiler_params=_cp(("arbitrary",)),
    )(state, heads(r), heads(k), heads(v), heads(dec), heads(a), *params)
    return s_new, z.reshape(NS, D)


RW_Q = D // 128


def _half_transpose(x):
    xt = x.T
    return jnp.concatenate([xt[0:RW_N], xt[RW_N:2 * RW_N]], axis=1)


def _rwkv_prompt_kernel(*refs):
    n_in = 5 * NB
    in_refs = refs[:n_in]
    params = refs[n_in:n_in + 5]
    zmain, zmeta, sf_ref = refs[n_in + 5:n_in + 8]
    scr = refs[n_in + 8:]
    seq, z_s, zo_s, s_s, tmp = scr[:5], scr[5], scr[6], scr[7], scr[8:]
    c = pl.program_id(0)

    @pl.when(c == 0)
    def _init():
        s_s[...] = jnp.zeros_like(s_s)
        z_s[...] = jnp.zeros_like(z_s)

    def load_token(t):
        rows = pl.ds(pl.multiple_of(t * RW_Q, RW_Q), RW_Q)
        for a in range(5):
            tile = jnp.concatenate([in_refs[a * NB + b][rows, :] for b in range(NB)], axis=0)
            seq[a][t] = _half_transpose(tile)

    load_token(0)

    def step(t, carry):
        tp = jnp.maximum(t - 1, 0)
        zo_s[tp] = _half_transpose(z_s[tp])
        _rwkv_step(t, *seq, z_s, s_s, tmp, *params)
        load_token(jnp.minimum(t + 1, RW_TS - 1))
        return carry

    lax.fori_loop(0, RW_TS, step, 0)
    zo_s[RW_TS - 1] = _half_transpose(z_s[RW_TS - 1])

    def write(z_ref):
        for t in range(RW_TS):
            for b in range(NB):
                z_ref[b, t * RW_Q:(t + 1) * RW_Q, :] = zo_s[t, b * RW_Q:(b + 1) * RW_Q, :]

    @pl.when(c == 0)
    def _write_meta():
        write(zmeta)

    @pl.when(c > 0)
    def _write_main():
        write(zmain)

    @pl.when(c == pl.num_programs(0) - 1)
    def _final():
        sf_ref[...] = s_s[...]


def _rwkv_prompt(r, k, v, dec, a, params):
    nc = LP // RW_TS
    blk = RW_TS * RW_Q
    arrs = [x.reshape(ROWS * RW_Q, 128) for x in (r, k, v, dec, a)]
    in_specs, args = [], []
    for x in arrs:
        for b in range(NB):
            in_specs.append(pl.BlockSpec(
                (blk, 128), lambda c, b=b: (jnp.where(c == 0, ROW_M // RW_TS + b, b * nc + c - 1), 0)))
            args.append(x)
    par = pl.BlockSpec((RW_N, 128), lambda c: (0, 0))
    outs = pl.pallas_call(
        _rwkv_prompt_kernel,
        grid=(nc + 1,),
        in_specs=in_specs + [par] * 5,
        out_specs=[pl.BlockSpec((NB, blk, 128), lambda c: (0, jnp.maximum(c - 1, 0), 0)),
                   pl.BlockSpec((NB, blk, 128), lambda c: (0, 0, 0)),
                   pl.BlockSpec((RW_N, RW_N, 128), lambda c: (0, 0, 0))],
        out_shape=[_sds((NB, LP * RW_Q, 128), f32), _sds((NB, NM * RW_Q, 128), f32),
                   _sds((RW_N, RW_N, 128), f32)],
        scratch_shapes=[pltpu.VMEM((RW_TS, RW_N, 128), f32)] * 7 + [pltpu.VMEM((RW_N, RW_N, 128), f32)]
                       + [pltpu.VMEM((RW_N, 128), f32)] * 5,
        name="rwkv_prompt",
        compiler_params=_cp(("arbitrary",)),
    )(*args, *params)
    return outs[0].reshape(ROWS_MAIN, RW_Q, 128), outs[1].reshape(NB * NM, D), outs[2]


def _rwkv_layer(h, norm_g, ffn_g, state, shift, j, mu, w_rkv, w0, w1, w2, a0, a1, a2, g1, g2, k_k, k_a, r_k,
                ln_w, ln_b, w_o):
    *xmix, shift_p, shift_s = _rwkv_premix(h, norm_g, mu[j], shift[j])
    tn = 1024
    common = dict(K=D, tm=TM_BIG, tn=tn, n_row_tiles=ROWS // TM_BIG, n_col_tiles=D // tn, prologue=_pro_id,
                  epilogue=lambda accs, cvs, rms: (accs[0],))
    o = (_sds((ROWS, RW_Q, 128), f32), 0, 0)
    r, k, v = [_mm(name="rwkv_rkv", xs=[(xmix[n], 0)], ws=[(w_rkv, (j, n), 0)], outs=[o], **common)[0]
               for n in range(3)]

    def padc(w):
        return jnp.pad(w, ((0, 0), (0, LORA_PAD - w.shape[1])))

    def padr(w):
        return jnp.pad(w, ((0, LORA_PAD - w.shape[0]), (0, 0)))

    dec, a, g = _rwkv_lora(xmix[3], xmix[4], xmix[5], padc(w1[j]), padr(w2[j]), padc(a1[j]), padr(a2[j]),
                           g1[j], g2[j], w0[j].reshape(1, D), a0[j].reshape(1, D))

    def prompt_param(p):
        pt = p.reshape(RW_Q, 2, RW_N).transpose(2, 1, 0)
        return jnp.broadcast_to(pt[:, :, None, :], (RW_N, 2, NB, RW_Q)).reshape(RW_N, 128)

    head_params = (k_k[j], k_a[j], r_k[j].reshape(D), ln_w[j], ln_b[j])
    z_main, z_meta, sT = _rwkv_prompt(r, k, v, dec, a, [prompt_param(p) for p in head_params])
    s_p = sT.reshape(RW_N, RW_N, 2, NB, RW_Q).transpose(3, 4, 2, 1, 0).reshape(NB, RW_H, RW_N, RW_N)
    s_s, z_s = _rwkv_sample(state, j, r, k, v, dec, a, [p.reshape(RW_H, RW_N) for p in head_params])

    z_aux = _aux_rows(z_s, z_meta)
    h, xn_f = _mixer_out(h, z_main, z_aux, [g], w_o, (j,), ffn_g, _pro_mul)
    return (h, xn_f), s_p, s_s, shift_p, shift_s


def kernel(x_prompt, x_sample, state_hgrn, state_lru_h, state_lru_conv, state_rwkv, state_rwkv_shift, meta_tokens, norm_mix, norm_ffn, norm_final, hgrn_w_in, hgrn_lb_logits, hgrn_norm, hgrn_w_out, lru_w_in, lru_conv_w, lru_conv_b, lru_gate_w, lru_gate_b, lru_lambda, lru_w_out, rwkv_mu, rwkv_w_rkv, rwkv_w0, rwkv_w1, rwkv_w2, rwkv_a0, rwkv_a1, rwkv_a2, rwkv_g1, rwkv_g2, rwkv_k_k, rwkv_k_a, rwkv_r_k, rwkv_ln_w, rwkv_ln_b, rwkv_w_o, ffn_w_in, ffn_w_out):
    depth = norm_mix.shape[0]
    x_aux = _aux_rows(x_sample.reshape(NS, D), jnp.tile(meta_tokens, (NB, 1)))
    h, xn0 = _assemble(x_prompt.reshape(ROWS_MAIN, D), x_aux, norm_mix[0].reshape(1, D))
    hg_p, lh_p, lh_s, lc_p, lc_s, rw_p, rw_s, rs_p, rs_s = [[] for _ in range(9)]
    hg_s = []
    for i in range(depth):
        m, j = i % 3, i // 3
        ng = norm_mix[i].reshape(1, D)
        fg = norm_ffn[i].reshape(1, D)
        if m == 0:
            hx, sp, ss = _hgrn_layer(h, ng, fg, state_hgrn, hg_s, j, hgrn_w_in, hgrn_lb_logits, hgrn_norm,
                                     hgrn_w_out, xn=xn0 if i == 0 else None)
            hg_p.append(sp)
            hg_s.append(ss)
        elif m == 1:
            hx, hp_, hs_, cp_, cs_ = _lru_layer(h, ng, fg, state_lru_h, state_lru_conv, j, lru_w_in,
                                                lru_conv_w, lru_conv_b, lru_gate_w, lru_gate_b, lru_lambda,
                                                lru_w_out)
            lh_p.append(hp_)
            lh_s.append(hs_)
            lc_p.append(cp_)
            lc_s.append(cs_)
        else:
            hx, sp, ss, shp, shs = _rwkv_layer(h, ng, fg, state_rwkv, state_rwkv_shift, j, rwkv_mu, rwkv_w_rkv,
                                               rwkv_w0, rwkv_w1, rwkv_w2, rwkv_a0, rwkv_a1, rwkv_a2, rwkv_g1,
                                               rwkv_g2, rwkv_k_k, rwkv_k_a, rwkv_r_k, rwkv_ln_w, rwkv_ln_b,
                                               rwkv_w_o)
            rw_p.append(sp)
            rw_s.append(ss)
            rs_p.append(shp)
            rs_s.append(shs)
        h = _ffn(*hx, ffn_w_in, ffn_w_out, i)
    nf = norm_final.reshape(1, D)
    y_main = _rmsnorm(h, nf, out_dtype=f32, tm=512, row_tile0=0, n_row_tiles=ROWS_MAIN // 512)
    y_aux = _rmsnorm(h, nf, out_dtype=f32, tm=AUX, row_tile0=ROWS_MAIN // AUX, n_row_tiles=1)
    y_prompt = y_main.reshape(NB, LP, D)
    y_sample = y_aux[:NS].reshape(NS, 1, D)
    def stack(xs):
        return xs[0][None] if len(xs) == 1 else jnp.stack(xs)

    return (y_prompt, y_sample, stack(hg_p), hg_s[-1], stack(lh_p), stack(lh_s),
            stack(lc_p), stack(lc_s), stack(rw_p), stack(rw_s), stack(rs_p), stack(rs_s))
```

```python
import functools

import jax
import jax.numpy as jnp
import numpy as np
from jax import lax
from jax.experimental import pallas as pl
from jax.experimental.pallas import tpu as pltpu

f32 = jnp.float32
bf16 = jnp.bfloat16

D = 2048
NB = 4
LP = 2048
NM = 16
NS = 128
ROWS_MAIN = NB * LP
ROW_S = ROWS_MAIN
ROW_M = ROW_S + NS
AUX = 256
ROWS = ROWS_MAIN + AUX
EPS = 1e-6
HG_H, HG_K = 16, 128
HG_C = 64
HG_UNROLL = 4
GATE_EXP_CLIP = 60.0
LRU_NBLK, LRU_BW = 8, 256
LRU_C = 8.0
LRU_T = 256
RW_H, RW_N = 32, 64
RW_LN_EPS = 64e-5
RW_TS = 16
PM_T = 256
LORA_PAD = 128
D_FF = 5632
TM_BIG = 1408
TM_MID = 768
VMEM_LIMIT = 56 * 1024 * 1024


def _cp(sem):
    return pltpu.CompilerParams(dimension_semantics=sem, vmem_limit_bytes=VMEM_LIMIT)


def _softplus(x):
    return jnp.maximum(x, 0.0) + jnp.log(1.0 + jnp.exp(-jnp.abs(x)))


def _norm_kernel(h_ref, g_ref, o_ref):
    x = h_ref[...]
    ms = jnp.mean(x * x, axis=-1, keepdims=True)
    o_ref[...] = (x * lax.rsqrt(ms + EPS) * g_ref[...]).astype(o_ref.dtype)


def _rmsnorm(h, g, *, out_dtype, tm, row_tile0, n_row_tiles):
    return pl.pallas_call(
        _norm_kernel,
        grid=(n_row_tiles,),
        in_specs=[pl.BlockSpec((tm, D), lambda i: (i + row_tile0, 0)),
                  pl.BlockSpec((1, D), lambda i: (0, 0))],
        out_specs=pl.BlockSpec((tm, D), lambda i: (i, 0)),
        out_shape=jax.ShapeDtypeStruct((n_row_tiles * tm, D), out_dtype),
        name="rmsnorm",
        compiler_params=_cp(("arbitrary",)),
    )(h, g)


def _assemble_kernel(xm_ref, xa_ref, g_ref, h_ref, xn_ref):
    def emit(x):
        h_ref[...] = x
        ms = jnp.mean(x * x, axis=-1, keepdims=True)
        xn_ref[...] = (x * lax.rsqrt(ms + EPS) * g_ref[...]).astype(xn_ref.dtype)

    @pl.when(pl.program_id(0) < ROWS_MAIN // AUX)
    def _main():
        emit(xm_ref[...])

    @pl.when(pl.program_id(0) == ROWS_MAIN // AUX)
    def _aux():
        emit(xa_ref[...])


def _assemble(x_main, x_aux, g):
    nm = ROWS_MAIN // AUX
    return pl.pallas_call(
        _assemble_kernel,
        grid=(nm + 1,),
        in_specs=[pl.BlockSpec((AUX, D), lambda i: (jnp.minimum(i, nm - 1), 0)),
                  pl.BlockSpec((AUX, D), lambda i: (0, 0)),
                  pl.BlockSpec((1, D), lambda i: (0, 0))],
        out_specs=[pl.BlockSpec((AUX, D), lambda i: (i, 0)), pl.BlockSpec((AUX, D), lambda i: (i, 0))],
        out_shape=[_sds((ROWS, D), f32), _sds((ROWS, D), bf16)],
        name="assemble",
        compiler_params=_cp(("arbitrary",)),
    )(x_main, x_aux, g)


def _mm_body(*refs, n_x, n_w, n_kv, n_cv, n_rm, n_out, prologue, epilogue, row_splits):
    p = 0
    x_refs = refs[p:p + n_x]; p += n_x
    w_refs = refs[p:p + n_w]; p += n_w
    kv_refs = refs[p:p + n_kv]; p += n_kv
    cv_refs = refs[p:p + n_cv]; p += n_cv
    rm_refs = refs[p:p + n_rm]; p += n_rm
    out_refs = refs[p:p + n_out]; p += n_out
    w_scr = refs[p:p + n_w]

    @pl.when(pl.program_id(1) == 0)
    def _cast_weights():
        for w, s in zip(w_refs, w_scr):
            s[...] = w[...].astype(bf16)

    def flat(v):
        return v.reshape(v.shape[0], v.shape[1] * v.shape[2]) if v.ndim == 3 else v

    tm = out_refs[0].shape[0]
    sub = tm // row_splits
    for s0 in range(0, tm, sub):
        rows = slice(s0, s0 + sub)
        x = prologue([flat(r[rows]) for r in x_refs], [r[...] for r in kv_refs])
        accs = [jnp.dot(x, s[...], preferred_element_type=f32) for s in w_scr]
        res = epilogue(accs, [r[...] for r in cv_refs], [r[rows] for r in rm_refs])
        for o, r in zip(out_refs, res):
            o[rows] = r.astype(o.dtype).reshape((sub,) + o.shape[1:])


def _mm(*, name="proj", xs, ws, kvecs=(), cvecs=(), rmats=(), outs, K, tm, tn, n_row_tiles, n_col_tiles,
        prologue, epilogue, aliases=None, row_splits=1):
    in_specs, args = [], []
    for a, r0 in xs:
        if a.ndim == 3:
            in_specs.append(pl.BlockSpec((tm, K // 128, 128), lambda j, i, r0=r0: (i + r0, 0, 0)))
        else:
            in_specs.append(pl.BlockSpec((tm, K), lambda j, i, r0=r0: (i + r0, 0)))
        args.append(a)
    for a, lead, c0 in ws:
        nl = len(lead)
        in_specs.append(pl.BlockSpec((None,) * nl + (K, tn), lambda j, i, lead=lead, c0=c0: lead + (0, j + c0)))
        args.append(a)
    for a in kvecs:
        in_specs.append(pl.BlockSpec(a.shape, lambda j, i: (0, 0)))
        args.append(a)
    for a, c0 in cvecs:
        in_specs.append(pl.BlockSpec((a.shape[0], tn), lambda j, i, c0=c0: (0, j + c0)))
        args.append(a)
    for a, r0, c0 in rmats:
        in_specs.append(pl.BlockSpec((tm, tn), lambda j, i, r0=r0, c0=c0: (i + r0, j + c0)))
        args.append(a)
    out_specs = [pl.BlockSpec((tm, tn // 128, 128), lambda j, i, r0=r0, c0=c0: (i + r0, j + c0, 0))
                 if len(s.shape) == 3 else
                 pl.BlockSpec((tm, tn), lambda j, i, r0=r0, c0=c0: (i + r0, j + c0)) for s, r0, c0 in outs]
    body = functools.partial(_mm_body, n_x=len(xs), n_w=len(ws), n_kv=len(kvecs), n_cv=len(cvecs),
                             n_rm=len(rmats), n_out=len(outs), prologue=prologue, epilogue=epilogue,
                             row_splits=row_splits)
    res = pl.pallas_call(
        body,
        grid=(n_col_tiles, n_row_tiles),
        in_specs=in_specs,
        out_specs=out_specs,
        out_shape=[s for s, _, _ in outs],
        scratch_shapes=[pltpu.VMEM((K, tn), bf16) for _ in ws],
        input_output_aliases=aliases or {},
        name=name,
        compiler_params=_cp(("arbitrary", "arbitrary")),
    )(*args)
    return res


def _pro_id(xs, kvs):
    return xs[0]


def _pro_mul(xs, kvs):
    return (xs[0] * xs[1]).astype(bf16)


def _sds(shape, dtype):
    return jax.ShapeDtypeStruct(shape, dtype)


def _mixer_out_kernel(*refs, n_u, prologue):
    xm_ref, xa_ref = refs[:2]
    u_refs = refs[2:2 + n_u]
    w_ref, h_ref, g_ref, ho_ref, xn_ref, w_s = refs[2 + n_u:]
    i = pl.program_id(0)

    @pl.when(i == 0)
    def _cast():
        w_s[...] = w_ref[...].astype(bf16)

    def run(x_ref):
        x = x_ref[...]
        if x.ndim == 3:
            x = x.reshape(x.shape[0], x.shape[1] * x.shape[2])
        hn = h_ref[...] + jnp.dot(prologue([x] + [u[...] for u in u_refs], []), w_s[...],
                                  preferred_element_type=f32)
        ho_ref[...] = hn
        ms = jnp.mean(hn * hn, axis=-1, keepdims=True)
        xn_ref[...] = (hn * lax.rsqrt(ms + EPS) * g_ref[...]).astype(xn_ref.dtype)

    @pl.when(i < ROWS_MAIN // AUX)
    def _main():
        run(xm_ref)

    @pl.when(i == ROWS_MAIN // AUX)
    def _aux():
        run(xa_ref)


def _mixer_out(h, x_main, x_aux, unified, w, lead, norm_g, prologue):
    nm = ROWS_MAIN // AUX
    def xspec(a, imap):
        if a.ndim == 3:
            return pl.BlockSpec((AUX,) + a.shape[1:], lambda i: (imap(i), 0, 0))
        return pl.BlockSpec((AUX, a.shape[1]), lambda i: (imap(i), 0))
    nl = len(lead)
    row = pl.BlockSpec((AUX, D), lambda i: (i, 0))
    in_specs = ([xspec(x_main, lambda i: jnp.minimum(i, nm - 1)), xspec(x_aux, lambda i: 0)]
                + [xspec(u, lambda i: i) for u in unified]
                + [pl.BlockSpec((None,) * nl + (D, D), lambda i: lead + (0, 0), pipeline_mode=pl.Buffered(1)),
                   row, pl.BlockSpec((1, D), lambda i: (0, 0))])
    return pl.pallas_call(
        functools.partial(_mixer_out_kernel, n_u=len(unified), prologue=prologue),
        grid=(nm + 1,),
        in_specs=in_specs,
        out_specs=[row, row],
        out_shape=[_sds((ROWS, D), f32), _sds((ROWS, D), bf16)],
        scratch_shapes=[pltpu.VMEM((D, D), bf16)],
        input_output_aliases={3 + len(unified): 0},
        name="mixer_out",
        compiler_params=_cp(("arbitrary",)),
    )(x_main, x_aux, *unified, w, h, norm_g)


def _ffn(h, xn, w_in, w_out, layer):
    tn = 512
    nct = D_FF // tn
    act = _mm(name="ffn_in", xs=[(xn, 0)], ws=[(w_in, (layer,), 0), (w_in, (layer,), nct)],
              outs=[(_sds((ROWS, D_FF), bf16), 0, 0)],
              K=D, tm=TM_BIG, tn=tn, n_row_tiles=ROWS // TM_BIG, n_col_tiles=nct,
              prologue=_pro_id,
              epilogue=lambda accs, cvs, rms: (jax.nn.silu(accs[0]) * accs[1],))[0]
    h = _mm(name="ffn_out", xs=[(act, 0)], ws=[(w_out, (layer,), 0)], rmats=[(h, 0, 0)],
            outs=[(_sds((ROWS, D), f32), 0, 0)],
            K=D_FF, tm=TM_MID, tn=tn, n_row_tiles=ROWS // TM_MID, n_col_tiles=D // tn,
            prologue=_pro_id, epilogue=lambda accs, cvs, rms: (rms[0] + accs[0],),
            aliases={2: 0})[0]
    return h


def _hgrn_lower_bound(logits, j):
    m = jnp.max(logits, axis=0, keepdims=True)
    e = jnp.exp(logits - m)
    p = e / jnp.sum(e, axis=0, keepdims=True)
    cs = p[0:1]
    for r in range(1, j + 1):
        cs = cs + p[r:r + 1]
    return jnp.clip(cs - p[0:1], 0.0, 1.0)


def _hgrn_in(xn, w_in, lb_logits, j):
    tn = 1024
    nct = D // tn
    common = dict(K=D, tm=TM_BIG, tn=tn, n_row_tiles=ROWS // TM_BIG, n_col_tiles=nct, prologue=_pro_id,
                  row_splits=2)
    o = (_sds((ROWS, D), f32), 0, 0)
    q = _mm(xs=[(xn, 0)], ws=[(w_in, (j,), 0)], outs=[o],
            epilogue=lambda accs, cvs, rms: (jax.nn.silu(accs[0]),), **common)[0]

    def f_epi(accs, cvs, rms):
        f = accs[0]
        lb = _hgrn_lower_bound(cvs[0], j)
        t = jnp.exp(-jnp.abs(f))
        log_f = (-(jnp.maximum(-f, 0.0) + jnp.log(1.0 + t))
                 + jnp.log(1.0 + lb * jnp.exp(jnp.minimum(-f, GATE_EXP_CLIP))))
        k = (1.0 - lb) * (jnp.where(f >= 0.0, t, 1.0) / (1.0 + t))
        return log_f, k

    log_f, k = _mm(xs=[(xn, 0)], ws=[(w_in, (j,), nct)], cvecs=[(lb_logits, 0)], outs=[o, o],
                   epilogue=f_epi, **{**common, "tm": TM_MID, "n_row_tiles": ROWS // TM_MID, "row_splits": 3})
    v = _mm(xs=[(xn, 0)], ws=[(w_in, (j,), 2 * nct)], outs=[o],
            epilogue=lambda accs, cvs, rms: (accs[0],), **common)[0]
    g = _mm(xs=[(xn, 0)], ws=[(w_in, (j,), 3 * nct)], outs=[o],
            epilogue=lambda accs, cvs, rms: (jax.nn.silu(accs[0]),), **common)[0]
    return q, k, v, log_f, g


def _split3(x):
    hi = x.astype(bf16)
    r = x - hi.astype(f32)
    mid = r.astype(bf16)
    lo = (r - mid.astype(f32)).astype(bf16)
    return hi, mid, lo


def _gla_consts(T):
    t = np.arange(T)[:, None]
    w = np.arange(T)[None, :]
    mats = [w <= t, w > t]
    masks = []
    s = T // 2
    while s >= 1:
        piece = t // (2 * s)
        lower = (t % (2 * s)) >= s
        ref = piece * 2 * s + s
        mats.append(np.where(lower, (w > ref) & (w <= t), (w > t) & (w <= ref)))
        upiece = w // (2 * s)
        ulower = (w % (2 * s)) >= s
        masks.append((piece == upiece) & lower & ~ulower)
        s //= 2
    masks.append(t == w)
    m = np.concatenate(mats, 0)
    eye = np.eye(HG_UNROLL, dtype=bool)
    gmasks = np.stack([np.kron(eye, pm) for pm in masks])
    return (jnp.asarray(np.concatenate([m, m, m], 1), dtype=bf16), jnp.asarray(gmasks, dtype=f32))


def _gla_block(q_ref, k_ref, v_ref, f_ref, g_ref, gn_ref, dm_ref, pm_ref, o_ref, st_ref, e_s, ql_s, kl_s, T):
    nlev = T.bit_length() - 1
    dn_t = (((1,), (1,)), ((), ()))
    dn_l = (((0,), (0,)), ((), ()))
    def factors(ls):
        e_s[0:(nlev + 2) * T, ls] = jnp.dot(dm_ref[...], jnp.concatenate(_split3(f_ref[:, ls]), axis=0),
                                            preferred_element_type=f32)
        q = q_ref[:, ls]
        k = k_ref[:, ls]
        ql_s[0, 0:T, ls] = (q * jnp.exp(e_s[0:T, ls])).astype(bf16)
        kl_s[0, 0:T, ls] = (k * jnp.exp(e_s[T:2 * T, ls])).astype(bf16)
        for l in range(nlev):
            ex = jnp.exp(e_s[(l + 2) * T:(l + 3) * T, ls])
            ql_s[l + 1, 0:T, ls] = (q * ex).astype(bf16)
            kl_s[l + 1, 0:T, ls] = (k * ex).astype(bf16)
        ql_s[nlev + 1, 0:T, ls] = q.astype(bf16)
        kl_s[nlev + 1, 0:T, ls] = k.astype(bf16)

    for h0 in range(0, HG_H, HG_UNROLL):
        factors(slice(h0 * HG_K, (h0 + HG_UNROLL) * HG_K))
        sls = [slice(h * HG_K, (h + 1) * HG_K) for h in range(h0, h0 + HG_UNROLL)]
        att = None
        for l in range(nlev + 1):
            a_l = lax.dot_general(jnp.concatenate([ql_s[l + 1, 0:T, sl] for sl in sls], axis=0),
                                  jnp.concatenate([kl_s[l + 1, 0:T, sl] for sl in sls], axis=0),
                                  dn_t, preferred_element_type=f32) * pm_ref[l]
            att = a_l if att is None else att + a_l
        vbs = [v_ref[:, sl].astype(bf16) for sl in sls]
        o_intra = jnp.dot(att.astype(bf16), jnp.concatenate(vbs, axis=0), preferred_element_type=f32)
        for p in range(0, HG_UNROLL, 2):
            ha, hb = h0 + p, h0 + p + 1
            sla, slb = sls[p], sls[p + 1]
            st2 = jnp.concatenate([st_ref[ha], st_ref[hb]], axis=0)
            oc = lax.dot_general(jnp.concatenate([ql_s[0, 0:T, sla], ql_s[0, 0:T, slb]], axis=0),
                                 st2.astype(bf16), dn_t, preferred_element_type=f32)
            up = lax.dot_general(jnp.concatenate([vbs[p], vbs[p + 1]], axis=1),
                                 jnp.concatenate([kl_s[0, 0:T, sla], kl_s[0, 0:T, slb]], axis=1),
                                 dn_l, preferred_element_type=f32)
            for i, (h, sl) in enumerate(((ha, sla), (hb, slb))):
                o = (oc[i * T:(i + 1) * T, i * HG_K:(i + 1) * HG_K]
                     + o_intra[(p + i) * T:(p + i + 1) * T])
                st_ref[h] = (st2[i * HG_K:(i + 1) * HG_K] * jnp.exp(e_s[T - 1:T, sl])
                             + up[i * HG_K:(i + 1) * HG_K, i * HG_K:(i + 1) * HG_K])
                ms = jnp.mean(o * o, axis=-1, keepdims=True)
                on = o * lax.rsqrt(ms + EPS) * gn_ref[:, sl]
                o_ref[:, sl] = (on * g_ref[:, sl]).astype(o_ref.dtype)


def _gla_kernel(qm, km, vm, fm, gm, qa, ka, va, fa, ga, gn_ref, dmm, pmm, dma, pma,
                om_ref, oa_ref, s_ref, st_ref, e_s, ql_s, kl_s):
    c = pl.program_id(1)

    @pl.when(c == 0)
    def _meta():
        st_ref[...] = jnp.zeros_like(st_ref)
        _gla_block(qa, ka, va, fa, ga, gn_ref, dma, pma, oa_ref, st_ref, e_s, ql_s, kl_s, NM)

    @pl.when(c > 0)
    def _main():
        _gla_block(qm, km, vm, fm, gm, gn_ref, dmm, pmm, om_ref, st_ref, e_s, ql_s, kl_s, HG_C)

    @pl.when(c == pl.num_programs(1) - 1)
    def _final():
        for h in range(HG_H):
            s_ref[0, h] = st_ref[h].T


def _hgrn_prompt(q, k, v, log_f, g, gn):
    nc = LP // HG_C
    main_spec = pl.BlockSpec((HG_C, D), lambda b, c: (b * nc + jnp.maximum(c - 1, 0), 0))
    meta_spec = pl.BlockSpec((NM, D), lambda b, c: (ROW_M // NM + b, 0))
    arrs = [q, k, v, log_f, g]
    consts = [*_gla_consts(HG_C), *_gla_consts(NM)]
    nslot = HG_C.bit_length() + 1
    return pl.pallas_call(
        _gla_kernel,
        grid=(NB, nc + 1),
        in_specs=[main_spec] * 5 + [meta_spec] * 5 + [pl.BlockSpec((1, D), lambda b, c: (0, 0))]
                 + [pl.BlockSpec(a.shape, lambda b, c, n=a.ndim: (0,) * n) for a in consts],
        out_specs=[pl.BlockSpec((HG_C, D), lambda b, c: (b * nc + jnp.maximum(c - 1, 0), 0)),
                   pl.BlockSpec((NM, D), lambda b, c: (b, 0)),
                   pl.BlockSpec((1, HG_H, HG_K, HG_K), lambda b, c: (b, 0, 0, 0))],
        out_shape=[_sds((ROWS_MAIN, D), bf16), _sds((NB * NM, D), bf16), _sds((NB, HG_H, HG_K, HG_K), f32)],
        scratch_shapes=[pltpu.VMEM((HG_H, HG_K, HG_K), f32),
                        pltpu.VMEM((nslot * HG_C, D), f32),
                        pltpu.VMEM((nslot, HG_C, D), bf16),
                        pltpu.VMEM((nslot, HG_C, D), bf16)],
        name="hgrn_prompt",
        compiler_params=_cp(("arbitrary", "arbitrary")),
    )(*arrs, *arrs, gn, *consts)


HG_BT = 8
HG_SB = 4


def _hgrn_dec_kernel(s_ref, qT, kT, fT, v_ref, g_ref, gn_ref, e3_ref, *rest, stack_slot):
    so_ref, o_ref, qb_s, kb_s, gb_s = rest[-5:]
    if stack_slot is not None:
        for m, e_ref in enumerate(rest[:-5]):
            so_ref[m] = e_ref[...]
    row0 = pl.program_id(1) * HG_SB

    def spread(m, dst):
        dst[...] = jnp.dot(jnp.concatenate(_split3(m), axis=1), e3_ref[...], preferred_element_type=f32)

    def per_b(bb, carry):
        spread(qT[bb], qb_s)
        spread(kT[bb], kb_s)
        spread(jnp.exp(fT[bb]), gb_s)
        row = pl.ds(row0 + bb, 1)
        vall = v_ref[row, :]
        outs = []
        for h in range(HG_H):
            hs = slice(h * HG_K, (h + 1) * HG_K)
            sn = gb_s[:, hs] * s_ref[bb, h] + kb_s[:, hs] * vall[:, hs]
            if stack_slot is None:
                so_ref[bb, h] = sn
            else:
                so_ref[stack_slot, bb, h] = sn
            o = jnp.sum(qb_s[:, hs] * sn, axis=0, keepdims=True)
            ms = jnp.mean(o * o, axis=-1, keepdims=True)
            outs.append(o * lax.rsqrt(ms + EPS))
        on = jnp.concatenate(outs, axis=1) * gn_ref[...]
        o_ref[row, :] = on * g_ref[row, :]
        return carry

    lax.fori_loop(0, HG_SB, per_b, 0)


def _hgrn_sample(state, j, q, k, v, log_f, g, gn, earlier):
    def colform(x):
        return x[ROW_S:ROW_S + NS].reshape(NS, HG_H, HG_K).transpose(0, 2, 1)

    n_a = state.shape[0]
    last = j == n_a - 1
    head_of_lane = np.arange(D) // HG_K
    sel = (np.arange(HG_H)[:, None] == head_of_lane[None, :])
    e3 = jnp.asarray(np.concatenate([sel, sel, sel], axis=0), dtype=bf16)
    nsb = HG_BT // HG_SB
    col_spec = pl.BlockSpec((HG_SB, HG_K, HG_H), lambda i, s: (i * nsb + s, 0, 0))
    row_spec = pl.BlockSpec((HG_BT, D), lambda i, s: (ROW_S // HG_BT + i, 0))
    new_spec = pl.BlockSpec((HG_SB, HG_H, HG_K, HG_K), lambda i, s: (i * nsb + s, 0, 0, 0))
    in_specs = [pl.BlockSpec((None, HG_SB, HG_H, HG_K, HG_K), lambda i, s: (j, i * nsb + s, 0, 0, 0)),
                col_spec, col_spec, col_spec, row_spec, row_spec,
                pl.BlockSpec((1, D), lambda i, s: (0, 0)), pl.BlockSpec((3 * HG_H, D), lambda i, s: (0, 0))]
    args = [state, colform(q), colform(k), colform(log_f), v, g, gn, e3]
    if last:
        in_specs += [new_spec] * len(earlier)
        args += list(earlier)
        st_out = pl.BlockSpec((n_a, HG_SB, HG_H, HG_K, HG_K), lambda i, s: (0, i * nsb + s, 0, 0, 0))
        st_shape = _sds(state.shape, f32)
    else:
        st_out, st_shape = new_spec, _sds(state.shape[1:], f32)
    return pl.pallas_call(
        functools.partial(_hgrn_dec_kernel, stack_slot=j if last else None),
        grid=(NS // HG_BT, nsb),
        in_specs=in_specs,
        out_specs=[st_out, pl.BlockSpec((HG_BT, D), lambda i, s: (i, 0))],
        out_shape=[st_shape, _sds((NS, D), f32)],
        scratch_shapes=[pltpu.VMEM((HG_K, D), f32)] * 3,
        name="hgrn_decode",
        compiler_params=_cp(("arbitrary", "arbitrary")),
    )(*args)


def _aux_rows(sample_rows, meta_rows):
    pad = jnp.zeros((AUX - NS - NB * NM, sample_rows.shape[1]), sample_rows.dtype)
    return jnp.concatenate([sample_rows, meta_rows.astype(sample_rows.dtype), pad], axis=0)


def _hgrn_layer(h, norm_g, ffn_g, state, earlier, j, w_in, lb_logits, gnorm, w_out, xn=None):
    if xn is None:
        xn = _rmsnorm(h, norm_g, out_dtype=bf16, tm=TM_MID, row_tile0=0, n_row_tiles=ROWS // TM_MID)
    q, k, v, log_f, g = _hgrn_in(xn, w_in, lb_logits, j)
    gn = gnorm[j].reshape(1, D)
    o_main, o_meta, s_p = _hgrn_prompt(q, k, v, log_f, g, gn)
    s_s, o_s = _hgrn_sample(state, j, q, k, v, log_f, g, gn, earlier)
    o_aux = _aux_rows(o_s.astype(bf16), o_meta)
    h, xn_f = _mixer_out(h, o_main, o_aux, [], w_out, (j,), ffn_g, _pro_id)
    return (h, xn_f), s_p, s_s


def _lru_in(xn, w_in, j):
    tn = 1024
    nct = D // tn
    common = dict(K=D, tm=TM_BIG, tn=tn, n_row_tiles=ROWS // TM_BIG, n_col_tiles=nct, prologue=_pro_id,
                  row_splits=2)
    o = (_sds((ROWS, D), f32), 0, 0)
    y = _mm(xs=[(xn, 0)], ws=[(w_in, (j,), 0)], outs=[o],
            epilogue=lambda accs, cvs, rms: (jax.nn.gelu(accs[0], approximate=True),), **common)[0]
    x = _mm(xs=[(xn, 0)], ws=[(w_in, (j,), nct)], outs=[o],
            epilogue=lambda accs, cvs, rms: (accs[0],), **common)[0]
    return y, x


def _lru_gates(xc, gw_ref, gb_ref, lam_ref, n):
    ls = slice(n * LRU_BW, (n + 1) * LRU_BW)
    xb = xc[:, ls].astype(bf16)
    r = jax.nn.sigmoid(jnp.dot(xb, gw_ref[0, n], preferred_element_type=f32) + gb_ref[0:1, ls])
    ig = jax.nn.sigmoid(jnp.dot(xb, gw_ref[1, n], preferred_element_type=f32) + gb_ref[1:2, ls])
    log_a = -LRU_C * r * _softplus(-lam_ref[:, ls])
    a = jnp.exp(log_a)
    mult = jnp.sqrt(jnp.maximum(1.0 - a * a, 0.0))
    return a, ig, mult


def _lru_conv(cb_ref, cw_ref, x0, x1, x2, x3):
    xc = cb_ref[...] + x0 * cw_ref[0:1, :]
    xc = xc + x1 * cw_ref[1:2, :]
    xc = xc + x2 * cw_ref[2:3, :]
    return xc + x3 * cw_ref[3:4, :]


def _lru_block(x_ref, y_ref, o_ref, cw_ref, cb_ref, gw_s, gb_ref, lam_ref, xbuf, a_s, b_s, hcar, T, first):
    xbuf[8:8 + T, :] = x_ref[...]
    xc = _lru_conv(cb_ref, cw_ref, xbuf[5:5 + T, :], xbuf[6:6 + T, :], xbuf[7:7 + T, :], xbuf[8:8 + T, :])
    xbuf[0:8, :] = xbuf[T:T + 8, :]
    row = lax.broadcasted_iota(jnp.int32, (T, 1), 0)
    for n in range(LRU_NBLK):
        ls = slice(n * LRU_BW, (n + 1) * LRU_BW)
        a, ig, mult = _lru_gates(xc, gw_s, gb_ref, lam_ref, n)
        if first:
            mult = jnp.where(row == 0, 1.0, mult)
        a_s[0:T, ls] = a
        b_s[0:T, ls] = xc[:, ls] * ig * mult

    def step(t, hprev):
        hnew = a_s[pl.ds(t, 1), :] * hprev + b_s[pl.ds(t, 1), :]
        b_s[pl.ds(t, 1), :] = hnew
        return hnew

    hcar[...] = lax.fori_loop(0, T, step, hcar[...], unroll=8)
    o_ref[...] = (b_s[0:T, :] * y_ref[...]).astype(o_ref.dtype)


def _lru_kernel(xm, ym, xa, ya, cw_ref, cb_ref, gw_ref, gb_ref, lam_ref,
                om_ref, oa_ref, hl_ref, cv_ref, gw_s, xbuf, a_s, b_s, hcar):
    c = pl.program_id(1)

    @pl.when((pl.program_id(0) == 0) & (c == 0))
    def _cast():
        gw_s[...] = gw_ref[...].astype(bf16)

    @pl.when(c == 0)
    def _meta():
        xbuf[0:8, :] = jnp.zeros((8, D), f32)
        hcar[...] = jnp.zeros_like(hcar)
        _lru_block(xa, ya, oa_ref, cw_ref, cb_ref, gw_s, gb_ref, lam_ref, xbuf, a_s, b_s, hcar, NM, True)

    @pl.when(c > 0)
    def _main():
        _lru_block(xm, ym, om_ref, cw_ref, cb_ref, gw_s, gb_ref, lam_ref, xbuf, a_s, b_s, hcar, LRU_T, False)

    @pl.when(c == pl.num_programs(1) - 1)
    def _final():
        hl_ref[0] = hcar[...]
        cv_ref[0] = xbuf[5:8, :]


def _lru_prompt(x, y, j, conv_w, conv_b, gate_w, gate_b, lam):
    nc = LP // LRU_T
    main_spec = pl.BlockSpec((LRU_T, D), lambda b, c: (b * nc + jnp.maximum(c - 1, 0), 0))
    meta_spec = pl.BlockSpec((NM, D), lambda b, c: (ROW_M // NM + b, 0))
    return pl.pallas_call(
        _lru_kernel,
        grid=(NB, nc + 1),
        in_specs=[main_spec, main_spec, meta_spec, meta_spec,
                  pl.BlockSpec((None, 4, D), lambda b, c: (j, 0, 0)),
                  pl.BlockSpec((1, D), lambda b, c: (j, 0)),
                  pl.BlockSpec((None, 2, LRU_NBLK, LRU_BW, LRU_BW), lambda b, c: (j, 0, 0, 0, 0)),
                  pl.BlockSpec((None, 2, D), lambda b, c: (j, 0, 0)),
                  pl.BlockSpec((1, D), lambda b, c: (j, 0))],
        out_specs=[pl.BlockSpec((LRU_T, D), lambda b, c: (b * nc + jnp.maximum(c - 1, 0), 0)),
                   pl.BlockSpec((NM, D), lambda b, c: (b, 0)),
                   pl.BlockSpec((1, 1, D), lambda b, c: (b, 0, 0)),
                   pl.BlockSpec((1, 3, D), lambda b, c: (b, 0, 0))],
        out_shape=[_sds((ROWS_MAIN, D), bf16), _sds((NB * NM, D), bf16),
                   _sds((NB, 1, D), f32), _sds((NB, 3, D), f32)],
        scratch_shapes=[pltpu.VMEM((2, LRU_NBLK, LRU_BW, LRU_BW), bf16),
                        pltpu.VMEM((LRU_T + 8, D), f32), pltpu.VMEM((LRU_T, D), f32),
                        pltpu.VMEM((LRU_T, D), f32), pltpu.VMEM((1, D), f32)],
        name="lru_prompt",
        compiler_params=_cp(("arbitrary", "arbitrary")),
    )(x, y, x, y, conv_w, conv_b, gate_w, gate_b, lam)


LRU_ST = 64


def _lru_dec_kernel(x_ref, y_ref, h0_ref, cbuf_ref, cw_ref, cb_ref, gw_ref, gb_ref, lam_ref,
                    o_ref, hn_ref, cn_ref, gw_s):
    gw_s[...] = gw_ref[...].astype(bf16)
    x = x_ref[...]
    b0 = cbuf_ref[:, 0, :]
    b1 = cbuf_ref[:, 1, :]
    b2 = cbuf_ref[:, 2, :]
    xc = _lru_conv(cb_ref, cw_ref, b0, b1, b2, x)
    cn_ref[:, 0, :] = b1
    cn_ref[:, 1, :] = b2
    cn_ref[:, 2, :] = x
    for n in range(LRU_NBLK):
        ls = slice(n * LRU_BW, (n + 1) * LRU_BW)
        a, ig, mult = _lru_gates(xc, gw_s, gb_ref, lam_ref, n)
        hn = a * h0_ref[:, ls] + xc[:, ls] * ig * mult
        hn_ref[:, ls] = hn
        o_ref[:, ls] = (hn * y_ref[:, ls]).astype(o_ref.dtype)


def _lru_sample(x, y, j, h0, cbuf, conv_w, conv_b, gate_w, gate_b, lam):
    row_spec = pl.BlockSpec((LRU_ST, D), lambda i: (ROW_S // LRU_ST + i, 0))
    return pl.pallas_call(
        _lru_dec_kernel,
        grid=(NS // LRU_ST,),
        in_specs=[row_spec, row_spec,
                  pl.BlockSpec((None, LRU_ST, D), lambda i: (j, i, 0)),
                  pl.BlockSpec((None, LRU_ST, 3, D), lambda i: (j, i, 0, 0)),
                  pl.BlockSpec((None, 4, D), lambda i: (j, 0, 0)),
                  pl.BlockSpec((1, D), lambda i: (j, 0)),
                  pl.BlockSpec((None, 2, LRU_NBLK, LRU_BW, LRU_BW), lambda i: (j, 0, 0, 0, 0)),
                  pl.BlockSpec((None, 2, D), lambda i: (j, 0, 0)),
                  pl.BlockSpec((1, D), lambda i: (j, 0))],
        out_specs=[pl.BlockSpec((LRU_ST, D), lambda i: (i, 0)),
                   pl.BlockSpec((LRU_ST, D), lambda i: (i, 0)),
                   pl.BlockSpec((LRU_ST, 3, D), lambda i: (i, 0, 0))],
        out_shape=[_sds((NS, D), bf16), _sds((NS, D), f32), _sds((NS, 3, D), f32)],
        scratch_shapes=[pltpu.VMEM((2, LRU_NBLK, LRU_BW, LRU_BW), bf16)],
        name="lru_decode",
        compiler_params=_cp(("arbitrary",)),
    )(x, y, h0, cbuf, conv_w, conv_b, gate_w, gate_b, lam)


def _lru_layer(h, norm_g, ffn_g, state_h, state_conv, j, w_in, conv_w, conv_b, gate_w, gate_b, lam, w_out):
    xn = _rmsnorm(h, norm_g, out_dtype=bf16, tm=TM_MID, row_tile0=0, n_row_tiles=ROWS // TM_MID)
    y, x = _lru_in(xn, w_in, j)
    o_main, o_meta, hl_p, cv_p = _lru_prompt(x, y, j, conv_w, conv_b, gate_w, gate_b, lam)
    o_s, hl_s, cv_s = _lru_sample(x, y, j, state_h, state_conv, conv_w, conv_b, gate_w, gate_b, lam)
    o_aux = _aux_rows(o_s, o_meta)
    h, xn_f = _mixer_out(h, o_main, o_aux, [], w_out, (j,), ffn_g, _pro_id)
    return (h, xn_f), hl_p.reshape(NB, D), hl_s, cv_p, cv_s


def _rwkv_premix_kernel(h_ref, g_ref, mu_ref, sh_ref, *refs):
    x_refs = refs[:6]
    sp_ref, ss_ref, xbuf, meta_last = refs[6:]
    i = pl.program_id(0)
    x = h_ref[...]
    ms = jnp.mean(x * x, axis=-1, keepdims=True)
    xn = x * lax.rsqrt(ms + EPS) * g_ref[...]
    xbuf[8:8 + PM_T, :] = xn
    row = lax.broadcasted_iota(jnp.int32, (PM_T, 1), 0)

    @pl.when(i == 0)
    def _aux():
        xbuf[7:8, :] = jnp.zeros((1, D), f32)
        ss_ref[...] = xn[0:NS]
        for b in range(NB):
            meta_last[b:b + 1, :] = xn[NS + b * NM + NM - 1:NS + b * NM + NM]

    @pl.when(i > 0)
    def _main():
        m = i - 1
        b = m // (LP // PM_T)

        @pl.when(m % (LP // PM_T) == 0)
        def _start():
            xbuf[7:8, :] = meta_last[pl.ds(b, 1), :]

        @pl.when(m % (LP // PM_T) == LP // PM_T - 1)
        def _end():
            sp_ref[pl.ds(b, 1), :] = xn[PM_T - 1:PM_T]

    shifted = xbuf[7:7 + PM_T, :]
    is_meta = (row >= NS) & (row < NS + NB * NM) & ((row - NS) % NM != 0)
    sh_pad = jnp.concatenate([sh_ref[...], jnp.zeros((PM_T - NS, D), f32)], axis=0)
    prev_aux = jnp.where(row < NS, sh_pad, jnp.where(is_meta, shifted, 0.0))
    prev = jnp.where(i == 0, prev_aux, shifted)
    dx = prev - xn
    for n in range(6):
        x_refs[n][...] = (xn + dx * mu_ref[n:n + 1, :]).astype(bf16)
    xbuf[7:8, :] = xn[PM_T - 1:PM_T]


def _rwkv_premix(h, norm_g, mu_j, shift_j):
    nt = ROWS // PM_T
    rows = lambda i: (jnp.where(i == 0, nt - 1, i - 1), 0)
    return pl.pallas_call(
        _rwkv_premix_kernel,
        grid=(nt,),
        in_specs=[pl.BlockSpec((PM_T, D), rows),
                  pl.BlockSpec((1, D), lambda i: (0, 0)),
                  pl.BlockSpec((6, D), lambda i: (0, 0)),
                  pl.BlockSpec((NS, D), lambda i: (0, 0))],
        out_specs=[pl.BlockSpec((PM_T, D), rows)] * 6 + [pl.BlockSpec((NB, D), lambda i: (0, 0)),
                                                         pl.BlockSpec((NS, D), lambda i: (0, 0))],
        out_shape=[_sds((ROWS, D), bf16)] * 6 + [_sds((NB, D), f32), _sds((NS, D), f32)],
        scratch_shapes=[pltpu.VMEM((PM_T + 8, D), f32), pltpu.VMEM((8, D), f32)],
        name="rwkv_premix",
        compiler_params=_cp(("arbitrary",)),
    )(h, norm_g, mu_j, shift_j)


def _rwkv_lora_kernel(xw_ref, xa_ref, xg_ref, w1_ref, w2_ref, a1_ref, a2_ref, g1_ref, g2_ref, w0_ref, a0_ref,
                      d_ref, a_ref, g_ref, w1_s, w2_s, a1_s, a2_s, g1_s, g2_s):
    @pl.when(pl.program_id(0) == 0)
    def _cast():
        for src, dst in ((w1_ref, w1_s), (w2_ref, w2_s), (a1_ref, a1_s), (a2_ref, a2_s), (g1_ref, g1_s),
                         (g2_ref, g2_s)):
            dst[...] = src[...].astype(bf16)

    def mm(u, w_s):
        return jnp.dot(u, w_s[...], preferred_element_type=f32)

    tw = jnp.tanh(mm(xw_ref[...], w1_s)).astype(bf16)
    x = w0_ref[...] + mm(tw, w2_s)
    d_ref[...] = jnp.exp(-np.float32(np.exp(-0.5)) * jax.nn.sigmoid(x)).reshape(d_ref.shape)
    ta = mm(xa_ref[...], a1_s).astype(bf16)
    a_ref[...] = jax.nn.sigmoid(a0_ref[...] + mm(ta, a2_s)).reshape(a_ref.shape)
    tg = jax.nn.sigmoid(mm(xg_ref[...], g1_s)).astype(bf16)
    g_ref[...] = mm(tg, g2_s)


def _rwkv_lora(xw, xa, xg, w1, w2, a1, a2, g1, g2, w0, a0):
    tm = 384
    full = lambda a: pl.BlockSpec(a.shape, lambda i: (0,) * a.ndim)
    row = pl.BlockSpec((tm, D), lambda i: (i, 0))
    row3 = pl.BlockSpec((tm, RW_Q, 128), lambda i: (i, 0, 0))
    ws = [w1, w2, a1, a2, g1, g2]
    return pl.pallas_call(
        _rwkv_lora_kernel,
        grid=(ROWS // tm,),
        in_specs=[row, row, row] + [full(a) for a in ws + [w0, a0]],
        out_specs=[row3, row3, row],
        out_shape=[_sds((ROWS, RW_Q, 128), f32)] * 2 + [_sds((ROWS, D), f32)],
        scratch_shapes=[pltpu.VMEM(a.shape, bf16) for a in ws],
        name="rwkv_lora",
        compiler_params=_cp(("arbitrary",)),
    )(xw, xa, xg, *ws, w0, a0)


def _rwkv_prep(t, r_s, k_s, d_s, a_s, tmp, kk_p, ka_p):
    kk_t, d_t, ka_t, k2_t, r_t = tmp
    k = k_s[t]
    a = a_s[t]
    kkr = k * kk_p[...]
    nrm = jnp.sqrt(jnp.sum(kkr * kkr, axis=0, keepdims=True))
    kk = kkr / jnp.maximum(nrm, 1e-12)
    kk_t[...] = kk
    d_t[...] = d_s[t]
    ka_t[...] = kk * a
    k2_t[...] = k * (1.0 + (a - 1.0) * ka_p[...])
    r_t[...] = r_s[t]


def _rwkv_body(v, s_s, tmp):
    kk_t, d_t, ka_t, k2_t, r_t = tmp
    sa = -(s_s[0] * kk_t[0:1, :])
    for j in range(1, RW_N):
        sa = sa - s_s[j] * kk_t[j:j + 1, :]
    y = None
    for j in range(RW_N):
        sn = s_s[j] * d_t[j:j + 1, :] + sa * ka_t[j:j + 1, :] + v * k2_t[j:j + 1, :]
        s_s[j] = sn
        yj = sn * r_t[j:j + 1, :]
        y = yj if y is None else y + yj
    return y


def _rwkv_tail(y, v, tmp, rk_p, lw_p, lb_p):
    _, _, _, k2_t, r_t = tmp
    mean = jnp.mean(y, axis=0, keepdims=True)
    yc = y - mean
    var = jnp.mean(yc * yc, axis=0, keepdims=True)
    yn = yc * lax.rsqrt(var + RW_LN_EPS) * lw_p[...] + lb_p[...]
    bonus = jnp.sum(r_t[...] * k2_t[...] * rk_p[...], axis=0, keepdims=True) * v
    return yn + bonus


def _rwkv_step(t, r_s, k_s, v_s, d_s, a_s, z_s, s_s, tmp, kk_p, ka_p, rk_p, lw_p, lb_p):
    _rwkv_prep(t, r_s, k_s, d_s, a_s, tmp, kk_p, ka_p)
    v = v_s[t]
    z_s[t] = _rwkv_tail(_rwkv_body(v, s_s, tmp), v, tmp, rk_p, lw_p, lb_p)


def _rwkv_dec_kernel(s_ref, r_ref, k_ref, v_ref, d_ref, a_ref, kk_p, ka_p, rk_p, lw_p, lb_p, so_ref, z_ref,
                     v_s, y_s):
    r = r_ref[0]
    k = k_ref[0]
    a = a_ref[0]
    d = d_ref[0]
    kkr = k * kk_p[0]
    nrm = jnp.sqrt(jnp.sum(kkr * kkr, axis=0, keepdims=True))
    kk = kkr / jnp.maximum(nrm, 1e-12)
    k2 = k * (1.0 + (a - 1.0) * ka_p[0])
    ka = kk * a
    v_s[...] = v_ref[0]
    for i in range(RW_N):
        s = s_ref[0, i]
        sa = -jnp.sum(s * kk, axis=0, keepdims=True)
        sn = s * d + sa * ka + v_s[i:i + 1, :] * k2
        so_ref[0, i] = sn
        y_s[i:i + 1, :] = jnp.sum(sn * r, axis=0, keepdims=True)
    y = y_s[...]
    mean = jnp.mean(y, axis=0, keepdims=True)
    yc = y - mean
    var = jnp.mean(yc * yc, axis=0, keepdims=True)
    yn = yc * lax.rsqrt(var + RW_LN_EPS) * lw_p[0] + lb_p[0]
    bonus = jnp.sum(r * k2 * rk_p[0], axis=0, keepdims=True) * v_s[...]
    z_ref[0] = yn + bonus


def _rwkv_sample(state, j, r, k, v, dec, a, head_params):
    def lanes(x):
        return x[ROW_S:ROW_S + NS].reshape(NS, D).T.reshape(RW_H, RW_N, NS)

    def param(p):
        return jnp.broadcast_to(p.reshape(RW_H, RW_N, 1), (RW_H, RW_N, NS))

    vec = pl.BlockSpec((1, RW_N, NS), lambda h: (h, 0, 0))
    st = pl.BlockSpec((1, RW_N, RW_N, NS), lambda h: (h, 0, 0, 0))
    s_new, z = pl.pallas_call(
        _rwkv_dec_kernel,
        grid=(RW_H,),
        in_specs=[st] + [vec] * 10,
        out_specs=[st, vec],
        out_shape=[_sds((RW_H, RW_N, RW_N, NS), f32), _sds((RW_H, RW_N, NS), f32)],
        scratch_shapes=[pltpu.VMEM((RW_N, NS), f32)] * 2,
        name="rwkv_decode",
        compiler_params=_cp(("arbitrary",)),
    )(state[j].transpose(1, 2, 3, 0), lanes(r), lanes(k), lanes(v), lanes(dec), lanes(a),
      *[param(p) for p in head_params])
    return s_new.transpose(3, 0, 1, 2), z.reshape(D, NS).T


RW_Q = D // 128


def _half_transpose(x):
    xt = x.T
    return jnp.concatenate([xt[0:RW_N], xt[RW_N:2 * RW_N]], axis=1)


def _rwkv_prompt_kernel(*refs):
    n_in = 5 * NB
    in_refs = refs[:n_in]
    params = refs[n_in:n_in + 5]
    zmain, zmeta, sf_ref = refs[n_in + 5:n_in + 8]
    scr = refs[n_in + 8:]
    seq, z_s, zo_s, s_s, tmp = scr[:5], scr[5], scr[6], scr[7], scr[8:]
    c = pl.program_id(0)

    @pl.when(c == 0)
    def _init():
        s_s[...] = jnp.zeros_like(s_s)
        z_s[...] = jnp.zeros_like(z_s)

    def load_token(t):
        rows = pl.ds(pl.multiple_of(t * RW_Q, RW_Q), RW_Q)
        for a in range(5):
            tile = jnp.concatenate([in_refs[a * NB + b][rows, :] for b in range(NB)], axis=0)
            seq[a][t] = _half_transpose(tile)

    load_token(0)

    def step(t, carry):
        tp = jnp.maximum(t - 1, 0)
        zo_s[tp] = _half_transpose(z_s[tp])
        _rwkv_step(t, *seq, z_s, s_s, tmp, *params)
        load_token(jnp.minimum(t + 1, RW_TS - 1))
        return carry

    lax.fori_loop(0, RW_TS, step, 0)
    zo_s[RW_TS - 1] = _half_transpose(z_s[RW_TS - 1])

    def write(z_ref):
        for t in range(RW_TS):
            for b in range(NB):
                z_ref[b, t * RW_Q:(t + 1) * RW_Q, :] = zo_s[t, b * RW_Q:(b + 1) * RW_Q, :]

    @pl.when(c == 0)
    def _write_meta():
        write(zmeta)

    @pl.when(c > 0)
    def _write_main():
        write(zmain)

    @pl.when(c == pl.num_programs(0) - 1)
    def _final():
        sf_ref[...] = s_s[...]


def _rwkv_prompt(r, k, v, dec, a, params):
    nc = LP // RW_TS
    blk = RW_TS * RW_Q
    arrs = [x.reshape(ROWS * RW_Q, 128) for x in (r, k, v, dec, a)]
    in_specs, args = [], []
    for x in arrs:
        for b in range(NB):
            in_specs.append(pl.BlockSpec(
                (blk, 128), lambda c, b=b: (jnp.where(c == 0, ROW_M // RW_TS + b, b * nc + c - 1), 0)))
            args.append(x)
    par = pl.BlockSpec((RW_N, 128), lambda c: (0, 0))
    outs = pl.pallas_call(
        _rwkv_prompt_kernel,
        grid=(nc + 1,),
        in_specs=in_specs + [par] * 5,
        out_specs=[pl.BlockSpec((NB, blk, 128), lambda c: (0, jnp.maximum(c - 1, 0), 0)),
                   pl.BlockSpec((NB, blk, 128), lambda c: (0, 0, 0)),
                   pl.BlockSpec((RW_N, RW_N, 128), lambda c: (0, 0, 0))],
        out_shape=[_sds((NB, LP * RW_Q, 128), f32), _sds((NB, NM * RW_Q, 128), f32),
                   _sds((RW_N, RW_N, 128), f32)],
        scratch_shapes=[pltpu.VMEM((RW_TS, RW_N, 128), f32)] * 7 + [pltpu.VMEM((RW_N, RW_N, 128), f32)]
                       + [pltpu.VMEM((RW_N, 128), f32)] * 5,
        name="rwkv_prompt",
        compiler_params=_cp(("arbitrary",)),
    )(*args, *params)
    return outs[0].reshape(ROWS_MAIN, RW_Q, 128), outs[1].reshape(NB * NM, D), outs[2]


def _rwkv_layer(h, norm_g, ffn_g, state, shift, j, mu, w_rkv, w0, w1, w2, a0, a1, a2, g1, g2, k_k, k_a, r_k,
                ln_w, ln_b, w_o):
    *xmix, shift_p, shift_s = _rwkv_premix(h, norm_g, mu[j], shift[j])
    tn = 1024
    common = dict(K=D, tm=TM_BIG, tn=tn, n_row_tiles=ROWS // TM_BIG, n_col_tiles=D // tn, prologue=_pro_id,
                  epilogue=lambda accs, cvs, rms: (accs[0],))
    o = (_sds((ROWS, RW_Q, 128), f32), 0, 0)
    r, k, v = [_mm(name="rwkv_rkv", xs=[(xmix[n], 0)], ws=[(w_rkv, (j, n), 0)], outs=[o], **common)[0]
               for n in range(3)]

    def padc(w):
        return jnp.pad(w, ((0, 0), (0, LORA_PAD - w.shape[1])))

    def padr(w):
        return jnp.pad(w, ((0, LORA_PAD - w.shape[0]), (0, 0)))

    dec, a, g = _rwkv_lora(xmix[3], xmix[4], xmix[5], padc(w1[j]), padr(w2[j]), padc(a1[j]), padr(a2[j]),
                           g1[j], g2[j], w0[j].reshape(1, D), a0[j].reshape(1, D))

    def prompt_param(p):
        pt = p.reshape(RW_Q, 2, RW_N).transpose(2, 1, 0)
        return jnp.broadcast_to(pt[:, :, None, :], (RW_N, 2, NB, RW_Q)).reshape(RW_N, 128)

    head_params = (k_k[j], k_a[j], r_k[j].reshape(D), ln_w[j], ln_b[j])
    z_main, z_meta, sT = _rwkv_prompt(r, k, v, dec, a, [prompt_param(p) for p in head_params])
    s_p = sT.reshape(RW_N, RW_N, 2, NB, RW_Q).transpose(3, 4, 2, 1, 0).reshape(NB, RW_H, RW_N, RW_N)
    s_s, z_s = _rwkv_sample(state, j, r, k, v, dec, a, head_params)

    z_aux = _aux_rows(z_s, z_meta)
    h, xn_f = _mixer_out(h, z_main, z_aux, [g], w_o, (j,), ffn_g, _pro_mul)
    return (h, xn_f), s_p, s_s, shift_p, shift_s


def kernel(x_prompt, x_sample, state_hgrn, state_lru_h, state_lru_conv, state_rwkv, state_rwkv_shift, meta_tokens, norm_mix, norm_ffn, norm_final, hgrn_w_in, hgrn_lb_logits, hgrn_norm, hgrn_w_out, lru_w_in, lru_conv_w, lru_conv_b, lru_gate_w, lru_gate_b, lru_lambda, lru_w_out, rwkv_mu, rwkv_w_rkv, rwkv_w0, rwkv_w1, rwkv_w2, rwkv_a0, rwkv_a1, rwkv_a2, rwkv_g1, rwkv_g2, rwkv_k_k, rwkv_k_a, rwkv_r_k, rwkv_ln_w, rwkv_ln_b, rwkv_w_o, ffn_w_in, ffn_w_out):
    depth = norm_mix.shape[0]
    x_aux = _aux_rows(x_sample.reshape(NS, D), jnp.tile(meta_tokens, (NB, 1)))
    h, xn0 = _assemble(x_prompt.reshape(ROWS_MAIN, D), x_aux, norm_mix[0].reshape(1, D))
    hg_p, lh_p, lh_s, lc_p, lc_s, rw_p, rw_s, rs_p, rs_s = [[] for _ in range(9)]
    hg_s = []
    for i in range(depth):
        m, j = i % 3, i // 3
        ng = norm_mix[i].reshape(1, D)
        fg = norm_ffn[i].reshape(1, D)
        if m == 0:
            hx, sp, ss = _hgrn_layer(h, ng, fg, state_hgrn, hg_s, j, hgrn_w_in, hgrn_lb_logits, hgrn_norm,
                                     hgrn_w_out, xn=xn0 if i == 0 else None)
            hg_p.append(sp)
            hg_s.append(ss)
        elif m == 1:
            hx, hp_, hs_, cp_, cs_ = _lru_layer(h, ng, fg, state_lru_h, state_lru_conv, j, lru_w_in,
                                                lru_conv_w, lru_conv_b, lru_gate_w, lru_gate_b, lru_lambda,
                                                lru_w_out)
            lh_p.append(hp_)
            lh_s.append(hs_)
            lc_p.append(cp_)
            lc_s.append(cs_)
        else:
            hx, sp, ss, shp, shs = _rwkv_layer(h, ng, fg, state_rwkv, state_rwkv_shift, j, rwkv_mu, rwkv_w_rkv,
                                               rwkv_w0, rwkv_w1, rwkv_w2, rwkv_a0, rwkv_a1, rwkv_a2, rwkv_g1,
                                               rwkv_g2, rwkv_k_k, rwkv_k_a, rwkv_r_k, rwkv_ln_w, rwkv_ln_b,
                                               rwkv_w_o)
            rw_p.append(sp)
            rw_s.append(ss)
            rs_p.append(shp)
            rs_s.append(shs)
        h = _ffn(*hx, ffn_w_in, ffn_w_out, i)
    nf = norm_final.reshape(1, D)
    y_main = _rmsnorm(h, nf, out_dtype=f32, tm=512, row_tile0=0, n_row_tiles=ROWS_MAIN // 512)
    y_aux = _rmsnorm(h, nf, out_dtype=f32, tm=AUX, row_tile0=ROWS_MAIN // AUX, n_row_tiles=1)
    y_prompt = y_main.reshape(NB, LP, D)
    y_sample = y_aux[:NS].reshape(NS, 1, D)
    def stack(xs):
        return xs[0][None] if len(xs) == 1 else jnp.stack(xs)

    return (y_prompt, y_sample, stack(hg_p), hg_s[-1], stack(lh_p), stack(lh_s),
            stack(lc_p), stack(lc_s), stack(rw_p), stack(rw_s), stack(rs_p), stack(rs_s))
```

```python
import functools

import jax
import jax.numpy as jnp
import numpy as np
from jax import lax
from jax.experimental import pallas as pl
from jax.experimental.pallas import tpu as pltpu

f32 = jnp.float32
bf16 = jnp.bfloat16

D = 2048
NB = 4
LP = 2048
NM = 16
NS = 128
ROWS_MAIN = NB * LP
ROW_S = ROWS_MAIN
ROW_M = ROW_S + NS
AUX = 256
ROWS = ROWS_MAIN + AUX
EPS = 1e-6
HG_H, HG_K = 16, 128
HG_C = 64
HG_UNROLL = 4
GATE_EXP_CLIP = 60.0
LRU_NBLK, LRU_BW = 8, 256
LRU_C = 8.0
LRU_T = 512
RW_H, RW_N = 32, 64
RW_LN_EPS = 64e-5
RW_TS = 16
PM_T = 256
LORA_PAD = 128
D_FF = 5632
TM_BIG = 1408
TM_MID = 768
VMEM_LIMIT = 56 * 1024 * 1024


def _cp(sem):
    return pltpu.CompilerParams(dimension_semantics=sem, vmem_limit_bytes=VMEM_LIMIT)


def _softplus(x):
    return jnp.maximum(x, 0.0) + jnp.log(1.0 + jnp.exp(-jnp.abs(x)))


def _norm_kernel(h_ref, g_ref, o_ref):
    x = h_ref[...]
    ms = jnp.mean(x * x, axis=-1, keepdims=True)
    o_ref[...] = (x * lax.rsqrt(ms + EPS) * g_ref[...]).astype(o_ref.dtype)


def _rmsnorm(h, g, *, out_dtype, tm, row_tile0, n_row_tiles):
    return pl.pallas_call(
        _norm_kernel,
        grid=(n_row_tiles,),
        in_specs=[pl.BlockSpec((tm, D), lambda i: (i + row_tile0, 0)),
                  pl.BlockSpec((1, D), lambda i: (0, 0))],
        out_specs=pl.BlockSpec((tm, D), lambda i: (i, 0)),
        out_shape=jax.ShapeDtypeStruct((n_row_tiles * tm, D), out_dtype),
        name="rmsnorm",
        compiler_params=_cp(("arbitrary",)),
    )(h, g)


def _assemble_kernel(xm_ref, xa_ref, g_ref, h_ref, xn_ref):
    def emit(x):
        h_ref[...] = x
        ms = jnp.mean(x * x, axis=-1, keepdims=True)
        xn_ref[...] = (x * lax.rsqrt(ms + EPS) * g_ref[...]).astype(xn_ref.dtype)

    @pl.when(pl.program_id(0) < ROWS_MAIN // AUX)
    def _main():
        emit(xm_ref[...])

    @pl.when(pl.program_id(0) == ROWS_MAIN // AUX)
    def _aux():
        emit(xa_ref[...])


def _assemble(x_main, x_aux, g):
    nm = ROWS_MAIN // AUX
    return pl.pallas_call(
        _assemble_kernel,
        grid=(nm + 1,),
        in_specs=[pl.BlockSpec((AUX, D), lambda i: (jnp.minimum(i, nm - 1), 0)),
                  pl.BlockSpec((AUX, D), lambda i: (0, 0)),
                  pl.BlockSpec((1, D), lambda i: (0, 0))],
        out_specs=[pl.BlockSpec((AUX, D), lambda i: (i, 0)), pl.BlockSpec((AUX, D), lambda i: (i, 0))],
        out_shape=[_sds((ROWS, D), f32), _sds((ROWS, D), bf16)],
        name="assemble",
        compiler_params=_cp(("arbitrary",)),
    )(x_main, x_aux, g)


def _mm_body(*refs, n_x, n_w, n_kv, n_cv, n_rm, n_out, prologue, epilogue, row_splits):
    p = 0
    x_refs = refs[p:p + n_x]; p += n_x
    w_refs = refs[p:p + n_w]; p += n_w
    kv_refs = refs[p:p + n_kv]; p += n_kv
    cv_refs = refs[p:p + n_cv]; p += n_cv
    rm_refs = refs[p:p + n_rm]; p += n_rm
    out_refs = refs[p:p + n_out]; p += n_out
    w_scr = refs[p:p + n_w]

    @pl.when(pl.program_id(1) == 0)
    def _cast_weights():
        for w, s in zip(w_refs, w_scr):
            s[...] = w[...].astype(bf16)

    def flat(v):
        return v.reshape(v.shape[0], v.shape[1] * v.shape[2]) if v.ndim == 3 else v

    tm = out_refs[0].shape[0]
    sub = tm // row_splits
    for s0 in range(0, tm, sub):
        rows = slice(s0, s0 + sub)
        x = prologue([flat(r[rows]) for r in x_refs], [r[...] for r in kv_refs])
        accs = [jnp.dot(x, s[...], preferred_element_type=f32) for s in w_scr]
        res = epilogue(accs, [r[...] for r in cv_refs], [r[rows] for r in rm_refs])
        for o, r in zip(out_refs, res):
            o[rows] = r.astype(o.dtype).reshape((sub,) + o.shape[1:])


def _mm(*, name="proj", xs, ws, kvecs=(), cvecs=(), rmats=(), outs, K, tm, tn, n_row_tiles, n_col_tiles,
        prologue, epilogue, aliases=None, row_splits=1):
    in_specs, args = [], []
    for a, r0 in xs:
        if a.ndim == 3:
            in_specs.append(pl.BlockSpec((tm, K // 128, 128), lambda j, i, r0=r0: (i + r0, 0, 0)))
        else:
            in_specs.append(pl.BlockSpec((tm, K), lambda j, i, r0=r0: (i + r0, 0)))
        args.append(a)
    for a, lead, c0 in ws:
        nl = len(lead)
        in_specs.append(pl.BlockSpec((None,) * nl + (K, tn), lambda j, i, lead=lead, c0=c0: lead + (0, j + c0)))
        args.append(a)
    for a in kvecs:
        in_specs.append(pl.BlockSpec(a.shape, lambda j, i: (0, 0)))
        args.append(a)
    for a, c0 in cvecs:
        in_specs.append(pl.BlockSpec((a.shape[0], tn), lambda j, i, c0=c0: (0, j + c0)))
        args.append(a)
    for a, r0, c0 in rmats:
        in_specs.append(pl.BlockSpec((tm, tn), lambda j, i, r0=r0, c0=c0: (i + r0, j + c0)))
        args.append(a)
    out_specs = [pl.BlockSpec((tm, tn // 128, 128), lambda j, i, r0=r0, c0=c0: (i + r0, j + c0, 0))
                 if len(s.shape) == 3 else
                 pl.BlockSpec((tm, tn), lambda j, i, r0=r0, c0=c0: (i + r0, j + c0)) for s, r0, c0 in outs]
    body = functools.partial(_mm_body, n_x=len(xs), n_w=len(ws), n_kv=len(kvecs), n_cv=len(cvecs),
                             n_rm=len(rmats), n_out=len(outs), prologue=prologue, epilogue=epilogue,
                             row_splits=row_splits)
    res = pl.pallas_call(
        body,
        grid=(n_col_tiles, n_row_tiles),
        in_specs=in_specs,
        out_specs=out_specs,
        out_shape=[s for s, _, _ in outs],
        scratch_shapes=[pltpu.VMEM((K, tn), bf16) for _ in ws],
        input_output_aliases=aliases or {},
        name=name,
        compiler_params=_cp(("arbitrary", "arbitrary")),
    )(*args)
    return res


def _pro_id(xs, kvs):
    return xs[0]


def _pro_mul(xs, kvs):
    return (xs[0] * xs[1]).astype(bf16)


def _sds(shape, dtype):
    return jax.ShapeDtypeStruct(shape, dtype)


def _mixer_out_kernel(*refs, n_u, prologue):
    xm_ref, xa_ref = refs[:2]
    u_refs = refs[2:2 + n_u]
    w_ref, h_ref, g_ref, ho_ref, xn_ref, w_s = refs[2 + n_u:]
    i = pl.program_id(0)

    @pl.when(i == 0)
    def _cast():
        w_s[...] = w_ref[...].astype(bf16)

    def run(x_ref):
        x = x_ref[...]
        if x.ndim == 3:
            x = x.reshape(x.shape[0], x.shape[1] * x.shape[2])
        hn = h_ref[...] + jnp.dot(prologue([x] + [u[...] for u in u_refs], []), w_s[...],
                                  preferred_element_type=f32)
        ho_ref[...] = hn
        ms = jnp.mean(hn * hn, axis=-1, keepdims=True)
        xn_ref[...] = (hn * lax.rsqrt(ms + EPS) * g_ref[...]).astype(xn_ref.dtype)

    @pl.when(i < ROWS_MAIN // AUX)
    def _main():
        run(xm_ref)

    @pl.when(i == ROWS_MAIN // AUX)
    def _aux():
        run(xa_ref)


def _mixer_out(h, x_main, x_aux, unified, w, lead, norm_g, prologue):
    nm = ROWS_MAIN // AUX
    def xspec(a, imap):
        if a.ndim == 3:
            return pl.BlockSpec((AUX,) + a.shape[1:], lambda i: (imap(i), 0, 0))
        return pl.BlockSpec((AUX, a.shape[1]), lambda i: (imap(i), 0))
    nl = len(lead)
    row = pl.BlockSpec((AUX, D), lambda i: (i, 0))
    in_specs = ([xspec(x_main, lambda i: jnp.minimum(i, nm - 1)), xspec(x_aux, lambda i: 0)]
                + [xspec(u, lambda i: i) for u in unified]
                + [pl.BlockSpec((None,) * nl + (D, D), lambda i: lead + (0, 0), pipeline_mode=pl.Buffered(1)),
                   row, pl.BlockSpec((1, D), lambda i: (0, 0))])
    return pl.pallas_call(
        functools.partial(_mixer_out_kernel, n_u=len(unified), prologue=prologue),
        grid=(nm + 1,),
        in_specs=in_specs,
        out_specs=[row, row],
        out_shape=[_sds((ROWS, D), f32), _sds((ROWS, D), bf16)],
        scratch_shapes=[pltpu.VMEM((D, D), bf16)],
        input_output_aliases={3 + len(unified): 0},
        name="mixer_out",
        compiler_params=_cp(("arbitrary",)),
    )(x_main, x_aux, *unified, w, h, norm_g)


def _ffn(h, xn, w_in, w_out, layer):
    tn = 512
    nct = D_FF // tn
    act = _mm(name="ffn_in", xs=[(xn, 0)], ws=[(w_in, (layer,), 0), (w_in, (layer,), nct)],
              outs=[(_sds((ROWS, D_FF), bf16), 0, 0)],
              K=D, tm=TM_BIG, tn=tn, n_row_tiles=ROWS // TM_BIG, n_col_tiles=nct,
              prologue=_pro_id,
              epilogue=lambda accs, cvs, rms: (jax.nn.silu(accs[0]) * accs[1],))[0]
    h = _mm(name="ffn_out", xs=[(act, 0)], ws=[(w_out, (layer,), 0)], rmats=[(h, 0, 0)],
            outs=[(_sds((ROWS, D), f32), 0, 0)],
            K=D_FF, tm=TM_MID, tn=tn, n_row_tiles=ROWS // TM_MID, n_col_tiles=D // tn,
            prologue=_pro_id, epilogue=lambda accs, cvs, rms: (rms[0] + accs[0],),
            aliases={2: 0})[0]
    return h


def _hgrn_lower_bound(logits, j):
    m = jnp.max(logits, axis=0, keepdims=True)
    e = jnp.exp(logits - m)
    p = e / jnp.sum(e, axis=0, keepdims=True)
    cs = p[0:1]
    for r in range(1, j + 1):
        cs = cs + p[r:r + 1]
    return jnp.clip(cs - p[0:1], 0.0, 1.0)


def _hgrn_in(xn, w_in, lb_logits, j):
    tn = 1024
    nct = D // tn
    common = dict(K=D, tm=TM_BIG, tn=tn, n_row_tiles=ROWS // TM_BIG, n_col_tiles=nct, prologue=_pro_id,
                  row_splits=2)
    o = (_sds((ROWS, D), f32), 0, 0)
    q = _mm(xs=[(xn, 0)], ws=[(w_in, (j,), 0)], outs=[o],
            epilogue=lambda accs, cvs, rms: (jax.nn.silu(accs[0]),), **common)[0]

    def f_epi(accs, cvs, rms):
        f = accs[0]
        lb = _hgrn_lower_bound(cvs[0], j)
        t = jnp.exp(-jnp.abs(f))
        log_f = (-(jnp.maximum(-f, 0.0) + jnp.log(1.0 + t))
                 + jnp.log(1.0 + lb * jnp.exp(jnp.minimum(-f, GATE_EXP_CLIP))))
        k = (1.0 - lb) * (jnp.where(f >= 0.0, t, 1.0) / (1.0 + t))
        return log_f, k

    log_f, k = _mm(xs=[(xn, 0)], ws=[(w_in, (j,), nct)], cvecs=[(lb_logits, 0)], outs=[o, o],
                   epilogue=f_epi, **{**common, "tm": TM_MID, "n_row_tiles": ROWS // TM_MID, "row_splits": 3})
    v = _mm(xs=[(xn, 0)], ws=[(w_in, (j,), 2 * nct)], outs=[o],
            epilogue=lambda accs, cvs, rms: (accs[0],), **common)[0]
    g = _mm(xs=[(xn, 0)], ws=[(w_in, (j,), 3 * nct)], outs=[o],
            epilogue=lambda accs, cvs, rms: (jax.nn.silu(accs[0]),), **common)[0]
    return q, k, v, log_f, g


def _split3(x):
    hi = x.astype(bf16)
    r = x - hi.astype(f32)
    mid = r.astype(bf16)
    lo = (r - mid.astype(f32)).astype(bf16)
    return hi, mid, lo


def _gla_consts(T):
    t = np.arange(T)[:, None]
    w = np.arange(T)[None, :]
    mats = [w <= t, w > t]
    masks = []
    s = T // 2
    while s >= 1:
        piece = t // (2 * s)
        lower = (t % (2 * s)) >= s
        ref = piece * 2 * s + s
        mats.append(np.where(lower, (w > ref) & (w <= t), (w > t) & (w <= ref)))
        upiece = w // (2 * s)
        ulower = (w % (2 * s)) >= s
        masks.append((piece == upiece) & lower & ~ulower)
        s //= 2
    masks.append(t == w)
    m = np.concatenate(mats, 0)
    eye = np.eye(HG_UNROLL, dtype=bool)
    gmasks = np.stack([np.kron(eye, pm) for pm in masks])
    return (jnp.asarray(np.concatenate([m, m, m], 1), dtype=bf16), jnp.asarray(gmasks, dtype=f32))


def _gla_block(q_ref, k_ref, v_ref, f_ref, g_ref, gn_ref, dm_ref, pm_ref, o_ref, st_ref, e_s, ql_s, kl_s, T):
    nlev = T.bit_length() - 1
    dn_t = (((1,), (1,)), ((), ()))
    dn_l = (((0,), (0,)), ((), ()))
    def factors(ls):
        e_s[0:(nlev + 2) * T, ls] = jnp.dot(dm_ref[...], jnp.concatenate(_split3(f_ref[:, ls]), axis=0),
                                            preferred_element_type=f32)
        q = q_ref[:, ls]
        k = k_ref[:, ls]
        ql_s[0, 0:T, ls] = (q * jnp.exp(e_s[0:T, ls])).astype(bf16)
        kl_s[0, 0:T, ls] = (k * jnp.exp(e_s[T:2 * T, ls])).astype(bf16)
        for l in range(nlev):
            ex = jnp.exp(e_s[(l + 2) * T:(l + 3) * T, ls])
            ql_s[l + 1, 0:T, ls] = (q * ex).astype(bf16)
            kl_s[l + 1, 0:T, ls] = (k * ex).astype(bf16)
        ql_s[nlev + 1, 0:T, ls] = q.astype(bf16)
        kl_s[nlev + 1, 0:T, ls] = k.astype(bf16)

    factors(slice(0, D))
    for h0 in range(0, HG_H, HG_UNROLL):
        sls = [slice(h * HG_K, (h + 1) * HG_K) for h in range(h0, h0 + HG_UNROLL)]
        att = None
        for l in range(nlev + 1):
            a_l = lax.dot_general(jnp.concatenate([ql_s[l + 1, 0:T, sl] for sl in sls], axis=0),
                                  jnp.concatenate([kl_s[l + 1, 0:T, sl] for sl in sls], axis=0),
                                  dn_t, preferred_element_type=f32) * pm_ref[l]
            att = a_l if att is None else att + a_l
        vbs = [v_ref[:, sl].astype(bf16) for sl in sls]
        o_intra = jnp.dot(att.astype(bf16), jnp.concatenate(vbs, axis=0), preferred_element_type=f32)
        for p in range(0, HG_UNROLL, 2):
            ha, hb = h0 + p, h0 + p + 1
            sla, slb = sls[p], sls[p + 1]
            st2 = jnp.concatenate([st_ref[ha], st_ref[hb]], axis=0)
            oc = lax.dot_general(jnp.concatenate([ql_s[0, 0:T, sla], ql_s[0, 0:T, slb]], axis=0),
                                 st2.astype(bf16), dn_t, preferred_element_type=f32)
            up = lax.dot_general(jnp.concatenate([vbs[p], vbs[p + 1]], axis=1),
                                 jnp.concatenate([kl_s[0, 0:T, sla], kl_s[0, 0:T, slb]], axis=1),
                                 dn_l, preferred_element_type=f32)
            for i, (h, sl) in enumerate(((ha, sla), (hb, slb))):
                o = (oc[i * T:(i + 1) * T, i * HG_K:(i + 1) * HG_K]
                     + o_intra[(p + i) * T:(p + i + 1) * T])
                st_ref[h] = (st2[i * HG_K:(i + 1) * HG_K] * jnp.exp(e_s[T - 1:T, sl])
                             + up[i * HG_K:(i + 1) * HG_K, i * HG_K:(i + 1) * HG_K])
                ms = jnp.mean(o * o, axis=-1, keepdims=True)
                on = o * lax.rsqrt(ms + EPS) * gn_ref[:, sl]
                o_ref[:, sl] = (on * g_ref[:, sl]).astype(o_ref.dtype)


def _gla_kernel(qm, km, vm, fm, gm, qa, ka, va, fa, ga, gn_ref, dmm, pmm, dma, pma,
                om_ref, oa_ref, s_ref, st_ref, e_s, ql_s, kl_s):
    c = pl.program_id(1)

    @pl.when(c == 0)
    def _meta():
        st_ref[...] = jnp.zeros_like(st_ref)
        _gla_block(qa, ka, va, fa, ga, gn_ref, dma, pma, oa_ref, st_ref, e_s, ql_s, kl_s, NM)

    @pl.when(c > 0)
    def _main():
        _gla_block(qm, km, vm, fm, gm, gn_ref, dmm, pmm, om_ref, st_ref, e_s, ql_s, kl_s, HG_C)

    @pl.when(c == pl.num_programs(1) - 1)
    def _final():
        for h in range(HG_H):
            s_ref[0, h] = st_ref[h].T


def _hgrn_prompt(q, k, v, log_f, g, gn):
    nc = LP // HG_C
    main_spec = pl.BlockSpec((HG_C, D), lambda b, c: (b * nc + jnp.maximum(c - 1, 0), 0))
    meta_spec = pl.BlockSpec((NM, D), lambda b, c: (ROW_M // NM + b, 0))
    arrs = [q, k, v, log_f, g]
    consts = [*_gla_consts(HG_C), *_gla_consts(NM)]
    nslot = HG_C.bit_length() + 1
    return pl.pallas_call(
        _gla_kernel,
        grid=(NB, nc + 1),
        in_specs=[main_spec] * 5 + [meta_spec] * 5 + [pl.BlockSpec((1, D), lambda b, c: (0, 0))]
                 + [pl.BlockSpec(a.shape, lambda b, c, n=a.ndim: (0,) * n) for a in consts],
        out_specs=[pl.BlockSpec((HG_C, D), lambda b, c: (b * nc + jnp.maximum(c - 1, 0), 0)),
                   pl.BlockSpec((NM, D), lambda b, c: (b, 0)),
                   pl.BlockSpec((1, HG_H, HG_K, HG_K), lambda b, c: (b, 0, 0, 0))],
        out_shape=[_sds((ROWS_MAIN, D), bf16), _sds((NB * NM, D), bf16), _sds((NB, HG_H, HG_K, HG_K), f32)],
        scratch_shapes=[pltpu.VMEM((HG_H, HG_K, HG_K), f32),
                        pltpu.VMEM((nslot * HG_C, D), f32),
                        pltpu.VMEM((nslot, HG_C, D), bf16),
                        pltpu.VMEM((nslot, HG_C, D), bf16)],
        name="hgrn_prompt",
        compiler_params=_cp(("arbitrary", "arbitrary")),
    )(*arrs, *arrs, gn, *consts)


HG_BT = 8
HG_SB = 4


def _hgrn_dec_kernel(s_ref, qT, kT, fT, v_ref, g_ref, gn_ref, e3_ref, *rest, stack_slot):
    so_ref, o_ref, qb_s, kb_s, gb_s = rest[-5:]
    if stack_slot is not None:
        for m, e_ref in enumerate(rest[:-5]):
            so_ref[m] = e_ref[...]
    row0 = pl.program_id(1) * HG_SB

    def spread(m, dst):
        dst[...] = jnp.dot(jnp.concatenate(_split3(m), axis=1), e3_ref[...], preferred_element_type=f32)

    def per_b(bb, carry):
        spread(qT[bb], qb_s)
        spread(kT[bb], kb_s)
        spread(jnp.exp(fT[bb]), gb_s)
        row = pl.ds(row0 + bb, 1)
        vall = v_ref[row, :]
        outs = []
        for h in range(HG_H):
            hs = slice(h * HG_K, (h + 1) * HG_K)
            sn = gb_s[:, hs] * s_ref[bb, h] + kb_s[:, hs] * vall[:, hs]
            if stack_slot is None:
                so_ref[bb, h] = sn
            else:
                so_ref[stack_slot, bb, h] = sn
            o = jnp.sum(qb_s[:, hs] * sn, axis=0, keepdims=True)
            ms = jnp.mean(o * o, axis=-1, keepdims=True)
            outs.append(o * lax.rsqrt(ms + EPS))
        on = jnp.concatenate(outs, axis=1) * gn_ref[...]
        o_ref[row, :] = on * g_ref[row, :]
        return carry

    lax.fori_loop(0, HG_SB, per_b, 0)


def _hgrn_sample(state, j, q, k, v, log_f, g, gn, earlier):
    def colform(x):
        return x[ROW_S:ROW_S + NS].reshape(NS, HG_H, HG_K).transpose(0, 2, 1)

    n_a = state.shape[0]
    last = j == n_a - 1
    head_of_lane = np.arange(D) // HG_K
    sel = (np.arange(HG_H)[:, None] == head_of_lane[None, :])
    e3 = jnp.asarray(np.concatenate([sel, sel, sel], axis=0), dtype=bf16)
    nsb = HG_BT // HG_SB
    col_spec = pl.BlockSpec((HG_SB, HG_K, HG_H), lambda i, s: (i * nsb + s, 0, 0))
    row_spec = pl.BlockSpec((HG_BT, D), lambda i, s: (ROW_S // HG_BT + i, 0))
    new_spec = pl.BlockSpec((HG_SB, HG_H, HG_K, HG_K), lambda i, s: (i * nsb + s, 0, 0, 0))
    in_specs = [pl.BlockSpec((None, HG_SB, HG_H, HG_K, HG_K), lambda i, s: (j, i * nsb + s, 0, 0, 0)),
                col_spec, col_spec, col_spec, row_spec, row_spec,
                pl.BlockSpec((1, D), lambda i, s: (0, 0)), pl.BlockSpec((3 * HG_H, D), lambda i, s: (0, 0))]
    args = [state, colform(q), colform(k), colform(log_f), v, g, gn, e3]
    if last:
        in_specs += [new_spec] * len(earlier)
        args += list(earlier)
        st_out = pl.BlockSpec((n_a, HG_SB, HG_H, HG_K, HG_K), lambda i, s: (0, i * nsb + s, 0, 0, 0))
        st_shape = _sds(state.shape, f32)
    else:
        st_out, st_shape = new_spec, _sds(state.shape[1:], f32)
    return pl.pallas_call(
        functools.partial(_hgrn_dec_kernel, stack_slot=j if last else None),
        grid=(NS // HG_BT, nsb),
        in_specs=in_specs,
        out_specs=[st_out, pl.BlockSpec((HG_BT, D), lambda i, s: (i, 0))],
        out_shape=[st_shape, _sds((NS, D), f32)],
        scratch_shapes=[pltpu.VMEM((HG_K, D), f32)] * 3,
        name="hgrn_decode",
        compiler_params=_cp(("arbitrary", "arbitrary")),
    )(*args)


def _aux_rows(sample_rows, meta_rows):
    pad = jnp.zeros((AUX - NS - NB * NM, sample_rows.shape[1]), sample_rows.dtype)
    return jnp.concatenate([sample_rows, meta_rows.astype(sample_rows.dtype), pad], axis=0)


def _hgrn_layer(h, norm_g, ffn_g, state, earlier, j, w_in, lb_logits, gnorm, w_out, xn=None):
    if xn is None:
        xn = _rmsnorm(h, norm_g, out_dtype=bf16, tm=TM_MID, row_tile0=0, n_row_tiles=ROWS // TM_MID)
    q, k, v, log_f, g = _hgrn_in(xn, w_in, lb_logits, j)
    gn = gnorm[j].reshape(1, D)
    o_main, o_meta, s_p = _hgrn_prompt(q, k, v, log_f, g, gn)
    s_s, o_s = _hgrn_sample(state, j, q, k, v, log_f, g, gn, earlier)
    o_aux = _aux_rows(o_s.astype(bf16), o_meta)
    h, xn_f = _mixer_out(h, o_main, o_aux, [], w_out, (j,), ffn_g, _pro_id)
    return (h, xn_f), s_p, s_s


def _lru_in(xn, w_in, j):
    tn = 1024
    nct = D // tn
    common = dict(K=D, tm=TM_BIG, tn=tn, n_row_tiles=ROWS // TM_BIG, n_col_tiles=nct, prologue=_pro_id,
                  row_splits=2)
    o = (_sds((ROWS, D), f32), 0, 0)
    y = _mm(xs=[(xn, 0)], ws=[(w_in, (j,), 0)], outs=[o],
            epilogue=lambda accs, cvs, rms: (jax.nn.gelu(accs[0], approximate=True),), **common)[0]
    x = _mm(xs=[(xn, 0)], ws=[(w_in, (j,), nct)], outs=[o],
            epilogue=lambda accs, cvs, rms: (accs[0],), **common)[0]
    return y, x


def _lru_gates(xc, gw_ref, gb_ref, lam_ref, n):
    ls = slice(n * LRU_BW, (n + 1) * LRU_BW)
    xb = xc[:, ls].astype(bf16)
    r = jax.nn.sigmoid(jnp.dot(xb, gw_ref[0, n], preferred_element_type=f32) + gb_ref[0:1, ls])
    ig = jax.nn.sigmoid(jnp.dot(xb, gw_ref[1, n], preferred_element_type=f32) + gb_ref[1:2, ls])
    log_a = -LRU_C * r * _softplus(-lam_ref[:, ls])
    a = jnp.exp(log_a)
    mult = jnp.sqrt(jnp.maximum(1.0 - a * a, 0.0))
    return a, ig, mult


def _lru_conv(cb_ref, cw_ref, x0, x1, x2, x3):
    xc = cb_ref[...] + x0 * cw_ref[0:1, :]
    xc = xc + x1 * cw_ref[1:2, :]
    xc = xc + x2 * cw_ref[2:3, :]
    return xc + x3 * cw_ref[3:4, :]


def _lru_block(x_ref, y_ref, o_ref, cw_ref, cb_ref, gw_s, gb_ref, lam_ref, xbuf, a_s, b_s, hcar, T, first):
    xbuf[8:8 + T, :] = x_ref[...]
    xc = _lru_conv(cb_ref, cw_ref, xbuf[5:5 + T, :], xbuf[6:6 + T, :], xbuf[7:7 + T, :], xbuf[8:8 + T, :])
    xbuf[0:8, :] = xbuf[T:T + 8, :]
    row = lax.broadcasted_iota(jnp.int32, (T, 1), 0)
    for n in range(LRU_NBLK):
        ls = slice(n * LRU_BW, (n + 1) * LRU_BW)
        a, ig, mult = _lru_gates(xc, gw_s, gb_ref, lam_ref, n)
        if first:
            mult = jnp.where(row == 0, 1.0, mult)
        a_s[0:T, ls] = a
        b_s[0:T, ls] = xc[:, ls] * ig * mult

    def step(t, hprev):
        hnew = a_s[pl.ds(t, 1), :] * hprev + b_s[pl.ds(t, 1), :]
        b_s[pl.ds(t, 1), :] = hnew
        return hnew

    hcar[...] = lax.fori_loop(0, T, step, hcar[...], unroll=8)
    o_ref[...] = (b_s[0:T, :] * y_ref[...]).astype(o_ref.dtype)


def _lru_kernel(xm, ym, xa, ya, cw_ref, cb_ref, gw_ref, gb_ref, lam_ref,
                om_ref, oa_ref, hl_ref, cv_ref, gw_s, xbuf, a_s, b_s, hcar):
    c = pl.program_id(1)

    @pl.when((pl.program_id(0) == 0) & (c == 0))
    def _cast():
        gw_s[...] = gw_ref[...].astype(bf16)

    @pl.when(c == 0)
    def _meta():
        xbuf[0:8, :] = jnp.zeros((8, D), f32)
        hcar[...] = jnp.zeros_like(hcar)
        _lru_block(xa, ya, oa_ref, cw_ref, cb_ref, gw_s, gb_ref, lam_ref, xbuf, a_s, b_s, hcar, NM, True)

    @pl.when(c > 0)
    def _main():
        _lru_block(xm, ym, om_ref, cw_ref, cb_ref, gw_s, gb_ref, lam_ref, xbuf, a_s, b_s, hcar, LRU_T, False)

    @pl.when(c == pl.num_programs(1) - 1)
    def _final():
        hl_ref[0] = hcar[...]
        cv_ref[0] = xbuf[5:8, :]


def _lru_prompt(x, y, j, conv_w, conv_b, gate_w, gate_b, lam):
    nc = LP // LRU_T
    main_spec = pl.BlockSpec((LRU_T, D), lambda b, c: (b * nc + jnp.maximum(c - 1, 0), 0))
    meta_spec = pl.BlockSpec((NM, D), lambda b, c: (ROW_M // NM + b, 0))
    return pl.pallas_call(
        _lru_kernel,
        grid=(NB, nc + 1),
        in_specs=[main_spec, main_spec, meta_spec, meta_spec,
                  pl.BlockSpec((None, 4, D), lambda b, c: (j, 0, 0)),
                  pl.BlockSpec((1, D), lambda b, c: (j, 0)),
                  pl.BlockSpec((None, 2, LRU_NBLK, LRU_BW, LRU_BW), lambda b, c: (j, 0, 0, 0, 0)),
                  pl.BlockSpec((None, 2, D), lambda b, c: (j, 0, 0)),
                  pl.BlockSpec((1, D), lambda b, c: (j, 0))],
        out_specs=[pl.BlockSpec((LRU_T, D), lambda b, c: (b * nc + jnp.maximum(c - 1, 0), 0)),
                   pl.BlockSpec((NM, D), lambda b, c: (b, 0)),
                   pl.BlockSpec((1, 1, D), lambda b, c: (b, 0, 0)),
                   pl.BlockSpec((1, 3, D), lambda b, c: (b, 0, 0))],
        out_shape=[_sds((ROWS_MAIN, D), bf16), _sds((NB * NM, D), bf16),
                   _sds((NB, 1, D), f32), _sds((NB, 3, D), f32)],
        scratch_shapes=[pltpu.VMEM((2, LRU_NBLK, LRU_BW, LRU_BW), bf16),
                        pltpu.VMEM((LRU_T + 8, D), f32), pltpu.VMEM((LRU_T, D), f32),
                        pltpu.VMEM((LRU_T, D), f32), pltpu.VMEM((1, D), f32)],
        name="lru_prompt",
        compiler_params=_cp(("arbitrary", "arbitrary")),
    )(x, y, x, y, conv_w, conv_b, gate_w, gate_b, lam)


LRU_ST = 64


def _lru_dec_kernel(x_ref, y_ref, h0_ref, cbuf_ref, cw_ref, cb_ref, gw_ref, gb_ref, lam_ref,
                    o_ref, hn_ref, cn_ref, gw_s):
    gw_s[...] = gw_ref[...].astype(bf16)
    x = x_ref[...]
    b0 = cbuf_ref[:, 0, :]
    b1 = cbuf_ref[:, 1, :]
    b2 = cbuf_ref[:, 2, :]
    xc = _lru_conv(cb_ref, cw_ref, b0, b1, b2, x)
    cn_ref[:, 0, :] = b1
    cn_ref[:, 1, :] = b2
    cn_ref[:, 2, :] = x
    for n in range(LRU_NBLK):
        ls = slice(n * LRU_BW, (n + 1) * LRU_BW)
        a, ig, mult = _lru_gates(xc, gw_s, gb_ref, lam_ref, n)
        hn = a * h0_ref[:, ls] + xc[:, ls] * ig * mult
        hn_ref[:, ls] = hn
        o_ref[:, ls] = (hn * y_ref[:, ls]).astype(o_ref.dtype)


def _lru_sample(x, y, j, h0, cbuf, conv_w, conv_b, gate_w, gate_b, lam):
    row_spec = pl.BlockSpec((LRU_ST, D), lambda i: (ROW_S // LRU_ST + i, 0))
    return pl.pallas_call(
        _lru_dec_kernel,
        grid=(NS // LRU_ST,),
        in_specs=[row_spec, row_spec,
                  pl.BlockSpec((None, LRU_ST, D), lambda i: (j, i, 0)),
                  pl.BlockSpec((None, LRU_ST, 3, D), lambda i: (j, i, 0, 0)),
                  pl.BlockSpec((None, 4, D), lambda i: (j, 0, 0)),
                  pl.BlockSpec((1, D), lambda i: (j, 0)),
                  pl.BlockSpec((None, 2, LRU_NBLK, LRU_BW, LRU_BW), lambda i: (j, 0, 0, 0, 0)),
                  pl.BlockSpec((None, 2, D), lambda i: (j, 0, 0)),
                  pl.BlockSpec((1, D), lambda i: (j, 0))],
        out_specs=[pl.BlockSpec((LRU_ST, D), lambda i: (i, 0)),
                   pl.BlockSpec((LRU_ST, D), lambda i: (i, 0)),
                   pl.BlockSpec((LRU_ST, 3, D), lambda i: (i, 0, 0))],
        out_shape=[_sds((NS, D), bf16), _sds((NS, D), f32), _sds((NS, 3, D), f32)],
        scratch_shapes=[pltpu.VMEM((2, LRU_NBLK, LRU_BW, LRU_BW), bf16)],
        name="lru_decode",
        compiler_params=_cp(("arbitrary",)),
    )(x, y, h0, cbuf, conv_w, conv_b, gate_w, gate_b, lam)


def _lru_layer(h, norm_g, ffn_g, state_h, state_conv, j, w_in, conv_w, conv_b, gate_w, gate_b, lam, w_out):
    xn = _rmsnorm(h, norm_g, out_dtype=bf16, tm=TM_MID, row_tile0=0, n_row_tiles=ROWS // TM_MID)
    y, x = _lru_in(xn, w_in, j)
    o_main, o_meta, hl_p, cv_p = _lru_prompt(x, y, j, conv_w, conv_b, gate_w, gate_b, lam)
    o_s, hl_s, cv_s = _lru_sample(x, y, j, state_h, state_conv, conv_w, conv_b, gate_w, gate_b, lam)
    o_aux = _aux_rows(o_s, o_meta)
    h, xn_f = _mixer_out(h, o_main, o_aux, [], w_out, (j,), ffn_g, _pro_id)
    return (h, xn_f), hl_p.reshape(NB, D), hl_s, cv_p, cv_s


def _rwkv_premix_kernel(h_ref, g_ref, mu_ref, sh_ref, *refs):
    x_refs = refs[:6]
    sp_ref, ss_ref, xbuf, meta_last = refs[6:]
    i = pl.program_id(0)
    x = h_ref[...]
    ms = jnp.mean(x * x, axis=-1, keepdims=True)
    xn = x * lax.rsqrt(ms + EPS) * g_ref[...]
    xbuf[8:8 + PM_T, :] = xn
    row = lax.broadcasted_iota(jnp.int32, (PM_T, 1), 0)

    @pl.when(i == 0)
    def _aux():
        xbuf[7:8, :] = jnp.zeros((1, D), f32)
        ss_ref[...] = xn[0:NS]
        for b in range(NB):
            meta_last[b:b + 1, :] = xn[NS + b * NM + NM - 1:NS + b * NM + NM]

    @pl.when(i > 0)
    def _main():
        m = i - 1
        b = m // (LP // PM_T)

        @pl.when(m % (LP // PM_T) == 0)
        def _start():
            xbuf[7:8, :] = meta_last[pl.ds(b, 1), :]

        @pl.when(m % (LP // PM_T) == LP // PM_T - 1)
        def _end():
            sp_ref[pl.ds(b, 1), :] = xn[PM_T - 1:PM_T]

    shifted = xbuf[7:7 + PM_T, :]
    is_meta = (row >= NS) & (row < NS + NB * NM) & ((row - NS) % NM != 0)
    sh_pad = jnp.concatenate([sh_ref[...], jnp.zeros((PM_T - NS, D), f32)], axis=0)
    prev_aux = jnp.where(row < NS, sh_pad, jnp.where(is_meta, shifted, 0.0))
    prev = jnp.where(i == 0, prev_aux, shifted)
    dx = prev - xn
    for n in range(6):
        x_refs[n][...] = (xn + dx * mu_ref[n:n + 1, :]).astype(bf16)
    xbuf[7:8, :] = xn[PM_T - 1:PM_T]


def _rwkv_premix(h, norm_g, mu_j, shift_j):
    nt = ROWS // PM_T
    rows = lambda i: (jnp.where(i == 0, nt - 1, i - 1), 0)
    return pl.pallas_call(
        _rwkv_premix_kernel,
        grid=(nt,),
        in_specs=[pl.BlockSpec((PM_T, D), rows),
                  pl.BlockSpec((1, D), lambda i: (0, 0)),
                  pl.BlockSpec((6, D), lambda i: (0, 0)),
                  pl.BlockSpec((NS, D), lambda i: (0, 0))],
        out_specs=[pl.BlockSpec((PM_T, D), rows)] * 6 + [pl.BlockSpec((NB, D), lambda i: (0, 0)),
                                                         pl.BlockSpec((NS, D), lambda i: (0, 0))],
        out_shape=[_sds((ROWS, D), bf16)] * 6 + [_sds((NB, D), f32), _sds((NS, D), f32)],
        scratch_shapes=[pltpu.VMEM((PM_T + 8, D), f32), pltpu.VMEM((8, D), f32)],
        name="rwkv_premix",
        compiler_params=_cp(("arbitrary",)),
    )(h, norm_g, mu_j, shift_j)


def _rwkv_lora_kernel(xw_ref, xa_ref, xg_ref, w1_ref, w2_ref, a1_ref, a2_ref, g1_ref, g2_ref, w0_ref, a0_ref,
                      d_ref, a_ref, g_ref, w1_s, w2_s, a1_s, a2_s, g1_s, g2_s):
    @pl.when(pl.program_id(0) == 0)
    def _cast():
        for src, dst in ((w1_ref, w1_s), (w2_ref, w2_s), (a1_ref, a1_s), (a2_ref, a2_s), (g1_ref, g1_s),
                         (g2_ref, g2_s)):
            dst[...] = src[...].astype(bf16)

    def mm(u, w_s):
        return jnp.dot(u, w_s[...], preferred_element_type=f32)

    tw = jnp.tanh(mm(xw_ref[...], w1_s)).astype(bf16)
    x = w0_ref[...] + mm(tw, w2_s)
    d_ref[...] = jnp.exp(-np.float32(np.exp(-0.5)) * jax.nn.sigmoid(x)).reshape(d_ref.shape)
    ta = mm(xa_ref[...], a1_s).astype(bf16)
    a_ref[...] = jax.nn.sigmoid(a0_ref[...] + mm(ta, a2_s)).reshape(a_ref.shape)
    tg = jax.nn.sigmoid(mm(xg_ref[...], g1_s)).astype(bf16)
    g_ref[...] = mm(tg, g2_s)


def _rwkv_lora(xw, xa, xg, w1, w2, a1, a2, g1, g2, w0, a0):
    tm = 384
    full = lambda a: pl.BlockSpec(a.shape, lambda i: (0,) * a.ndim)
    row = pl.BlockSpec((tm, D), lambda i: (i, 0))
    row3 = pl.BlockSpec((tm, RW_Q, 128), lambda i: (i, 0, 0))
    ws = [w1, w2, a1, a2, g1, g2]
    return pl.pallas_call(
        _rwkv_lora_kernel,
        grid=(ROWS // tm,),
        in_specs=[row, row, row] + [full(a) for a in ws + [w0, a0]],
        out_specs=[row3, row3, row],
        out_shape=[_sds((ROWS, RW_Q, 128), f32)] * 2 + [_sds((ROWS, D), f32)],
        scratch_shapes=[pltpu.VMEM(a.shape, bf16) for a in ws],
        name="rwkv_lora",
        compiler_params=_cp(("arbitrary",)),
    )(xw, xa, xg, *ws, w0, a0)


def _rwkv_prep(t, r_s, k_s, d_s, a_s, tmp, kk_p, ka_p):
    kk_t, d_t, ka_t, k2_t, r_t = tmp
    k = k_s[t]
    a = a_s[t]
    kkr = k * kk_p[...]
    nrm = jnp.sqrt(jnp.sum(kkr * kkr, axis=0, keepdims=True))
    kk = kkr / jnp.maximum(nrm, 1e-12)
    kk_t[...] = kk
    d_t[...] = d_s[t]
    ka_t[...] = kk * a
    k2_t[...] = k * (1.0 + (a - 1.0) * ka_p[...])
    r_t[...] = r_s[t]


def _rwkv_body(v, s_s, tmp):
    kk_t, d_t, ka_t, k2_t, r_t = tmp
    sa = -(s_s[0] * kk_t[0:1, :])
    for j in range(1, RW_N):
        sa = sa - s_s[j] * kk_t[j:j + 1, :]
    y = None
    for j in range(RW_N):
        sn = s_s[j] * d_t[j:j + 1, :] + sa * ka_t[j:j + 1, :] + v * k2_t[j:j + 1, :]
        s_s[j] = sn
        yj = sn * r_t[j:j + 1, :]
        y = yj if y is None else y + yj
    return y


def _rwkv_tail(y, v, tmp, rk_p, lw_p, lb_p):
    _, _, _, k2_t, r_t = tmp
    mean = jnp.mean(y, axis=0, keepdims=True)
    yc = y - mean
    var = jnp.mean(yc * yc, axis=0, keepdims=True)
    yn = yc * lax.rsqrt(var + RW_LN_EPS) * lw_p[...] + lb_p[...]
    bonus = jnp.sum(r_t[...] * k2_t[...] * rk_p[...], axis=0, keepdims=True) * v
    return yn + bonus


def _rwkv_step(t, r_s, k_s, v_s, d_s, a_s, z_s, s_s, tmp, kk_p, ka_p, rk_p, lw_p, lb_p):
    _rwkv_prep(t, r_s, k_s, d_s, a_s, tmp, kk_p, ka_p)
    v = v_s[t]
    z_s[t] = _rwkv_tail(_rwkv_body(v, s_s, tmp), v, tmp, rk_p, lw_p, lb_p)


def _rwkv_dec_kernel(s_ref, r_ref, k_ref, v_ref, d_ref, a_ref, kk_p, ka_p, rk_p, lw_p, lb_p, so_ref, z_ref,
                     v_s, y_s):
    r = r_ref[0]
    k = k_ref[0]
    a = a_ref[0]
    d = d_ref[0]
    kkr = k * kk_p[0]
    nrm = jnp.sqrt(jnp.sum(kkr * kkr, axis=0, keepdims=True))
    kk = kkr / jnp.maximum(nrm, 1e-12)
    k2 = k * (1.0 + (a - 1.0) * ka_p[0])
    ka = kk * a
    v_s[...] = v_ref[0]
    for i in range(RW_N):
        s = s_ref[0, i]
        sa = -jnp.sum(s * kk, axis=0, keepdims=True)
        sn = s * d + sa * ka + v_s[i:i + 1, :] * k2
        so_ref[0, i] = sn
        y_s[i:i + 1, :] = jnp.sum(sn * r, axis=0, keepdims=True)
    y = y_s[...]
    mean = jnp.mean(y, axis=0, keepdims=True)
    yc = y - mean
    var = jnp.mean(yc * yc, axis=0, keepdims=True)
    yn = yc * lax.rsqrt(var + RW_LN_EPS) * lw_p[0] + lb_p[0]
    bonus = jnp.sum(r * k2 * rk_p[0], axis=0, keepdims=True) * v_s[...]
    z_ref[0] = yn + bonus


def _rwkv_sample(state, j, r, k, v, dec, a, head_params):
    def lanes(x):
        return x[ROW_S:ROW_S + NS].reshape(NS, D).T.reshape(RW_H, RW_N, NS)

    def param(p):
        return jnp.broadcast_to(p.reshape(RW_H, RW_N, 1), (RW_H, RW_N, NS))

    vec = pl.BlockSpec((1, RW_N, NS), lambda h: (h, 0, 0))
    st = pl.BlockSpec((1, RW_N, RW_N, NS), lambda h: (h, 0, 0, 0))
    s_new, z = pl.pallas_call(
        _rwkv_dec_kernel,
        grid=(RW_H,),
        in_specs=[st] + [vec] * 10,
        out_specs=[st, vec],
        out_shape=[_sds((RW_H, RW_N, RW_N, NS), f32), _sds((RW_H, RW_N, NS), f32)],
        scratch_shapes=[pltpu.VMEM((RW_N, NS), f32)] * 2,
        name="rwkv_decode",
        compiler_params=_cp(("arbitrary",)),
    )(state[j].transpose(1, 2, 3, 0), lanes(r), lanes(k), lanes(v), lanes(dec), lanes(a),
      *[param(p) for p in head_params])
    return s_new.transpose(3, 0, 1, 2), z.reshape(D, NS).T


RW_Q = D // 128


def _half_transpose(x):
    xt = x.T
    return jnp.concatenate([xt[0:RW_N], xt[RW_N:2 * RW_N]], axis=1)


def _rwkv_prompt_kernel(*refs):
    n_in = 5 * NB
    in_refs = refs[:n_in]
    params = refs[n_in:n_in + 5]
    zmain, zmeta, sf_ref = refs[n_in + 5:n_in + 8]
    scr = refs[n_in + 8:]
    seq, z_s, zo_s, s_s, tmp = scr[:5], scr[5], scr[6], scr[7], scr[8:]
    c = pl.program_id(0)

    @pl.when(c == 0)
    def _init():
        s_s[...] = jnp.zeros_like(s_s)
        z_s[...] = jnp.zeros_like(z_s)

    def load_token(t):
        rows = pl.ds(pl.multiple_of(t * RW_Q, RW_Q), RW_Q)
        for a in range(5):
            tile = jnp.concatenate([in_refs[a * NB + b][rows, :] for b in range(NB)], axis=0)
            seq[a][t] = _half_transpose(tile)

    load_token(0)

    def step(t, carry):
        tp = jnp.maximum(t - 1, 0)
        zo_s[tp] = _half_transpose(z_s[tp])
        _rwkv_step(t, *seq, z_s, s_s, tmp, *params)
        load_token(jnp.minimum(t + 1, RW_TS - 1))
        return carry

    lax.fori_loop(0, RW_TS, step, 0)
    zo_s[RW_TS - 1] = _half_transpose(z_s[RW_TS - 1])

    def write(z_ref):
        for t in range(RW_TS):
            for b in range(NB):
                z_ref[b, t * RW_Q:(t + 1) * RW_Q, :] = zo_s[t, b * RW_Q:(b + 1) * RW_Q, :]

    @pl.when(c == 0)
    def _write_meta():
        write(zmeta)

    @pl.when(c > 0)
    def _write_main():
        write(zmain)

    @pl.when(c == pl.num_programs(0) - 1)
    def _final():
        sf_ref[...] = s_s[...]


def _rwkv_prompt(r, k, v, dec, a, params):
    nc = LP // RW_TS
    blk = RW_TS * RW_Q
    arrs = [x.reshape(ROWS * RW_Q, 128) for x in (r, k, v, dec, a)]
    in_specs, args = [], []
    for x in arrs:
        for b in range(NB):
            in_specs.append(pl.BlockSpec(
                (blk, 128), lambda c, b=b: (jnp.where(c == 0, ROW_M // RW_TS + b, b * nc + c - 1), 0)))
            args.append(x)
    par = pl.BlockSpec((RW_N, 128), lambda c: (0, 0))
    outs = pl.pallas_call(
        _rwkv_prompt_kernel,
        grid=(nc + 1,),
        in_specs=in_specs + [par] * 5,
        out_specs=[pl.BlockSpec((NB, blk, 128), lambda c: (0, jnp.maximum(c - 1, 0), 0)),
                   pl.BlockSpec((NB, blk, 128), lambda c: (0, 0, 0)),
                   pl.BlockSpec((RW_N, RW_N, 128), lambda c: (0, 0, 0))],
        out_shape=[_sds((NB, LP * RW_Q, 128), f32), _sds((NB, NM * RW_Q, 128), f32),
                   _sds((RW_N, RW_N, 128), f32)],
        scratch_shapes=[pltpu.VMEM((RW_TS, RW_N, 128), f32)] * 7 + [pltpu.VMEM((RW_N, RW_N, 128), f32)]
                       + [pltpu.VMEM((RW_N, 128), f32)] * 5,
        name="rwkv_prompt",
        compiler_params=_cp(("arbitrary",)),
    )(*args, *params)
    return outs[0].reshape(ROWS_MAIN, RW_Q, 128), outs[1].reshape(NB * NM, D), outs[2]


def _rwkv_layer(h, norm_g, ffn_g, state, shift, j, mu, w_rkv, w0, w1, w2, a0, a1, a2, g1, g2, k_k, k_a, r_k,
                ln_w, ln_b, w_o):
    *xmix, shift_p, shift_s = _rwkv_premix(h, norm_g, mu[j], shift[j])
    tn = 1024
    common = dict(K=D, tm=TM_BIG, tn=tn, n_row_tiles=ROWS // TM_BIG, n_col_tiles=D // tn, prologue=_pro_id,
                  epilogue=lambda accs, cvs, rms: (accs[0],))
    o = (_sds((ROWS, RW_Q, 128), f32), 0, 0)
    r, k, v = [_mm(name="rwkv_rkv", xs=[(xmix[n], 0)], ws=[(w_rkv, (j, n), 0)], outs=[o], **common)[0]
               for n in range(3)]

    def padc(w):
        return jnp.pad(w, ((0, 0), (0, LORA_PAD - w.shape[1])))

    def padr(w):
        return jnp.pad(w, ((0, LORA_PAD - w.shape[0]), (0, 0)))

    dec, a, g = _rwkv_lora(xmix[3], xmix[4], xmix[5], padc(w1[j]), padr(w2[j]), padc(a1[j]), padr(a2[j]),
                           g1[j], g2[j], w0[j].reshape(1, D), a0[j].reshape(1, D))

    def prompt_param(p):
        pt = p.reshape(RW_Q, 2, RW_N).transpose(2, 1, 0)
        return jnp.broadcast_to(pt[:, :, None, :], (RW_N, 2, NB, RW_Q)).reshape(RW_N, 128)

    head_params = (k_k[j], k_a[j], r_k[j].reshape(D), ln_w[j], ln_b[j])
    z_main, z_meta, sT = _rwkv_prompt(r, k, v, dec, a, [prompt_param(p) for p in head_params])
    s_p = sT.reshape(RW_N, RW_N, 2, NB, RW_Q).transpose(3, 4, 2, 1, 0).reshape(NB, RW_H, RW_N, RW_N)
    s_s, z_s = _rwkv_sample(state, j, r, k, v, dec, a, head_params)

    z_aux = _aux_rows(z_s, z_meta)
    h, xn_f = _mixer_out(h, z_main, z_aux, [g], w_o, (j,), ffn_g, _pro_mul)
    return (h, xn_f), s_p, s_s, shift_p, shift_s


def kernel(x_prompt, x_sample, state_hgrn, state_lru_h, state_lru_conv, state_rwkv, state_rwkv_shift, meta_tokens, norm_mix, norm_ffn, norm_final, hgrn_w_in, hgrn_lb_logits, hgrn_norm, hgrn_w_out, lru_w_in, lru_conv_w, lru_conv_b, lru_gate_w, lru_gate_b, lru_lambda, lru_w_out, rwkv_mu, rwkv_w_rkv, rwkv_w0, rwkv_w1, rwkv_w2, rwkv_a0, rwkv_a1, rwkv_a2, rwkv_g1, rwkv_g2, rwkv_k_k, rwkv_k_a, rwkv_r_k, rwkv_ln_w, rwkv_ln_b, rwkv_w_o, ffn_w_in, ffn_w_out):
    depth = norm_mix.shape[0]
    x_aux = _aux_rows(x_sample.reshape(NS, D), jnp.tile(meta_tokens, (NB, 1)))
    h, xn0 = _assemble(x_prompt.reshape(ROWS_MAIN, D), x_aux, norm_mix[0].reshape(1, D))
    hg_p, lh_p, lh_s, lc_p, lc_s, rw_p, rw_s, rs_p, rs_s = [[] for _ in range(9)]
    hg_s = []
    for i in range(depth):
        m, j = i % 3, i // 3
        ng = norm_mix[i].reshape(1, D)
        fg = norm_ffn[i].reshape(1, D)
        if m == 0:
            hx, sp, ss = _hgrn_layer(h, ng, fg, state_hgrn, hg_s, j, hgrn_w_in, hgrn_lb_logits, hgrn_norm,
                                     hgrn_w_out, xn=xn0 if i == 0 else None)
            hg_p.append(sp)
            hg_s.append(ss)
        elif m == 1:
            hx, hp_, hs_, cp_, cs_ = _lru_layer(h, ng, fg, state_lru_h, state_lru_conv, j, lru_w_in,
                                                lru_conv_w, lru_conv_b, lru_gate_w, lru_gate_b, lru_lambda,
                                                lru_w_out)
            lh_p.append(hp_)
            lh_s.append(hs_)
            lc_p.append(cp_)
            lc_s.append(cs_)
        else:
            hx, sp, ss, shp, shs = _rwkv_layer(h, ng, fg, state_rwkv, state_rwkv_shift, j, rwkv_mu, rwkv_w_rkv,
                                               rwkv_w0, rwkv_w1, rwkv_w2, rwkv_a0, rwkv_a1, rwkv_a2, rwkv_g1,
                                               rwkv_g2, rwkv_k_k, rwkv_k_a, rwkv_r_k, rwkv_ln_w, rwkv_ln_b,
                                               rwkv_w_o)
            rw_p.append(sp)
            rw_s.append(ss)
            rs_p.append(shp)
            rs_s.append(shs)
        h = _ffn(*hx, ffn_w_in, ffn_w_out, i)
    nf = norm_final.reshape(1, D)
    y_main = _rmsnorm(h, nf, out_dtype=f32, tm=512, row_tile0=0, n_row_tiles=ROWS_MAIN // 512)
    y_aux = _rmsnorm(h, nf, out_dtype=f32, tm=AUX, row_tile0=ROWS_MAIN // AUX, n_row_tiles=1)
    y_prompt = y_main.reshape(NB, LP, D)
    y_sample = y_aux[:NS].reshape(NS, 1, D)
    def stack(xs):
        return xs[0][None] if len(xs) == 1 else jnp.stack(xs)

    return (y_prompt, y_sample, stack(hg_p), hg_s[-1], stack(lh_p), stack(lh_s),
            stack(lc_p), stack(lc_s), stack(rw_p), stack(rw_s), stack(rs_p), stack(rs_s))
```

```python
import functools

import jax
import jax.numpy as jnp
import numpy as np
from jax import lax
from jax.experimental import pallas as pl
from jax.experimental.pallas import tpu as pltpu

f32 = jnp.float32
bf16 = jnp.bfloat16

D = 2048
NB = 4
LP = 2048
NM = 16
NS = 128
ROWS_MAIN = NB * LP
ROW_S = ROWS_MAIN
ROW_M = ROW_S + NS
AUX = 256
ROWS = ROWS_MAIN + AUX
EPS = 1e-6
HG_H, HG_K = 16, 128
HG_C = 64
HG_UNROLL = 4
GATE_EXP_CLIP = 60.0
LRU_NBLK, LRU_BW = 8, 256
LRU_C = 8.0
LRU_T = 512
RW_H, RW_N = 32, 64
RW_LN_EPS = 64e-5
RW_TS = 16
PM_T = 256
LORA_PAD = 128
D_FF = 5632
TM_BIG = 1408
TM_MID = 768
VMEM_LIMIT = 56 * 1024 * 1024


def _cp(sem):
    return pltpu.CompilerParams(dimension_semantics=sem, vmem_limit_bytes=VMEM_LIMIT)


def _softplus(x):
    return jnp.maximum(x, 0.0) + jnp.log(1.0 + jnp.exp(-jnp.abs(x)))


def _norm_kernel(h_ref, g_ref, o_ref):
    x = h_ref[...]
    ms = jnp.mean(x * x, axis=-1, keepdims=True)
    o_ref[...] = (x * lax.rsqrt(ms + EPS) * g_ref[...]).astype(o_ref.dtype)


def _rmsnorm(h, g, *, out_dtype, tm, row_tile0, n_row_tiles):
    return pl.pallas_call(
        _norm_kernel,
        grid=(n_row_tiles,),
        in_specs=[pl.BlockSpec((tm, D), lambda i: (i + row_tile0, 0)),
                  pl.BlockSpec((1, D), lambda i: (0, 0))],
        out_specs=pl.BlockSpec((tm, D), lambda i: (i, 0)),
        out_shape=jax.ShapeDtypeStruct((n_row_tiles * tm, D), out_dtype),
        name="rmsnorm",
        compiler_params=_cp(("arbitrary",)),
    )(h, g)


def _assemble_kernel(xm_ref, xa_ref, g_ref, h_ref, xn_ref):
    def emit(x):
        h_ref[...] = x
        ms = jnp.mean(x * x, axis=-1, keepdims=True)
        xn_ref[...] = (x * lax.rsqrt(ms + EPS) * g_ref[...]).astype(xn_ref.dtype)

    @pl.when(pl.program_id(0) < ROWS_MAIN // AUX)
    def _main():
        emit(xm_ref[...])

    @pl.when(pl.program_id(0) == ROWS_MAIN // AUX)
    def _aux():
        emit(xa_ref[...])


def _assemble(x_main, x_aux, g):
    nm = ROWS_MAIN // AUX
    return pl.pallas_call(
        _assemble_kernel,
        grid=(nm + 1,),
        in_specs=[pl.BlockSpec((AUX, D), lambda i: (jnp.minimum(i, nm - 1), 0)),
                  pl.BlockSpec((AUX, D), lambda i: (0, 0)),
                  pl.BlockSpec((1, D), lambda i: (0, 0))],
        out_specs=[pl.BlockSpec((AUX, D), lambda i: (i, 0)), pl.BlockSpec((AUX, D), lambda i: (i, 0))],
        out_shape=[_sds((ROWS, D), f32), _sds((ROWS, D), bf16)],
        name="assemble",
        compiler_params=_cp(("arbitrary",)),
    )(x_main, x_aux, g)


def _mm_body(*refs, n_x, n_w, n_kv, n_cv, n_rm, n_out, prologue, epilogue, row_splits):
    p = 0
    x_refs = refs[p:p + n_x]; p += n_x
    w_refs = refs[p:p + n_w]; p += n_w
    kv_refs = refs[p:p + n_kv]; p += n_kv
    cv_refs = refs[p:p + n_cv]; p += n_cv
    rm_refs = refs[p:p + n_rm]; p += n_rm
    out_refs = refs[p:p + n_out]; p += n_out
    w_scr = refs[p:p + n_w]

    @pl.when(pl.program_id(1) == 0)
    def _cast_weights():
        for w, s in zip(w_refs, w_scr):
            s[...] = w[...].astype(bf16)

    def flat(v):
        return v.reshape(v.shape[0], v.shape[1] * v.shape[2]) if v.ndim == 3 else v

    tm = out_refs[0].shape[0]
    sub = tm // row_splits
    for s0 in range(0, tm, sub):
        rows = slice(s0, s0 + sub)
        x = prologue([flat(r[rows]) for r in x_refs], [r[...] for r in kv_refs])
        accs = [jnp.dot(x, s[...], preferred_element_type=f32) for s in w_scr]
        res = epilogue(accs, [r[...] for r in cv_refs], [r[rows] for r in rm_refs])
        for o, r in zip(out_refs, res):
            o[rows] = r.astype(o.dtype).reshape((sub,) + o.shape[1:])


def _mm(*, name="proj", xs, ws, kvecs=(), cvecs=(), rmats=(), outs, K, tm, tn, n_row_tiles, n_col_tiles,
        prologue, epilogue, aliases=None, row_splits=1):
    in_specs, args = [], []
    for a, r0 in xs:
        if a.ndim == 3:
            in_specs.append(pl.BlockSpec((tm, K // 128, 128), lambda j, i, r0=r0: (i + r0, 0, 0)))
        else:
            in_specs.append(pl.BlockSpec((tm, K), lambda j, i, r0=r0: (i + r0, 0)))
        args.append(a)
    for a, lead, c0 in ws:
        nl = len(lead)
        in_specs.append(pl.BlockSpec((None,) * nl + (K, tn), lambda j, i, lead=lead, c0=c0: lead + (0, j + c0)))
        args.append(a)
    for a in kvecs:
        in_specs.append(pl.BlockSpec(a.shape, lambda j, i: (0, 0)))
        args.append(a)
    for a, c0 in cvecs:
        in_specs.append(pl.BlockSpec((a.shape[0], tn), lambda j, i, c0=c0: (0, j + c0)))
        args.append(a)
    for a, r0, c0 in rmats:
        in_specs.append(pl.BlockSpec((tm, tn), lambda j, i, r0=r0, c0=c0: (i + r0, j + c0)))
        args.append(a)
    out_specs = [pl.BlockSpec((tm, tn // 128, 128), lambda j, i, r0=r0, c0=c0: (i + r0, j + c0, 0))
                 if len(s.shape) == 3 else
                 pl.BlockSpec((tm, tn), lambda j, i, r0=r0, c0=c0: (i + r0, j + c0)) for s, r0, c0 in outs]
    body = functools.partial(_mm_body, n_x=len(xs), n_w=len(ws), n_kv=len(kvecs), n_cv=len(cvecs),
                             n_rm=len(rmats), n_out=len(outs), prologue=prologue, epilogue=epilogue,
                             row_splits=row_splits)
    res = pl.pallas_call(
        body,
        grid=(n_col_tiles, n_row_tiles),
        in_specs=in_specs,
        out_specs=out_specs,
        out_shape=[s for s, _, _ in outs],
        scratch_shapes=[pltpu.VMEM((K, tn), bf16) for _ in ws],
        input_output_aliases=aliases or {},
        name=name,
        compiler_params=_cp(("arbitrary", "arbitrary")),
    )(*args)
    return res


def _pro_id(xs, kvs):
    return xs[0]


def _pro_mul(xs, kvs):
    return (xs[0] * xs[1]).astype(bf16)


def _sds(shape, dtype):
    return jax.ShapeDtypeStruct(shape, dtype)


def _mixer_out_kernel(*refs, n_u, prologue):
    xm_ref, xa_ref = refs[:2]
    u_refs = refs[2:2 + n_u]
    w_ref, h_ref, g_ref, ho_ref, xn_ref, w_s = refs[2 + n_u:]
    i = pl.program_id(0)

    @pl.when(i == 0)
    def _cast():
        w_s[...] = w_ref[...].astype(bf16)

    def run(x_ref):
        x = x_ref[...]
        if x.ndim == 3:
            x = x.reshape(x.shape[0], x.shape[1] * x.shape[2])
        hn = h_ref[...] + jnp.dot(prologue([x] + [u[...] for u in u_refs], []), w_s[...],
                                  preferred_element_type=f32)
        ho_ref[...] = hn
        ms = jnp.mean(hn * hn, axis=-1, keepdims=True)
        xn_ref[...] = (hn * lax.rsqrt(ms + EPS) * g_ref[...]).astype(xn_ref.dtype)

    @pl.when(i < ROWS_MAIN // AUX)
    def _main():
        run(xm_ref)

    @pl.when(i == ROWS_MAIN // AUX)
    def _aux():
        run(xa_ref)


def _mixer_out(h, x_main, x_aux, unified, w, lead, norm_g, prologue):
    nm = ROWS_MAIN // AUX
    def xspec(a, imap):
        if a.ndim == 3:
            return pl.BlockSpec((AUX,) + a.shape[1:], lambda i: (imap(i), 0, 0))
        return pl.BlockSpec((AUX, a.shape[1]), lambda i: (imap(i), 0))
    nl = len(lead)
    row = pl.BlockSpec((AUX, D), lambda i: (i, 0))
    in_specs = ([xspec(x_main, lambda i: jnp.minimum(i, nm - 1)), xspec(x_aux, lambda i: 0)]
                + [xspec(u, lambda i: i) for u in unified]
                + [pl.BlockSpec((None,) * nl + (D, D), lambda i: lead + (0, 0), pipeline_mode=pl.Buffered(1)),
                   row, pl.BlockSpec((1, D), lambda i: (0, 0))])
    return pl.pallas_call(
        functools.partial(_mixer_out_kernel, n_u=len(unified), prologue=prologue),
        grid=(nm + 1,),
        in_specs=in_specs,
        out_specs=[row, row],
        out_shape=[_sds((ROWS, D), f32), _sds((ROWS, D), bf16)],
        scratch_shapes=[pltpu.VMEM((D, D), bf16)],
        input_output_aliases={3 + len(unified): 0},
        name="mixer_out",
        compiler_params=_cp(("arbitrary",)),
    )(x_main, x_aux, *unified, w, h, norm_g)


def _ffn(h, xn, w_in, w_out, layer):
    tn = 512
    nct = D_FF // tn
    act = _mm(name="ffn_in", xs=[(xn, 0)], ws=[(w_in, (layer,), 0), (w_in, (layer,), nct)],
              outs=[(_sds((ROWS, D_FF), bf16), 0, 0)],
              K=D, tm=TM_BIG, tn=tn, n_row_tiles=ROWS // TM_BIG, n_col_tiles=nct,
              prologue=_pro_id, row_splits=8,
              epilogue=lambda accs, cvs, rms: (jax.nn.silu(accs[0]) * accs[1],))[0]
    h = _mm(name="ffn_out", xs=[(act, 0)], ws=[(w_out, (layer,), 0)], rmats=[(h, 0, 0)],
            outs=[(_sds((ROWS, D), f32), 0, 0)],
            K=D_FF, tm=TM_MID, tn=tn, n_row_tiles=ROWS // TM_MID, n_col_tiles=D // tn,
            prologue=_pro_id, epilogue=lambda accs, cvs, rms: (rms[0] + accs[0],),
            aliases={2: 0})[0]
    return h


def _hgrn_lower_bound(logits, j):
    m = jnp.max(logits, axis=0, keepdims=True)
    e = jnp.exp(logits - m)
    p = e / jnp.sum(e, axis=0, keepdims=True)
    cs = p[0:1]
    for r in range(1, j + 1):
        cs = cs + p[r:r + 1]
    return jnp.clip(cs - p[0:1], 0.0, 1.0)


def _hgrn_in(xn, w_in, lb_logits, j):
    tn = 1024
    nct = D // tn
    common = dict(K=D, tm=TM_BIG, tn=tn, n_row_tiles=ROWS // TM_BIG, n_col_tiles=nct, prologue=_pro_id,
                  row_splits=4)
    o = (_sds((ROWS, D), f32), 0, 0)
    q = _mm(xs=[(xn, 0)], ws=[(w_in, (j,), 0)], outs=[o],
            epilogue=lambda accs, cvs, rms: (jax.nn.silu(accs[0]),), **common)[0]

    def f_epi(accs, cvs, rms):
        f = accs[0]
        lb = _hgrn_lower_bound(cvs[0], j)
        t = jnp.exp(-jnp.abs(f))
        log_f = (-(jnp.maximum(-f, 0.0) + jnp.log(1.0 + t))
                 + jnp.log(1.0 + lb * jnp.exp(jnp.minimum(-f, GATE_EXP_CLIP))))
        k = (1.0 - lb) * (jnp.where(f >= 0.0, t, 1.0) / (1.0 + t))
        return log_f, k

    log_f, k = _mm(xs=[(xn, 0)], ws=[(w_in, (j,), nct)], cvecs=[(lb_logits, 0)], outs=[o, o],
                   epilogue=f_epi, **{**common, "tm": TM_MID, "n_row_tiles": ROWS // TM_MID, "row_splits": 3})
    v = _mm(xs=[(xn, 0)], ws=[(w_in, (j,), 2 * nct)], outs=[o],
            epilogue=lambda accs, cvs, rms: (accs[0],), **common)[0]
    g = _mm(xs=[(xn, 0)], ws=[(w_in, (j,), 3 * nct)], outs=[o],
            epilogue=lambda accs, cvs, rms: (jax.nn.silu(accs[0]),), **common)[0]
    return q, k, v, log_f, g


def _split3(x):
    hi = x.astype(bf16)
    r = x - hi.astype(f32)
    mid = r.astype(bf16)
    lo = (r - mid.astype(f32)).astype(bf16)
    return hi, mid, lo


def _gla_consts(T):
    t = np.arange(T)[:, None]
    w = np.arange(T)[None, :]
    mats = [w <= t, w > t]
    masks = []
    s = T // 2
    while s >= 1:
        piece = t // (2 * s)
        lower = (t % (2 * s)) >= s
        ref = piece * 2 * s + s
        mats.append(np.where(lower, (w > ref) & (w <= t), (w > t) & (w <= ref)))
        upiece = w // (2 * s)
        ulower = (w % (2 * s)) >= s
        masks.append((piece == upiece) & lower & ~ulower)
        s //= 2
    masks.append(t == w)
    m = np.concatenate(mats, 0)
    eye = np.eye(HG_UNROLL, dtype=bool)
    gmasks = np.stack([np.kron(eye, pm) for pm in masks])
    return (jnp.asarray(np.concatenate([m, m, m], 1), dtype=bf16), jnp.asarray(gmasks, dtype=f32))


def _gla_block(q_ref, k_ref, v_ref, f_ref, g_ref, gn_ref, dm_ref, pm_ref, o_ref, st_ref, e_s, ql_s, kl_s, T):
    nlev = T.bit_length() - 1
    dn_t = (((1,), (1,)), ((), ()))
    dn_l = (((0,), (0,)), ((), ()))
    def factors(ls):
        e_s[0:(nlev + 2) * T, ls] = jnp.dot(dm_ref[...], jnp.concatenate(_split3(f_ref[:, ls]), axis=0),
                                            preferred_element_type=f32)
        q = q_ref[:, ls]
        k = k_ref[:, ls]
        ql_s[0, 0:T, ls] = (q * jnp.exp(e_s[0:T, ls])).astype(bf16)
        kl_s[0, 0:T, ls] = (k * jnp.exp(e_s[T:2 * T, ls])).astype(bf16)
        for l in range(nlev):
            ex = jnp.exp(e_s[(l + 2) * T:(l + 3) * T, ls])
            ql_s[l + 1, 0:T, ls] = (q * ex).astype(bf16)
            kl_s[l + 1, 0:T, ls] = (k * ex).astype(bf16)
        ql_s[nlev + 1, 0:T, ls] = q.astype(bf16)
        kl_s[nlev + 1, 0:T, ls] = k.astype(bf16)

    factors(slice(0, D))
    for h0 in range(0, HG_H, HG_UNROLL):
        sls = [slice(h * HG_K, (h + 1) * HG_K) for h in range(h0, h0 + HG_UNROLL)]
        att = None
        for l in range(nlev + 1):
            a_l = lax.dot_general(jnp.concatenate([ql_s[l + 1, 0:T, sl] for sl in sls], axis=0),
                                  jnp.concatenate([kl_s[l + 1, 0:T, sl] for sl in sls], axis=0),
                                  dn_t, preferred_element_type=f32) * pm_ref[l]
            att = a_l if att is None else att + a_l
        vbs = [v_ref[:, sl].astype(bf16) for sl in sls]
        o_intra = jnp.dot(att.astype(bf16), jnp.concatenate(vbs, axis=0), preferred_element_type=f32)
        for p in range(0, HG_UNROLL, 2):
            ha, hb = h0 + p, h0 + p + 1
            sla, slb = sls[p], sls[p + 1]
            st2 = jnp.concatenate([st_ref[ha], st_ref[hb]], axis=0)
            oc = lax.dot_general(jnp.concatenate([ql_s[0, 0:T, sla], ql_s[0, 0:T, slb]], axis=0),
                                 st2.astype(bf16), dn_t, preferred_element_type=f32)
            up = lax.dot_general(jnp.concatenate([vbs[p], vbs[p + 1]], axis=1),
                                 jnp.concatenate([kl_s[0, 0:T, sla], kl_s[0, 0:T, slb]], axis=1),
                                 dn_l, preferred_element_type=f32)
            for i, (h, sl) in enumerate(((ha, sla), (hb, slb))):
                o = (oc[i * T:(i + 1) * T, i * HG_K:(i + 1) * HG_K]
                     + o_intra[(p + i) * T:(p + i + 1) * T])
                st_ref[h] = (st2[i * HG_K:(i + 1) * HG_K] * jnp.exp(e_s[T - 1:T, sl])
                             + up[i * HG_K:(i + 1) * HG_K, i * HG_K:(i + 1) * HG_K])
                ms = jnp.mean(o * o, axis=-1, keepdims=True)
                on = o * lax.rsqrt(ms + EPS) * gn_ref[:, sl]
                o_ref[:, sl] = (on * g_ref[:, sl]).astype(o_ref.dtype)


def _gla_kernel(qm, km, vm, fm, gm, qa, ka, va, fa, ga, gn_ref, dmm, pmm, dma, pma,
                om_ref, oa_ref, s_ref, st_ref, e_s, ql_s, kl_s):
    c = pl.program_id(1)

    @pl.when(c == 0)
    def _meta():
        st_ref[...] = jnp.zeros_like(st_ref)
        _gla_block(qa, ka, va, fa, ga, gn_ref, dma, pma, oa_ref, st_ref, e_s, ql_s, kl_s, NM)

    @pl.when(c > 0)
    def _main():
        _gla_block(qm, km, vm, fm, gm, gn_ref, dmm, pmm, om_ref, st_ref, e_s, ql_s, kl_s, HG_C)

    @pl.when(c == pl.num_programs(1) - 1)
    def _final():
        for h in range(HG_H):
            s_ref[0, h] = st_ref[h].T


def _hgrn_prompt(q, k, v, log_f, g, gn):
    nc = LP // HG_C
    main_spec = pl.BlockSpec((HG_C, D), lambda b, c: (b * nc + jnp.maximum(c - 1, 0), 0))
    meta_spec = pl.BlockSpec((NM, D), lambda b, c: (ROW_M // NM + b, 0))
    arrs = [q, k, v, log_f, g]
    consts = [*_gla_consts(HG_C), *_gla_consts(NM)]
    nslot = HG_C.bit_length() + 1
    return pl.pallas_call(
        _gla_kernel,
        grid=(NB, nc + 1),
        in_specs=[main_spec] * 5 + [meta_spec] * 5 + [pl.BlockSpec((1, D), lambda b, c: (0, 0))]
                 + [pl.BlockSpec(a.shape, lambda b, c, n=a.ndim: (0,) * n) for a in consts],
        out_specs=[pl.BlockSpec((HG_C, D), lambda b, c: (b * nc + jnp.maximum(c - 1, 0), 0)),
                   pl.BlockSpec((NM, D), lambda b, c: (b, 0)),
                   pl.BlockSpec((1, HG_H, HG_K, HG_K), lambda b, c: (b, 0, 0, 0))],
        out_shape=[_sds((ROWS_MAIN, D), bf16), _sds((NB * NM, D), bf16), _sds((NB, HG_H, HG_K, HG_K), f32)],
        scratch_shapes=[pltpu.VMEM((HG_H, HG_K, HG_K), f32),
                        pltpu.VMEM((nslot * HG_C, D), f32),
                        pltpu.VMEM((nslot, HG_C, D), bf16),
                        pltpu.VMEM((nslot, HG_C, D), bf16)],
        name="hgrn_prompt",
        compiler_params=_cp(("arbitrary", "arbitrary")),
    )(*arrs, *arrs, gn, *consts)


HG_BT = 8
HG_SB = 4


def _hgrn_dec_kernel(s_ref, qT, kT, fT, v_ref, g_ref, gn_ref, e3_ref, *rest, stack_slot):
    so_ref, o_ref, qb_s, kb_s, gb_s = rest[-5:]
    if stack_slot is not None:
        for m, e_ref in enumerate(rest[:-5]):
            so_ref[m] = e_ref[...]
    row0 = pl.program_id(1) * HG_SB

    def spread(m, dst):
        dst[...] = jnp.dot(jnp.concatenate(_split3(m), axis=1), e3_ref[...], preferred_element_type=f32)

    def per_b(bb, carry):
        spread(qT[bb], qb_s)
        spread(kT[bb], kb_s)
        spread(jnp.exp(fT[bb]), gb_s)
        row = pl.ds(row0 + bb, 1)
        vall = v_ref[row, :]
        outs = []
        for h in range(HG_H):
            hs = slice(h * HG_K, (h + 1) * HG_K)
            sn = gb_s[:, hs] * s_ref[bb, h] + kb_s[:, hs] * vall[:, hs]
            if stack_slot is None:
                so_ref[bb, h] = sn
            else:
                so_ref[stack_slot, bb, h] = sn
            o = jnp.sum(qb_s[:, hs] * sn, axis=0, keepdims=True)
            ms = jnp.mean(o * o, axis=-1, keepdims=True)
            outs.append(o * lax.rsqrt(ms + EPS))
        on = jnp.concatenate(outs, axis=1) * gn_ref[...]
        o_ref[row, :] = on * g_ref[row, :]
        return carry

    lax.fori_loop(0, HG_SB, per_b, 0)


def _hgrn_sample(state, j, q, k, v, log_f, g, gn, earlier):
    def colform(x):
        return x[ROW_S:ROW_S + NS].reshape(NS, HG_H, HG_K).transpose(0, 2, 1)

    n_a = state.shape[0]
    last = j == n_a - 1
    head_of_lane = np.arange(D) // HG_K
    sel = (np.arange(HG_H)[:, None] == head_of_lane[None, :])
    e3 = jnp.asarray(np.concatenate([sel, sel, sel], axis=0), dtype=bf16)
    nsb = HG_BT // HG_SB
    col_spec = pl.BlockSpec((HG_SB, HG_K, HG_H), lambda i, s: (i * nsb + s, 0, 0))
    row_spec = pl.BlockSpec((HG_BT, D), lambda i, s: (ROW_S // HG_BT + i, 0))
    new_spec = pl.BlockSpec((HG_SB, HG_H, HG_K, HG_K), lambda i, s: (i * nsb + s, 0, 0, 0))
    in_specs = [pl.BlockSpec((None, HG_SB, HG_H, HG_K, HG_K), lambda i, s: (j, i * nsb + s, 0, 0, 0)),
                col_spec, col_spec, col_spec, row_spec, row_spec,
                pl.BlockSpec((1, D), lambda i, s: (0, 0)), pl.BlockSpec((3 * HG_H, D), lambda i, s: (0, 0))]
    args = [state, colform(q), colform(k), colform(log_f), v, g, gn, e3]
    if last:
        in_specs += [new_spec] * len(earlier)
        args += list(earlier)
        st_out = pl.BlockSpec((n_a, HG_SB, HG_H, HG_K, HG_K), lambda i, s: (0, i * nsb + s, 0, 0, 0))
        st_shape = _sds(state.shape, f32)
    else:
        st_out, st_shape = new_spec, _sds(state.shape[1:], f32)
    return pl.pallas_call(
        functools.partial(_hgrn_dec_kernel, stack_slot=j if last else None),
        grid=(NS // HG_BT, nsb),
        in_specs=in_specs,
        out_specs=[st_out, pl.BlockSpec((HG_BT, D), lambda i, s: (i, 0))],
        out_shape=[st_shape, _sds((NS, D), f32)],
        scratch_shapes=[pltpu.VMEM((HG_K, D), f32)] * 3,
        name="hgrn_decode",
        compiler_params=_cp(("arbitrary", "arbitrary")),
    )(*args)


def _aux_rows(sample_rows, meta_rows):
    pad = jnp.zeros((AUX - NS - NB * NM, sample_rows.shape[1]), sample_rows.dtype)
    return jnp.concatenate([sample_rows, meta_rows.astype(sample_rows.dtype), pad], axis=0)


def _hgrn_layer(h, norm_g, ffn_g, state, earlier, j, w_in, lb_logits, gnorm, w_out, xn=None):
    if xn is None:
        xn = _rmsnorm(h, norm_g, out_dtype=bf16, tm=TM_MID, row_tile0=0, n_row_tiles=ROWS // TM_MID)
    q, k, v, log_f, g = _hgrn_in(xn, w_in, lb_logits, j)
    gn = gnorm[j].reshape(1, D)
    o_main, o_meta, s_p = _hgrn_prompt(q, k, v, log_f, g, gn)
    s_s, o_s = _hgrn_sample(state, j, q, k, v, log_f, g, gn, earlier)
    o_aux = _aux_rows(o_s.astype(bf16), o_meta)
    h, xn_f = _mixer_out(h, o_main, o_aux, [], w_out, (j,), ffn_g, _pro_id)
    return (h, xn_f), s_p, s_s


def _lru_in(xn, w_in, j):
    tn = 1024
    nct = D // tn
    common = dict(K=D, tm=TM_BIG, tn=tn, n_row_tiles=ROWS // TM_BIG, n_col_tiles=nct, prologue=_pro_id,
                  row_splits=4)
    o = (_sds((ROWS, D), f32), 0, 0)
    y = _mm(xs=[(xn, 0)], ws=[(w_in, (j,), 0)], outs=[o],
            epilogue=lambda accs, cvs, rms: (jax.nn.gelu(accs[0], approximate=True),), **common)[0]
    x = _mm(xs=[(xn, 0)], ws=[(w_in, (j,), nct)], outs=[o],
            epilogue=lambda accs, cvs, rms: (accs[0],), **common)[0]
    return y, x


def _lru_gates(xc, gw_ref, gb_ref, lam_ref, n):
    ls = slice(n * LRU_BW, (n + 1) * LRU_BW)
    xb = xc[:, ls].astype(bf16)
    r = jax.nn.sigmoid(jnp.dot(xb, gw_ref[0, n], preferred_element_type=f32) + gb_ref[0:1, ls])
    ig = jax.nn.sigmoid(jnp.dot(xb, gw_ref[1, n], preferred_element_type=f32) + gb_ref[1:2, ls])
    log_a = -LRU_C * r * _softplus(-lam_ref[:, ls])
    a = jnp.exp(log_a)
    mult = jnp.sqrt(jnp.maximum(1.0 - a * a, 0.0))
    return a, ig, mult


def _lru_conv(cb_ref, cw_ref, x0, x1, x2, x3):
    xc = cb_ref[...] + x0 * cw_ref[0:1, :]
    xc = xc + x1 * cw_ref[1:2, :]
    xc = xc + x2 * cw_ref[2:3, :]
    return xc + x3 * cw_ref[3:4, :]


def _lru_block(x_ref, y_ref, o_ref, cw_ref, cb_ref, gw_s, gb_ref, lam_ref, xbuf, a_s, b_s, hcar, T, first):
    xbuf[8:8 + T, :] = x_ref[...]
    xc = _lru_conv(cb_ref, cw_ref, xbuf[5:5 + T, :], xbuf[6:6 + T, :], xbuf[7:7 + T, :], xbuf[8:8 + T, :])
    xbuf[0:8, :] = xbuf[T:T + 8, :]
    row = lax.broadcasted_iota(jnp.int32, (T, 1), 0)
    for n in range(LRU_NBLK):
        ls = slice(n * LRU_BW, (n + 1) * LRU_BW)
        a, ig, mult = _lru_gates(xc, gw_s, gb_ref, lam_ref, n)
        if first:
            mult = jnp.where(row == 0, 1.0, mult)
        a_s[0:T, ls] = a
        b_s[0:T, ls] = xc[:, ls] * ig * mult

    def step(t, hprev):
        hnew = a_s[pl.ds(t, 1), :] * hprev + b_s[pl.ds(t, 1), :]
        b_s[pl.ds(t, 1), :] = hnew
        return hnew

    hcar[...] = lax.fori_loop(0, T, step, hcar[...], unroll=8)
    o_ref[...] = (b_s[0:T, :] * y_ref[...]).astype(o_ref.dtype)


def _lru_kernel(xm, ym, xa, ya, cw_ref, cb_ref, gw_ref, gb_ref, lam_ref,
                om_ref, oa_ref, hl_ref, cv_ref, gw_s, xbuf, a_s, b_s, hcar):
    c = pl.program_id(1)

    @pl.when((pl.program_id(0) == 0) & (c == 0))
    def _cast():
        gw_s[...] = gw_ref[...].astype(bf16)

    @pl.when(c == 0)
    def _meta():
        xbuf[0:8, :] = jnp.zeros((8, D), f32)
        hcar[...] = jnp.zeros_like(hcar)
        _lru_block(xa, ya, oa_ref, cw_ref, cb_ref, gw_s, gb_ref, lam_ref, xbuf, a_s, b_s, hcar, NM, True)

    @pl.when(c > 0)
    def _main():
        _lru_block(xm, ym, om_ref, cw_ref, cb_ref, gw_s, gb_ref, lam_ref, xbuf, a_s, b_s, hcar, LRU_T, False)

    @pl.when(c == pl.num_programs(1) - 1)
    def _final():
        hl_ref[0] = hcar[...]
        cv_ref[0] = xbuf[5:8, :]


def _lru_prompt(x, y, j, conv_w, conv_b, gate_w, gate_b, lam):
    nc = LP // LRU_T
    main_spec = pl.BlockSpec((LRU_T, D), lambda b, c: (b * nc + jnp.maximum(c - 1, 0), 0))
    meta_spec = pl.BlockSpec((NM, D), lambda b, c: (ROW_M // NM + b, 0))
    return pl.pallas_call(
        _lru_kernel,
        grid=(NB, nc + 1),
        in_specs=[main_spec, main_spec, meta_spec, meta_spec,
                  pl.BlockSpec((None, 4, D), lambda b, c: (j, 0, 0)),
                  pl.BlockSpec((1, D), lambda b, c: (j, 0)),
                  pl.BlockSpec((None, 2, LRU_NBLK, LRU_BW, LRU_BW), lambda b, c: (j, 0, 0, 0, 0)),
                  pl.BlockSpec((None, 2, D), lambda b, c: (j, 0, 0)),
                  pl.BlockSpec((1, D), lambda b, c: (j, 0))],
        out_specs=[pl.BlockSpec((LRU_T, D), lambda b, c: (b * nc + jnp.maximum(c - 1, 0), 0)),
                   pl.BlockSpec((NM, D), lambda b, c: (b, 0)),
                   pl.BlockSpec((1, 1, D), lambda b, c: (b, 0, 0)),
                   pl.BlockSpec((1, 3, D), lambda b, c: (b, 0, 0))],
        out_shape=[_sds((ROWS_MAIN, D), bf16), _sds((NB * NM, D), bf16),
                   _sds((NB, 1, D), f32), _sds((NB, 3, D), f32)],
        scratch_shapes=[pltpu.VMEM((2, LRU_NBLK, LRU_BW, LRU_BW), bf16),
                        pltpu.VMEM((LRU_T + 8, D), f32), pltpu.VMEM((LRU_T, D), f32),
                        pltpu.VMEM((LRU_T, D), f32), pltpu.VMEM((1, D), f32)],
        name="lru_prompt",
        compiler_params=_cp(("arbitrary", "arbitrary")),
    )(x, y, x, y, conv_w, conv_b, gate_w, gate_b, lam)


LRU_ST = 64


def _lru_dec_kernel(x_ref, y_ref, h0_ref, cbuf_ref, cw_ref, cb_ref, gw_ref, gb_ref, lam_ref,
                    o_ref, hn_ref, cn_ref, gw_s):
    gw_s[...] = gw_ref[...].astype(bf16)
    x = x_ref[...]
    b0 = cbuf_ref[:, 0, :]
    b1 = cbuf_ref[:, 1, :]
    b2 = cbuf_ref[:, 2, :]
    xc = _lru_conv(cb_ref, cw_ref, b0, b1, b2, x)
    cn_ref[:, 0, :] = b1
    cn_ref[:, 1, :] = b2
    cn_ref[:, 2, :] = x
    for n in range(LRU_NBLK):
        ls = slice(n * LRU_BW, (n + 1) * LRU_BW)
        a, ig, mult = _lru_gates(xc, gw_s, gb_ref, lam_ref, n)
        hn = a * h0_ref[:, ls] + xc[:, ls] * ig * mult
        hn_ref[:, ls] = hn
        o_ref[:, ls] = (hn * y_ref[:, ls]).astype(o_ref.dtype)


def _lru_sample(x, y, j, h0, cbuf, conv_w, conv_b, gate_w, gate_b, lam):
    row_spec = pl.BlockSpec((LRU_ST, D), lambda i: (ROW_S // LRU_ST + i, 0))
    return pl.pallas_call(
        _lru_dec_kernel,
        grid=(NS // LRU_ST,),
        in_specs=[row_spec, row_spec,
                  pl.BlockSpec((None, LRU_ST, D), lambda i: (j, i, 0)),
                  pl.BlockSpec((None, LRU_ST, 3, D), lambda i: (j, i, 0, 0)),
                  pl.BlockSpec((None, 4, D), lambda i: (j, 0, 0)),
                  pl.BlockSpec((1, D), lambda i: (j, 0)),
                  pl.BlockSpec((None, 2, LRU_NBLK, LRU_BW, LRU_BW), lambda i: (j, 0, 0, 0, 0)),
                  pl.BlockSpec((None, 2, D), lambda i: (j, 0, 0)),
                  pl.BlockSpec((1, D), lambda i: (j, 0))],
        out_specs=[pl.BlockSpec((LRU_ST, D), lambda i: (i, 0)),
                   pl.BlockSpec((LRU_ST, D), lambda i: (i, 0)),
                   pl.BlockSpec((LRU_ST, 3, D), lambda i: (i, 0, 0))],
        out_shape=[_sds((NS, D), bf16), _sds((NS, D), f32), _sds((NS, 3, D), f32)],
        scratch_shapes=[pltpu.VMEM((2, LRU_NBLK, LRU_BW, LRU_BW), bf16)],
        name="lru_decode",
        compiler_params=_cp(("arbitrary",)),
    )(x, y, h0, cbuf, conv_w, conv_b, gate_w, gate_b, lam)


def _lru_layer(h, norm_g, ffn_g, state_h, state_conv, j, w_in, conv_w, conv_b, gate_w, gate_b, lam, w_out):
    xn = _rmsnorm(h, norm_g, out_dtype=bf16, tm=TM_MID, row_tile0=0, n_row_tiles=ROWS // TM_MID)
    y, x = _lru_in(xn, w_in, j)
    o_main, o_meta, hl_p, cv_p = _lru_prompt(x, y, j, conv_w, conv_b, gate_w, gate_b, lam)
    o_s, hl_s, cv_s = _lru_sample(x, y, j, state_h, state_conv, conv_w, conv_b, gate_w, gate_b, lam)
    o_aux = _aux_rows(o_s, o_meta)
    h, xn_f = _mixer_out(h, o_main, o_aux, [], w_out, (j,), ffn_g, _pro_id)
    return (h, xn_f), hl_p.reshape(NB, D), hl_s, cv_p, cv_s


def _rwkv_premix_kernel(h_ref, g_ref, mu_ref, sh_ref, *refs):
    x_refs = refs[:6]
    sp_ref, ss_ref, xbuf, meta_last = refs[6:]
    i = pl.program_id(0)
    x = h_ref[...]
    ms = jnp.mean(x * x, axis=-1, keepdims=True)
    xn = x * lax.rsqrt(ms + EPS) * g_ref[...]
    xbuf[8:8 + PM_T, :] = xn
    row = lax.broadcasted_iota(jnp.int32, (PM_T, 1), 0)

    @pl.when(i == 0)
    def _aux():
        xbuf[7:8, :] = jnp.zeros((1, D), f32)
        ss_ref[...] = xn[0:NS]
        for b in range(NB):
            meta_last[b:b + 1, :] = xn[NS + b * NM + NM - 1:NS + b * NM + NM]

    @pl.when(i > 0)
    def _main():
        m = i - 1
        b = m // (LP // PM_T)

        @pl.when(m % (LP // PM_T) == 0)
        def _start():
            xbuf[7:8, :] = meta_last[pl.ds(b, 1), :]

        @pl.when(m % (LP // PM_T) == LP // PM_T - 1)
        def _end():
            sp_ref[pl.ds(b, 1), :] = xn[PM_T - 1:PM_T]

    shifted = xbuf[7:7 + PM_T, :]
    is_meta = (row >= NS) & (row < NS + NB * NM) & ((row - NS) % NM != 0)
    sh_pad = jnp.concatenate([sh_ref[...], jnp.zeros((PM_T - NS, D), f32)], axis=0)
    prev_aux = jnp.where(row < NS, sh_pad, jnp.where(is_meta, shifted, 0.0))
    prev = jnp.where(i == 0, prev_aux, shifted)
    dx = prev - xn
    for n in range(6):
        x_refs[n][...] = (xn + dx * mu_ref[n:n + 1, :]).astype(bf16)
    xbuf[7:8, :] = xn[PM_T - 1:PM_T]


def _rwkv_premix(h, norm_g, mu_j, shift_j):
    nt = ROWS // PM_T
    rows = lambda i: (jnp.where(i == 0, nt - 1, i - 1), 0)
    return pl.pallas_call(
        _rwkv_premix_kernel,
        grid=(nt,),
        in_specs=[pl.BlockSpec((PM_T, D), rows),
                  pl.BlockSpec((1, D), lambda i: (0, 0)),
                  pl.BlockSpec((6, D), lambda i: (0, 0)),
                  pl.BlockSpec((NS, D), lambda i: (0, 0))],
        out_specs=[pl.BlockSpec((PM_T, D), rows)] * 6 + [pl.BlockSpec((NB, D), lambda i: (0, 0)),
                                                         pl.BlockSpec((NS, D), lambda i: (0, 0))],
        out_shape=[_sds((ROWS, D), bf16)] * 6 + [_sds((NB, D), f32), _sds((NS, D), f32)],
        scratch_shapes=[pltpu.VMEM((PM_T + 8, D), f32), pltpu.VMEM((8, D), f32)],
        name="rwkv_premix",
        compiler_params=_cp(("arbitrary",)),
    )(h, norm_g, mu_j, shift_j)


def _rwkv_lora_kernel(xw_ref, xa_ref, xg_ref, w1_ref, w2_ref, a1_ref, a2_ref, g1_ref, g2_ref, w0_ref, a0_ref,
                      d_ref, a_ref, g_ref, w1_s, w2_s, a1_s, a2_s, g1_s, g2_s):
    @pl.when(pl.program_id(0) == 0)
    def _cast():
        for src, dst in ((w1_ref, w1_s), (w2_ref, w2_s), (a1_ref, a1_s), (a2_ref, a2_s), (g1_ref, g1_s),
                         (g2_ref, g2_s)):
            dst[...] = src[...].astype(bf16)

    def mm(u, w_s):
        return jnp.dot(u, w_s[...], preferred_element_type=f32)

    tw = jnp.tanh(mm(xw_ref[...], w1_s)).astype(bf16)
    x = w0_ref[...] + mm(tw, w2_s)
    d_ref[...] = jnp.exp(-np.float32(np.exp(-0.5)) * jax.nn.sigmoid(x)).reshape(d_ref.shape)
    ta = mm(xa_ref[...], a1_s).astype(bf16)
    a_ref[...] = jax.nn.sigmoid(a0_ref[...] + mm(ta, a2_s)).reshape(a_ref.shape)
    tg = jax.nn.sigmoid(mm(xg_ref[...], g1_s)).astype(bf16)
    g_ref[...] = mm(tg, g2_s)


def _rwkv_lora(xw, xa, xg, w1, w2, a1, a2, g1, g2, w0, a0):
    tm = 384
    full = lambda a: pl.BlockSpec(a.shape, lambda i: (0,) * a.ndim)
    row = pl.BlockSpec((tm, D), lambda i: (i, 0))
    row3 = pl.BlockSpec((tm, RW_Q, 128), lambda i: (i, 0, 0))
    ws = [w1, w2, a1, a2, g1, g2]
    return pl.pallas_call(
        _rwkv_lora_kernel,
        grid=(ROWS // tm,),
        in_specs=[row, row, row] + [full(a) for a in ws + [w0, a0]],
        out_specs=[row3, row3, row],
        out_shape=[_sds((ROWS, RW_Q, 128), f32)] * 2 + [_sds((ROWS, D), f32)],
        scratch_shapes=[pltpu.VMEM(a.shape, bf16) for a in ws],
        name="rwkv_lora",
        compiler_params=_cp(("arbitrary",)),
    )(xw, xa, xg, *ws, w0, a0)


def _rwkv_prep(t, r_s, k_s, d_s, a_s, tmp, kk_p, ka_p):
    kk_t, d_t, ka_t, k2_t, r_t = tmp
    k = k_s[t]
    a = a_s[t]
    kkr = k * kk_p[...]
    nrm = jnp.sqrt(jnp.sum(kkr * kkr, axis=0, keepdims=True))
    kk = kkr / jnp.maximum(nrm, 1e-12)
    kk_t[...] = kk
    d_t[...] = d_s[t]
    ka_t[...] = kk * a
    k2_t[...] = k * (1.0 + (a - 1.0) * ka_p[...])
    r_t[...] = r_s[t]


def _rwkv_body(v, s_s, tmp):
    kk_t, d_t, ka_t, k2_t, r_t = tmp
    sa = -(s_s[0] * kk_t[0:1, :])
    for j in range(1, RW_N):
        sa = sa - s_s[j] * kk_t[j:j + 1, :]
    y = None
    for j in range(RW_N):
        sn = s_s[j] * d_t[j:j + 1, :] + sa * ka_t[j:j + 1, :] + v * k2_t[j:j + 1, :]
        s_s[j] = sn
        yj = sn * r_t[j:j + 1, :]
        y = yj if y is None else y + yj
    return y


def _rwkv_tail(y, v, tmp, rk_p, lw_p, lb_p):
    _, _, _, k2_t, r_t = tmp
    mean = jnp.mean(y, axis=0, keepdims=True)
    yc = y - mean
    var = jnp.mean(yc * yc, axis=0, keepdims=True)
    yn = yc * lax.rsqrt(var + RW_LN_EPS) * lw_p[...] + lb_p[...]
    bonus = jnp.sum(r_t[...] * k2_t[...] * rk_p[...], axis=0, keepdims=True) * v
    return yn + bonus


def _rwkv_step(t, r_s, k_s, v_s, d_s, a_s, z_s, s_s, tmp, kk_p, ka_p, rk_p, lw_p, lb_p):
    _rwkv_prep(t, r_s, k_s, d_s, a_s, tmp, kk_p, ka_p)
    v = v_s[t]
    z_s[t] = _rwkv_tail(_rwkv_body(v, s_s, tmp), v, tmp, rk_p, lw_p, lb_p)


def _rwkv_dec_kernel(s_ref, r_ref, k_ref, v_ref, d_ref, a_ref, kk_p, ka_p, rk_p, lw_p, lb_p, so_ref, z_ref,
                     v_s, y_s):
    r = r_ref[0]
    k = k_ref[0]
    a = a_ref[0]
    d = d_ref[0]
    kkr = k * kk_p[0]
    nrm = jnp.sqrt(jnp.sum(kkr * kkr, axis=0, keepdims=True))
    kk = kkr / jnp.maximum(nrm, 1e-12)
    k2 = k * (1.0 + (a - 1.0) * ka_p[0])
    ka = kk * a
    v_s[...] = v_ref[0]
    for i in range(RW_N):
        s = s_ref[0, i]
        sa = -jnp.sum(s * kk, axis=0, keepdims=True)
        sn = s * d + sa * ka + v_s[i:i + 1, :] * k2
        so_ref[0, i] = sn
        y_s[i:i + 1, :] = jnp.sum(sn * r, axis=0, keepdims=True)
    y = y_s[...]
    mean = jnp.mean(y, axis=0, keepdims=True)
    yc = y - mean
    var = jnp.mean(yc * yc, axis=0, keepdims=True)
    yn = yc * lax.rsqrt(var + RW_LN_EPS) * lw_p[0] + lb_p[0]
    bonus = jnp.sum(r * k2 * rk_p[0], axis=0, keepdims=True) * v_s[...]
    z_ref[0] = yn + bonus


def _rwkv_sample(state, j, r, k, v, dec, a, head_params):
    def lanes(x):
        return x[ROW_S:ROW_S + NS].reshape(NS, D).T.reshape(RW_H, RW_N, NS)

    def param(p):
        return jnp.broadcast_to(p.reshape(RW_H, RW_N, 1), (RW_H, RW_N, NS))

    vec = pl.BlockSpec((1, RW_N, NS), lambda h: (h, 0, 0))
    st = pl.BlockSpec((1, RW_N, RW_N, NS), lambda h: (h, 0, 0, 0))
    s_new, z = pl.pallas_call(
        _rwkv_dec_kernel,
        grid=(RW_H,),
        in_specs=[st] + [vec] * 10,
        out_specs=[st, vec],
        out_shape=[_sds((RW_H, RW_N, RW_N, NS), f32), _sds((RW_H, RW_N, NS), f32)],
        scratch_shapes=[pltpu.VMEM((RW_N, NS), f32)] * 2,
        name="rwkv_decode",
        compiler_params=_cp(("arbitrary",)),
    )(state[j].transpose(1, 2, 3, 0), lanes(r), lanes(k), lanes(v), lanes(dec), lanes(a),
      *[param(p) for p in head_params])
    return s_new.transpose(3, 0, 1, 2), z.reshape(D, NS).T


RW_Q = D // 128


def _half_transpose(x):
    xt = x.T
    return jnp.concatenate([xt[0:RW_N], xt[RW_N:2 * RW_N]], axis=1)


def _rwkv_prompt_kernel(*refs):
    n_in = 5 * NB
    in_refs = refs[:n_in]
    params = refs[n_in:n_in + 5]
    zmain, zmeta, sf_ref = refs[n_in + 5:n_in + 8]
    scr = refs[n_in + 8:]
    seq, z_s, zo_s, s_s, tmp = scr[:5], scr[5], scr[6], scr[7], scr[8:]
    c = pl.program_id(0)

    @pl.when(c == 0)
    def _init():
        s_s[...] = jnp.zeros_like(s_s)
        z_s[...] = jnp.zeros_like(z_s)

    def load_token(t):
        rows = pl.ds(pl.multiple_of(t * RW_Q, RW_Q), RW_Q)
        for a in range(5):
            tile = jnp.concatenate([in_refs[a * NB + b][rows, :] for b in range(NB)], axis=0)
            seq[a][t] = _half_transpose(tile)

    load_token(0)

    def step(t, carry):
        tp = jnp.maximum(t - 1, 0)
        zo_s[tp] = _half_transpose(z_s[tp])
        _rwkv_step(t, *seq, z_s, s_s, tmp, *params)
        load_token(jnp.minimum(t + 1, RW_TS - 1))
        return carry

    lax.fori_loop(0, RW_TS, step, 0)
    zo_s[RW_TS - 1] = _half_transpose(z_s[RW_TS - 1])

    def write(z_ref):
        for t in range(RW_TS):
            for b in range(NB):
                z_ref[b, t * RW_Q:(t + 1) * RW_Q, :] = zo_s[t, b * RW_Q:(b + 1) * RW_Q, :]

    @pl.when(c == 0)
    def _write_meta():
        write(zmeta)

    @pl.when(c > 0)
    def _write_main():
        write(zmain)

    @pl.when(c == pl.num_programs(0) - 1)
    def _final():
        sf_ref[...] = s_s[...]


def _rwkv_prompt(r, k, v, dec, a, params):
    nc = LP // RW_TS
    blk = RW_TS * RW_Q
    arrs = [x.reshape(ROWS * RW_Q, 128) for x in (r, k, v, dec, a)]
    in_specs, args = [], []
    for x in arrs:
        for b in range(NB):
            in_specs.append(pl.BlockSpec(
                (blk, 128), lambda c, b=b: (jnp.where(c == 0, ROW_M // RW_TS + b, b * nc + c - 1), 0)))
            args.append(x)
    par = pl.BlockSpec((RW_N, 128), lambda c: (0, 0))
    outs = pl.pallas_call(
        _rwkv_prompt_kernel,
        grid=(nc + 1,),
        in_specs=in_specs + [par] * 5,
        out_specs=[pl.BlockSpec((NB, blk, 128), lambda c: (0, jnp.maximum(c - 1, 0), 0)),
                   pl.BlockSpec((NB, blk, 128), lambda c: (0, 0, 0)),
                   pl.BlockSpec((RW_N, RW_N, 128), lambda c: (0, 0, 0))],
        out_shape=[_sds((NB, LP * RW_Q, 128), f32), _sds((NB, NM * RW_Q, 128), f32),
                   _sds((RW_N, RW_N, 128), f32)],
        scratch_shapes=[pltpu.VMEM((RW_TS, RW_N, 128), f32)] * 7 + [pltpu.VMEM((RW_N, RW_N, 128), f32)]
                       + [pltpu.VMEM((RW_N, 128), f32)] * 5,
        name="rwkv_prompt",
        compiler_params=_cp(("arbitrary",)),
    )(*args, *params)
    return outs[0].reshape(ROWS_MAIN, RW_Q, 128), outs[1].reshape(NB * NM, D), outs[2]


def _rwkv_layer(h, norm_g, ffn_g, state, shift, j, mu, w_rkv, w0, w1, w2, a0, a1, a2, g1, g2, k_k, k_a, r_k,
                ln_w, ln_b, w_o):
    *xmix, shift_p, shift_s = _rwkv_premix(h, norm_g, mu[j], shift[j])
    tn = 1024
    common = dict(K=D, tm=TM_BIG, tn=tn, n_row_tiles=ROWS // TM_BIG, n_col_tiles=D // tn, prologue=_pro_id,
                  row_splits=4, epilogue=lambda accs, cvs, rms: (accs[0],))
    o = (_sds((ROWS, RW_Q, 128), f32), 0, 0)
    r, k, v = [_mm(name="rwkv_rkv", xs=[(xmix[n], 0)], ws=[(w_rkv, (j, n), 0)], outs=[o], **common)[0]
               for n in range(3)]

    def padc(w):
        return jnp.pad(w, ((0, 0), (0, LORA_PAD - w.shape[1])))

    def padr(w):
        return jnp.pad(w, ((0, LORA_PAD - w.shape[0]), (0, 0)))

    dec, a, g = _rwkv_lora(xmix[3], xmix[4], xmix[5], padc(w1[j]), padr(w2[j]), padc(a1[j]), padr(a2[j]),
                           g1[j], g2[j], w0[j].reshape(1, D), a0[j].reshape(1, D))

    def prompt_param(p):
        pt = p.reshape(RW_Q, 2, RW_N).transpose(2, 1, 0)
        return jnp.broadcast_to(pt[:, :, None, :], (RW_N, 2, NB, RW_Q)).reshape(RW_N, 128)

    head_params = (k_k[j], k_a[j], r_k[j].reshape(D), ln_w[j], ln_b[j])
    z_main, z_meta, sT = _rwkv_prompt(r, k, v, dec, a, [prompt_param(p) for p in head_params])
    s_p = sT.reshape(RW_N, RW_N, 2, NB, RW_Q).transpose(3, 4, 2, 1, 0).reshape(NB, RW_H, RW_N, RW_N)
    s_s, z_s = _rwkv_sample(state, j, r, k, v, dec, a, head_params)

    z_aux = _aux_rows(z_s, z_meta)
    h, xn_f = _mixer_out(h, z_main, z_aux, [g], w_o, (j,), ffn_g, _pro_mul)
    return (h, xn_f), s_p, s_s, shift_p, shift_s


def kernel(x_prompt, x_sample, state_hgrn, state_lru_h, state_lru_conv, state_rwkv, state_rwkv_shift, meta_tokens, norm_mix, norm_ffn, norm_final, hgrn_w_in, hgrn_lb_logits, hgrn_norm, hgrn_w_out, lru_w_in, lru_conv_w, lru_conv_b, lru_gate_w, lru_gate_b, lru_lambda, lru_w_out, rwkv_mu, rwkv_w_rkv, rwkv_w0, rwkv_w1, rwkv_w2, rwkv_a0, rwkv_a1, rwkv_a2, rwkv_g1, rwkv_g2, rwkv_k_k, rwkv_k_a, rwkv_r_k, rwkv_ln_w, rwkv_ln_b, rwkv_w_o, ffn_w_in, ffn_w_out):
    depth = norm_mix.shape[0]
    x_aux = _aux_rows(x_sample.reshape(NS, D), jnp.tile(meta_tokens, (NB, 1)))
    h, xn0 = _assemble(x_prompt.reshape(ROWS_MAIN, D), x_aux, norm_mix[0].reshape(1, D))
    hg_p, lh_p, lh_s, lc_p, lc_s, rw_p, rw_s, rs_p, rs_s = [[] for _ in range(9)]
    hg_s = []
    for i in range(depth):
        m, j = i % 3, i // 3
        ng = norm_mix[i].reshape(1, D)
        fg = norm_ffn[i].reshape(1, D)
        if m == 0:
            hx, sp, ss = _hgrn_layer(h, ng, fg, state_hgrn, hg_s, j, hgrn_w_in, hgrn_lb_logits, hgrn_norm,
                                     hgrn_w_out, xn=xn0 if i == 0 else None)
            hg_p.append(sp)
            hg_s.append(ss)
        elif m == 1:
            hx, hp_, hs_, cp_, cs_ = _lru_layer(h, ng, fg, state_lru_h, state_lru_conv, j, lru_w_in,
                                                lru_conv_w, lru_conv_b, lru_gate_w, lru_gate_b, lru_lambda,
                                                lru_w_out)
            lh_p.append(hp_)
            lh_s.append(hs_)
            lc_p.append(cp_)
            lc_s.append(cs_)
        else:
            hx, sp, ss, shp, shs = _rwkv_layer(h, ng, fg, state_rwkv, state_rwkv_shift, j, rwkv_mu, rwkv_w_rkv,
                                               rwkv_w0, rwkv_w1, rwkv_w2, rwkv_a0, rwkv_a1, rwkv_a2, rwkv_g1,
                                               rwkv_g2, rwkv_k_k, rwkv_k_a, rwkv_r_k, rwkv_ln_w, rwkv_ln_b,
                                               rwkv_w_o)
            rw_p.append(sp)
            rw_s.append(ss)
            rs_p.append(shp)
            rs_s.append(shs)
        h = _ffn(*hx, ffn_w_in, ffn_w_out, i)
    nf = norm_final.reshape(1, D)
    y_main = _rmsnorm(h, nf, out_dtype=f32, tm=512, row_tile0=0, n_row_tiles=ROWS_MAIN // 512)
    y_aux = _rmsnorm(h, nf, out_dtype=f32, tm=AUX, row_tile0=ROWS_MAIN // AUX, n_row_tiles=1)
    y_prompt = y_main.reshape(NB, LP, D)
    y_sample = y_aux[:NS].reshape(NS, 1, D)
    def stack(xs):
        return xs[0][None] if len(xs) == 1 else jnp.stack(xs)

    return (y_prompt, y_sample, stack(hg_p), hg_s[-1], stack(lh_p), stack(lh_s),
            stack(lc_p), stack(lc_s), stack(rw_p), stack(rw_s), stack(rs_p), stack(rs_s))
```
